```python
import math
import jax, jax.numpy as jnp
from jax import lax
import numpy as np

D_MODEL = 1024
BATCH = 16
SEQ = 2048
DEPTH = 2

HEAD_DIM = 64
GDN_HEADS = 8
GDN_WIDTH = GDN_HEADS * HEAD_DIM
GDN_CONV = 4
GDN_CHUNK = 64
NSA_HEADS = 8
NSA_KV_HEADS = 2
NSA_GROUP = NSA_HEADS // NSA_KV_HEADS
NSA_WIDTH = NSA_HEADS * HEAD_DIM
NSA_KV_WIDTH = NSA_KV_HEADS * HEAD_DIM
CMP_BLOCK = 32
CMP_STRIDE = 16
CMP_HIDDEN = 256
SLC_BLOCK = 64
SLC_TOPN = 8
WINDOW = 512
SPARSE_Q_BLOCK = 64
MIX_WIDTH = GDN_WIDTH + NSA_WIDTH
N_EXPERTS = 16
N_EXPERT_GROUPS = 4
EXPERTS_PER_GROUP = N_EXPERTS // N_EXPERT_GROUPS
MOE_TOP_K = 2
EXPERT_FF = 512
MOE_ROW_BLOCK = 256
RMS_EPS = 1e-6
NEG_BIG = -1e30
SEL_BIG = 1e9
IN_SPLITS = (GDN_WIDTH,) * 4 + (GDN_HEADS,) * 2 + (NSA_WIDTH,) + (NSA_KV_WIDTH,) * 6 + (3 * NSA_HEADS,)
IN_COLS = sum(IN_SPLITS)

kernel_name = 'hybrid_gdn_nsa_grouped_moe_adaln'


def rms_norm(x, g):
    xf = x.astype(jnp.float32)
    y = xf * lax.rsqrt(jnp.mean(xf * xf, axis=-1, keepdims=True) + RMS_EPS)
    return (y * g.astype(jnp.float32)).astype(x.dtype)


def l2_norm(x):
    return x * lax.rsqrt(jnp.sum(x * x, axis=-1, keepdims=True) + RMS_EPS)


def alibi_slopes(n):
    return (2.0 ** (-8.0 * np.arange(1, n + 1) / n)).astype(np.float32)


def masked_softmax(scores, mask):
    return jax.nn.softmax(jnp.where(mask, scores, NEG_BIG), axis=-1) * mask


def causal_depthwise_conv(x, w):
    return lax.conv_general_dilated(
        x, w[:, None, :].astype(x.dtype), window_strides=(1,), padding=[(GDN_CONV - 1, 0)],
        dimension_numbers=('NWC', 'WIO', 'NWC'), feature_group_count=x.shape[-1])


def gated_delta_rule_chunked(q, k, v, g, beta):
    B, H, S, DK = q.shape
    DV = v.shape[-1]
    C = GDN_CHUNK
    N = S // C
    q = (q * DK ** -0.5).reshape(B, H, N, C, DK)
    k = k.reshape(B, H, N, C, DK)
    v = v.reshape(B, H, N, C, DV)
    beta = beta.reshape(B, H, N, C, 1)
    g = jnp.cumsum(g.reshape(B, H, N, C), axis=-1)
    pos = jnp.arange(C)
    causal = pos[:, None] >= pos[None, :]
    strict = pos[:, None] > pos[None, :]
    decay = jnp.exp(jnp.where(causal, g[..., :, None] - g[..., None, :], -jnp.inf))
    kb = k * beta
    lower = jnp.where(strict, jnp.einsum('bhnid,bhnjd->bhnij', kb, k) * decay, 0.0)
    eye = jnp.eye(C, dtype=jnp.float32)
    tmat = lax.linalg.triangular_solve(eye + lower, jnp.broadcast_to(eye, lower.shape),
                                       left_side=True, lower=True, unit_diagonal=True)
    u = tmat @ (v * beta)
    w = tmat @ (kb * jnp.exp(g)[..., None])
    qk = jnp.einsum('bhnid,bhnjd->bhnij', q, k) * decay
    g_last = g[..., -1:]
    q_dec = q * jnp.exp(g)[..., None]
    k_dec = k * jnp.exp(g_last - g)[..., None]
    chunk_decay = jnp.exp(g_last)[..., None]

    def step(state, inp):
        q_i, k_i, u_i, w_i, qk_i, cd_i = inp
        v_new = u_i - w_i @ state
        o_i = q_i @ state + qk_i @ v_new
        state = state * cd_i + jnp.swapaxes(k_i, -1, -2) @ v_new
        return state, o_i

    xs = tuple(jnp.moveaxis(t, 2, 0) for t in (q_dec, k_dec, u, w, qk, chunk_decay))
    _, o = lax.scan(step, jnp.zeros((B, H, DK, DV), jnp.float32), xs)
    return jnp.moveaxis(o, 0, 2).reshape(B, H, S, DV)


def gdn_mixer(q, k, v, z, a, b, conv_w, a_log, dt_bias, norm_g):
    B, S, _ = q.shape
    qkv = jax.nn.silu(causal_depthwise_conv(jnp.concatenate([q, k, v], axis=-1), conv_w))
    q, k, v = jnp.split(qkv, 3, axis=-1)
    heads = lambda t: t.reshape(B, S, GDN_HEADS, HEAD_DIM).transpose(0, 2, 1, 3).astype(jnp.float32)
    q, k, v = l2_norm(heads(q)), l2_norm(heads(k)), heads(v)
    beta = jax.nn.sigmoid(b.astype(jnp.float32)).transpose(0, 2, 1)
    g = (-jnp.exp(a_log.astype(jnp.float32))
         * jax.nn.softplus(a.astype(jnp.float32) + dt_bias.astype(jnp.float32))).transpose(0, 2, 1)
    o = gated_delta_rule_chunked(q, k, v, g, beta).transpose(0, 2, 1, 3)
    o = rms_norm(o, norm_g) * jax.nn.silu(z.reshape(B, S, GDN_HEADS, HEAD_DIM).astype(jnp.float32))
    return o.reshape(B, S, GDN_WIDTH).astype(z.dtype)


def compress_blocks(x, pe, w1, b1, w2, b2):
    S = x.shape[2]
    n_cmp = (S - CMP_BLOCK) // CMP_STRIDE + 1
    idx = np.arange(n_cmp)[:, None] * CMP_STRIDE + np.arange(CMP_BLOCK)[None, :]
    blk = x[:, :, idx] + pe
    flat = blk.reshape(blk.shape[0], blk.shape[1], n_cmp, CMP_BLOCK * HEAD_DIM)
    return jax.nn.silu(flat @ w1 + b1) @ w2 + b2


def overlap_matrix(n_cmp, n_slc):
    cs = np.arange(n_cmp)[:, None] * CMP_STRIDE
    ss = np.arange(n_slc)[None, :] * SLC_BLOCK
    ov = np.clip(np.minimum(cs + CMP_BLOCK, ss + SLC_BLOCK) - np.maximum(cs, ss), 0, None)
    return (ov / CMP_BLOCK).astype(np.float32)


def gather_blocks(blocks, idx):
    return jax.vmap(jax.vmap(lambda bl, ix: bl[ix]))(blocks, idx)


def nsa_mixer(q, k_cmp, v_cmp, k_slc, v_slc, k_win, v_win, gates, q_norm_g, k_norm_g,
              cmp_pe, cmp_w1, cmp_b1, cmp_w2, cmp_b2):
    B, S, _ = q.shape
    G, R, QB = NSA_KV_HEADS, NSA_GROUP, SPARSE_Q_BLOCK
    scale = HEAD_DIM ** -0.5
    slopes = jnp.asarray(alibi_slopes(NSA_HEADS).reshape(G, R, 1, 1))
    qh = rms_norm(q.reshape(B, S, G, R, HEAD_DIM), q_norm_g).transpose(0, 2, 3, 1, 4)
    kv = lambda t: t.reshape(B, S, G, HEAD_DIM).transpose(0, 2, 1, 3)
    pos = jnp.arange(S)

    kc = rms_norm(compress_blocks(kv(k_cmp), cmp_pe[0], cmp_w1[0], cmp_b1[0], cmp_w2[0], cmp_b2[0]), k_norm_g[0])
    vc = compress_blocks(kv(v_cmp), cmp_pe[1], cmp_w1[1], cmp_b1[1], cmp_w2[1], cmp_b2[1])
    n_cmp = kc.shape[2]
    cmp_end = jnp.arange(n_cmp) * CMP_STRIDE + CMP_BLOCK - 1
    cdist = (pos[:, None] - cmp_end[None, :]).astype(jnp.float32)
    s_cmp = jnp.einsum('bgrtd,bgnd->bgrtn', qh, kc).astype(jnp.float32) * scale - slopes * cdist
    p_cmp = masked_softmax(s_cmp, cdist >= 0)
    o_cmp = jnp.einsum('bgrtn,bgnd->bgrtd', p_cmp.astype(vc.dtype), vc)

    n_slc = S // SLC_BLOCK
    imp = jnp.einsum('bgrtn,nj->bgtj', p_cmp, jnp.asarray(overlap_matrix(n_cmp, n_slc)))
    blk = jnp.arange(n_slc)[None, :]
    cur = (pos // SLC_BLOCK)[:, None]
    valid = blk <= cur
    forced = (blk == 0) | (blk == cur) | (blk == cur - 1)
    sel_score = jnp.where(forced, SEL_BIG, jnp.where(valid, imp, -SEL_BIG))
    n_top = min(SLC_TOPN, n_slc)
    top_val, top_idx = lax.top_k(sel_score, n_top)
    top_ok = top_val > -0.5 * SEL_BIG

    ks_blocks = rms_norm(kv(k_slc), k_norm_g[1]).reshape(B, G, n_slc, SLC_BLOCK, HEAD_DIM)
    vs_blocks = kv(v_slc).reshape(B, G, n_slc, SLC_BLOCK, HEAD_DIM)
    pad = ((0, 0), (0, 0), (WINDOW, 0), (0, 0))
    kw_pad = jnp.pad(rms_norm(kv(k_win), k_norm_g[2]), pad)
    vw_pad = jnp.pad(kv(v_win), pad)
    nqb = S // QB
    q_blocks = jnp.moveaxis(qh.reshape(B, G, R, nqb, QB, HEAD_DIM), 3, 0)
    idx_blocks = jnp.moveaxis(top_idx.reshape(B, G, nqb, QB, n_top), 2, 0)
    ok_blocks = jnp.moveaxis(top_ok.reshape(B, G, nqb, QB, n_top), 2, 0)

    def sparse_block(args):
        i, qb, ib, okb = args
        t = i * QB + jnp.arange(QB)
        kg = gather_blocks(ks_blocks, ib).reshape(B, G, QB, n_top * SLC_BLOCK, HEAD_DIM)
        vg = gather_blocks(vs_blocks, ib).reshape(B, G, QB, n_top * SLC_BLOCK, HEAD_DIM)
        s_pos = (ib[..., None] * SLC_BLOCK + jnp.arange(SLC_BLOCK)).reshape(B, G, QB, n_top * SLC_BLOCK)
        sdist = t[:, None] - s_pos
        ok = jnp.repeat(okb, SLC_BLOCK, axis=-1) & (sdist >= 0)
        s = (jnp.einsum('bgrqd,bgqkd->bgrqk', qb, kg).astype(jnp.float32) * scale
             - slopes * sdist[:, :, None].astype(jnp.float32))
        p = masked_softmax(s, ok[:, :, None])
        o_slc = jnp.einsum('bgrqk,bgqkd->bgrqd', p.astype(vg.dtype), vg)
        kwb = lax.dynamic_slice_in_dim(kw_pad, i * QB, QB + WINDOW, axis=2)
        vwb = lax.dynamic_slice_in_dim(vw_pad, i * QB, QB + WINDOW, axis=2)
        w_pos = i * QB - WINDOW + jnp.arange(QB + WINDOW)
        wd = t[:, None] - w_pos[None, :]
        w_ok = (w_pos[None, :] >= 0) & (wd >= 0) & (wd < WINDOW)
        s = (jnp.einsum('bgrqd,bgkd->bgrqk', qb, kwb).astype(jnp.float32) * scale
             - slopes * wd.astype(jnp.float32))
        p = masked_softmax(s, w_ok)
        o_win = jnp.einsum('bgrqk,bgkd->bgrqd', p.astype(vwb.dtype), vwb)
        return o_slc, o_win

    o_slc, o_win = lax.map(sparse_block, (jnp.arange(nqb), q_blocks, idx_blocks, ok_blocks))
    unblock = lambda t: jnp.moveaxis(t, 0, 3).reshape(B, G, R, S, HEAD_DIM)
    o_slc, o_win = unblock(o_slc), unblock(o_win)
    gt = jax.nn.sigmoid(gates.reshape(B, S, 3, G, R)).transpose(2, 0, 3, 4, 1)[..., None]
    o = gt[0] * o_cmp + gt[1] * o_slc + gt[2] * o_win
    return o.transpose(0, 3, 1, 2, 4).reshape(B, S, NSA_WIDTH).astype(q.dtype)


def grouped_moe(h, router_w, router_bias, w_gate, w_up, w_down):
    B, S, D = h.shape
    T = B * S
    A = T * MOE_TOP_K
    hf = h.reshape(T, D)
    scores = jax.nn.sigmoid((hf @ router_w).astype(jnp.float32))
    biased = scores + router_bias.astype(jnp.float32)
    grp_score = lax.top_k(biased.reshape(T, N_EXPERT_GROUPS, EXPERTS_PER_GROUP), 2)[0].sum(-1)
    best_group = jnp.argmax(grp_score, axis=-1)
    in_group = (jnp.arange(N_EXPERTS) // EXPERTS_PER_GROUP)[None, :] == best_group[:, None]
    _, ids = lax.top_k(jnp.where(in_group, biased, -jnp.inf), MOE_TOP_K)
    gate = jnp.take_along_axis(scores, ids, axis=-1)
    gate = gate / jnp.sum(gate, axis=-1, keepdims=True)
    flat_ids = ids.reshape(A)
    order = jnp.argsort(flat_ids)
    sorted_ids = flat_ids[order]
    tok = order // MOE_TOP_K
    counts = jnp.bincount(flat_ids, length=N_EXPERTS)
    starts = jnp.cumsum(counts) - counts
    padded = (counts + MOE_ROW_BLOCK - 1) // MOE_ROW_BLOCK * MOE_ROW_BLOCK
    pad_ends = jnp.cumsum(padded)
    pad_starts = pad_ends - padded
    dest = pad_starts[sorted_ids] + jnp.arange(A) - starts[sorted_ids]
    n_blocks = -(-(A + N_EXPERTS * (MOE_ROW_BLOCK - 1)) // MOE_ROW_BLOCK)
    rows = n_blocks * MOE_ROW_BLOCK
    xs = jnp.zeros((rows, D), h.dtype).at[dest].set(hf[tok])
    block_expert = jnp.minimum(
        jnp.searchsorted(pad_ends, jnp.arange(n_blocks) * MOE_ROW_BLOCK, side='right'), N_EXPERTS - 1)

    def expert_block(args):
        xb, e = args
        return (jax.nn.silu(xb @ w_gate[e]) * (xb @ w_up[e])) @ w_down[e]

    ys = lax.map(expert_block, (xs.reshape(n_blocks, MOE_ROW_BLOCK, D), block_expert)).reshape(rows, D)
    y = ys[dest] * gate.reshape(A)[order][:, None].astype(ys.dtype)
    return jnp.zeros((T, D), ys.dtype).at[tok].add(y).reshape(B, S, D)


def setup_inputs(seed: int = 0) -> dict:
    key = jax.random.key(seed)
    ks = iter(jax.random.split(key, 32))
    nrm = lambda shape, s: jax.random.normal(next(ks), shape, jnp.float32) * s
    L = DEPTH
    x = nrm((BATCH, SEQ, D_MODEL), 1.0)
    c = nrm((BATCH, D_MODEL), 1.0)
    ada_w = nrm((L, D_MODEL, 6 * D_MODEL), 0.5 * D_MODEL ** -0.5)
    ada_b = nrm((L, 6 * D_MODEL), 0.02)
    norm1_g = 1.0 + nrm((L, D_MODEL), 0.05)
    norm2_g = 1.0 + nrm((L, D_MODEL), 0.05)
    w_in = nrm((L, D_MODEL, IN_COLS), D_MODEL ** -0.5)
    gdn_conv_w = nrm((L, GDN_CONV, 3 * GDN_WIDTH), GDN_CONV ** -0.5)
    gdn_a_log = jnp.log(jax.random.uniform(next(ks), (L, GDN_HEADS), jnp.float32, 1.0, 16.0))
    dt = jnp.exp(jax.random.uniform(next(ks), (L, GDN_HEADS), jnp.float32, math.log(1e-3), math.log(1e-1)))
    gdn_dt_bias = dt + jnp.log(-jnp.expm1(-dt))
    gdn_norm_g = 1.0 + nrm((L, HEAD_DIM), 0.05)
    nsa_q_norm_g = 1.0 + nrm((L, HEAD_DIM), 0.05)
    nsa_k_norm_g = 1.0 + nrm((L, 3, HEAD_DIM), 0.05)
    cmp_pe = nrm((L, 2, CMP_BLOCK, HEAD_DIM), 0.1)
    cmp_w1 = nrm((L, 2, CMP_BLOCK * HEAD_DIM, CMP_HIDDEN), (CMP_BLOCK * HEAD_DIM) ** -0.5)
    cmp_b1 = nrm((L, 2, CMP_HIDDEN), 0.02)
    cmp_w2 = nrm((L, 2, CMP_HIDDEN, HEAD_DIM), CMP_HIDDEN ** -0.5)
    cmp_b2 = nrm((L, 2, HEAD_DIM), 0.02)
    w_out = nrm((L, MIX_WIDTH, D_MODEL), MIX_WIDTH ** -0.5)
    router_w = nrm((D_MODEL, N_EXPERTS), D_MODEL ** -0.5)
    router_bias = nrm((N_EXPERTS,), 0.01)
    exp_w_gate = nrm((L, N_EXPERTS, D_MODEL, EXPERT_FF), D_MODEL ** -0.5)
    exp_w_up = nrm((L, N_EXPERTS, D_MODEL, EXPERT_FF), D_MODEL ** -0.5)
    exp_w_down = nrm((L, N_EXPERTS, EXPERT_FF, D_MODEL), EXPERT_FF ** -0.5)
    return {'x': x, 'c': c, 'ada_w': ada_w, 'ada_b': ada_b, 'norm1_g': norm1_g, 'norm2_g': norm2_g,
            'w_in': w_in, 'gdn_conv_w': gdn_conv_w, 'gdn_a_log': gdn_a_log, 'gdn_dt_bias': gdn_dt_bias,
            'gdn_norm_g': gdn_norm_g, 'nsa_q_norm_g': nsa_q_norm_g, 'nsa_k_norm_g': nsa_k_norm_g,
            'cmp_pe': cmp_pe, 'cmp_w1': cmp_w1, 'cmp_b1': cmp_b1, 'cmp_w2': cmp_w2, 'cmp_b2': cmp_b2,
            'w_out': w_out, 'router_w': router_w, 'router_bias': router_bias,
            'exp_w_gate': exp_w_gate, 'exp_w_up': exp_w_up, 'exp_w_down': exp_w_down}


def reference(x, c, ada_w, ada_b, norm1_g, norm2_g, w_in, gdn_conv_w, gdn_a_log, gdn_dt_bias,
              gdn_norm_g, nsa_q_norm_g, nsa_k_norm_g, cmp_pe, cmp_w1, cmp_b1, cmp_w2, cmp_b2,
              w_out, router_w, router_bias, exp_w_gate, exp_w_up, exp_w_down):
    cond = jax.nn.silu(c)
    split_points = np.cumsum(IN_SPLITS)[:-1].tolist()
    for layer in range(DEPTH):
        mod = (cond @ ada_w[layer] + ada_b[layer])[:, None, :]
        sh1, sc1, g1, sh2, sc2, g2 = jnp.split(mod, 6, axis=-1)
        h = rms_norm(x, norm1_g[layer]) * (1.0 + sc1) + sh1
        (gq, gk, gv, gz, ga, gb, nq, nkc, nvc, nks, nvs, nkw, nvw, ngate) = jnp.split(
            h @ w_in[layer], split_points, axis=-1)
        y_gdn = gdn_mixer(gq, gk, gv, gz, ga, gb, gdn_conv_w[layer], gdn_a_log[layer],
                          gdn_dt_bias[layer], gdn_norm_g[layer])
        y_nsa = nsa_mixer(nq, nkc, nvc, nks, nvs, nkw, nvw, ngate, nsa_q_norm_g[layer],
                          nsa_k_norm_g[layer], cmp_pe[layer], cmp_w1[layer], cmp_b1[layer],
                          cmp_w2[layer], cmp_b2[layer])
        x = x + g1 * (jnp.concatenate([y_gdn, y_nsa], axis=-1) @ w_out[layer])
        h = rms_norm(x, norm2_g[layer]) * (1.0 + sc2) + sh2
        x = x + g2 * grouped_moe(h, router_w, router_bias, exp_w_gate[layer], exp_w_up[layer],
                                 exp_w_down[layer])
    return x
```

```python
import functools

import numpy as np
import jax
import jax.numpy as jnp
from jax import lax
from jax.experimental import pallas as pl
from jax.experimental.pallas import tpu as pltpu

F32 = jnp.float32
BF16 = jnp.bfloat16

HEAD_DIM = 64
GDN_HEADS = 8
GDN_WIDTH = GDN_HEADS * HEAD_DIM
GDN_CONV = 4
GDN_CHUNK = 64
NSA_HEADS = 8
NSA_KV_HEADS = 2
NSA_GROUP = NSA_HEADS // NSA_KV_HEADS
NSA_WIDTH = NSA_HEADS * HEAD_DIM
NSA_KV_WIDTH = NSA_KV_HEADS * HEAD_DIM
CMP_BLOCK = 32
CMP_STRIDE = 16
CMP_HIDDEN = 256
SLC_BLOCK = 64
SLC_TOPN = 8
WINDOW = 512
N_EXPERTS = 16
EXPERTS_PER_GROUP = 4
EXPERT_FF = 512
RMS_EPS = 1e-6
NEG_BIG = -1e30
SEL_BIG = 1e9

LANES = 128
VMEM_LIMIT_BYTES = 56 * 1024 * 1024

C_QKV = 0
C_Z = 3 * GDN_WIDTH
C_NQ = C_Z + GDN_WIDTH
C_KV = C_NQ + NSA_WIDTH
C_AEXP = C_KV + 6 * NSA_KV_WIDTH
C_BEXP = C_AEXP + GDN_WIDTH
C_MISC = C_BEXP + GDN_WIDTH
IN_PAD_COLS = C_MISC + LANES
MISC_GATE0 = 16


def _cparams(*sem):
    return pltpu.CompilerParams(dimension_semantics=sem, vmem_limit_bytes=VMEM_LIMIT_BYTES)


def _dot(a, b):
    return jnp.dot(a.astype(BF16), b.astype(BF16), preferred_element_type=F32)


def _dot_nt(a, b):
    return lax.dot_general(a.astype(BF16), b.astype(BF16), (((1,), (1,)), ((), ())),
                           preferred_element_type=F32)


def _dot_tn(a, b):
    return lax.dot_general(a.astype(BF16), b.astype(BF16), (((0,), (0,)), ((), ())),
                           preferred_element_type=F32)


def _split2(x):
    hi = x.astype(BF16)
    lo = (x - hi.astype(F32)).astype(BF16)
    return hi, lo


def _split3(x):
    hi = x.astype(BF16)
    r = x - hi.astype(F32)
    mid = r.astype(BF16)
    lo = (r - mid.astype(F32)).astype(BF16)
    return hi, mid, lo


def _dot_hi(a, b):
    ah, al = _split2(a)
    bh, bl = _split2(b)
    return (jnp.dot(ah, bh, preferred_element_type=F32) + jnp.dot(al, bh, preferred_element_type=F32)
            + jnp.dot(ah, bl, preferred_element_type=F32))


def _dot_nt_hi(a, b):
    ah, al = _split2(a)
    bh, bl = _split2(b)
    dn = (((1,), (1,)), ((), ()))
    return (lax.dot_general(ah, bh, dn, preferred_element_type=F32)
            + lax.dot_general(al, bh, dn, preferred_element_type=F32)
            + lax.dot_general(ah, bl, dn, preferred_element_type=F32))


def _dot_exact_rhs(a, b_bf16):
    hi, mid, lo = _split3(a)
    return (jnp.dot(hi, b_bf16, preferred_element_type=F32) + jnp.dot(mid, b_bf16, preferred_element_type=F32)
            + jnp.dot(lo, b_bf16, preferred_element_type=F32))


def _rms(x, g):
    return x * lax.rsqrt(jnp.mean(x * x, axis=-1, keepdims=True) + RMS_EPS) * g


def _sigmoid(x):
    return 1.0 / (1.0 + jnp.exp(-x))


def _silu(x):
    return x * _sigmoid(x)


def _ada_kernel(c_ref, w_ref, b_ref, o_ref):
    o_ref[...] = _dot_hi(_silu(c_ref[...]), w_ref[...]) + b_ref[...]


def _ada_call(c, ada_w, ada_b):
    L, D, N = ada_w.shape
    B = c.shape[0]
    tn = 1024
    return pl.pallas_call(
        _ada_kernel,
        grid=(L, N // tn),
        in_specs=[pl.BlockSpec((B, D), lambda l, j: (0, 0)),
                  pl.BlockSpec((None, D, tn), lambda l, j: (l, 0, j)),
                  pl.BlockSpec((None, 1, tn), lambda l, j: (l, 0, j))],
        out_specs=pl.BlockSpec((None, B, tn), lambda l, j: (l, 0, j)),
        out_shape=jax.ShapeDtypeStruct((L, B, N), F32),
        compiler_params=_cparams("parallel", "parallel"),
        name="ada_mod",
    )(c, ada_w, ada_b.reshape(L, 1, N))


IN_SEGS = ((C_QKV, 3 * GDN_WIDTH), (C_Z, GDN_WIDTH), (C_NQ, NSA_WIDTH), (C_KV, 6 * NSA_KV_WIDTH),
           (C_AEXP, GDN_WIDTH), (C_BEXP, GDN_WIDTH), (C_MISC, LANES))


def _in_proj_kernel(x_ref, sc_ref, sh_ref, g_ref, w_ref, *o_refs):
    h = _rms(x_ref[...], g_ref[...]) * (1.0 + sc_ref[...]) + sh_ref[...]
    hb = h.astype(BF16)
    for (c0, n), o_ref in zip(IN_SEGS, o_refs):
        o_ref[...] = jnp.dot(hb, w_ref[:, c0:c0 + n], preferred_element_type=F32)


def _in_proj_call(x2, sc, sh, g, w_pad, S):
    T, D = x2.shape
    tm = 256
    per_b = S // tm
    row = lambda i: (i, 0)
    bat = lambda i: (i // per_b, 0, 0)
    return pl.pallas_call(
        _in_proj_kernel,
        grid=(T // tm,),
        in_specs=[pl.BlockSpec((tm, D), row),
                  pl.BlockSpec((None, 1, D), bat),
                  pl.BlockSpec((None, 1, D), bat),
                  pl.BlockSpec((1, D), lambda i: (0, 0)),
                  pl.BlockSpec((D, IN_PAD_COLS), lambda i: (0, 0))],
        out_specs=[pl.BlockSpec((tm, n), row) for _, n in IN_SEGS],
        out_shape=[jax.ShapeDtypeStruct((T, n), F32) for _, n in IN_SEGS],
        compiler_params=_cparams("parallel"),
        name="in_proj",
    )(x2, sc, sh, g, w_pad)


def _pad_w_in(w_in):
    D = w_in.shape[0]
    W = GDN_WIDTH
    o = 0
    gq, gk, gv, gz = (w_in[:, o + i * W:o + (i + 1) * W] for i in range(4))
    o += 4 * W
    ga = w_in[:, o:o + GDN_HEADS]
    gb = w_in[:, o + GDN_HEADS:o + 2 * GDN_HEADS]
    o += 2 * GDN_HEADS
    nq = w_in[:, o:o + NSA_WIDTH]
    o += NSA_WIDTH
    kv = w_in[:, o:o + 6 * NSA_KV_WIDTH]
    o += 6 * NSA_KV_WIDTH
    gates = w_in[:, o:o + 3 * NSA_HEADS]
    misc = jnp.zeros((D, LANES), w_in.dtype)
    misc = misc.at[:, 0:GDN_HEADS].set(ga).at[:, MISC_GATE0:MISC_GATE0 + 3 * NSA_HEADS].set(gates)
    cols = [gq, gk, gv, gz, nq, kv, jnp.repeat(ga, HEAD_DIM, axis=1), jnp.repeat(gb, HEAD_DIM, axis=1), misc]
    return jnp.concatenate(cols, axis=1).astype(BF16)


GDN_HPB = 2


def _gdn_kernel(q_ref, k_ref, v_ref, z_ref, a_ref, b_ref, ar_ref, cwq_ref, cwk_ref, cwv_ref,
                alog_ref, dtb_ref, alogr_ref, dtbr_ref, ng_ref, o_ref,
                qs, ks, kbs, x0s, gcs, grs, us, ws, qks, os_, *, S):
    C = GDN_CHUNK
    N = S // C
    W = GDN_HPB * HEAD_DIM
    row = lax.broadcasted_iota(jnp.int32, (S, W), 0)
    lane = lax.broadcasted_iota(jnp.int32, (S, W), 1)

    def conv_silu(x, w):
        y = x * w[GDN_CONV - 1:GDN_CONV, :]
        for s in range(1, GDN_CONV):
            xs = jnp.where(row >= s, pltpu.roll(x, s, axis=0), 0.0)
            y = y + xs * w[GDN_CONV - 1 - s:GDN_CONV - s, :]
        return _silu(y)

    def head_l2(x):
        sq = x * x
        parts = [jnp.sum(sq[:, h * HEAD_DIM:(h + 1) * HEAD_DIM], axis=-1, keepdims=True) for h in range(GDN_HPB)]
        ss = jnp.where(lane < HEAD_DIM, parts[0], parts[1])
        return x * lax.rsqrt(ss + RMS_EPS)

    q = head_l2(conv_silu(q_ref[...], cwq_ref[...])) * (HEAD_DIM ** -0.5)
    k = head_l2(conv_silu(k_ref[...], cwk_ref[...]))
    v = conv_silu(v_ref[...], cwv_ref[...])
    beta = _sigmoid(b_ref[...])

    def log_decay(a, alog, dtb):
        xx = a + dtb
        sp = jnp.maximum(xx, 0.0) + jnp.log(1.0 + jnp.exp(-jnp.abs(xx)))
        return -jnp.exp(alog) * sp

    gc = log_decay(a_ref[...], alog_ref[...], dtb_ref[...])
    pos = row % C
    for s in (1, 2, 4, 8, 16, 32):
        gc = gc + jnp.where(pos >= s, pltpu.roll(gc, s, axis=0), 0.0)
    gr = log_decay(ar_ref[...], alogr_ref[...], dtbr_ref[...])
    lpos = lax.broadcasted_iota(jnp.int32, (N, W), 1) % C
    for s in (1, 2, 4, 8, 16, 32):
        gr = gr + jnp.where(lpos >= s, pltpu.roll(gr, s, axis=1), 0.0)

    kb = k * beta
    vb = v * beta
    kbe = kb * jnp.exp(gc)
    for h in range(GDN_HPB):
        sl = slice(h * HEAD_DIM, (h + 1) * HEAD_DIM)
        qs[h] = q[:, sl]
        ks[h] = k[:, sl]
        kbs[h] = kb[:, sl]
        gcs[h] = gc[:, sl]
        grs[h] = gr[:, sl]
        x0s[h] = jnp.concatenate([kbe[:, sl], vb[:, sl]], axis=1)

    ci = lax.broadcasted_iota(jnp.int32, (C, C), 0)
    cj = lax.broadcasted_iota(jnp.int32, (C, C), 1)

    def intra(n, carry):
        sl = pl.ds(pl.multiple_of(n * C, C), C)
        for h in range(GDN_HPB):
            kk = ks[h, sl, :]
            gcol = gcs[h, sl, :]
            grow = grs[h, pl.ds(n, 1), :]
            decay = jnp.exp(jnp.where(ci >= cj, gcol - grow, NEG_BIG))
            a2 = _dot_nt(jnp.concatenate([kbs[h, sl, :], qs[h, sl, :]], axis=0), kk)
            m = -jnp.where(ci > cj, a2[:C] * decay, 0.0)
            qks[h, sl, :] = a2[C:] * decay
            x = x0s[h, sl, :]
            for it in range(6):
                x = x + _dot(m, x)
                if it < 5:
                    m = _dot(m, m)
            ws[h, sl, :] = x[:, :HEAD_DIM]
            us[h, sl, :] = x[:, HEAD_DIM:]
        return carry

    lax.fori_loop(0, N, intra, 0)

    def inter(n, states):
        sl = pl.ds(pl.multiple_of(n * C, C), C)
        new_states = []
        for h in range(GDN_HPB):
            st = states[h]
            gcol = gcs[h, sl, :]
            glast = gcs[h, pl.ds(n * C + C - 1, 1), :]
            qd = qs[h, sl, :] * jnp.exp(gcol)
            kd = ks[h, sl, :] * jnp.exp(glast - gcol)
            wq = _dot(jnp.concatenate([ws[h, sl, :], qd], axis=0), st)
            vn = us[h, sl, :] - wq[:C]
            os_[h, sl, :] = wq[C:] + _dot(qks[h, sl, :], vn)
            new_states.append(st * jnp.exp(glast) + _dot_tn(kd, vn))
        return tuple(new_states)

    lax.fori_loop(0, N, inter, tuple(jnp.zeros((HEAD_DIM, HEAD_DIM), F32) for _ in range(GDN_HPB)))

    ng = ng_ref[...]
    outs = [_rms(os_[h], ng) for h in range(GDN_HPB)]
    o_ref[...] = jnp.concatenate(outs, axis=1) * _silu(z_ref[...])


def _gdn_call(qkv, z, aexp, bexp, a_rows, conv_w, alog_exp, dtb_exp, norm_g, B, S):
    T = B * S
    W = GDN_HPB * HEAD_DIM
    P = GDN_HEADS // GDN_HPB
    N = S // GDN_CHUNK
    col = lambda off: (lambda b, p: (b, off + p))
    par = lambda off: (lambda b, p: (0, off + p))
    hs = lambda n: pltpu.VMEM((GDN_HPB, S, n), F32)
    return pl.pallas_call(
        functools.partial(_gdn_kernel, S=S),
        grid=(B, P),
        in_specs=[pl.BlockSpec((S, W), col(0)), pl.BlockSpec((S, W), col(P)), pl.BlockSpec((S, W), col(2 * P)),
                  pl.BlockSpec((S, W), col(0)), pl.BlockSpec((S, W), col(0)), pl.BlockSpec((S, W), col(0)),
                  pl.BlockSpec((None, None, N, W), lambda b, p: (b, p, 0, 0)),
                  pl.BlockSpec((GDN_CONV, W), par(0)), pl.BlockSpec((GDN_CONV, W), par(P)),
                  pl.BlockSpec((GDN_CONV, W), par(2 * P)),
                  pl.BlockSpec((1, W), par(0)), pl.BlockSpec((1, W), par(0)),
                  pl.BlockSpec((1, W), par(0)), pl.BlockSpec((1, W), par(0)),
                  pl.BlockSpec((1, HEAD_DIM), lambda b, p: (0, 0))],
        out_specs=pl.BlockSpec((S, W), col(0)),
        out_shape=jax.ShapeDtypeStruct((T, GDN_WIDTH), F32),
        scratch_shapes=[hs(HEAD_DIM), hs(HEAD_DIM), hs(HEAD_DIM), hs(2 * HEAD_DIM), hs(HEAD_DIM),
                        pltpu.VMEM((GDN_HPB, N, HEAD_DIM), F32),
                        hs(HEAD_DIM), hs(HEAD_DIM), hs(HEAD_DIM), hs(HEAD_DIM)],
        compiler_params=_cparams("parallel", "parallel"),
        name="gdn",
    )(qkv, qkv, qkv, z, aexp, bexp, a_rows, conv_w, conv_w, conv_w,
      alog_exp, dtb_exp, alog_exp, dtb_exp, norm_g)


def _cmp_kernel(x_ref, pe_ref, w1_ref, b1_ref, w2_ref, b2_ref, kg_ref, o_ref):
    half = (CMP_BLOCK // 2) * HEAD_DIM
    x = x_ref[...]
    nrows = x.shape[0]
    pe = pe_ref[...]
    p = _dot(x + pe[:, :half], w1_ref[:half, :])
    q = _dot(x + pe[:, half:], w1_ref[half:, :])
    hid = p + pltpu.roll(q, nrows - 1, axis=0) + b1_ref[...]
    out = _dot(_silu(hid), w2_ref[...]) + b2_ref[...]
    is_key = pl.program_id(0) == 0
    o_ref[...] = jnp.where(is_key, _rms(out, kg_ref[...]), out)


def _cmp_call(xkv, pe, w1, b1, w2, b2, kg):
    _, BG, NC, half = xkv.shape
    sel = lambda j, i: (j, 0, 0)
    return pl.pallas_call(
        _cmp_kernel,
        grid=(2, BG),
        in_specs=[pl.BlockSpec((None, None, NC, half), lambda j, i: (j, i, 0, 0)),
                  pl.BlockSpec((None, 1, 2 * half), sel),
                  pl.BlockSpec((None, 2 * half, CMP_HIDDEN), sel),
                  pl.BlockSpec((None, 1, CMP_HIDDEN), sel),
                  pl.BlockSpec((None, CMP_HIDDEN, HEAD_DIM), sel),
                  pl.BlockSpec((None, 1, HEAD_DIM), sel),
                  pl.BlockSpec((1, HEAD_DIM), lambda j, i: (0, 0))],
        out_specs=pl.BlockSpec((None, None, NC, HEAD_DIM), lambda j, i: (j, i, 0, 0)),
        out_shape=jax.ShapeDtypeStruct((2, BG, NC, HEAD_DIM), F32),
        compiler_params=_cparams("parallel", "parallel"),
        name="nsa_compress",
    )(xkv, pe, w1, b1, w2, b2, kg)


NSA_TQ = 128
NSA_TK = 256


def _nsa_kernel(q_ref, kv_ref, cmp_ref, misc_ref, qg_ref, kg_ref, ov_ref, e_ref, o_ref,
                ks_s, vs_s, kw_s, vw_s, *, S):
    tq, tk = NSA_TQ, NSA_TK
    G, R = NSA_KV_HEADS, NSA_GROUP
    M = R * tq
    NC = S // CMP_STRIDE
    NS = S // SLC_BLOCK
    n_top = min(SLC_TOPN, NS)
    qi = pl.program_id(1)
    t0 = qi * tq

    @pl.when(qi == 0)
    def _():
        for g in range(G):
            c = 2 * NSA_KV_WIDTH + g * HEAD_DIM
            ks_s[g] = _rms(kv_ref[:, c:c + HEAD_DIM], kg_ref[1:2, :]).astype(BF16)
            c += NSA_KV_WIDTH
            vs_s[g] = kv_ref[:, c:c + HEAD_DIM].astype(BF16)
            c += NSA_KV_WIDTH
            kw_s[g] = _rms(kv_ref[:, c:c + HEAD_DIM], kg_ref[2:3, :]).astype(BF16)
            c += NSA_KV_WIDTH
            vw_s[g] = kv_ref[:, c:c + HEAD_DIM].astype(BF16)

    rowm = lax.broadcasted_iota(jnp.int32, (M, 1), 0)
    t_col = (t0 + rowm % tq).astype(F32)
    gate_sig = _sigmoid(misc_ref[:, MISC_GATE0:MISC_GATE0 + 3 * NSA_HEADS])

    def flash(Q, slope_col, k_s, v_s, g, kt_lo, kt_hi, mask_fn):
        def body(kt, carry):
            m, l, acc = carry
            k0 = pl.multiple_of(kt * tk, tk)
            kblk = k_s[g, pl.ds(k0, tk), :]
            s_pos = (k0 + lax.broadcasted_iota(jnp.int32, (1, tk), 1)).astype(F32)
            dist = t_col - s_pos
            s = lax.dot_general(Q, kblk, (((1,), (1,)), ((), ())), preferred_element_type=F32) - slope_col * dist
            mask = mask_fn(kt, dist)
            s = jnp.where(mask, s, NEG_BIG)
            m_new = jnp.maximum(m, jnp.max(s, axis=1, keepdims=True))
            alpha = jnp.exp(m - m_new)
            p = jnp.where(mask, jnp.exp(s - m_new), 0.0)
            l = alpha * l + jnp.sum(p, axis=1, keepdims=True)
            acc = alpha * acc + jnp.dot(p.astype(BF16), v_s[g, pl.ds(k0, tk), :], preferred_element_type=F32)
            return m_new, l, acc

        init = (jnp.full((M, 1), NEG_BIG, F32), jnp.zeros((M, 1), F32), jnp.zeros((M, HEAD_DIM), F32))
        _, l, acc = lax.fori_loop(kt_lo, kt_hi, body, init)
        return acc / jnp.where(l > 0.0, l, 1.0)

    kt_hi = (t0 + tq + tk - 1) // tk
    kt_lo_win = jnp.maximum(t0 - (WINDOW - 1), 0) // tk

    for g in range(G):
        qn = [_rms(q_ref[:, (g * R + r) * HEAD_DIM:(g * R + r + 1) * HEAD_DIM], qg_ref[...]) * (HEAD_DIM ** -0.5)
              for r in range(R)]
        Q = jnp.concatenate(qn, axis=0).astype(BF16)
        slope_col = jnp.zeros((M, 1), F32)
        for r in range(R):
            slope_col = jnp.where(rowm // tq == r, 2.0 ** (-8.0 * (g * R + r + 1) / NSA_HEADS), slope_col)

        kc = cmp_ref[0, g]
        vc = cmp_ref[1, g]
        n_row = lax.broadcasted_iota(jnp.int32, (1, NC), 1)
        cdist = t_col - (n_row * CMP_STRIDE + (CMP_BLOCK - 1)).astype(F32)
        cmask = (cdist >= 0.0) & (n_row < NC - 1)
        s = jnp.where(cmask, _dot_nt(Q, kc) - slope_col * cdist, NEG_BIG)
        e = jnp.where(cmask, jnp.exp(s - jnp.max(s, axis=1, keepdims=True)), 0.0)
        l = jnp.sum(e, axis=1, keepdims=True)
        p_cmp = e / jnp.where(l > 0.0, l, 1.0)
        o_cmp = _dot(p_cmp, vc)

        psum = p_cmp[0:tq]
        for r in range(1, R):
            psum = psum + p_cmp[r * tq:(r + 1) * tq]
        imp = _dot_exact_rhs(psum, ov_ref[...])
        blk = lax.broadcasted_iota(jnp.int32, (tq, NS), 1)
        cur = (t0 + lax.broadcasted_iota(jnp.int32, (tq, NS), 0)) // SLC_BLOCK
        valid = blk <= cur
        forced = (blk == 0) | (blk == cur) | (blk == cur - 1)
        score = jnp.where(forced, SEL_BIG, jnp.where(valid, imp, -SEL_BIG))
        rank = jnp.zeros((tq, NS), F32)
        for j in range(NS):
            cj = score[:, j:j + 1]
            beats = (cj > score) | ((cj == score) & (blk > j))
            rank = rank + jnp.where(beats, 1.0, 0.0)
        sel = jnp.where((rank < n_top) & valid, 1.0, 0.0).astype(BF16)

        def slc_mask(kt, dist, sel=sel):
            em = jnp.dot(sel, e_ref[kt], preferred_element_type=F32)
            em = jnp.concatenate([em] * R, axis=0)
            return (dist >= 0.0) & (em > 0.5)

        def win_mask(kt, dist):
            return (dist >= 0.0) & (dist < float(WINDOW))

        o_slc = flash(Q, slope_col, ks_s, vs_s, g, 0, kt_hi, slc_mask)
        o_win = flash(Q, slope_col, kw_s, vw_s, g, kt_lo_win, kt_hi, win_mask)

        for r in range(R):
            hh = g * R + r
            rs = slice(r * tq, (r + 1) * tq)
            o = (gate_sig[:, hh:hh + 1] * o_cmp[rs]
                 + gate_sig[:, NSA_HEADS + hh:NSA_HEADS + hh + 1] * o_slc[rs]
                 + gate_sig[:, 2 * NSA_HEADS + hh:2 * NSA_HEADS + hh + 1] * o_win[rs])
            o_ref[:, hh * HEAD_DIM:(hh + 1) * HEAD_DIM] = o


def _nsa_consts(S):
    NC = S // CMP_STRIDE
    NS = S // SLC_BLOCK
    cs = np.arange(NC)[:, None] * CMP_STRIDE
    ss = np.arange(NS)[None, :] * SLC_BLOCK
    ov = np.clip(np.minimum(cs + CMP_BLOCK, ss + SLC_BLOCK) - np.maximum(cs, ss), 0, None) / CMP_BLOCK
    ov[NC - 1:] = 0.0
    pos = np.arange(S).reshape(S // NSA_TK, 1, NSA_TK)
    e = (pos // SLC_BLOCK == np.arange(NS)[None, :, None]).astype(np.float32)
    return jnp.asarray(ov, BF16), jnp.asarray(e, BF16)


def _nsa_call(nq, kv, cmp_kv, misc, q_norm_g, k_norm_g, B, S):
    T = B * S
    tq = NSA_TQ
    nqt = S // tq
    NC = S // CMP_STRIDE
    NS = S // SLC_BLOCK
    ov, e = _nsa_consts(S)
    row = lambda b, i: (b * nqt + i, 0)
    cst = lambda b, i: (0, 0)
    return pl.pallas_call(
        functools.partial(_nsa_kernel, S=S),
        grid=(B, nqt),
        in_specs=[pl.BlockSpec((tq, NSA_WIDTH), row),
                  pl.BlockSpec((S, 6 * NSA_KV_WIDTH), lambda b, i: (b, 0)),
                  pl.BlockSpec((2, None, NSA_KV_HEADS, NC, HEAD_DIM), lambda b, i: (0, b, 0, 0, 0)),
                  pl.BlockSpec((tq, LANES), row),
                  pl.BlockSpec((1, HEAD_DIM), cst),
                  pl.BlockSpec((3, HEAD_DIM), cst),
                  pl.BlockSpec((NC, NS), cst),
                  pl.BlockSpec((S // NSA_TK, NS, NSA_TK), lambda b, i: (0, 0, 0))],
        out_specs=pl.BlockSpec((tq, NSA_WIDTH), row),
        out_shape=jax.ShapeDtypeStruct((T, NSA_WIDTH), F32),
        scratch_shapes=[pltpu.VMEM((NSA_KV_HEADS, S, HEAD_DIM), BF16) for _ in range(4)],
        compiler_params=_cparams("parallel", "arbitrary"),
        name="nsa_attn",
    )(nq, kv, cmp_kv, misc, q_norm_g, k_norm_g, ov, e)


def _out_proj_kernel(yg_ref, yn_ref, x_ref, g1_ref, sc_ref, sh_ref, ng_ref, wo_ref, rwt_ref, rb_ref,
                     x1_ref, h2_ref, gate_ref):
    mix = _dot(yg_ref[...], wo_ref[:GDN_WIDTH, :]) + _dot(yn_ref[...], wo_ref[GDN_WIDTH:, :])
    x1 = x_ref[...] + g1_ref[...] * mix
    x1_ref[...] = x1
    h2 = _rms(x1, ng_ref[...]) * (1.0 + sc_ref[...]) + sh_ref[...]
    h2_ref[...] = h2.astype(BF16)

    tm = h2.shape[0]
    logits = _dot_nt_hi(rwt_ref[...], h2)
    score = _sigmoid(logits)
    biased = score + rb_ref[...]
    b = [biased[e:e + 1, :] for e in range(N_EXPERTS)]
    n_groups = N_EXPERTS // EXPERTS_PER_GROUP
    gscore = []
    for gi in range(n_groups):
        vals = b[gi * EXPERTS_PER_GROUP:(gi + 1) * EXPERTS_PER_GROUP]
        best = None
        for i in range(EXPERTS_PER_GROUP):
            for j in range(i + 1, EXPERTS_PER_GROUP):
                pair = vals[i] + vals[j]
                best = pair if best is None else jnp.maximum(best, pair)
        gscore.append(best)
    gbest = jnp.zeros((1, tm), jnp.int32)
    top = gscore[0]
    for gi in range(1, n_groups):
        better = gscore[gi] > top
        gbest = jnp.where(better, gi, gbest)
        top = jnp.where(better, gscore[gi], top)
    erow = lax.broadcasted_iota(jnp.int32, (N_EXPERTS, tm), 0)
    gates = jnp.zeros((N_EXPERTS, tm), F32)
    for e in range(N_EXPERTS):
        gi = e // EXPERTS_PER_GROUP
        rank = jnp.zeros((1, tm), F32)
        for e2 in range(gi * EXPERTS_PER_GROUP, (gi + 1) * EXPERTS_PER_GROUP):
            if e2 == e:
                continue
            beats = (b[e2] > b[e]) | ((b[e2] == b[e]) & (e2 < e))
            rank = rank + jnp.where(beats, 1.0, 0.0)
        chosen = (gbest == gi) & (rank < 2.0)
        ge = jnp.where(chosen, score[e:e + 1, :], 0.0)
        gates = gates + jnp.where(erow == e, ge, 0.0)
    gates = gates / jnp.sum(gates, axis=0, keepdims=True)
    ident = (lax.broadcasted_iota(jnp.int32, (N_EXPERTS, LANES), 0)
             == lax.broadcasted_iota(jnp.int32, (N_EXPERTS, LANES), 1))
    ident = jnp.where(ident, 1.0, 0.0).astype(BF16)
    hi, mid, lo = _split3(gates)
    dn = (((0,), (0,)), ((), ()))
    gate_ref[...] = (lax.dot_general(hi, ident, dn, preferred_element_type=F32)
                     + lax.dot_general(mid, ident, dn, preferred_element_type=F32)
                     + lax.dot_general(lo, ident, dn, preferred_element_type=F32))


def _out_proj_call(yg, yn, x2, g1, sc2, sh2, ng, wo, rwt, rb, S):
    T, D = x2.shape
    tm = 256
    per_b = S // tm
    row = lambda i: (i, 0)
    bat = lambda i: (i // per_b, 0, 0)
    cst = lambda i: (0, 0)
    return pl.pallas_call(
        _out_proj_kernel,
        grid=(T // tm,),
        in_specs=[pl.BlockSpec((tm, GDN_WIDTH), row), pl.BlockSpec((tm, NSA_WIDTH), row), pl.BlockSpec((tm, D), row),
                  pl.BlockSpec((None, 1, D), bat), pl.BlockSpec((None, 1, D), bat), pl.BlockSpec((None, 1, D), bat),
                  pl.BlockSpec((1, D), cst), pl.BlockSpec((D, D), cst),
                  pl.BlockSpec((N_EXPERTS, D), cst), pl.BlockSpec((N_EXPERTS, 1), cst)],
        out_specs=[pl.BlockSpec((tm, D), row), pl.BlockSpec((tm, D), row), pl.BlockSpec((tm, LANES), row)],
        out_shape=[jax.ShapeDtypeStruct((T, D), F32), jax.ShapeDtypeStruct((T, D), BF16),
                   jax.ShapeDtypeStruct((T, LANES), F32)],
        compiler_params=_cparams("parallel"),
        name="out_proj_router",
    )(yg, yn, x2, g1, sc2, sh2, ng, wo, rwt, rb)


def _moe_kernel(h_ref, gate_ref, x1_ref, g2_ref, wg_ref, wu_ref, wd_ref, o_ref):
    e = pl.program_id(1)
    h = h_ref[...]
    lane = lax.broadcasted_iota(jnp.int32, gate_ref.shape, 1)
    gcol = jnp.sum(jnp.where(lane == e, gate_ref[...], 0.0), axis=1, keepdims=True)
    act = _silu(jnp.dot(h, wg_ref[...], preferred_element_type=F32)) * jnp.dot(h, wu_ref[...],
                                                                              preferred_element_type=F32)
    y = _dot(act * gcol, wd_ref[...])

    @pl.when(e == 0)
    def _():
        o_ref[...] = y

    @pl.when(e > 0)
    def _():
        o_ref[...] += y

    @pl.when(e == N_EXPERTS - 1)
    def _():
        o_ref[...] = x1_ref[...] + g2_ref[...] * o_ref[...]


def _moe_call(h2, gates, x1, g2, wg, wu, wd, S):
    T, D = x1.shape
    tm = 1024
    per_b = S // tm
    row = lambda i, e: (i, 0)
    wsel = lambda i, e: (e, 0, 0)
    return pl.pallas_call(
        _moe_kernel,
        grid=(T // tm, N_EXPERTS),
        in_specs=[pl.BlockSpec((tm, D), row), pl.BlockSpec((tm, LANES), row), pl.BlockSpec((tm, D), row),
                  pl.BlockSpec((None, 1, D), lambda i, e: (i // per_b, 0, 0)),
                  pl.BlockSpec((None, D, EXPERT_FF), wsel), pl.BlockSpec((None, D, EXPERT_FF), wsel),
                  pl.BlockSpec((None, EXPERT_FF, D), wsel)],
        out_specs=pl.BlockSpec((tm, D), row),
        out_shape=jax.ShapeDtypeStruct((T, D), F32),
        compiler_params=_cparams("parallel", "arbitrary"),
        name="moe",
    )(h2, gates, x1, g2, wg, wu, wd)


def kernel(x, c, ada_w, ada_b, norm1_g, norm2_g, w_in, gdn_conv_w, gdn_a_log, gdn_dt_bias, gdn_norm_g,
           nsa_q_norm_g, nsa_k_norm_g, cmp_pe, cmp_w1, cmp_b1, cmp_w2, cmp_b2, w_out, router_w, router_bias,
           exp_w_gate, exp_w_up, exp_w_down):
    B, S, D = x.shape
    L = ada_w.shape[0]
    T = B * S
    G = NSA_KV_HEADS
    NC = S // CMP_STRIDE
    N = S // GDN_CHUNK
    P = GDN_HEADS // GDN_HPB
    W = GDN_HPB * HEAD_DIM

    mod = _ada_call(c, ada_w, ada_b)
    rwt = router_w.T
    rb = router_bias.reshape(N_EXPERTS, 1)
    x2 = x.reshape(T, D)
    for l in range(L):
        m6 = mod[l].reshape(B, 6, 1, D)
        sh1, sc1, g1, sh2, sc2, g2 = (m6[:, i] for i in range(6))
        qkv, z, nq, kv, aexp, bexp, misc = _in_proj_call(
            x2, sc1, sh1, norm1_g[l].reshape(1, D), _pad_w_in(w_in[l]), S)

        a_rows = misc[:, :GDN_HEADS].reshape(B, N, GDN_CHUNK, P, GDN_HPB).transpose(0, 3, 1, 4, 2).reshape(B, P, N, W)
        rep = lambda t: jnp.repeat(t, HEAD_DIM).reshape(1, GDN_WIDTH)
        y_gdn = _gdn_call(qkv, z, aexp, bexp, a_rows, gdn_conv_w[l], rep(gdn_a_log[l]), rep(gdn_dt_bias[l]),
                          gdn_norm_g[l].reshape(1, HEAD_DIM), B, S)

        xkv = kv[:, :2 * NSA_KV_WIDTH].reshape(B, S, 2, G, HEAD_DIM).transpose(2, 0, 3, 1, 4)
        xkv = xkv.reshape(2, B * G, NC, CMP_STRIDE * HEAD_DIM)
        cmp_kv = _cmp_call(xkv, cmp_pe[l].reshape(2, 1, CMP_BLOCK * HEAD_DIM), cmp_w1[l].astype(BF16),
                           cmp_b1[l].reshape(2, 1, CMP_HIDDEN), cmp_w2[l].astype(BF16),
                           cmp_b2[l].reshape(2, 1, HEAD_DIM), nsa_k_norm_g[l, 0].reshape(1, HEAD_DIM))
        cmp_kv = cmp_kv.reshape(2, B, G, NC, HEAD_DIM)
        y_nsa = _nsa_call(nq, kv, cmp_kv, misc, nsa_q_norm_g[l].reshape(1, HEAD_DIM), nsa_k_norm_g[l], B, S)

        x1, h2, gates = _out_proj_call(y_gdn, y_nsa, x2, g1, sc2, sh2, norm2_g[l].reshape(1, D),
                                       w_out[l].astype(BF16), rwt, rb, S)
        x2 = _moe_call(h2, gates, x1, g2, exp_w_gate[l].astype(BF16), exp_w_up[l].astype(BF16),
                       exp_w_down[l].astype(BF16), S)
    return x2.reshape(B, S, D)
```

```python
import functools

import numpy as np
import jax
import jax.numpy as jnp
from jax import lax
from jax.experimental import pallas as pl
from jax.experimental.pallas import tpu as pltpu

F32 = jnp.float32
BF16 = jnp.bfloat16

HEAD_DIM = 64
GDN_HEADS = 8
GDN_WIDTH = GDN_HEADS * HEAD_DIM
GDN_CONV = 4
GDN_CHUNK = 64
NSA_HEADS = 8
NSA_KV_HEADS = 2
NSA_GROUP = NSA_HEADS // NSA_KV_HEADS
NSA_WIDTH = NSA_HEADS * HEAD_DIM
NSA_KV_WIDTH = NSA_KV_HEADS * HEAD_DIM
CMP_BLOCK = 32
CMP_STRIDE = 16
CMP_HIDDEN = 256
SLC_BLOCK = 64
SLC_TOPN = 8
WINDOW = 512
N_EXPERTS = 16
EXPERTS_PER_GROUP = 4
EXPERT_FF = 512
RMS_EPS = 1e-6
NEG_BIG = -1e30
SEL_BIG = 1e9

LANES = 128
VMEM_LIMIT_BYTES = 56 * 1024 * 1024

C_QKV = 0
C_Z = 3 * GDN_WIDTH
C_NQ = C_Z + GDN_WIDTH
C_KV = C_NQ + NSA_WIDTH
C_AEXP = C_KV + 6 * NSA_KV_WIDTH
C_BEXP = C_AEXP + GDN_WIDTH
C_MISC = C_BEXP + GDN_WIDTH
IN_PAD_COLS = C_MISC + LANES
MISC_GATE0 = 16


def _cparams(*sem):
    return pltpu.CompilerParams(dimension_semantics=sem, vmem_limit_bytes=VMEM_LIMIT_BYTES)


def _dot(a, b):
    return jnp.dot(a.astype(BF16), b.astype(BF16), preferred_element_type=F32)


def _dot_nt(a, b):
    return lax.dot_general(a.astype(BF16), b.astype(BF16), (((1,), (1,)), ((), ())),
                           preferred_element_type=F32)


def _dot_tn(a, b):
    return lax.dot_general(a.astype(BF16), b.astype(BF16), (((0,), (0,)), ((), ())),
                           preferred_element_type=F32)


def _split2(x):
    hi = x.astype(BF16)
    lo = (x - hi.astype(F32)).astype(BF16)
    return hi, lo


def _split3(x):
    hi = x.astype(BF16)
    r = x - hi.astype(F32)
    mid = r.astype(BF16)
    lo = (r - mid.astype(F32)).astype(BF16)
    return hi, mid, lo


def _dot_hi(a, b):
    ah, al = _split2(a)
    bh, bl = _split2(b)
    return (jnp.dot(ah, bh, preferred_element_type=F32) + jnp.dot(al, bh, preferred_element_type=F32)
            + jnp.dot(ah, bl, preferred_element_type=F32))


def _dot_nt_hi(a, b):
    ah, al = _split2(a)
    bh, bl = _split2(b)
    dn = (((1,), (1,)), ((), ()))
    return (lax.dot_general(ah, bh, dn, preferred_element_type=F32)
            + lax.dot_general(al, bh, dn, preferred_element_type=F32)
            + lax.dot_general(ah, bl, dn, preferred_element_type=F32))


def _dot_exact_rhs(a, b_bf16):
    hi, mid, lo = _split3(a)
    return (jnp.dot(hi, b_bf16, preferred_element_type=F32) + jnp.dot(mid, b_bf16, preferred_element_type=F32)
            + jnp.dot(lo, b_bf16, preferred_element_type=F32))


def _rms(x, g):
    return x * lax.rsqrt(jnp.mean(x * x, axis=-1, keepdims=True) + RMS_EPS) * g


def _sigmoid(x):
    return 1.0 / (1.0 + jnp.exp(-x))


def _silu(x):
    return x * _sigmoid(x)


def _ada_kernel(c_ref, w_ref, b_ref, o_ref):
    o_ref[...] = _dot_hi(_silu(c_ref[...]), w_ref[...]) + b_ref[...]


def _ada_call(c, ada_w, ada_b):
    L, D, N = ada_w.shape
    B = c.shape[0]
    tn = 1024
    return pl.pallas_call(
        _ada_kernel,
        grid=(L, N // tn),
        in_specs=[pl.BlockSpec((B, D), lambda l, j: (0, 0)),
                  pl.BlockSpec((None, D, tn), lambda l, j: (l, 0, j)),
                  pl.BlockSpec((None, 1, tn), lambda l, j: (l, 0, j))],
        out_specs=pl.BlockSpec((None, B, tn), lambda l, j: (l, 0, j)),
        out_shape=jax.ShapeDtypeStruct((L, B, N), F32),
        compiler_params=_cparams("parallel", "parallel"),
        name="ada_mod",
    )(c, ada_w, ada_b.reshape(L, 1, N))


IN_SEGS = ((C_QKV, 3 * GDN_WIDTH), (C_Z, GDN_WIDTH), (C_NQ, NSA_WIDTH), (C_KV, 6 * NSA_KV_WIDTH),
           (C_AEXP, GDN_WIDTH), (C_BEXP, GDN_WIDTH), (C_MISC, LANES))


def _in_proj_kernel(x_ref, sc_ref, sh_ref, g_ref, w_ref, *o_refs):
    h = _rms(x_ref[...], g_ref[...]) * (1.0 + sc_ref[...]) + sh_ref[...]
    hb = h.astype(BF16)
    for (c0, n), o_ref in zip(IN_SEGS, o_refs):
        o_ref[...] = jnp.dot(hb, w_ref[:, c0:c0 + n], preferred_element_type=F32)


def _in_proj_call(x2, sc, sh, g, w_pad, S):
    T, D = x2.shape
    tm = 256
    per_b = S // tm
    row = lambda i: (i, 0)
    bat = lambda i: (i // per_b, 0, 0)
    return pl.pallas_call(
        _in_proj_kernel,
        grid=(T // tm,),
        in_specs=[pl.BlockSpec((tm, D), row),
                  pl.BlockSpec((None, 1, D), bat),
                  pl.BlockSpec((None, 1, D), bat),
                  pl.BlockSpec((1, D), lambda i: (0, 0)),
                  pl.BlockSpec((D, IN_PAD_COLS), lambda i: (0, 0))],
        out_specs=[pl.BlockSpec((tm, n), row) for _, n in IN_SEGS],
        out_shape=[jax.ShapeDtypeStruct((T, n), F32) for _, n in IN_SEGS],
        compiler_params=_cparams("parallel"),
        name="in_proj",
    )(x2, sc, sh, g, w_pad)


def _pad_w_in(w_in):
    D = w_in.shape[0]
    W = GDN_WIDTH
    o = 0
    gq, gk, gv, gz = (w_in[:, o + i * W:o + (i + 1) * W] for i in range(4))
    o += 4 * W
    ga = w_in[:, o:o + GDN_HEADS]
    gb = w_in[:, o + GDN_HEADS:o + 2 * GDN_HEADS]
    o += 2 * GDN_HEADS
    nq = w_in[:, o:o + NSA_WIDTH]
    o += NSA_WIDTH
    kv = w_in[:, o:o + 6 * NSA_KV_WIDTH]
    o += 6 * NSA_KV_WIDTH
    gates = w_in[:, o:o + 3 * NSA_HEADS]
    misc = jnp.zeros((D, LANES), w_in.dtype)
    misc = misc.at[:, 0:GDN_HEADS].set(ga).at[:, MISC_GATE0:MISC_GATE0 + 3 * NSA_HEADS].set(gates)
    cols = [gq, gk, gv, gz, nq, kv, jnp.repeat(ga, HEAD_DIM, axis=1), jnp.repeat(gb, HEAD_DIM, axis=1), misc]
    return jnp.concatenate(cols, axis=1).astype(BF16)


GDN_HPB = 2
GDN_NB = 8


def _gdn_kernel(q_ref, k_ref, v_ref, z_ref, a_ref, b_ref, ar_ref, cwq_ref, cwk_ref, cwv_ref,
                alog_ref, dtb_ref, alogr_ref, dtbr_ref, ng_ref, o_ref,
                qs, ks, kbs, xs, gcs, grs, ps, qq, qks, os_, sst, *, S):
    C = GDN_CHUNK
    N = S // C
    W = GDN_HPB * HEAD_DIM
    row = lax.broadcasted_iota(jnp.int32, (S, W), 0)
    lane = lax.broadcasted_iota(jnp.int32, (S, W), 1)

    def conv_silu(x, w):
        y = x * w[GDN_CONV - 1:GDN_CONV, :]
        for s in range(1, GDN_CONV):
            xs = jnp.where(row >= s, pltpu.roll(x, s, axis=0), 0.0)
            y = y + xs * w[GDN_CONV - 1 - s:GDN_CONV - s, :]
        return _silu(y)

    def head_l2(x):
        sq = x * x
        parts = [jnp.sum(sq[:, h * HEAD_DIM:(h + 1) * HEAD_DIM], axis=-1, keepdims=True) for h in range(GDN_HPB)]
        ss = jnp.where(lane < HEAD_DIM, parts[0], parts[1])
        return x * lax.rsqrt(ss + RMS_EPS)

    q = head_l2(conv_silu(q_ref[...], cwq_ref[...])) * (HEAD_DIM ** -0.5)
    k = head_l2(conv_silu(k_ref[...], cwk_ref[...]))
    v = conv_silu(v_ref[...], cwv_ref[...])
    beta = _sigmoid(b_ref[...])

    def log_decay(a, alog, dtb):
        xx = a + dtb
        sp = jnp.maximum(xx, 0.0) + jnp.log(1.0 + jnp.exp(-jnp.abs(xx)))
        return -jnp.exp(alog) * sp

    gc = log_decay(a_ref[...], alog_ref[...], dtb_ref[...])
    pos = row % C
    for s in (1, 2, 4, 8, 16, 32):
        gc = gc + jnp.where(pos >= s, pltpu.roll(gc, s, axis=0), 0.0)
    gr = log_decay(ar_ref[...], alogr_ref[...], dtbr_ref[...])
    lpos = lax.broadcasted_iota(jnp.int32, (N, W), 1) % C
    for s in (1, 2, 4, 8, 16, 32):
        gr = gr + jnp.where(lpos >= s, pltpu.roll(gr, s, axis=1), 0.0)

    kb = k * beta
    vb = v * beta
    kbe = kb * jnp.exp(gc)
    for h in range(GDN_HPB):
        sl = slice(h * HEAD_DIM, (h + 1) * HEAD_DIM)
        qs[h] = q[:, sl]
        ks[h] = k[:, sl]
        kbs[h] = kb[:, sl]
        gcs[h] = gc[:, sl]
        for n in range(N):
            grs[h, n] = jnp.broadcast_to(gr[n:n + 1, sl], (8, HEAD_DIM))
        xs[h] = jnp.concatenate([kbe[:, sl], vb[:, sl]], axis=1)

    ci = lax.broadcasted_iota(jnp.int32, (C, C), 0)
    cj = lax.broadcasted_iota(jnp.int32, (C, C), 1)

    NB = GDN_NB
    ci3 = lax.broadcasted_iota(jnp.int32, (1, C, C), 1)
    cj3 = lax.broadcasted_iota(jnp.int32, (1, C, C), 2)

    def bmm(a, b):
        return lax.dot_general(a.astype(BF16), b.astype(BF16), (((2,), (1,)), ((0,), (0,))),
                               preferred_element_type=F32)

    def bmm_nt(a, b):
        return lax.dot_general(a.astype(BF16), b.astype(BF16), (((2,), (2,)), ((0,), (0,))),
                               preferred_element_type=F32)

    def intra(nb, carry):
        rows = pl.ds(pl.multiple_of(nb * (NB * C), NB * C), NB * C)
        for h in range(GDN_HPB):
            kk = ks[h, rows, :].reshape(NB, C, HEAD_DIM)
            kbq = jnp.concatenate([kbs[h, rows, :].reshape(NB, C, HEAD_DIM),
                                   qs[h, rows, :].reshape(NB, C, HEAD_DIM)], axis=1)
            a2 = bmm_nt(kbq, kk)
            gcol = gcs[h, rows, :].reshape(NB, C, HEAD_DIM)
            grow = grs[h, pl.ds(nb * NB, NB), 0:1, :]
            decay = jnp.exp(jnp.where(ci3 >= cj3, gcol - grow, NEG_BIG))
            m = -jnp.where(ci3 > cj3, a2[:, :C] * decay, 0.0)
            qks[h, rows, :] = (a2[:, C:] * decay).reshape(NB * C, C)
            x = xs[h, rows, :].reshape(NB, C, 2 * HEAD_DIM)
            for it in range(6):
                mb = m.astype(BF16)
                x = x + bmm(mb, x)
                if it < 5:
                    m = bmm(mb, mb)
            xs[h, rows, :] = x.reshape(NB * C, 2 * HEAD_DIM)
            kd = kk * jnp.exp(gcol[:, C - 1:C, :] - gcol)
            pq = lax.dot_general(kd.astype(BF16), x.astype(BF16), (((1,), (1,)), ((0,), (0,))),
                                 preferred_element_type=F32)
            ps[h, rows, :] = pq[:, :, :HEAD_DIM].reshape(NB * C, HEAD_DIM)
            qq[h, rows, :] = pq[:, :, HEAD_DIM:].reshape(NB * C, HEAD_DIM)
        return carry

    lax.fori_loop(0, N // NB, intra, 0)

    def inter(n, states):
        sl = pl.ds(pl.multiple_of(n * C, C), C)
        new_states = []
        for h in range(GDN_HPB):
            st = states[h]
            sst[h, n] = st
            glast = gcs[h, pl.ds(n * C + C - 1, 1), :]
            new_states.append(st * jnp.exp(glast) - _dot_hi(ps[h, sl, :], st) + qq[h, sl, :])
        return tuple(new_states)

    lax.fori_loop(0, N, inter, tuple(jnp.zeros((HEAD_DIM, HEAD_DIM), F32) for _ in range(GDN_HPB)))

    def outputs(nb, carry):
        rows = pl.ds(pl.multiple_of(nb * (NB * C), NB * C), NB * C)
        for h in range(GDN_HPB):
            st = sst[h, pl.ds(nb * NB, NB)]
            x = xs[h, rows, :].reshape(NB, C, 2 * HEAD_DIM)
            gcol = gcs[h, rows, :].reshape(NB, C, HEAD_DIM)
            qd = qs[h, rows, :].reshape(NB, C, HEAD_DIM) * jnp.exp(gcol)
            t1 = bmm(jnp.concatenate([x[:, :, :HEAD_DIM], qd], axis=1), st)
            vn = x[:, :, HEAD_DIM:] - t1[:, :C]
            o = t1[:, C:] + bmm(qks[h, rows, :].reshape(NB, C, C), vn)
            os_[h, rows, :] = o.reshape(NB * C, HEAD_DIM)
        return carry

    lax.fori_loop(0, N // NB, outputs, 0)

    ng = ng_ref[...]
    outs = [_rms(os_[h], ng) for h in range(GDN_HPB)]
    o_ref[...] = jnp.concatenate(outs, axis=1) * _silu(z_ref[...])


def _gdn_call(qkv, z, aexp, bexp, a_rows, conv_w, alog_exp, dtb_exp, norm_g, B, S):
    T = B * S
    W = GDN_HPB * HEAD_DIM
    P = GDN_HEADS // GDN_HPB
    N = S // GDN_CHUNK
    col = lambda off: (lambda b, p: (b, off + p))
    par = lambda off: (lambda b, p: (0, off + p))
    hs = lambda n: pltpu.VMEM((GDN_HPB, S, n), F32)
    return pl.pallas_call(
        functools.partial(_gdn_kernel, S=S),
        grid=(B, P),
        in_specs=[pl.BlockSpec((S, W), col(0)), pl.BlockSpec((S, W), col(P)), pl.BlockSpec((S, W), col(2 * P)),
                  pl.BlockSpec((S, W), col(0)), pl.BlockSpec((S, W), col(0)), pl.BlockSpec((S, W), col(0)),
                  pl.BlockSpec((None, None, N, W), lambda b, p: (b, p, 0, 0)),
                  pl.BlockSpec((GDN_CONV, W), par(0)), pl.BlockSpec((GDN_CONV, W), par(P)),
                  pl.BlockSpec((GDN_CONV, W), par(2 * P)),
                  pl.BlockSpec((1, W), par(0)), pl.BlockSpec((1, W), par(0)),
                  pl.BlockSpec((1, W), par(0)), pl.BlockSpec((1, W), par(0)),
                  pl.BlockSpec((1, HEAD_DIM), lambda b, p: (0, 0))],
        out_specs=pl.BlockSpec((S, W), col(0)),
        out_shape=jax.ShapeDtypeStruct((T, GDN_WIDTH), F32),
        scratch_shapes=[hs(HEAD_DIM), hs(HEAD_DIM), hs(HEAD_DIM), hs(2 * HEAD_DIM), hs(HEAD_DIM),
                        pltpu.VMEM((GDN_HPB, N, 8, HEAD_DIM), F32),
                        hs(HEAD_DIM), hs(HEAD_DIM), hs(HEAD_DIM), hs(HEAD_DIM),
                        pltpu.VMEM((GDN_HPB, N, HEAD_DIM, HEAD_DIM), F32)],
        compiler_params=_cparams("parallel", "parallel"),
        name="gdn",
    )(qkv, qkv, qkv, z, aexp, bexp, a_rows, conv_w, conv_w, conv_w,
      alog_exp, dtb_exp, alog_exp, dtb_exp, norm_g)


def _cmp_kernel(x_ref, pe_ref, w1_ref, b1_ref, w2_ref, b2_ref, kg_ref, o_ref):
    half = (CMP_BLOCK // 2) * HEAD_DIM
    x = x_ref[...]
    nrows = x.shape[0]
    pe = pe_ref[...]
    p = _dot(x + pe[:, :half], w1_ref[:half, :])
    q = _dot(x + pe[:, half:], w1_ref[half:, :])
    hid = p + pltpu.roll(q, nrows - 1, axis=0) + b1_ref[...]
    out = _dot(_silu(hid), w2_ref[...]) + b2_ref[...]
    is_key = pl.program_id(0) == 0
    o_ref[...] = jnp.where(is_key, _rms(out, kg_ref[...]), out)


def _cmp_call(xkv, pe, w1, b1, w2, b2, kg):
    _, BG, NC, half = xkv.shape
    sel = lambda j, i: (j, 0, 0)
    return pl.pallas_call(
        _cmp_kernel,
        grid=(2, BG),
        in_specs=[pl.BlockSpec((None, None, NC, half), lambda j, i: (j, i, 0, 0)),
                  pl.BlockSpec((None, 1, 2 * half), sel),
                  pl.BlockSpec((None, 2 * half, CMP_HIDDEN), sel),
                  pl.BlockSpec((None, 1, CMP_HIDDEN), sel),
                  pl.BlockSpec((None, CMP_HIDDEN, HEAD_DIM), sel),
                  pl.BlockSpec((None, 1, HEAD_DIM), sel),
                  pl.BlockSpec((1, HEAD_DIM), lambda j, i: (0, 0))],
        out_specs=pl.BlockSpec((None, None, NC, HEAD_DIM), lambda j, i: (j, i, 0, 0)),
        out_shape=jax.ShapeDtypeStruct((2, BG, NC, HEAD_DIM), F32),
        compiler_params=_cparams("parallel", "parallel"),
        name="nsa_compress",
    )(xkv, pe, w1, b1, w2, b2, kg)


NSA_TQ = 128
NSA_TK = 256


def _nsa_kernel(q_ref, kv_ref, cmp_ref, misc_ref, qg_ref, kg_ref, ov_ref, e_ref, o_ref,
                ks_s, vs_s, kw_s, vw_s, *, S):
    tq, tk = NSA_TQ, NSA_TK
    G, R = NSA_KV_HEADS, NSA_GROUP
    M = R * tq
    NC = S // CMP_STRIDE
    NS = S // SLC_BLOCK
    n_top = min(SLC_TOPN, NS)
    qi = pl.program_id(1)
    t0 = qi * tq

    @pl.when(qi == 0)
    def _():
        for g in range(G):
            c = 2 * NSA_KV_WIDTH + g * HEAD_DIM
            ks_s[g] = _rms(kv_ref[:, c:c + HEAD_DIM], kg_ref[1:2, :]).astype(BF16)
            c += NSA_KV_WIDTH
            vs_s[g] = kv_ref[:, c:c + HEAD_DIM].astype(BF16)
            c += NSA_KV_WIDTH
            kw_s[g] = _rms(kv_ref[:, c:c + HEAD_DIM], kg_ref[2:3, :]).astype(BF16)
            c += NSA_KV_WIDTH
            vw_s[g] = kv_ref[:, c:c + HEAD_DIM].astype(BF16)

    rowm = lax.broadcasted_iota(jnp.int32, (M, 1), 0)
    t_col = (t0 + rowm % tq).astype(F32)
    gate_sig = _sigmoid(misc_ref[:, MISC_GATE0:MISC_GATE0 + 3 * NSA_HEADS])

    def flash(Q, slope_col, k_s, v_s, g, kt_lo, kt_hi, mask_fn):
        def body(kt, carry):
            m, l, acc = carry
            k0 = pl.multiple_of(kt * tk, tk)
            kblk = k_s[g, pl.ds(k0, tk), :]
            s_pos = (k0 + lax.broadcasted_iota(jnp.int32, (1, tk), 1)).astype(F32)
            dist = t_col - s_pos
            s = lax.dot_general(Q, kblk, (((1,), (1,)), ((), ())), preferred_element_type=F32) - slope_col * dist
            mask = mask_fn(kt, dist)
            s = jnp.where(mask, s, NEG_BIG)
            m_new = jnp.maximum(m, jnp.max(s, axis=1, keepdims=True))
            alpha = jnp.exp(m - m_new)
            p = jnp.where(mask, jnp.exp(s - m_new), 0.0)
            l = alpha * l + jnp.sum(p, axis=1, keepdims=True)
            acc = alpha * acc + jnp.dot(p.astype(BF16), v_s[g, pl.ds(k0, tk), :], preferred_element_type=F32)
            return m_new, l, acc

        init = (jnp.full((M, 1), NEG_BIG, F32), jnp.zeros((M, 1), F32), jnp.zeros((M, HEAD_DIM), F32))
        _, l, acc = lax.fori_loop(kt_lo, kt_hi, body, init)
        return acc / jnp.where(l > 0.0, l, 1.0)

    kt_hi = (t0 + tq + tk - 1) // tk
    kt_lo_win = jnp.maximum(t0 - (WINDOW - 1), 0) // tk

    for g in range(G):
        qn = [_rms(q_ref[:, (g * R + r) * HEAD_DIM:(g * R + r + 1) * HEAD_DIM], qg_ref[...]) * (HEAD_DIM ** -0.5)
              for r in range(R)]
        Q = jnp.concatenate(qn, axis=0).astype(BF16)
        slope_col = jnp.zeros((M, 1), F32)
        for r in range(R):
            slope_col = jnp.where(rowm // tq == r, 2.0 ** (-8.0 * (g * R + r + 1) / NSA_HEADS), slope_col)

        kc = cmp_ref[0, g]
        vc = cmp_ref[1, g]
        n_row = lax.broadcasted_iota(jnp.int32, (1, NC), 1)
        cdist = t_col - (n_row * CMP_STRIDE + (CMP_BLOCK - 1)).astype(F32)
        cmask = (cdist >= 0.0) & (n_row < NC - 1)
        s = jnp.where(cmask, _dot_nt(Q, kc) - slope_col * cdist, NEG_BIG)
        e = jnp.where(cmask, jnp.exp(s - jnp.max(s, axis=1, keepdims=True)), 0.0)
        l = jnp.sum(e, axis=1, keepdims=True)
        p_cmp = e / jnp.where(l > 0.0, l, 1.0)
        o_cmp = _dot(p_cmp, vc)

        psum = p_cmp[0:tq]
        for r in range(1, R):
            psum = psum + p_cmp[r * tq:(r + 1) * tq]
        imp = _dot_exact_rhs(psum, ov_ref[...])
        blk = lax.broadcasted_iota(jnp.int32, (tq, NS), 1)
        cur = (t0 + lax.broadcasted_iota(jnp.int32, (tq, NS), 0)) // SLC_BLOCK
        valid = blk <= cur
        forced = (blk == 0) | (blk == cur) | (blk == cur - 1)
        score = jnp.where(forced, SEL_BIG, jnp.where(valid, imp, -SEL_BIG))
        rank = jnp.zeros((tq, NS), F32)
        for j in range(NS):
            cj = score[:, j:j + 1]
            beats = (cj > score) | ((cj == score) & (blk > j))
            rank = rank + jnp.where(beats, 1.0, 0.0)
        sel = jnp.where((rank < n_top) & valid, 1.0, 0.0).astype(BF16)

        def slc_mask(kt, dist, sel=sel):
            em = jnp.dot(sel, e_ref[kt], preferred_element_type=F32)
            em = jnp.concatenate([em] * R, axis=0)
            return (dist >= 0.0) & (em > 0.5)

        def win_mask(kt, dist):
            return (dist >= 0.0) & (dist < float(WINDOW))

        o_slc = flash(Q, slope_col, ks_s, vs_s, g, 0, kt_hi, slc_mask)
        o_win = flash(Q, slope_col, kw_s, vw_s, g, kt_lo_win, kt_hi, win_mask)

        for r in range(R):
            hh = g * R + r
            rs = slice(r * tq, (r + 1) * tq)
            o = (gate_sig[:, hh:hh + 1] * o_cmp[rs]
                 + gate_sig[:, NSA_HEADS + hh:NSA_HEADS + hh + 1] * o_slc[rs]
                 + gate_sig[:, 2 * NSA_HEADS + hh:2 * NSA_HEADS + hh + 1] * o_win[rs])
            o_ref[:, hh * HEAD_DIM:(hh + 1) * HEAD_DIM] = o


def _nsa_consts(S):
    NC = S // CMP_STRIDE
    NS = S // SLC_BLOCK
    cs = np.arange(NC)[:, None] * CMP_STRIDE
    ss = np.arange(NS)[None, :] * SLC_BLOCK
    ov = np.clip(np.minimum(cs + CMP_BLOCK, ss + SLC_BLOCK) - np.maximum(cs, ss), 0, None) / CMP_BLOCK
    ov[NC - 1:] = 0.0
    pos = np.arange(S).reshape(S // NSA_TK, 1, NSA_TK)
    e = (pos // SLC_BLOCK == np.arange(NS)[None, :, None]).astype(np.float32)
    return jnp.asarray(ov, BF16), jnp.asarray(e, BF16)


def _nsa_call(nq, kv, cmp_kv, misc, q_norm_g, k_norm_g, B, S):
    T = B * S
    tq = NSA_TQ
    nqt = S // tq
    NC = S // CMP_STRIDE
    NS = S // SLC_BLOCK
    ov, e = _nsa_consts(S)
    row = lambda b, i: (b * nqt + i, 0)
    cst = lambda b, i: (0, 0)
    return pl.pallas_call(
        functools.partial(_nsa_kernel, S=S),
        grid=(B, nqt),
        in_specs=[pl.BlockSpec((tq, NSA_WIDTH), row),
                  pl.BlockSpec((S, 6 * NSA_KV_WIDTH), lambda b, i: (b, 0)),
                  pl.BlockSpec((2, None, NSA_KV_HEADS, NC, HEAD_DIM), lambda b, i: (0, b, 0, 0, 0)),
                  pl.BlockSpec((tq, LANES), row),
                  pl.BlockSpec((1, HEAD_DIM), cst),
                  pl.BlockSpec((3, HEAD_DIM), cst),
                  pl.BlockSpec((NC, NS), cst),
                  pl.BlockSpec((S // NSA_TK, NS, NSA_TK), lambda b, i: (0, 0, 0))],
        out_specs=pl.BlockSpec((tq, NSA_WIDTH), row),
        out_shape=jax.ShapeDtypeStruct((T, NSA_WIDTH), F32),
        scratch_shapes=[pltpu.VMEM((NSA_KV_HEADS, S, HEAD_DIM), BF16) for _ in range(4)],
        compiler_params=_cparams("parallel", "arbitrary"),
        name="nsa_attn",
    )(nq, kv, cmp_kv, misc, q_norm_g, k_norm_g, ov, e)


def _out_proj_kernel(yg_ref, yn_ref, x_ref, g1_ref, sc_ref, sh_ref, ng_ref, wo_ref, rwt_ref, rb_ref,
                     x1_ref, h2_ref, gate_ref):
    mix = _dot(yg_ref[...], wo_ref[:GDN_WIDTH, :]) + _dot(yn_ref[...], wo_ref[GDN_WIDTH:, :])
    x1 = x_ref[...] + g1_ref[...] * mix
    x1_ref[...] = x1
    h2 = _rms(x1, ng_ref[...]) * (1.0 + sc_ref[...]) + sh_ref[...]
    h2_ref[...] = h2.astype(BF16)

    tm = h2.shape[0]
    logits = _dot_nt_hi(rwt_ref[...], h2)
    score = _sigmoid(logits)
    biased = score + rb_ref[...]
    b = [biased[e:e + 1, :] for e in range(N_EXPERTS)]
    n_groups = N_EXPERTS // EXPERTS_PER_GROUP
    gscore = []
    for gi in range(n_groups):
        vals = b[gi * EXPERTS_PER_GROUP:(gi + 1) * EXPERTS_PER_GROUP]
        best = None
        for i in range(EXPERTS_PER_GROUP):
            for j in range(i + 1, EXPERTS_PER_GROUP):
                pair = vals[i] + vals[j]
                best = pair if best is None else jnp.maximum(best, pair)
        gscore.append(best)
    gbest = jnp.zeros((1, tm), jnp.int32)
    top = gscore[0]
    for gi in range(1, n_groups):
        better = gscore[gi] > top
        gbest = jnp.where(better, gi, gbest)
        top = jnp.where(better, gscore[gi], top)
    erow = lax.broadcasted_iota(jnp.int32, (N_EXPERTS, tm), 0)
    gates = jnp.zeros((N_EXPERTS, tm), F32)
    for e in range(N_EXPERTS):
        gi = e // EXPERTS_PER_GROUP
        rank = jnp.zeros((1, tm), F32)
        for e2 in range(gi * EXPERTS_PER_GROUP, (gi + 1) * EXPERTS_PER_GROUP):
            if e2 == e:
                continue
            beats = (b[e2] > b[e]) | ((b[e2] == b[e]) & (e2 < e))
            rank = rank + jnp.where(beats, 1.0, 0.0)
        chosen = (gbest == gi) & (rank < 2.0)
        ge = jnp.where(chosen, score[e:e + 1, :], 0.0)
        gates = gates + jnp.where(erow == e, ge, 0.0)
    gates = gates / jnp.sum(gates, axis=0, keepdims=True)
    ident = (lax.broadcasted_iota(jnp.int32, (N_EXPERTS, LANES), 0)
             == lax.broadcasted_iota(jnp.int32, (N_EXPERTS, LANES), 1))
    ident = jnp.where(ident, 1.0, 0.0).astype(BF16)
    hi, mid, lo = _split3(gates)
    dn = (((0,), (0,)), ((), ()))
    gate_ref[...] = (lax.dot_general(hi, ident, dn, preferred_element_type=F32)
                     + lax.dot_general(mid, ident, dn, preferred_element_type=F32)
                     + lax.dot_general(lo, ident, dn, preferred_element_type=F32))


def _out_proj_call(yg, yn, x2, g1, sc2, sh2, ng, wo, rwt, rb, S):
    T, D = x2.shape
    tm = 256
    per_b = S // tm
    row = lambda i: (i, 0)
    bat = lambda i: (i // per_b, 0, 0)
    cst = lambda i: (0, 0)
    return pl.pallas_call(
        _out_proj_kernel,
        grid=(T // tm,),
        in_specs=[pl.BlockSpec((tm, GDN_WIDTH), row), pl.BlockSpec((tm, NSA_WIDTH), row), pl.BlockSpec((tm, D), row),
                  pl.BlockSpec((None, 1, D), bat), pl.BlockSpec((None, 1, D), bat), pl.BlockSpec((None, 1, D), bat),
                  pl.BlockSpec((1, D), cst), pl.BlockSpec((D, D), cst),
                  pl.BlockSpec((N_EXPERTS, D), cst), pl.BlockSpec((N_EXPERTS, 1), cst)],
        out_specs=[pl.BlockSpec((tm, D), row), pl.BlockSpec((tm, D), row), pl.BlockSpec((tm, LANES), row)],
        out_shape=[jax.ShapeDtypeStruct((T, D), F32), jax.ShapeDtypeStruct((T, D), BF16),
                   jax.ShapeDtypeStruct((T, LANES), F32)],
        compiler_params=_cparams("parallel"),
        name="out_proj_router",
    )(yg, yn, x2, g1, sc2, sh2, ng, wo, rwt, rb)


def _moe_kernel(h_ref, gate_ref, x1_ref, g2_ref, wg_ref, wu_ref, wd_ref, o_ref):
    e = pl.program_id(1)
    h = h_ref[...]
    lane = lax.broadcasted_iota(jnp.int32, gate_ref.shape, 1)
    gcol = jnp.sum(jnp.where(lane == e, gate_ref[...], 0.0), axis=1, keepdims=True)
    act = _silu(jnp.dot(h, wg_ref[...], preferred_element_type=F32)) * jnp.dot(h, wu_ref[...],
                                                                              preferred_element_type=F32)
    y = _dot(act * gcol, wd_ref[...])

    @pl.when(e == 0)
    def _():
        o_ref[...] = y

    @pl.when(e > 0)
    def _():
        o_ref[...] += y

    @pl.when(e == N_EXPERTS - 1)
    def _():
        o_ref[...] = x1_ref[...] + g2_ref[...] * o_ref[...]


def _moe_call(h2, gates, x1, g2, wg, wu, wd, S):
    T, D = x1.shape
    tm = 1024
    per_b = S // tm
    row = lambda i, e: (i, 0)
    wsel = lambda i, e: (e, 0, 0)
    return pl.pallas_call(
        _moe_kernel,
        grid=(T // tm, N_EXPERTS),
        in_specs=[pl.BlockSpec((tm, D), row), pl.BlockSpec((tm, LANES), row), pl.BlockSpec((tm, D), row),
                  pl.BlockSpec((None, 1, D), lambda i, e: (i // per_b, 0, 0)),
                  pl.BlockSpec((None, D, EXPERT_FF), wsel), pl.BlockSpec((None, D, EXPERT_FF), wsel),
                  pl.BlockSpec((None, EXPERT_FF, D), wsel)],
        out_specs=pl.BlockSpec((tm, D), row),
        out_shape=jax.ShapeDtypeStruct((T, D), F32),
        compiler_params=_cparams("parallel", "arbitrary"),
        name="moe",
    )(h2, gates, x1, g2, wg, wu, wd)


def kernel(x, c, ada_w, ada_b, norm1_g, norm2_g, w_in, gdn_conv_w, gdn_a_log, gdn_dt_bias, gdn_norm_g,
           nsa_q_norm_g, nsa_k_norm_g, cmp_pe, cmp_w1, cmp_b1, cmp_w2, cmp_b2, w_out, router_w, router_bias,
           exp_w_gate, exp_w_up, exp_w_down):
    B, S, D = x.shape
    L = ada_w.shape[0]
    T = B * S
    G = NSA_KV_HEADS
    NC = S // CMP_STRIDE
    N = S // GDN_CHUNK
    P = GDN_HEADS // GDN_HPB
    W = GDN_HPB * HEAD_DIM

    mod = _ada_call(c, ada_w, ada_b)
    rwt = router_w.T
    rb = router_bias.reshape(N_EXPERTS, 1)
    x2 = x.reshape(T, D)
    for l in range(L):
        m6 = mod[l].reshape(B, 6, 1, D)
        sh1, sc1, g1, sh2, sc2, g2 = (m6[:, i] for i in range(6))
        qkv, z, nq, kv, aexp, bexp, misc = _in_proj_call(
            x2, sc1, sh1, norm1_g[l].reshape(1, D), _pad_w_in(w_in[l]), S)

        a_rows = misc[:, :GDN_HEADS].reshape(B, N, GDN_CHUNK, P, GDN_HPB).transpose(0, 3, 1, 4, 2).reshape(B, P, N, W)
        rep = lambda t: jnp.repeat(t, HEAD_DIM).reshape(1, GDN_WIDTH)
        y_gdn = _gdn_call(qkv, z, aexp, bexp, a_rows, gdn_conv_w[l], rep(gdn_a_log[l]), rep(gdn_dt_bias[l]),
                          gdn_norm_g[l].reshape(1, HEAD_DIM), B, S)

        xkv = kv[:, :2 * NSA_KV_WIDTH].reshape(B, S, 2, G, HEAD_DIM).transpose(2, 0, 3, 1, 4)
        xkv = xkv.reshape(2, B * G, NC, CMP_STRIDE * HEAD_DIM)
        cmp_kv = _cmp_call(xkv, cmp_pe[l].reshape(2, 1, CMP_BLOCK * HEAD_DIM), cmp_w1[l].astype(BF16),
                           cmp_b1[l].reshape(2, 1, CMP_HIDDEN), cmp_w2[l].astype(BF16),
                           cmp_b2[l].reshape(2, 1, HEAD_DIM), nsa_k_norm_g[l, 0].reshape(1, HEAD_DIM))
        cmp_kv = cmp_kv.reshape(2, B, G, NC, HEAD_DIM)
        y_nsa = _nsa_call(nq, kv, cmp_kv, misc, nsa_q_norm_g[l].reshape(1, HEAD_DIM), nsa_k_norm_g[l], B, S)

        x1, h2, gates = _out_proj_call(y_gdn, y_nsa, x2, g1, sc2, sh2, norm2_g[l].reshape(1, D),
                                       w_out[l].astype(BF16), rwt, rb, S)
        x2 = _moe_call(h2, gates, x1, g2, exp_w_gate[l].astype(BF16), exp_w_up[l].astype(BF16),
                       exp_w_down[l].astype(BF16), S)
    return x2.reshape(B, S, D)
```

```python
import functools

import numpy as np
import jax
import jax.numpy as jnp
from jax import lax
from jax.experimental import pallas as pl
from jax.experimental.pallas import tpu as pltpu

F32 = jnp.float32
BF16 = jnp.bfloat16

HEAD_DIM = 64
GDN_HEADS = 8
GDN_WIDTH = GDN_HEADS * HEAD_DIM
GDN_CONV = 4
GDN_CHUNK = 64
NSA_HEADS = 8
NSA_KV_HEADS = 2
NSA_GROUP = NSA_HEADS // NSA_KV_HEADS
NSA_WIDTH = NSA_HEADS * HEAD_DIM
NSA_KV_WIDTH = NSA_KV_HEADS * HEAD_DIM
CMP_BLOCK = 32
CMP_STRIDE = 16
CMP_HIDDEN = 256
SLC_BLOCK = 64
SLC_TOPN = 8
WINDOW = 512
N_EXPERTS = 16
EXPERTS_PER_GROUP = 4
EXPERT_FF = 512
RMS_EPS = 1e-6
NEG_BIG = -1e30
SEL_BIG = 1e9

LANES = 128
VMEM_LIMIT_BYTES = 56 * 1024 * 1024

C_QKV = 0
C_Z = 3 * GDN_WIDTH
C_NQ = C_Z + GDN_WIDTH
C_KV = C_NQ + NSA_WIDTH
C_AEXP = C_KV + 6 * NSA_KV_WIDTH
C_BEXP = C_AEXP + GDN_WIDTH
C_MISC = C_BEXP + GDN_WIDTH
IN_PAD_COLS = C_MISC + LANES
MISC_GATE0 = 16


def _cparams(*sem):
    return pltpu.CompilerParams(dimension_semantics=sem, vmem_limit_bytes=VMEM_LIMIT_BYTES)


def _dot(a, b):
    return jnp.dot(a.astype(BF16), b.astype(BF16), preferred_element_type=F32)


def _dot_nt(a, b):
    return lax.dot_general(a.astype(BF16), b.astype(BF16), (((1,), (1,)), ((), ())),
                           preferred_element_type=F32)


def _dot_tn(a, b):
    return lax.dot_general(a.astype(BF16), b.astype(BF16), (((0,), (0,)), ((), ())),
                           preferred_element_type=F32)


def _split2(x):
    hi = x.astype(BF16)
    lo = (x - hi.astype(F32)).astype(BF16)
    return hi, lo


def _split3(x):
    hi = x.astype(BF16)
    r = x - hi.astype(F32)
    mid = r.astype(BF16)
    lo = (r - mid.astype(F32)).astype(BF16)
    return hi, mid, lo


def _dot_hi(a, b):
    ah, al = _split2(a)
    bh, bl = _split2(b)
    return (jnp.dot(ah, bh, preferred_element_type=F32) + jnp.dot(al, bh, preferred_element_type=F32)
            + jnp.dot(ah, bl, preferred_element_type=F32))


def _dot_nt_hi(a, b):
    ah, al = _split2(a)
    bh, bl = _split2(b)
    dn = (((1,), (1,)), ((), ()))
    return (lax.dot_general(ah, bh, dn, preferred_element_type=F32)
            + lax.dot_general(al, bh, dn, preferred_element_type=F32)
            + lax.dot_general(ah, bl, dn, preferred_element_type=F32))


def _dot_exact_rhs(a, b_bf16):
    hi, mid, lo = _split3(a)
    return (jnp.dot(hi, b_bf16, preferred_element_type=F32) + jnp.dot(mid, b_bf16, preferred_element_type=F32)
            + jnp.dot(lo, b_bf16, preferred_element_type=F32))


def _rms(x, g):
    return x * lax.rsqrt(jnp.mean(x * x, axis=-1, keepdims=True) + RMS_EPS) * g


def _sigmoid(x):
    return 0.5 * (jnp.tanh(0.5 * x) + 1.0)


def _silu(x):
    return x * _sigmoid(x)


def _ada_kernel(c_ref, w_ref, b_ref, o_ref):
    o_ref[...] = _dot_hi(_silu(c_ref[...]), w_ref[...]) + b_ref[...]


def _ada_call(c, ada_w, ada_b):
    L, D, N = ada_w.shape
    B = c.shape[0]
    tn = 1024
    return pl.pallas_call(
        _ada_kernel,
        grid=(L, N // tn),
        in_specs=[pl.BlockSpec((B, D), lambda l, j: (0, 0)),
                  pl.BlockSpec((None, D, tn), lambda l, j: (l, 0, j)),
                  pl.BlockSpec((None, 1, tn), lambda l, j: (l, 0, j))],
        out_specs=pl.BlockSpec((None, B, tn), lambda l, j: (l, 0, j)),
        out_shape=jax.ShapeDtypeStruct((L, B, N), F32),
        compiler_params=_cparams("parallel", "parallel"),
        name="ada_mod",
    )(c, ada_w, ada_b.reshape(L, 1, N))


IN_SEGS = ((C_QKV, 3 * GDN_WIDTH), (C_Z, GDN_WIDTH), (C_NQ, NSA_WIDTH), (C_KV, 6 * NSA_KV_WIDTH),
           (C_AEXP, GDN_WIDTH), (C_BEXP, GDN_WIDTH), (C_MISC, LANES))


def _in_proj_kernel(x_ref, sc_ref, sh_ref, g_ref, w_ref, wmt_ref, *o_refs):
    h = _rms(x_ref[...], g_ref[...]) * (1.0 + sc_ref[...]) + sh_ref[...]
    hb = h.astype(BF16)
    for (c0, n), o_ref in zip(IN_SEGS, o_refs[:-1]):
        o_ref[...] = jnp.dot(hb, w_ref[:, c0:c0 + n], preferred_element_type=F32)
    o_refs[-1][...] = lax.dot_general(wmt_ref[...], hb, (((1,), (1,)), ((), ())), preferred_element_type=F32)


def _in_proj_call(x2, sc, sh, g, w_pad, S):
    T, D = x2.shape
    tm = 256
    per_b = S // tm
    row = lambda i: (i, 0)
    bat = lambda i: (i // per_b, 0, 0)
    return pl.pallas_call(
        _in_proj_kernel,
        grid=(T // tm,),
        in_specs=[pl.BlockSpec((tm, D), row),
                  pl.BlockSpec((None, 1, D), bat),
                  pl.BlockSpec((None, 1, D), bat),
                  pl.BlockSpec((1, D), lambda i: (0, 0)),
                  pl.BlockSpec((D, IN_PAD_COLS), lambda i: (0, 0)),
                  pl.BlockSpec((LANES, D), lambda i: (0, 0))],
        out_specs=[pl.BlockSpec((tm, n), row) for _, n in IN_SEGS] + [pl.BlockSpec((LANES, tm), lambda i: (0, i))],
        out_shape=[jax.ShapeDtypeStruct((T, n), F32) for _, n in IN_SEGS] + [jax.ShapeDtypeStruct((LANES, T), F32)],
        compiler_params=_cparams("parallel"),
        name="in_proj",
    )(x2, sc, sh, g, w_pad, w_pad[:, C_MISC:].T)


def _pad_w_in(w_in):
    D = w_in.shape[0]
    W = GDN_WIDTH
    o = 0
    gq, gk, gv, gz = (w_in[:, o + i * W:o + (i + 1) * W] for i in range(4))
    o += 4 * W
    ga = w_in[:, o:o + GDN_HEADS]
    gb = w_in[:, o + GDN_HEADS:o + 2 * GDN_HEADS]
    o += 2 * GDN_HEADS
    nq = w_in[:, o:o + NSA_WIDTH]
    o += NSA_WIDTH
    kv = w_in[:, o:o + 6 * NSA_KV_WIDTH]
    o += 6 * NSA_KV_WIDTH
    gates = w_in[:, o:o + 3 * NSA_HEADS]
    misc = jnp.zeros((D, LANES), w_in.dtype)
    misc = misc.at[:, 0:GDN_HEADS].set(ga).at[:, MISC_GATE0:MISC_GATE0 + 3 * NSA_HEADS].set(gates)
    cols = [gq, gk, gv, gz, nq, kv, jnp.repeat(ga, HEAD_DIM, axis=1), jnp.repeat(gb, HEAD_DIM, axis=1), misc]
    return jnp.concatenate(cols, axis=1).astype(BF16)


GDN_HPB = 2
GDN_NB = 8


def _gdn_kernel(q_ref, k_ref, v_ref, z_ref, a_ref, b_ref, ar_ref, cwq_ref, cwk_ref, cwv_ref,
                alog_ref, dtb_ref, alogr_ref, dtbr_ref, ng_ref, o_ref,
                qs, ks, kbs, xs, gcs, grs, ps, qq, qks, os_, sst, *, S):
    C = GDN_CHUNK
    N = S // C
    W = GDN_HPB * HEAD_DIM
    row = lax.broadcasted_iota(jnp.int32, (S, W), 0)
    row8 = lax.broadcasted_iota(jnp.int32, (8, W), 0)

    def conv_silu(x, w):
        y = x * w[GDN_CONV - 1:GDN_CONV, :]
        for s in range(1, GDN_CONV):
            xs = pltpu.roll(x, s, axis=0)
            xs = jnp.concatenate([jnp.where(row8 >= s, xs[:8], 0.0), xs[8:]], axis=0)
            y = y + xs * w[GDN_CONV - 1 - s:GDN_CONV - s, :]
        return _silu(y)

    bd = jnp.where(lax.broadcasted_iota(jnp.int32, (W, W), 0) // HEAD_DIM
                   == lax.broadcasted_iota(jnp.int32, (W, W), 1) // HEAD_DIM, 1.0, 0.0).astype(BF16)

    def head_sumsq(x):
        hi, lo = _split2(x * x)
        return jnp.dot(hi, bd, preferred_element_type=F32) + jnp.dot(lo, bd, preferred_element_type=F32)

    def head_l2(x):
        return x * lax.rsqrt(head_sumsq(x) + RMS_EPS)

    q = head_l2(conv_silu(q_ref[...], cwq_ref[...])) * (HEAD_DIM ** -0.5)
    k = head_l2(conv_silu(k_ref[...], cwk_ref[...]))
    v = conv_silu(v_ref[...], cwv_ref[...])
    beta = _sigmoid(b_ref[...])

    def log_decay(a, alog, dtb):
        xx = a + dtb
        sp = jnp.maximum(xx, 0.0) + jnp.log(1.0 + jnp.exp(-jnp.abs(xx)))
        return -jnp.exp(alog) * sp

    gc = log_decay(a_ref[...], alog_ref[...], dtb_ref[...])
    pos = row % C
    for s in (1, 2, 4, 8, 16, 32):
        gc = gc + jnp.where(pos >= s, pltpu.roll(gc, s, axis=0), 0.0)
    gr = log_decay(ar_ref[...], alogr_ref[...], dtbr_ref[...])
    lpos = lax.broadcasted_iota(jnp.int32, (N, W), 1) % C
    for s in (1, 2, 4, 8, 16, 32):
        gr = gr + jnp.where(lpos >= s, pltpu.roll(gr, s, axis=1), 0.0)

    kb = k * beta
    vb = v * beta
    kbe = kb * jnp.exp(gc)
    for h in range(GDN_HPB):
        sl = slice(h * HEAD_DIM, (h + 1) * HEAD_DIM)
        qs[h] = q[:, sl]
        ks[h] = k[:, sl]
        kbs[h] = kb[:, sl]
        gcs[h] = gc[:, sl]
        for n in range(N):
            grs[h, n] = jnp.broadcast_to(gr[n:n + 1, sl], (8, HEAD_DIM))
        xs[h] = jnp.concatenate([kbe[:, sl], vb[:, sl]], axis=1)

    NB = GDN_NB
    ci3 = lax.broadcasted_iota(jnp.int32, (1, C, C), 1)
    cj3 = lax.broadcasted_iota(jnp.int32, (1, C, C), 2)
    eye3 = jnp.where(ci3 == cj3, 1.0, 0.0)

    def bmm(a, b):
        return lax.dot_general(a.astype(BF16), b.astype(BF16), (((2,), (1,)), ((0,), (0,))),
                               preferred_element_type=F32)

    def bmm_nt(a, b):
        return lax.dot_general(a.astype(BF16), b.astype(BF16), (((2,), (2,)), ((0,), (0,))),
                               preferred_element_type=F32)

    def intra(nb, carry):
        rows = pl.ds(pl.multiple_of(nb * (NB * C), NB * C), NB * C)
        for h in range(GDN_HPB):
            kk = ks[h, rows, :].reshape(NB, C, HEAD_DIM)
            kbq = jnp.concatenate([kbs[h, rows, :].reshape(NB, C, HEAD_DIM),
                                   qs[h, rows, :].reshape(NB, C, HEAD_DIM)], axis=1)
            a2 = bmm_nt(kbq, kk)
            gcol = gcs[h, rows, :].reshape(NB, C, HEAD_DIM)
            grow = grs[h, pl.ds(nb * NB, NB), 0:1, :]
            decay = jnp.exp(jnp.where(ci3 >= cj3, gcol - grow, NEG_BIG))
            low = jnp.where(ci3 > cj3, a2[:, :C] * decay, 0.0)
            qks[h, rows, :] = (a2[:, C:] * decay).reshape(NB * C, C)
            tinv = eye3 - jnp.where(ci3 // 2 == cj3 // 2, low, 0.0)
            for lv in range(1, 6):
                s = 2 ** lv
                off = jnp.where((ci3 // (2 * s) == cj3 // (2 * s)) & (ci3 // s != cj3 // s), low, 0.0)
                tinv = tinv - bmm(tinv, bmm(off, tinv))
            x = bmm(tinv, xs[h, rows, :].reshape(NB, C, 2 * HEAD_DIM))
            xs[h, rows, :] = x.reshape(NB * C, 2 * HEAD_DIM)
            kd = kk * jnp.exp(gcol[:, C - 1:C, :] - gcol)
            pq = lax.dot_general(kd.astype(BF16), x.astype(BF16), (((1,), (1,)), ((0,), (0,))),
                                 preferred_element_type=F32)
            ps[h, rows, :] = pq[:, :, :HEAD_DIM].reshape(NB * C, HEAD_DIM)
            qq[h, rows, :] = pq[:, :, HEAD_DIM:].reshape(NB * C, HEAD_DIM)
        return carry

    lax.fori_loop(0, N // NB, intra, 0)

    def inter(n, states):
        sl = pl.ds(pl.multiple_of(n * C, C), C)
        new_states = []
        for h in range(GDN_HPB):
            st = states[h]
            sst[h, n] = st
            glast = gcs[h, pl.ds(n * C + C - 1, 1), :]
            new_states.append(st * jnp.exp(glast) - _dot_hi(ps[h, sl, :], st) + qq[h, sl, :])
        return tuple(new_states)

    lax.fori_loop(0, N, inter, tuple(jnp.zeros((HEAD_DIM, HEAD_DIM), F32) for _ in range(GDN_HPB)))

    def outputs(nb, carry):
        rows = pl.ds(pl.multiple_of(nb * (NB * C), NB * C), NB * C)
        for h in range(GDN_HPB):
            st = sst[h, pl.ds(nb * NB, NB)]
            x = xs[h, rows, :].reshape(NB, C, 2 * HEAD_DIM)
            gcol = gcs[h, rows, :].reshape(NB, C, HEAD_DIM)
            qd = qs[h, rows, :].reshape(NB, C, HEAD_DIM) * jnp.exp(gcol)
            t1 = bmm(jnp.concatenate([x[:, :, :HEAD_DIM], qd], axis=1), st)
            vn = x[:, :, HEAD_DIM:] - t1[:, :C]
            o = t1[:, C:] + bmm(qks[h, rows, :].reshape(NB, C, C), vn)
            os_[h, rows, :] = o.reshape(NB * C, HEAD_DIM)
        return carry

    lax.fori_loop(0, N // NB, outputs, 0)

    o = jnp.concatenate([os_[h] for h in range(GDN_HPB)], axis=1)
    o = o * lax.rsqrt(head_sumsq(o) * (1.0 / HEAD_DIM) + RMS_EPS) * ng_ref[...]
    o_ref[...] = o * _silu(z_ref[...])


def _gdn_call(qkv, z, aexp, bexp, a_rows, conv_w, alog_exp, dtb_exp, norm_g, B, S):
    T = B * S
    W = GDN_HPB * HEAD_DIM
    P = GDN_HEADS // GDN_HPB
    N = S // GDN_CHUNK
    col = lambda off: (lambda b, p: (b, off + p))
    par = lambda off: (lambda b, p: (0, off + p))
    hs = lambda n: pltpu.VMEM((GDN_HPB, S, n), F32)
    return pl.pallas_call(
        functools.partial(_gdn_kernel, S=S),
        grid=(B, P),
        in_specs=[pl.BlockSpec((S, W), col(0)), pl.BlockSpec((S, W), col(P)), pl.BlockSpec((S, W), col(2 * P)),
                  pl.BlockSpec((S, W), col(0)), pl.BlockSpec((S, W), col(0)), pl.BlockSpec((S, W), col(0)),
                  pl.BlockSpec((None, None, N, W), lambda b, p: (b, p, 0, 0)),
                  pl.BlockSpec((GDN_CONV, W), par(0)), pl.BlockSpec((GDN_CONV, W), par(P)),
                  pl.BlockSpec((GDN_CONV, W), par(2 * P)),
                  pl.BlockSpec((1, W), par(0)), pl.BlockSpec((1, W), par(0)),
                  pl.BlockSpec((1, W), par(0)), pl.BlockSpec((1, W), par(0)),
                  pl.BlockSpec((1, W), lambda b, p: (0, 0))],
        out_specs=pl.BlockSpec((S, W), col(0)),
        out_shape=jax.ShapeDtypeStruct((T, GDN_WIDTH), F32),
        scratch_shapes=[hs(HEAD_DIM), hs(HEAD_DIM), hs(HEAD_DIM), hs(2 * HEAD_DIM), hs(HEAD_DIM),
                        pltpu.VMEM((GDN_HPB, N, 8, HEAD_DIM), F32),
                        hs(HEAD_DIM), hs(HEAD_DIM), hs(HEAD_DIM), hs(HEAD_DIM),
                        pltpu.VMEM((GDN_HPB, N, HEAD_DIM, HEAD_DIM), F32)],
        compiler_params=_cparams("parallel", "parallel"),
        name="gdn",
    )(qkv, qkv, qkv, z, aexp, bexp, a_rows, conv_w, conv_w, conv_w,
      alog_exp, dtb_exp, alog_exp, dtb_exp, norm_g)


def _cmp_kernel(xk_ref, xv_ref, pe_ref, w1_ref, b1_ref, w2k_ref, w2vt_ref, b2k_ref, b2v_ref, kg_ref,
                kc_ref, vct_ref):
    half = (CMP_BLOCK // 2) * HEAD_DIM
    nrows = xk_ref.shape[0]

    def hidden(x, j):
        pe = pe_ref[j]
        p = _dot(x + pe[:, :half], w1_ref[j, :half, :])
        q = _dot(x + pe[:, half:], w1_ref[j, half:, :])
        return _silu(p + pltpu.roll(q, nrows - 1, axis=0) + b1_ref[j])

    kc = _dot(hidden(xk_ref[...], 0), w2k_ref[...]) + b2k_ref[...]
    ms = jnp.sum(kc * kc, axis=-1, keepdims=True) * (1.0 / HEAD_DIM)
    kcn = kc * lax.rsqrt(ms + RMS_EPS) * kg_ref[...]
    lane = lax.broadcasted_iota(jnp.int32, kc.shape, 1)
    cend = lax.broadcasted_iota(jnp.int32, kc.shape, 0) * CMP_STRIDE + (CMP_BLOCK - 1)
    ext = jnp.where(lane == HEAD_DIM, cend // LANES, jnp.where(lane == HEAD_DIM + 1, cend % LANES, 0)).astype(F32)
    kc_ref[...] = jnp.where(lane < HEAD_DIM, kcn, ext)
    vct_ref[...] = _dot_nt(w2vt_ref[...], hidden(xv_ref[...], 1)) + b2v_ref[...]


def _cmp_call(xk, xv, pe, w1, b1, w2k, w2vt, b2k, b2v, kg):
    BG, NC, half = xk.shape
    blk = pl.BlockSpec((None, NC, half), lambda i: (i, 0, 0))
    full = lambda a: pl.BlockSpec(a.shape, lambda i: (0,) * a.ndim)
    return pl.pallas_call(
        _cmp_kernel,
        grid=(BG,),
        in_specs=[blk, blk] + [full(a) for a in (pe, w1, b1, w2k, w2vt, b2k, b2v, kg)],
        out_specs=[pl.BlockSpec((None, NC, LANES), lambda i: (i, 0, 0)),
                   pl.BlockSpec((None, HEAD_DIM, NC), lambda i: (i, 0, 0))],
        out_shape=[jax.ShapeDtypeStruct((BG, NC, LANES), F32), jax.ShapeDtypeStruct((BG, HEAD_DIM, NC), F32)],
        compiler_params=_cparams("parallel"),
        name="nsa_compress",
    )(xk, xv, pe, w1, b1, w2k, w2vt, b2k, b2v, kg)


NSA_TQ = 128
NSA_TK = 128
NSA_M = NSA_GROUP * NSA_TQ
NSA_MS = NSA_M
MASK_BIG = 2.0 ** 100
SEL_LANE0 = HEAD_DIM + 2


def _nsa_kernel(q_ref, kv_ref, vt_ref, kcx_ref, vct_ref, misct_ref, cbias_ref, tbl_ref, qg_ref, kg_ref,
                ovt_ref, ish_ref, o_ref, ks_s, kw_s, *, S):
    tq, tk, M = NSA_TQ, NSA_TK, NSA_M
    MS = NSA_MS
    G, R = NSA_KV_HEADS, NSA_GROUP
    NS = S // SLC_BLOCK
    n_top = min(SLC_TOPN, NS)
    qi = pl.program_id(1)
    t0 = qi * tq
    nt_dims = (((1,), (1,)), ((), ()))

    @pl.when(qi == 0)
    def _():
        rowi = lax.broadcasted_iota(jnp.int32, (S, LANES), 0)
        lane = lax.broadcasted_iota(jnp.int32, (S, LANES), 1)
        ext = jnp.where(lane == HEAD_DIM, rowi // LANES, jnp.where(lane == HEAD_DIM + 1, rowi % LANES, 0)).astype(F32)
        ext_slc = jnp.where(lane - SEL_LANE0 == rowi // SLC_BLOCK, -MASK_BIG, ext)

        def prep(c0, g, gain, extra):
            kraw = kv_ref[:, c0:c0 + LANES]
            if g == 1:
                kraw = pltpu.roll(kraw, HEAD_DIM, axis=1)
            ms = jnp.sum(jnp.where(lane < HEAD_DIM, kraw * kraw, 0.0), axis=-1, keepdims=True) * (1.0 / HEAD_DIM)
            return jnp.where(lane < HEAD_DIM, kraw * lax.rsqrt(ms + RMS_EPS) * gain, extra).astype(BF16)

        for g in range(G):
            ks_s[g] = prep(2 * NSA_KV_WIDTH, g, kg_ref[1:2, :], ext_slc)
            kw_s[g] = prep(4 * NSA_KV_WIDTH, g, kg_ref[2:3, :], ext)

    lane_q = lax.broadcasted_iota(jnp.int32, (tq, LANES), 1)
    gate_sig = _sigmoid(misct_ref[MISC_GATE0:MISC_GATE0 + 3 * NSA_HEADS, :])
    kt_d = t0 // tk

    def flash(streams, kt_lo):
        def scores(st, kt):
            qpp, k_s, g, _, idx_fn = st
            kblk = k_s[g, pl.ds(pl.multiple_of(kt * tk, tk), tk), :]
            return (lax.dot_general(kblk, qpp, nt_dims, preferred_element_type=F32)
                    + tbl_ref[idx_fn(kt), :, 0:MS])

        def values(st, kt, p):
            voff = st[3]
            return jnp.dot(vt_ref[kt, voff:voff + HEAD_DIM, :], p, preferred_element_type=F32)

        def body(kt, carry):
            out = []
            for st, (s_cur, p_prev, alpha, m, l, acc) in zip(streams, carry):
                acc = alpha * acc + values(st, jnp.maximum(kt - 1, kt_lo), p_prev)
                s_next = scores(st, jnp.minimum(kt + 1, kt_d))
                m_new = jnp.maximum(m, jnp.max(s_cur, axis=0, keepdims=True))
                alpha = jnp.exp(m - m_new)
                p = jnp.exp(s_cur - m_new)
                l = alpha * l + jnp.sum(p, axis=0, keepdims=True)
                out.append((s_next, p.astype(BF16), alpha, m_new, l, acc))
            return tuple(out)

        init = tuple((scores(st, kt_lo), jnp.zeros((tk, MS), BF16), jnp.ones((1, MS), F32),
                      jnp.full((1, MS), NEG_BIG, F32), jnp.zeros((1, MS), F32), jnp.zeros((HEAD_DIM, MS), F32))
                     for st in streams)
        res = lax.fori_loop(kt_lo, kt_d + 1, body, init)
        outs = []
        for st, (_, p_last, alpha, _, l, acc) in zip(streams, res):
            acc = alpha * acc + values(st, kt_d, p_last)
            outs.append(acc / jnp.where(l > 0.0, l, 1.0))
        return outs

    qpps, o_cmps = [], []
    for g in range(G):
        qrows = []
        for r in range(R):
            hh = g * R + r
            qpair = q_ref[:, (hh // 2) * LANES:(hh // 2 + 1) * LANES]
            if hh % 2 == 1:
                qpair = pltpu.roll(qpair, HEAD_DIM, axis=1)
            ms = jnp.sum(jnp.where(lane_q < HEAD_DIM, qpair * qpair, 0.0), axis=-1, keepdims=True) * (1.0 / HEAD_DIM)
            qn = qpair * lax.rsqrt(ms + RMS_EPS) * qg_ref[...] * (HEAD_DIM ** -0.5)
            slope = 2.0 ** (-8.0 * (hh + 1) / NSA_HEADS)
            ex = jnp.where(lane_q == HEAD_DIM, slope * LANES, jnp.where(lane_q == HEAD_DIM + 1, slope, 0.0))
            qrows.append(jnp.where(lane_q < HEAD_DIM, qn, ex))
        qbase = jnp.concatenate(qrows, axis=0)

        cb = cbias_ref[...]
        s = (lax.dot_general(kcx_ref[g].astype(BF16), qbase.astype(BF16), nt_dims, preferred_element_type=F32)
             + jnp.concatenate([cb] * R, axis=1))
        mx = jnp.maximum(jnp.max(s, axis=0, keepdims=True), NEG_BIG)
        e = jnp.exp(s - mx)
        l = jnp.sum(e, axis=0, keepdims=True)
        p_cmp = e / jnp.where(l > 0.0, l, 1.0)
        o_cmps.append(_dot(vct_ref[g], p_cmp))

        psum = p_cmp[:, 0:tq]
        for r in range(1, R):
            psum = psum + p_cmp[:, r * tq:(r + 1) * tq]
        hi, mid, lo = _split3(psum)
        ovt = ovt_ref[...]
        imp = (jnp.dot(ovt, hi, preferred_element_type=F32) + jnp.dot(ovt, mid, preferred_element_type=F32)
               + jnp.dot(ovt, lo, preferred_element_type=F32))
        blk = lax.broadcasted_iota(jnp.int32, (NS, tq), 0)
        cur = (t0 + lax.broadcasted_iota(jnp.int32, (NS, tq), 1)) // SLC_BLOCK
        valid = blk <= cur
        forced = (blk == 0) | (blk == cur) | (blk == cur - 1)
        score = jnp.where(forced, SEL_BIG, jnp.where(valid, imp, -SEL_BIG))
        rank = jnp.zeros((NS, tq), F32)
        for j in range(NS):
            cj = score[j:j + 1, :]
            beats = (cj > score) | ((cj == score) & (blk > j))
            rank = rank + jnp.where(beats, 1.0, 0.0)
        nsel = jnp.where((rank < n_top) & valid, 0.0, 1.0).astype(BF16)
        nsel_q = lax.dot_general(nsel, ish_ref[...], (((0,), (0,)), ((), ())),
                                 preferred_element_type=F32)
        in_sel = (lane_q >= SEL_LANE0) & (lane_q < SEL_LANE0 + NS)
        qpps.append(jnp.concatenate([jnp.where(in_sel, nsel_q, qrows[r]) for r in range(R)], axis=0).astype(BF16))

    slc_idx = lambda kt: jnp.where(kt == kt_d, 1, 0)
    win_lo = kt_d - WINDOW // tk
    win_idx = lambda kt: jnp.where(kt == kt_d, 1, jnp.where(kt == win_lo, 2, 0))
    halves = [(g, hf) for g in range(G) for hf in range(M // MS)]
    o_slcs = flash([(qpps[g][hf * MS:(hf + 1) * MS], ks_s, g, g * HEAD_DIM, slc_idx) for g, hf in halves], 0)
    o_wins = flash([(qpps[g][hf * MS:(hf + 1) * MS], kw_s, g, (G + g) * HEAD_DIM, win_idx) for g, hf in halves],
                   jnp.maximum(win_lo, 0))
    o_slcs = [jnp.concatenate(o_slcs[g * (M // MS):(g + 1) * (M // MS)], axis=1) for g in range(G)]
    o_wins = [jnp.concatenate(o_wins[g * (M // MS):(g + 1) * (M // MS)], axis=1) for g in range(G)]

    for g in range(G):
        for r in range(R):
            hh = g * R + r
            cs = slice(r * tq, (r + 1) * tq)
            o_ref[hh * HEAD_DIM:(hh + 1) * HEAD_DIM, :] = (
                gate_sig[hh:hh + 1, :] * o_cmps[g][:, cs]
                + gate_sig[NSA_HEADS + hh:NSA_HEADS + hh + 1, :] * o_slcs[g][:, cs]
                + gate_sig[2 * NSA_HEADS + hh:2 * NSA_HEADS + hh + 1, :] * o_wins[g][:, cs])


def _nsa_consts(S):
    tq, tk, M = NSA_TQ, NSA_TK, NSA_M
    NC = S // CMP_STRIDE
    NS = S // SLC_BLOCK
    cs = np.arange(NC)[:, None] * CMP_STRIDE
    ss = np.arange(NS)[None, :] * SLC_BLOCK
    ov = np.clip(np.minimum(cs + CMP_BLOCK, ss + SLC_BLOCK) - np.maximum(cs, ss), 0, None) / CMP_BLOCK
    ov[NC - 1:] = 0.0
    ish = np.zeros((NS, LANES), np.float32)
    ish[np.arange(NS), SEL_LANE0 + np.arange(NS)] = 1.0
    t = np.arange(S).reshape(S // tq, 1, tq)
    n = np.arange(NC).reshape(1, NC, 1)
    cbias = np.where((t >= n * CMP_STRIDE + CMP_BLOCK - 1) & (n < NC - 1), 0.0, -MASK_BIG).astype(np.float32)
    i = np.arange(tk)[:, None]
    j = np.arange(M)[None, :] % tq
    tbl = np.stack([np.zeros((tk, M)), np.where(i <= j, 0.0, -MASK_BIG), np.where(i > j, 0.0, -MASK_BIG)])
    return (jnp.asarray(ov.T, BF16), jnp.asarray(ish, BF16), jnp.asarray(cbias), jnp.asarray(tbl, F32))


def _nsa_call(nq, kv, vt, kcx, vct, misct, q_norm_g, k_norm_g, B, S):
    T = B * S
    tq, tk, M = NSA_TQ, NSA_TK, NSA_M
    G = NSA_KV_HEADS
    nqt = S // tq
    NC = S // CMP_STRIDE
    NS = S // SLC_BLOCK
    ovt, ish, cbias, tbl = _nsa_consts(S)
    tile2 = lambda a: jnp.concatenate([a, a], axis=-1)
    row = lambda b, i: (b * nqt + i, 0)
    cst = lambda b, i: (0, 0)
    return pl.pallas_call(
        functools.partial(_nsa_kernel, S=S),
        grid=(B, nqt),
        in_specs=[pl.BlockSpec((tq, NSA_WIDTH), row),
                  pl.BlockSpec((S, 6 * NSA_KV_WIDTH), lambda b, i: (b, 0)),
                  pl.BlockSpec((None, S // tk, 2 * NSA_KV_WIDTH, tk), lambda b, i: (b, 0, 0, 0)),
                  pl.BlockSpec((None, G, NC, LANES), lambda b, i: (b, 0, 0, 0)),
                  pl.BlockSpec((None, G, HEAD_DIM, NC), lambda b, i: (b, 0, 0, 0)),
                  pl.BlockSpec((LANES, tq), lambda b, i: (0, b * nqt + i)),
                  pl.BlockSpec((None, NC, tq), lambda b, i: (i, 0, 0)),
                  pl.BlockSpec((3, tk, M), lambda b, i: (0, 0, 0)),
                  pl.BlockSpec((1, LANES), cst),
                  pl.BlockSpec((3, LANES), cst),
                  pl.BlockSpec((NS, NC), cst),
                  pl.BlockSpec((NS, LANES), cst)],
        out_specs=pl.BlockSpec((NSA_WIDTH, tq), lambda b, i: (0, b * nqt + i)),
        out_shape=jax.ShapeDtypeStruct((NSA_WIDTH, T), F32),
        scratch_shapes=[pltpu.VMEM((G, S, LANES), BF16), pltpu.VMEM((G, S, LANES), BF16)],
        compiler_params=_cparams("parallel", "arbitrary"),
        name="nsa_attn",
    )(nq, kv, vt, kcx, vct, misct, cbias, tbl, tile2(q_norm_g), tile2(k_norm_g), ovt, ish)


def _out_proj_kernel(yg_ref, yn_ref, x_ref, g1_ref, sc_ref, sh_ref, ng_ref, wo_ref, rwt_ref, rb_ref,
                     x1_ref, h2_ref, gate_ref):
    mix = _dot(yg_ref[...], wo_ref[:GDN_WIDTH, :]) + _dot(yn_ref[...], wo_ref[GDN_WIDTH:, :])
    x1 = x_ref[...] + g1_ref[...] * mix
    x1_ref[...] = x1
    h2 = _rms(x1, ng_ref[...]) * (1.0 + sc_ref[...]) + sh_ref[...]
    h2_ref[...] = h2.astype(BF16)

    tm = h2.shape[0]
    logits = _dot_nt_hi(rwt_ref[...], h2)
    score = _sigmoid(logits)
    biased = score + rb_ref[...]
    b = [biased[e:e + 1, :] for e in range(N_EXPERTS)]
    n_groups = N_EXPERTS // EXPERTS_PER_GROUP
    gscore = []
    for gi in range(n_groups):
        vals = b[gi * EXPERTS_PER_GROUP:(gi + 1) * EXPERTS_PER_GROUP]
        best = None
        for i in range(EXPERTS_PER_GROUP):
            for j in range(i + 1, EXPERTS_PER_GROUP):
                pair = vals[i] + vals[j]
                best = pair if best is None else jnp.maximum(best, pair)
        gscore.append(best)
    gbest = jnp.zeros((1, tm), jnp.int32)
    top = gscore[0]
    for gi in range(1, n_groups):
        better = gscore[gi] > top
        gbest = jnp.where(better, gi, gbest)
        top = jnp.where(better, gscore[gi], top)
    erow = lax.broadcasted_iota(jnp.int32, (N_EXPERTS, tm), 0)
    gates = jnp.zeros((N_EXPERTS, tm), F32)
    for e in range(N_EXPERTS):
        gi = e // EXPERTS_PER_GROUP
        rank = jnp.zeros((1, tm), F32)
        for e2 in range(gi * EXPERTS_PER_GROUP, (gi + 1) * EXPERTS_PER_GROUP):
            if e2 == e:
                continue
            beats = (b[e2] > b[e]) | ((b[e2] == b[e]) & (e2 < e))
            rank = rank + jnp.where(beats, 1.0, 0.0)
        chosen = (gbest == gi) & (rank < 2.0)
        ge = jnp.where(chosen, score[e:e + 1, :], 0.0)
        gates = gates + jnp.where(erow == e, ge, 0.0)
    gates = gates / jnp.sum(gates, axis=0, keepdims=True)
    ident = (lax.broadcasted_iota(jnp.int32, (N_EXPERTS, LANES), 0)
             == lax.broadcasted_iota(jnp.int32, (N_EXPERTS, LANES), 1))
    ident = jnp.where(ident, 1.0, 0.0).astype(BF16)
    hi, mid, lo = _split3(gates)
    dn = (((0,), (0,)), ((), ()))
    gate_ref[...] = (lax.dot_general(hi, ident, dn, preferred_element_type=F32)
                     + lax.dot_general(mid, ident, dn, preferred_element_type=F32)
                     + lax.dot_general(lo, ident, dn, preferred_element_type=F32))


def _out_proj_call(yg, yn, x2, g1, sc2, sh2, ng, wo, rwt, rb, S):
    T, D = x2.shape
    tm = 256
    per_b = S // tm
    row = lambda i: (i, 0)
    bat = lambda i: (i // per_b, 0, 0)
    cst = lambda i: (0, 0)
    return pl.pallas_call(
        _out_proj_kernel,
        grid=(T // tm,),
        in_specs=[pl.BlockSpec((tm, GDN_WIDTH), row), pl.BlockSpec((tm, NSA_WIDTH), row), pl.BlockSpec((tm, D), row),
                  pl.BlockSpec((None, 1, D), bat), pl.BlockSpec((None, 1, D), bat), pl.BlockSpec((None, 1, D), bat),
                  pl.BlockSpec((1, D), cst), pl.BlockSpec((D, D), cst),
                  pl.BlockSpec((N_EXPERTS, D), cst), pl.BlockSpec((N_EXPERTS, 1), cst)],
        out_specs=[pl.BlockSpec((tm, D), row), pl.BlockSpec((tm, D), row), pl.BlockSpec((tm, LANES), row)],
        out_shape=[jax.ShapeDtypeStruct((T, D), F32), jax.ShapeDtypeStruct((T, D), BF16),
                   jax.ShapeDtypeStruct((T, LANES), F32)],
        compiler_params=_cparams("parallel"),
        name="out_proj_router",
    )(yg, yn, x2, g1, sc2, sh2, ng, wo, rwt, rb)


def _moe_kernel(h_ref, gate_ref, x1_ref, g2_ref, wg_ref, wu_ref, wd_ref, o_ref):
    e = pl.program_id(1)
    h = h_ref[...]
    lane = lax.broadcasted_iota(jnp.int32, gate_ref.shape, 1)
    gcol = jnp.sum(jnp.where(lane == e, gate_ref[...], 0.0), axis=1, keepdims=True)
    act = _silu(jnp.dot(h, wg_ref[...], preferred_element_type=F32)) * jnp.dot(h, wu_ref[...],
                                                                              preferred_element_type=F32)
    y = _dot(act * gcol, wd_ref[...])

    @pl.when(e == 0)
    def _():
        o_ref[...] = y

    @pl.when(e > 0)
    def _():
        o_ref[...] += y

    @pl.when(e == N_EXPERTS - 1)
    def _():
        o_ref[...] = x1_ref[...] + g2_ref[...] * o_ref[...]


def _moe_call(h2, gates, x1, g2, wg, wu, wd, S):
    T, D = x1.shape
    tm = 1024
    per_b = S // tm
    row = lambda i, e: (i, 0)
    wsel = lambda i, e: (e, 0, 0)
    return pl.pallas_call(
        _moe_kernel,
        grid=(T // tm, N_EXPERTS),
        in_specs=[pl.BlockSpec((tm, D), row), pl.BlockSpec((tm, LANES), row), pl.BlockSpec((tm, D), row),
                  pl.BlockSpec((None, 1, D), lambda i, e: (i // per_b, 0, 0)),
                  pl.BlockSpec((None, D, EXPERT_FF), wsel), pl.BlockSpec((None, D, EXPERT_FF), wsel),
                  pl.BlockSpec((None, EXPERT_FF, D), wsel)],
        out_specs=pl.BlockSpec((tm, D), row),
        out_shape=jax.ShapeDtypeStruct((T, D), F32),
        compiler_params=_cparams("parallel", "arbitrary"),
        name="moe",
    )(h2, gates, x1, g2, wg, wu, wd)


def kernel(x, c, ada_w, ada_b, norm1_g, norm2_g, w_in, gdn_conv_w, gdn_a_log, gdn_dt_bias, gdn_norm_g,
           nsa_q_norm_g, nsa_k_norm_g, cmp_pe, cmp_w1, cmp_b1, cmp_w2, cmp_b2, w_out, router_w, router_bias,
           exp_w_gate, exp_w_up, exp_w_down):
    B, S, D = x.shape
    L = ada_w.shape[0]
    T = B * S
    G = NSA_KV_HEADS
    NC = S // CMP_STRIDE
    N = S // GDN_CHUNK
    P = GDN_HEADS // GDN_HPB
    W = GDN_HPB * HEAD_DIM

    mod = _ada_call(c, ada_w, ada_b)
    rwt = router_w.T
    rb = router_bias.reshape(N_EXPERTS, 1)
    x2 = x.reshape(T, D)
    for l in range(L):
        m6 = mod[l].reshape(B, 6, 1, D)
        sh1, sc1, g1, sh2, sc2, g2 = (m6[:, i] for i in range(6))
        qkv, z, nq, kv, aexp, bexp, misc, misct = _in_proj_call(
            x2, sc1, sh1, norm1_g[l].reshape(1, D), _pad_w_in(w_in[l]), S)

        a_rows = misc[:, :GDN_HEADS].reshape(B, N, GDN_CHUNK, P, GDN_HPB).transpose(0, 3, 1, 4, 2).reshape(B, P, N, W)
        rep = lambda t: jnp.repeat(t, HEAD_DIM).reshape(1, GDN_WIDTH)
        y_gdn = _gdn_call(qkv, z, aexp, bexp, a_rows, gdn_conv_w[l], rep(gdn_a_log[l]), rep(gdn_dt_bias[l]),
                          jnp.tile(gdn_norm_g[l], GDN_HPB).reshape(1, W), B, S)

        xkv = kv[:, :2 * NSA_KV_WIDTH].reshape(B, S, 2, G, HEAD_DIM).transpose(2, 0, 3, 1, 4)
        xkv = xkv.reshape(2, B * G, NC, CMP_STRIDE * HEAD_DIM)
        w2 = cmp_w2[l]
        kcx, vct = _cmp_call(
            xkv[0], xkv[1], cmp_pe[l].reshape(2, 1, CMP_BLOCK * HEAD_DIM), cmp_w1[l].astype(BF16),
            cmp_b1[l].reshape(2, 1, CMP_HIDDEN), jnp.pad(w2[0], ((0, 0), (0, LANES - HEAD_DIM))).astype(BF16),
            w2[1].T.astype(BF16), jnp.pad(cmp_b2[l, 0], (0, LANES - HEAD_DIM)).reshape(1, LANES),
            cmp_b2[l, 1].reshape(HEAD_DIM, 1), jnp.pad(nsa_k_norm_g[l, 0], (0, LANES - HEAD_DIM)).reshape(1, LANES))
        vsw = jnp.concatenate([kv[:, 3 * NSA_KV_WIDTH:4 * NSA_KV_WIDTH], kv[:, 5 * NSA_KV_WIDTH:]], axis=1)
        vt = vsw.astype(BF16).reshape(B, S // NSA_TK, NSA_TK, 2 * NSA_KV_WIDTH).transpose(0, 1, 3, 2)
        y_nsa_t = _nsa_call(nq, kv, vt, kcx.reshape(B, G, NC, LANES), vct.reshape(B, G, HEAD_DIM, NC), misct,
                            nsa_q_norm_g[l].reshape(1, HEAD_DIM), nsa_k_norm_g[l], B, S)
        y_nsa = y_nsa_t.T

        x1, h2, gates = _out_proj_call(y_gdn, y_nsa, x2, g1, sc2, sh2, norm2_g[l].reshape(1, D),
                                       w_out[l].astype(BF16), rwt, rb, S)
        x2 = _moe_call(h2, gates, x1, g2, exp_w_gate[l].astype(BF16), exp_w_up[l].astype(BF16),
                       exp_w_down[l].astype(BF16), S)
    return x2.reshape(B, S, D)
```

```python
import functools

import numpy as np
import jax
import jax.numpy as jnp
from jax import lax
from jax.experimental import pallas as pl
from jax.experimental.pallas import tpu as pltpu

F32 = jnp.float32
BF16 = jnp.bfloat16

HEAD_DIM = 64
GDN_HEADS = 8
GDN_WIDTH = GDN_HEADS * HEAD_DIM
GDN_CONV = 4
GDN_CHUNK = 64
NSA_HEADS = 8
NSA_KV_HEADS = 2
NSA_GROUP = NSA_HEADS // NSA_KV_HEADS
NSA_WIDTH = NSA_HEADS * HEAD_DIM
NSA_KV_WIDTH = NSA_KV_HEADS * HEAD_DIM
CMP_BLOCK = 32
CMP_STRIDE = 16
CMP_HIDDEN = 256
SLC_BLOCK = 64
SLC_TOPN = 8
WINDOW = 512
N_EXPERTS = 16
EXPERTS_PER_GROUP = 4
EXPERT_FF = 512
RMS_EPS = 1e-6
NEG_BIG = -1e30
SEL_BIG = 1e9

LANES = 128
VMEM_LIMIT_BYTES = 56 * 1024 * 1024

C_QKV = 0
C_Z = 3 * GDN_WIDTH
C_NQ = C_Z + GDN_WIDTH
C_KV = C_NQ + NSA_WIDTH
C_AEXP = C_KV + 6 * NSA_KV_WIDTH
C_BEXP = C_AEXP + GDN_WIDTH
C_MISC = C_BEXP + GDN_WIDTH
IN_PAD_COLS = C_MISC + LANES
MISC_GATE0 = 16


def _cparams(*sem):
    return pltpu.CompilerParams(dimension_semantics=sem, vmem_limit_bytes=VMEM_LIMIT_BYTES)


def _dot(a, b):
    return jnp.dot(a.astype(BF16), b.astype(BF16), preferred_element_type=F32)


def _dot_nt(a, b):
    return lax.dot_general(a.astype(BF16), b.astype(BF16), (((1,), (1,)), ((), ())),
                           preferred_element_type=F32)


def _dot_tn(a, b):
    return lax.dot_general(a.astype(BF16), b.astype(BF16), (((0,), (0,)), ((), ())),
                           preferred_element_type=F32)


def _split2(x):
    hi = x.astype(BF16)
    lo = (x - hi.astype(F32)).astype(BF16)
    return hi, lo


def _split3(x):
    hi = x.astype(BF16)
    r = x - hi.astype(F32)
    mid = r.astype(BF16)
    lo = (r - mid.astype(F32)).astype(BF16)
    return hi, mid, lo


def _dot_hi(a, b):
    ah, al = _split2(a)
    bh, bl = _split2(b)
    return (jnp.dot(ah, bh, preferred_element_type=F32) + jnp.dot(al, bh, preferred_element_type=F32)
            + jnp.dot(ah, bl, preferred_element_type=F32))


def _dot_nt_hi(a, b):
    ah, al = _split2(a)
    bh, bl = _split2(b)
    dn = (((1,), (1,)), ((), ()))
    return (lax.dot_general(ah, bh, dn, preferred_element_type=F32)
            + lax.dot_general(al, bh, dn, preferred_element_type=F32)
            + lax.dot_general(ah, bl, dn, preferred_element_type=F32))


def _dot_exact_rhs(a, b_bf16):
    hi, mid, lo = _split3(a)
    return (jnp.dot(hi, b_bf16, preferred_element_type=F32) + jnp.dot(mid, b_bf16, preferred_element_type=F32)
            + jnp.dot(lo, b_bf16, preferred_element_type=F32))


def _rms(x, g):
    return x * lax.rsqrt(jnp.mean(x * x, axis=-1, keepdims=True) + RMS_EPS) * g


def _sigmoid(x):
    return 0.5 * (jnp.tanh(0.5 * x) + 1.0)


def _silu(x):
    return x * _sigmoid(x)


def _ada_kernel(c_ref, w_ref, b_ref, o_ref):
    o_ref[...] = _dot_hi(_silu(c_ref[...]), w_ref[...]) + b_ref[...]


def _ada_call(c, ada_w, ada_b):
    L, D, N = ada_w.shape
    B = c.shape[0]
    tn = 1024
    return pl.pallas_call(
        _ada_kernel,
        grid=(L, N // tn),
        in_specs=[pl.BlockSpec((B, D), lambda l, j: (0, 0)),
                  pl.BlockSpec((None, D, tn), lambda l, j: (l, 0, j)),
                  pl.BlockSpec((None, 1, tn), lambda l, j: (l, 0, j))],
        out_specs=pl.BlockSpec((None, B, tn), lambda l, j: (l, 0, j)),
        out_shape=jax.ShapeDtypeStruct((L, B, N), F32),
        compiler_params=_cparams("parallel", "parallel"),
        name="ada_mod",
    )(c, ada_w, ada_b.reshape(L, 1, N))


IN_SEGS = ((C_QKV, 3 * GDN_WIDTH), (C_Z, GDN_WIDTH), (C_NQ, NSA_WIDTH), (C_KV, 6 * NSA_KV_WIDTH),
           (C_AEXP, GDN_WIDTH), (C_BEXP, GDN_WIDTH), (C_MISC, LANES))


def _in_proj_kernel(x_ref, sc_ref, sh_ref, g_ref, w_ref, wmt_ref, *o_refs):
    h = _rms(x_ref[...], g_ref[...]) * (1.0 + sc_ref[...]) + sh_ref[...]
    hb = h.astype(BF16)
    for (c0, n), o_ref in zip(IN_SEGS, o_refs[:-1]):
        o_ref[...] = jnp.dot(hb, w_ref[:, c0:c0 + n], preferred_element_type=F32)
    o_refs[-1][...] = lax.dot_general(wmt_ref[...], hb, (((1,), (1,)), ((), ())), preferred_element_type=F32)


def _in_proj_call(x2, sc, sh, g, w_pad, S):
    T, D = x2.shape
    tm = 256
    per_b = S // tm
    row = lambda i: (i, 0)
    bat = lambda i: (i // per_b, 0, 0)
    return pl.pallas_call(
        _in_proj_kernel,
        grid=(T // tm,),
        in_specs=[pl.BlockSpec((tm, D), row),
                  pl.BlockSpec((None, 1, D), bat),
                  pl.BlockSpec((None, 1, D), bat),
                  pl.BlockSpec((1, D), lambda i: (0, 0)),
                  pl.BlockSpec((D, IN_PAD_COLS), lambda i: (0, 0)),
                  pl.BlockSpec((LANES, D), lambda i: (0, 0))],
        out_specs=[pl.BlockSpec((tm, n), row) for _, n in IN_SEGS] + [pl.BlockSpec((LANES, tm), lambda i: (0, i))],
        out_shape=[jax.ShapeDtypeStruct((T, n), F32) for _, n in IN_SEGS] + [jax.ShapeDtypeStruct((LANES, T), F32)],
        compiler_params=_cparams("parallel"),
        name="in_proj",
    )(x2, sc, sh, g, w_pad, w_pad[:, C_MISC:].T)


def _pad_w_in(w_in):
    D = w_in.shape[0]
    W = GDN_WIDTH
    o = 0
    gq, gk, gv, gz = (w_in[:, o + i * W:o + (i + 1) * W] for i in range(4))
    o += 4 * W
    ga = w_in[:, o:o + GDN_HEADS]
    gb = w_in[:, o + GDN_HEADS:o + 2 * GDN_HEADS]
    o += 2 * GDN_HEADS
    nq = w_in[:, o:o + NSA_WIDTH]
    o += NSA_WIDTH
    kv = w_in[:, o:o + 6 * NSA_KV_WIDTH]
    o += 6 * NSA_KV_WIDTH
    gates = w_in[:, o:o + 3 * NSA_HEADS]
    misc = jnp.zeros((D, LANES), w_in.dtype)
    misc = misc.at[:, 0:GDN_HEADS].set(ga).at[:, MISC_GATE0:MISC_GATE0 + 3 * NSA_HEADS].set(gates)
    cols = [gq, gk, gv, gz, nq, kv, jnp.repeat(ga, HEAD_DIM, axis=1), jnp.repeat(gb, HEAD_DIM, axis=1), misc]
    return jnp.concatenate(cols, axis=1).astype(BF16)


GDN_HPB = 2
GDN_NB = 32


def _gdn_kernel(q_ref, k_ref, v_ref, z_ref, a_ref, b_ref, ar_ref, cwq_ref, cwk_ref, cwv_ref,
                alog_ref, dtb_ref, alogr_ref, dtbr_ref, ng_ref, o_ref,
                qs, ks, kbs, xs, gcs, grs, ps, qq, qks, os_, sst, *, S):
    C = GDN_CHUNK
    N = S // C
    W = GDN_HPB * HEAD_DIM
    row = lax.broadcasted_iota(jnp.int32, (S, W), 0)
    row8 = lax.broadcasted_iota(jnp.int32, (8, W), 0)

    def conv_silu(x, w):
        y = x * w[GDN_CONV - 1:GDN_CONV, :]
        for s in range(1, GDN_CONV):
            xs = pltpu.roll(x, s, axis=0)
            xs = jnp.concatenate([jnp.where(row8 >= s, xs[:8], 0.0), xs[8:]], axis=0)
            y = y + xs * w[GDN_CONV - 1 - s:GDN_CONV - s, :]
        return _silu(y)

    bd = jnp.where(lax.broadcasted_iota(jnp.int32, (W, W), 0) // HEAD_DIM
                   == lax.broadcasted_iota(jnp.int32, (W, W), 1) // HEAD_DIM, 1.0, 0.0).astype(BF16)

    def head_sumsq(x):
        hi, lo = _split2(x * x)
        return jnp.dot(hi, bd, preferred_element_type=F32) + jnp.dot(lo, bd, preferred_element_type=F32)

    def head_l2(x):
        return x * lax.rsqrt(head_sumsq(x) + RMS_EPS)

    q = head_l2(conv_silu(q_ref[...], cwq_ref[...])) * (HEAD_DIM ** -0.5)
    k = head_l2(conv_silu(k_ref[...], cwk_ref[...]))
    v = conv_silu(v_ref[...], cwv_ref[...])
    beta = _sigmoid(b_ref[...])

    def log_decay(a, alog, dtb):
        xx = a + dtb
        sp = jnp.maximum(xx, 0.0) + jnp.log(1.0 + jnp.exp(-jnp.abs(xx)))
        return -jnp.exp(alog) * sp

    gc = log_decay(a_ref[...], alog_ref[...], dtb_ref[...])
    pos = row % C
    for s in (1, 2, 4, 8, 16, 32):
        gc = gc + jnp.where(pos >= s, pltpu.roll(gc, s, axis=0), 0.0)
    gr = log_decay(ar_ref[...], alogr_ref[...], dtbr_ref[...])
    lpos = lax.broadcasted_iota(jnp.int32, (N, W), 1) % C
    for s in (1, 2, 4, 8, 16, 32):
        gr = gr + jnp.where(lpos >= s, pltpu.roll(gr, s, axis=1), 0.0)

    kb = k * beta
    vb = v * beta
    kbe = kb * jnp.exp(gc)
    for h in range(GDN_HPB):
        sl = slice(h * HEAD_DIM, (h + 1) * HEAD_DIM)
        qs[h] = q[:, sl]
        ks[h] = k[:, sl]
        kbs[h] = kb[:, sl]
        gcs[h] = gc[:, sl]
        for n in range(N):
            grs[h, n] = jnp.broadcast_to(gr[n:n + 1, sl], (8, HEAD_DIM))
        xs[h] = jnp.concatenate([kbe[:, sl], vb[:, sl]], axis=1)

    NB = GDN_NB
    ci3 = lax.broadcasted_iota(jnp.int32, (1, C, C), 1)
    cj3 = lax.broadcasted_iota(jnp.int32, (1, C, C), 2)
    eye3 = jnp.where(ci3 == cj3, 1.0, 0.0)

    def bmm(a, b):
        return lax.dot_general(a.astype(BF16), b.astype(BF16), (((2,), (1,)), ((0,), (0,))),
                               preferred_element_type=F32)

    def bmm_nt(a, b):
        return lax.dot_general(a.astype(BF16), b.astype(BF16), (((2,), (2,)), ((0,), (0,))),
                               preferred_element_type=F32)

    def intra(nb, carry):
        rows = pl.ds(pl.multiple_of(nb * (NB * C), NB * C), NB * C)
        for h in range(GDN_HPB):
            kk = ks[h, rows, :].reshape(NB, C, HEAD_DIM)
            kbq = jnp.concatenate([kbs[h, rows, :].reshape(NB, C, HEAD_DIM),
                                   qs[h, rows, :].reshape(NB, C, HEAD_DIM)], axis=1)
            a2 = bmm_nt(kbq, kk)
            gcol = gcs[h, rows, :].reshape(NB, C, HEAD_DIM)
            grow = grs[h, pl.ds(nb * NB, NB), 0:1, :]
            decay = jnp.exp(jnp.where(ci3 >= cj3, gcol - grow, NEG_BIG))
            low = jnp.where(ci3 > cj3, a2[:, :C] * decay, 0.0)
            qks[h, rows, :] = (a2[:, C:] * decay).reshape(NB * C, C)
            tinv = eye3 - jnp.where(ci3 // 2 == cj3 // 2, low, 0.0)
            for lv in range(1, 6):
                s = 2 ** lv
                off = jnp.where((ci3 // (2 * s) == cj3 // (2 * s)) & (ci3 // s != cj3 // s), low, 0.0)
                tinv = tinv - bmm(tinv, bmm(off, tinv))
            x = bmm(tinv, xs[h, rows, :].reshape(NB, C, 2 * HEAD_DIM))
            xs[h, rows, :] = x.reshape(NB * C, 2 * HEAD_DIM)
            kd = kk * jnp.exp(gcol[:, C - 1:C, :] - gcol)
            pq = lax.dot_general(kd.astype(BF16), x.astype(BF16), (((1,), (1,)), ((0,), (0,))),
                                 preferred_element_type=F32)
            ps[h, rows, :] = pq[:, :, :HEAD_DIM].reshape(NB * C, HEAD_DIM)
            qq[h, rows, :] = pq[:, :, HEAD_DIM:].reshape(NB * C, HEAD_DIM)
        return carry

    lax.fori_loop(0, N // NB, intra, 0)

    def inter(n, states):
        sl = pl.ds(pl.multiple_of(n * C, C), C)
        new_states = []
        for h in range(GDN_HPB):
            st = states[h]
            sst[h, n] = st
            glast = gcs[h, pl.ds(n * C + C - 1, 1), :]
            new_states.append(st * jnp.exp(glast) - _dot_hi(ps[h, sl, :], st) + qq[h, sl, :])
        return tuple(new_states)

    lax.fori_loop(0, N, inter, tuple(jnp.zeros((HEAD_DIM, HEAD_DIM), F32) for _ in range(GDN_HPB)))

    def outputs(nb, carry):
        rows = pl.ds(pl.multiple_of(nb * (NB * C), NB * C), NB * C)
        for h in range(GDN_HPB):
            st = sst[h, pl.ds(nb * NB, NB)]
            x = xs[h, rows, :].reshape(NB, C, 2 * HEAD_DIM)
            gcol = gcs[h, rows, :].reshape(NB, C, HEAD_DIM)
            qd = qs[h, rows, :].reshape(NB, C, HEAD_DIM) * jnp.exp(gcol)
            t1 = bmm(jnp.concatenate([x[:, :, :HEAD_DIM], qd], axis=1), st)
            vn = x[:, :, HEAD_DIM:] - t1[:, :C]
            o = t1[:, C:] + bmm(qks[h, rows, :].reshape(NB, C, C), vn)
            os_[h, rows, :] = o.reshape(NB * C, HEAD_DIM)
        return carry

    lax.fori_loop(0, N // NB, outputs, 0)

    o = jnp.concatenate([os_[h] for h in range(GDN_HPB)], axis=1)
    o = o * lax.rsqrt(head_sumsq(o) * (1.0 / HEAD_DIM) + RMS_EPS) * ng_ref[...]
    o_ref[...] = o * _silu(z_ref[...])


def _gdn_call(qkv, z, aexp, bexp, a_rows, conv_w, alog_exp, dtb_exp, norm_g, B, S):
    T = B * S
    W = GDN_HPB * HEAD_DIM
    P = GDN_HEADS // GDN_HPB
    N = S // GDN_CHUNK
    col = lambda off: (lambda b, p: (b, off + p))
    par = lambda off: (lambda b, p: (0, off + p))
    hs = lambda n: pltpu.VMEM((GDN_HPB, S, n), F32)
    return pl.pallas_call(
        functools.partial(_gdn_kernel, S=S),
        grid=(B, P),
        in_specs=[pl.BlockSpec((S, W), col(0)), pl.BlockSpec((S, W), col(P)), pl.BlockSpec((S, W), col(2 * P)),
                  pl.BlockSpec((S, W), col(0)), pl.BlockSpec((S, W), col(0)), pl.BlockSpec((S, W), col(0)),
                  pl.BlockSpec((None, None, N, W), lambda b, p: (b, p, 0, 0)),
                  pl.BlockSpec((GDN_CONV, W), par(0)), pl.BlockSpec((GDN_CONV, W), par(P)),
                  pl.BlockSpec((GDN_CONV, W), par(2 * P)),
                  pl.BlockSpec((1, W), par(0)), pl.BlockSpec((1, W), par(0)),
                  pl.BlockSpec((1, W), par(0)), pl.BlockSpec((1, W), par(0)),
                  pl.BlockSpec((1, W), lambda b, p: (0, 0))],
        out_specs=pl.BlockSpec((S, W), col(0)),
        out_shape=jax.ShapeDtypeStruct((T, GDN_WIDTH), F32),
        scratch_shapes=[hs(HEAD_DIM), hs(HEAD_DIM), hs(HEAD_DIM), hs(2 * HEAD_DIM), hs(HEAD_DIM),
                        pltpu.VMEM((GDN_HPB, N, 8, HEAD_DIM), F32),
                        hs(HEAD_DIM), hs(HEAD_DIM), hs(HEAD_DIM), hs(HEAD_DIM),
                        pltpu.VMEM((GDN_HPB, N, HEAD_DIM, HEAD_DIM), F32)],
        compiler_params=_cparams("parallel", "parallel"),
        name="gdn",
    )(qkv, qkv, qkv, z, aexp, bexp, a_rows, conv_w, conv_w, conv_w,
      alog_exp, dtb_exp, alog_exp, dtb_exp, norm_g)


def _cmp_kernel(xk_ref, xv_ref, pe_ref, w1_ref, b1_ref, w2k_ref, w2vt_ref, b2k_ref, b2v_ref, kg_ref,
                kc_ref, vct_ref):
    half = (CMP_BLOCK // 2) * HEAD_DIM
    nrows = xk_ref.shape[0]

    def hidden(x, j):
        pe = pe_ref[j]
        p = _dot(x + pe[:, :half], w1_ref[j, :half, :])
        q = _dot(x + pe[:, half:], w1_ref[j, half:, :])
        return _silu(p + pltpu.roll(q, nrows - 1, axis=0) + b1_ref[j])

    kc = _dot(hidden(xk_ref[...], 0), w2k_ref[...]) + b2k_ref[...]
    ms = jnp.sum(kc * kc, axis=-1, keepdims=True) * (1.0 / HEAD_DIM)
    kcn = kc * lax.rsqrt(ms + RMS_EPS) * kg_ref[...]
    lane = lax.broadcasted_iota(jnp.int32, kc.shape, 1)
    cend = lax.broadcasted_iota(jnp.int32, kc.shape, 0) * CMP_STRIDE + (CMP_BLOCK - 1)
    ext = jnp.where(lane == HEAD_DIM, cend // LANES, jnp.where(lane == HEAD_DIM + 1, cend % LANES, 0)).astype(F32)
    kc_ref[...] = jnp.where(lane < HEAD_DIM, kcn, ext)
    vct_ref[...] = _dot_nt(w2vt_ref[...], hidden(xv_ref[...], 1)) + b2v_ref[...]


def _cmp_call(xk, xv, pe, w1, b1, w2k, w2vt, b2k, b2v, kg):
    BG, NC, half = xk.shape
    blk = pl.BlockSpec((None, NC, half), lambda i: (i, 0, 0))
    full = lambda a: pl.BlockSpec(a.shape, lambda i: (0,) * a.ndim)
    return pl.pallas_call(
        _cmp_kernel,
        grid=(BG,),
        in_specs=[blk, blk] + [full(a) for a in (pe, w1, b1, w2k, w2vt, b2k, b2v, kg)],
        out_specs=[pl.BlockSpec((None, NC, LANES), lambda i: (i, 0, 0)),
                   pl.BlockSpec((None, HEAD_DIM, NC), lambda i: (i, 0, 0))],
        out_shape=[jax.ShapeDtypeStruct((BG, NC, LANES), F32), jax.ShapeDtypeStruct((BG, HEAD_DIM, NC), F32)],
        compiler_params=_cparams("parallel"),
        name="nsa_compress",
    )(xk, xv, pe, w1, b1, w2k, w2vt, b2k, b2v, kg)


NSA_TQ = 128
NSA_TK = 128
NSA_M = NSA_GROUP * NSA_TQ
NSA_MS = NSA_M
MASK_BIG = 2.0 ** 100
SEL_LANE0 = HEAD_DIM + 2


def _nsa_kernel(q_ref, kv_ref, vt_ref, kcx_ref, vct_ref, misct_ref, cbias_ref, tbl_ref, qg_ref, kg_ref,
                ovt_ref, ish_ref, o_ref, ks_s, kw_s, *, S):
    tq, tk, M = NSA_TQ, NSA_TK, NSA_M
    MS = NSA_MS
    G, R = NSA_KV_HEADS, NSA_GROUP
    NS = S // SLC_BLOCK
    n_top = min(SLC_TOPN, NS)
    qi = pl.program_id(1)
    t0 = qi * tq
    nt_dims = (((1,), (1,)), ((), ()))

    @pl.when(qi == 0)
    def _():
        rowi = lax.broadcasted_iota(jnp.int32, (S, LANES), 0)
        lane = lax.broadcasted_iota(jnp.int32, (S, LANES), 1)
        ext = jnp.where(lane == HEAD_DIM, rowi // LANES, jnp.where(lane == HEAD_DIM + 1, rowi % LANES, 0)).astype(F32)
        ext_slc = jnp.where(lane - SEL_LANE0 == rowi // SLC_BLOCK, -MASK_BIG, ext)

        def prep(c0, g, gain, extra):
            kraw = kv_ref[:, c0:c0 + LANES]
            if g == 1:
                kraw = pltpu.roll(kraw, HEAD_DIM, axis=1)
            ms = jnp.sum(jnp.where(lane < HEAD_DIM, kraw * kraw, 0.0), axis=-1, keepdims=True) * (1.0 / HEAD_DIM)
            return jnp.where(lane < HEAD_DIM, kraw * lax.rsqrt(ms + RMS_EPS) * gain, extra).astype(BF16)

        for g in range(G):
            ks_s[g] = prep(2 * NSA_KV_WIDTH, g, kg_ref[1:2, :], ext_slc)
            kw_s[g] = prep(4 * NSA_KV_WIDTH, g, kg_ref[2:3, :], ext)

    lane_q = lax.broadcasted_iota(jnp.int32, (tq, LANES), 1)
    gate_sig = _sigmoid(misct_ref[MISC_GATE0:MISC_GATE0 + 3 * NSA_HEADS, :])
    kt_d = t0 // tk

    def score_tiles(streams, kt, biased):
        rows = pl.ds(pl.multiple_of(kt * tk, tk), tk)
        out = []
        for qpp, k_s, g, _, idx_fn in streams:
            s = lax.dot_general(k_s[g, rows, :], qpp, nt_dims, preferred_element_type=F32)
            out.append(s + tbl_ref[idx_fn(kt)] if biased else s)
        return tuple(out)

    def softmax_pv(streams, kt, scores, states):
        soft = []
        for s, (m, l, _) in zip(scores, states):
            m_new = jnp.maximum(m, jnp.max(s, axis=0, keepdims=True))
            alpha = jnp.exp(m - m_new)
            p = jnp.exp(s - m_new)
            soft.append((m_new, alpha, alpha * l + jnp.sum(p, axis=0, keepdims=True), p.astype(BF16)))
        pvs = [jnp.dot(vt_ref[kt, st[3]:st[3] + HEAD_DIM, :], sf[3], preferred_element_type=F32)
               for st, sf in zip(streams, soft)]
        return tuple((m_new, l, alpha * acc + pv)
                     for (m_new, alpha, l, _), (_, _, acc), pv in zip(soft, states, pvs))

    flash_init = (jnp.full((1, M), NEG_BIG, F32), jnp.zeros((1, M), F32), jnp.zeros((HEAD_DIM, M), F32))

    qpps, o_cmps = [], []
    for g in range(G):
        qrows = []
        for r in range(R):
            hh = g * R + r
            qpair = q_ref[:, (hh // 2) * LANES:(hh // 2 + 1) * LANES]
            if hh % 2 == 1:
                qpair = pltpu.roll(qpair, HEAD_DIM, axis=1)
            ms = jnp.sum(jnp.where(lane_q < HEAD_DIM, qpair * qpair, 0.0), axis=-1, keepdims=True) * (1.0 / HEAD_DIM)
            qn = qpair * lax.rsqrt(ms + RMS_EPS) * qg_ref[...] * (HEAD_DIM ** -0.5)
            slope = 2.0 ** (-8.0 * (hh + 1) / NSA_HEADS)
            ex = jnp.where(lane_q == HEAD_DIM, slope * LANES, jnp.where(lane_q == HEAD_DIM + 1, slope, 0.0))
            qrows.append(jnp.where(lane_q < HEAD_DIM, qn, ex))
        qbase = jnp.concatenate(qrows, axis=0)

        cb = cbias_ref[...]
        s = (lax.dot_general(kcx_ref[g].astype(BF16), qbase.astype(BF16), nt_dims, preferred_element_type=F32)
             + jnp.concatenate([cb] * R, axis=1))
        mx = jnp.maximum(jnp.max(s, axis=0, keepdims=True), NEG_BIG)
        e = jnp.exp(s - mx)
        l = jnp.sum(e, axis=0, keepdims=True)
        p_cmp = e / jnp.where(l > 0.0, l, 1.0)
        o_cmps.append(_dot(vct_ref[g], p_cmp))

        psum = p_cmp[:, 0:tq]
        for r in range(1, R):
            psum = psum + p_cmp[:, r * tq:(r + 1) * tq]
        hi, mid, lo = _split3(psum)
        ovt = ovt_ref[...]
        imp = (jnp.dot(ovt, hi, preferred_element_type=F32) + jnp.dot(ovt, mid, preferred_element_type=F32)
               + jnp.dot(ovt, lo, preferred_element_type=F32))
        blk = lax.broadcasted_iota(jnp.int32, (NS, tq), 0)
        cur = (t0 + lax.broadcasted_iota(jnp.int32, (NS, tq), 1)) // SLC_BLOCK
        valid = blk <= cur
        forced = (blk == 0) | (blk == cur) | (blk == cur - 1)
        score = jnp.where(forced, SEL_BIG, jnp.where(valid, imp, -SEL_BIG))
        rank = jnp.zeros((NS, tq), F32)
        for j in range(NS):
            cj = score[j:j + 1, :]
            beats = (cj > score) | ((cj == score) & (blk > j))
            rank = rank + jnp.where(beats, 1.0, 0.0)
        nsel = jnp.where((rank < n_top) & valid, 0.0, 1.0).astype(BF16)
        nsel_q = lax.dot_general(nsel, ish_ref[...], (((0,), (0,)), ((), ())),
                                 preferred_element_type=F32)
        in_sel = (lane_q >= SEL_LANE0) & (lane_q < SEL_LANE0 + NS)
        qpps.append(jnp.concatenate([jnp.where(in_sel, nsel_q, qrows[r]) for r in range(R)], axis=0).astype(BF16))

    slc_idx = lambda kt: jnp.where(kt == kt_d, 1, 0)
    win_lo = kt_d - WINDOW // tk
    win_idx = lambda kt: jnp.where(kt == kt_d, 1, jnp.where(kt == win_lo, 2, 0))
    slc_streams = [(qpps[g], ks_s, g, g * HEAD_DIM, slc_idx) for g in range(G)]
    win_streams = [(qpps[g], kw_s, g, (G + g) * HEAD_DIM, win_idx) for g in range(G)]
    kt_w = jnp.maximum(win_lo, 0)
    def slc_only(kt, carry):
        scores, states = carry
        nxt = score_tiles(slc_streams, jnp.minimum(kt + 1, kt_w), False)
        return nxt, softmax_pv(slc_streams, kt, scores, states)

    _, st_slc = lax.fori_loop(0, kt_w, slc_only, (score_tiles(slc_streams, 0, False), (flash_init,) * G))
    both = slc_streams + win_streams
    st_all = lax.fori_loop(kt_w, kt_d + 1, lambda kt, st: softmax_pv(both, kt, score_tiles(both, kt, True), st),
                           st_slc + (flash_init,) * G)
    o_all = [acc / jnp.where(l > 0.0, l, 1.0) for _, l, acc in st_all]
    o_slcs, o_wins = o_all[:G], o_all[G:]

    for g in range(G):
        for r in range(R):
            hh = g * R + r
            cs = slice(r * tq, (r + 1) * tq)
            o_ref[hh * HEAD_DIM:(hh + 1) * HEAD_DIM, :] = (
                gate_sig[hh:hh + 1, :] * o_cmps[g][:, cs]
                + gate_sig[NSA_HEADS + hh:NSA_HEADS + hh + 1, :] * o_slcs[g][:, cs]
                + gate_sig[2 * NSA_HEADS + hh:2 * NSA_HEADS + hh + 1, :] * o_wins[g][:, cs])


def _nsa_consts(S):
    tq, tk, M = NSA_TQ, NSA_TK, NSA_M
    NC = S // CMP_STRIDE
    NS = S // SLC_BLOCK
    cs = np.arange(NC)[:, None] * CMP_STRIDE
    ss = np.arange(NS)[None, :] * SLC_BLOCK
    ov = np.clip(np.minimum(cs + CMP_BLOCK, ss + SLC_BLOCK) - np.maximum(cs, ss), 0, None) / CMP_BLOCK
    ov[NC - 1:] = 0.0
    ish = np.zeros((NS, LANES), np.float32)
    ish[np.arange(NS), SEL_LANE0 + np.arange(NS)] = 1.0
    t = np.arange(S).reshape(S // tq, 1, tq)
    n = np.arange(NC).reshape(1, NC, 1)
    cbias = np.where((t >= n * CMP_STRIDE + CMP_BLOCK - 1) & (n < NC - 1), 0.0, -MASK_BIG).astype(np.float32)
    i = np.arange(tk)[:, None]
    j = np.arange(M)[None, :] % tq
    tbl = np.stack([np.zeros((tk, M)), np.where(i <= j, 0.0, -MASK_BIG), np.where(i > j, 0.0, -MASK_BIG)])
    return (jnp.asarray(ov.T, BF16), jnp.asarray(ish, BF16), jnp.asarray(cbias), jnp.asarray(tbl, F32))


def _nsa_call(nq, kv, vt, kcx, vct, misct, q_norm_g, k_norm_g, B, S):
    T = B * S
    tq, tk, M = NSA_TQ, NSA_TK, NSA_M
    G = NSA_KV_HEADS
    nqt = S // tq
    NC = S // CMP_STRIDE
    NS = S // SLC_BLOCK
    ovt, ish, cbias, tbl = _nsa_consts(S)
    tile2 = lambda a: jnp.concatenate([a, a], axis=-1)
    row = lambda b, i: (b * nqt + i, 0)
    cst = lambda b, i: (0, 0)
    return pl.pallas_call(
        functools.partial(_nsa_kernel, S=S),
        grid=(B, nqt),
        in_specs=[pl.BlockSpec((tq, NSA_WIDTH), row),
                  pl.BlockSpec((S, 6 * NSA_KV_WIDTH), lambda b, i: (b, 0)),
                  pl.BlockSpec((None, S // tk, 2 * NSA_KV_WIDTH, tk), lambda b, i: (b, 0, 0, 0)),
                  pl.BlockSpec((None, G, NC, LANES), lambda b, i: (b, 0, 0, 0)),
                  pl.BlockSpec((None, G, HEAD_DIM, NC), lambda b, i: (b, 0, 0, 0)),
                  pl.BlockSpec((LANES, tq), lambda b, i: (0, b * nqt + i)),
                  pl.BlockSpec((None, NC, tq), lambda b, i: (i, 0, 0)),
                  pl.BlockSpec((3, tk, M), lambda b, i: (0, 0, 0)),
                  pl.BlockSpec((1, LANES), cst),
                  pl.BlockSpec((3, LANES), cst),
                  pl.BlockSpec((NS, NC), cst),
                  pl.BlockSpec((NS, LANES), cst)],
        out_specs=pl.BlockSpec((NSA_WIDTH, tq), lambda b, i: (0, b * nqt + i)),
        out_shape=jax.ShapeDtypeStruct((NSA_WIDTH, T), F32),
        scratch_shapes=[pltpu.VMEM((G, S, LANES), BF16), pltpu.VMEM((G, S, LANES), BF16)],
        compiler_params=_cparams("parallel", "arbitrary"),
        name="nsa_attn",
    )(nq, kv, vt, kcx, vct, misct, cbias, tbl, tile2(q_norm_g), tile2(k_norm_g), ovt, ish)


def _out_proj_kernel(yg_ref, yn_ref, x_ref, g1_ref, sc_ref, sh_ref, ng_ref, wo_ref, rwt_ref, rb_ref,
                     x1_ref, h2_ref, gate_ref):
    mix = _dot(yg_ref[...], wo_ref[:GDN_WIDTH, :]) + _dot(yn_ref[...], wo_ref[GDN_WIDTH:, :])
    x1 = x_ref[...] + g1_ref[...] * mix
    x1_ref[...] = x1
    h2 = _rms(x1, ng_ref[...]) * (1.0 + sc_ref[...]) + sh_ref[...]
    h2_ref[...] = h2.astype(BF16)

    tm = h2.shape[0]
    logits = _dot_nt_hi(rwt_ref[...], h2)
    score = _sigmoid(logits)
    biased = score + rb_ref[...]
    b = [biased[e:e + 1, :] for e in range(N_EXPERTS)]
    n_groups = N_EXPERTS // EXPERTS_PER_GROUP
    gscore = []
    for gi in range(n_groups):
        vals = b[gi * EXPERTS_PER_GROUP:(gi + 1) * EXPERTS_PER_GROUP]
        best = None
        for i in range(EXPERTS_PER_GROUP):
            for j in range(i + 1, EXPERTS_PER_GROUP):
                pair = vals[i] + vals[j]
                best = pair if best is None else jnp.maximum(best, pair)
        gscore.append(best)
    gbest = jnp.zeros((1, tm), jnp.int32)
    top = gscore[0]
    for gi in range(1, n_groups):
        better = gscore[gi] > top
        gbest = jnp.where(better, gi, gbest)
        top = jnp.where(better, gscore[gi], top)
    erow = lax.broadcasted_iota(jnp.int32, (N_EXPERTS, tm), 0)
    gates = jnp.zeros((N_EXPERTS, tm), F32)
    for e in range(N_EXPERTS):
        gi = e // EXPERTS_PER_GROUP
        rank = jnp.zeros((1, tm), F32)
        for e2 in range(gi * EXPERTS_PER_GROUP, (gi + 1) * EXPERTS_PER_GROUP):
            if e2 == e:
                continue
            beats = (b[e2] > b[e]) | ((b[e2] == b[e]) & (e2 < e))
            rank = rank + jnp.where(beats, 1.0, 0.0)
        chosen = (gbest == gi) & (rank < 2.0)
        ge = jnp.where(chosen, score[e:e + 1, :], 0.0)
        gates = gates + jnp.where(erow == e, ge, 0.0)
    gates = gates / jnp.sum(gates, axis=0, keepdims=True)
    ident = (lax.broadcasted_iota(jnp.int32, (N_EXPERTS, LANES), 0)
             == lax.broadcasted_iota(jnp.int32, (N_EXPERTS, LANES), 1))
    ident = jnp.where(ident, 1.0, 0.0).astype(BF16)
    hi, mid, lo = _split3(gates)
    dn = (((0,), (0,)), ((), ()))
    gate_ref[...] = (lax.dot_general(hi, ident, dn, preferred_element_type=F32)
                     + lax.dot_general(mid, ident, dn, preferred_element_type=F32)
                     + lax.dot_general(lo, ident, dn, preferred_element_type=F32))


def _out_proj_call(yg, yn, x2, g1, sc2, sh2, ng, wo, rwt, rb, S):
    T, D = x2.shape
    tm = 256
    per_b = S // tm
    row = lambda i: (i, 0)
    bat = lambda i: (i // per_b, 0, 0)
    cst = lambda i: (0, 0)
    return pl.pallas_call(
        _out_proj_kernel,
        grid=(T // tm,),
        in_specs=[pl.BlockSpec((tm, GDN_WIDTH), row), pl.BlockSpec((tm, NSA_WIDTH), row), pl.BlockSpec((tm, D), row),
                  pl.BlockSpec((None, 1, D), bat), pl.BlockSpec((None, 1, D), bat), pl.BlockSpec((None, 1, D), bat),
                  pl.BlockSpec((1, D), cst), pl.BlockSpec((D, D), cst),
                  pl.BlockSpec((N_EXPERTS, D), cst), pl.BlockSpec((N_EXPERTS, 1), cst)],
        out_specs=[pl.BlockSpec((tm, D), row), pl.BlockSpec((tm, D), row), pl.BlockSpec((tm, LANES), row)],
        out_shape=[jax.ShapeDtypeStruct((T, D), F32), jax.ShapeDtypeStruct((T, D), BF16),
                   jax.ShapeDtypeStruct((T, LANES), F32)],
        compiler_params=_cparams("parallel"),
        name="out_proj_router",
    )(yg, yn, x2, g1, sc2, sh2, ng, wo, rwt, rb)


def _moe_kernel(h_ref, gate_ref, x1_ref, g2_ref, wg_ref, wu_ref, wd_ref, o_ref):
    e = pl.program_id(1)
    h = h_ref[...]
    lane = lax.broadcasted_iota(jnp.int32, gate_ref.shape, 1)
    gcol = jnp.sum(jnp.where(lane == e, gate_ref[...], 0.0), axis=1, keepdims=True)
    act = _silu(jnp.dot(h, wg_ref[...], preferred_element_type=F32)) * jnp.dot(h, wu_ref[...],
                                                                              preferred_element_type=F32)
    y = _dot(act * gcol, wd_ref[...])

    @pl.when(e == 0)
    def _():
        o_ref[...] = y

    @pl.when(e > 0)
    def _():
        o_ref[...] += y

    @pl.when(e == N_EXPERTS - 1)
    def _():
        o_ref[...] = x1_ref[...] + g2_ref[...] * o_ref[...]


def _moe_call(h2, gates, x1, g2, wg, wu, wd, S):
    T, D = x1.shape
    tm = 1024
    per_b = S // tm
    row = lambda i, e: (i, 0)
    wsel = lambda i, e: (e, 0, 0)
    return pl.pallas_call(
        _moe_kernel,
        grid=(T // tm, N_EXPERTS),
        in_specs=[pl.BlockSpec((tm, D), row), pl.BlockSpec((tm, LANES), row), pl.BlockSpec((tm, D), row),
                  pl.BlockSpec((None, 1, D), lambda i, e: (i // per_b, 0, 0)),
                  pl.BlockSpec((None, D, EXPERT_FF), wsel), pl.BlockSpec((None, D, EXPERT_FF), wsel),
                  pl.BlockSpec((None, EXPERT_FF, D), wsel)],
        out_specs=pl.BlockSpec((tm, D), row),
        out_shape=jax.ShapeDtypeStruct((T, D), F32),
        compiler_params=_cparams("parallel", "arbitrary"),
        name="moe",
    )(h2, gates, x1, g2, wg, wu, wd)


def kernel(x, c, ada_w, ada_b, norm1_g, norm2_g, w_in, gdn_conv_w, gdn_a_log, gdn_dt_bias, gdn_norm_g,
           nsa_q_norm_g, nsa_k_norm_g, cmp_pe, cmp_w1, cmp_b1, cmp_w2, cmp_b2, w_out, router_w, router_bias,
           exp_w_gate, exp_w_up, exp_w_down):
    B, S, D = x.shape
    L = ada_w.shape[0]
    T = B * S
    G = NSA_KV_HEADS
    NC = S // CMP_STRIDE
    N = S // GDN_CHUNK
    P = GDN_HEADS // GDN_HPB
    W = GDN_HPB * HEAD_DIM

    mod = _ada_call(c, ada_w, ada_b)
    rwt = router_w.T
    rb = router_bias.reshape(N_EXPERTS, 1)
    x2 = x.reshape(T, D)
    for l in range(L):
        m6 = mod[l].reshape(B, 6, 1, D)
        sh1, sc1, g1, sh2, sc2, g2 = (m6[:, i] for i in range(6))
        qkv, z, nq, kv, aexp, bexp, misc, misct = _in_proj_call(
            x2, sc1, sh1, norm1_g[l].reshape(1, D), _pad_w_in(w_in[l]), S)

        a_rows = misc[:, :GDN_HEADS].reshape(B, N, GDN_CHUNK, P, GDN_HPB).transpose(0, 3, 1, 4, 2).reshape(B, P, N, W)
        rep = lambda t: jnp.repeat(t, HEAD_DIM).reshape(1, GDN_WIDTH)
        y_gdn = _gdn_call(qkv, z, aexp, bexp, a_rows, gdn_conv_w[l], rep(gdn_a_log[l]), rep(gdn_dt_bias[l]),
                          jnp.tile(gdn_norm_g[l], GDN_HPB).reshape(1, W), B, S)

        xkv = kv[:, :2 * NSA_KV_WIDTH].reshape(B, S, 2, G, HEAD_DIM).transpose(2, 0, 3, 1, 4)
        xkv = xkv.reshape(2, B * G, NC, CMP_STRIDE * HEAD_DIM)
        w2 = cmp_w2[l]
        kcx, vct = _cmp_call(
            xkv[0], xkv[1], cmp_pe[l].reshape(2, 1, CMP_BLOCK * HEAD_DIM), cmp_w1[l].astype(BF16),
            cmp_b1[l].reshape(2, 1, CMP_HIDDEN), jnp.pad(w2[0], ((0, 0), (0, LANES - HEAD_DIM))).astype(BF16),
            w2[1].T.astype(BF16), jnp.pad(cmp_b2[l, 0], (0, LANES - HEAD_DIM)).reshape(1, LANES),
            cmp_b2[l, 1].reshape(HEAD_DIM, 1), jnp.pad(nsa_k_norm_g[l, 0], (0, LANES - HEAD_DIM)).reshape(1, LANES))
        vsw = jnp.concatenate([kv[:, 3 * NSA_KV_WIDTH:4 * NSA_KV_WIDTH], kv[:, 5 * NSA_KV_WIDTH:]], axis=1)
        vt = vsw.astype(BF16).reshape(B, S // NSA_TK, NSA_TK, 2 * NSA_KV_WIDTH).transpose(0, 1, 3, 2)
        y_nsa_t = _nsa_call(nq, kv, vt, kcx.reshape(B, G, NC, LANES), vct.reshape(B, G, HEAD_DIM, NC), misct,
                            nsa_q_norm_g[l].reshape(1, HEAD_DIM), nsa_k_norm_g[l], B, S)
        y_nsa = y_nsa_t.T

        x1, h2, gates = _out_proj_call(y_gdn, y_nsa, x2, g1, sc2, sh2, norm2_g[l].reshape(1, D),
                                       w_out[l].astype(BF16), rwt, rb, S)
        x2 = _moe_call(h2, gates, x1, g2, exp_w_gate[l].astype(BF16), exp_w_up[l].astype(BF16),
                       exp_w_down[l].astype(BF16), S)
    return x2.reshape(B, S, D)
```

```python
import functools

import numpy as np
import jax
import jax.numpy as jnp
from jax import lax
from jax.experimental import pallas as pl
from jax.experimental.pallas import tpu as pltpu

F32 = jnp.float32
BF16 = jnp.bfloat16

HEAD_DIM = 64
GDN_HEADS = 8
GDN_WIDTH = GDN_HEADS * HEAD_DIM
GDN_CONV = 4
GDN_CHUNK = 64
NSA_HEADS = 8
NSA_KV_HEADS = 2
NSA_GROUP = NSA_HEADS // NSA_KV_HEADS
NSA_WIDTH = NSA_HEADS * HEAD_DIM
NSA_KV_WIDTH = NSA_KV_HEADS * HEAD_DIM
CMP_BLOCK = 32
CMP_STRIDE = 16
CMP_HIDDEN = 256
SLC_BLOCK = 64
SLC_TOPN = 8
WINDOW = 512
N_EXPERTS = 16
EXPERTS_PER_GROUP = 4
EXPERT_FF = 512
RMS_EPS = 1e-6
NEG_BIG = -1e30
SEL_BIG = 1e9

LANES = 128
VMEM_LIMIT_BYTES = 56 * 1024 * 1024

C_QKV = 0
C_Z = 3 * GDN_WIDTH
C_NQ = C_Z + GDN_WIDTH
C_KV = C_NQ + NSA_WIDTH
C_AEXP = C_KV + 6 * NSA_KV_WIDTH
C_BEXP = C_AEXP + GDN_WIDTH
C_MISC = C_BEXP + GDN_WIDTH
IN_PAD_COLS = C_MISC + LANES
MISC_GATE0 = 16


def _cparams(*sem):
    return pltpu.CompilerParams(dimension_semantics=sem, vmem_limit_bytes=VMEM_LIMIT_BYTES)


def _dot(a, b):
    return jnp.dot(a.astype(BF16), b.astype(BF16), preferred_element_type=F32)


def _dot_nt(a, b):
    return lax.dot_general(a.astype(BF16), b.astype(BF16), (((1,), (1,)), ((), ())),
                           preferred_element_type=F32)


def _dot_tn(a, b):
    return lax.dot_general(a.astype(BF16), b.astype(BF16), (((0,), (0,)), ((), ())),
                           preferred_element_type=F32)


def _split2(x):
    hi = x.astype(BF16)
    lo = (x - hi.astype(F32)).astype(BF16)
    return hi, lo


def _split3(x):
    hi = x.astype(BF16)
    r = x - hi.astype(F32)
    mid = r.astype(BF16)
    lo = (r - mid.astype(F32)).astype(BF16)
    return hi, mid, lo


def _dot_hi(a, b):
    ah, al = _split2(a)
    bh, bl = _split2(b)
    return (jnp.dot(ah, bh, preferred_element_type=F32) + jnp.dot(al, bh, preferred_element_type=F32)
            + jnp.dot(ah, bl, preferred_element_type=F32))


def _dot_nt_hi(a, b):
    ah, al = _split2(a)
    bh, bl = _split2(b)
    dn = (((1,), (1,)), ((), ()))
    return (lax.dot_general(ah, bh, dn, preferred_element_type=F32)
            + lax.dot_general(al, bh, dn, preferred_element_type=F32)
            + lax.dot_general(ah, bl, dn, preferred_element_type=F32))


def _dot_exact_rhs(a, b_bf16):
    hi, mid, lo = _split3(a)
    return (jnp.dot(hi, b_bf16, preferred_element_type=F32) + jnp.dot(mid, b_bf16, preferred_element_type=F32)
            + jnp.dot(lo, b_bf16, preferred_element_type=F32))


def _rms(x, g):
    return x * lax.rsqrt(jnp.mean(x * x, axis=-1, keepdims=True) + RMS_EPS) * g


def _sigmoid(x):
    return 0.5 * (jnp.tanh(0.5 * x) + 1.0)


def _silu(x):
    return x * _sigmoid(x)


def _ada_kernel(c_ref, w_ref, b_ref, o_ref):
    o_ref[...] = _dot_hi(_silu(c_ref[...]), w_ref[...]) + b_ref[...]


def _ada_call(c, ada_w, ada_b):
    L, D, N = ada_w.shape
    B = c.shape[0]
    tn = 1024
    return pl.pallas_call(
        _ada_kernel,
        grid=(L, N // tn),
        in_specs=[pl.BlockSpec((B, D), lambda l, j: (0, 0)),
                  pl.BlockSpec((None, D, tn), lambda l, j: (l, 0, j)),
                  pl.BlockSpec((None, 1, tn), lambda l, j: (l, 0, j))],
        out_specs=pl.BlockSpec((None, B, tn), lambda l, j: (l, 0, j)),
        out_shape=jax.ShapeDtypeStruct((L, B, N), F32),
        compiler_params=_cparams("parallel", "parallel"),
        name="ada_mod",
    )(c, ada_w, ada_b.reshape(L, 1, N))


IN_SEGS = ((C_QKV, 3 * GDN_WIDTH), (C_Z, GDN_WIDTH), (C_NQ, NSA_WIDTH), (C_KV, 6 * NSA_KV_WIDTH),
           (C_AEXP, GDN_WIDTH), (C_BEXP, GDN_WIDTH), (C_MISC, LANES))


def _in_proj_kernel(x_ref, sc_ref, sh_ref, g_ref, w_ref, wmt_ref, *o_refs):
    h = _rms(x_ref[...], g_ref[...]) * (1.0 + sc_ref[...]) + sh_ref[...]
    hb = h.astype(BF16)
    for (c0, n), o_ref in zip(IN_SEGS, o_refs[:-1]):
        o_ref[...] = jnp.dot(hb, w_ref[:, c0:c0 + n], preferred_element_type=F32)
    o_refs[-1][...] = lax.dot_general(wmt_ref[...], hb, (((1,), (1,)), ((), ())), preferred_element_type=F32)


def _in_proj_call(x2, sc, sh, g, w_pad, S):
    T, D = x2.shape
    tm = 256
    per_b = S // tm
    row = lambda i: (i, 0)
    bat = lambda i: (i // per_b, 0, 0)
    return pl.pallas_call(
        _in_proj_kernel,
        grid=(T // tm,),
        in_specs=[pl.BlockSpec((tm, D), row),
                  pl.BlockSpec((None, 1, D), bat),
                  pl.BlockSpec((None, 1, D), bat),
                  pl.BlockSpec((1, D), lambda i: (0, 0)),
                  pl.BlockSpec((D, IN_PAD_COLS), lambda i: (0, 0)),
                  pl.BlockSpec((LANES, D), lambda i: (0, 0))],
        out_specs=[pl.BlockSpec((tm, n), row) for _, n in IN_SEGS] + [pl.BlockSpec((LANES, tm), lambda i: (0, i))],
        out_shape=[jax.ShapeDtypeStruct((T, n), F32) for _, n in IN_SEGS] + [jax.ShapeDtypeStruct((LANES, T), F32)],
        compiler_params=_cparams("parallel"),
        name="in_proj",
    )(x2, sc, sh, g, w_pad, w_pad[:, C_MISC:].T)


def _pad_w_in(w_in):
    D = w_in.shape[0]
    W = GDN_WIDTH
    o = 0
    gq, gk, gv, gz = (w_in[:, o + i * W:o + (i + 1) * W] for i in range(4))
    o += 4 * W
    ga = w_in[:, o:o + GDN_HEADS]
    gb = w_in[:, o + GDN_HEADS:o + 2 * GDN_HEADS]
    o += 2 * GDN_HEADS
    nq = w_in[:, o:o + NSA_WIDTH]
    o += NSA_WIDTH
    kv = w_in[:, o:o + 6 * NSA_KV_WIDTH]
    o += 6 * NSA_KV_WIDTH
    gates = w_in[:, o:o + 3 * NSA_HEADS]
    misc = jnp.zeros((D, LANES), w_in.dtype)
    misc = misc.at[:, 0:GDN_HEADS].set(ga).at[:, MISC_GATE0:MISC_GATE0 + 3 * NSA_HEADS].set(gates)
    cols = [gq, gk, gv, gz, nq, kv, jnp.repeat(ga, HEAD_DIM, axis=1), jnp.repeat(gb, HEAD_DIM, axis=1), misc]
    return jnp.concatenate(cols, axis=1).astype(BF16)


GDN_HPB = 2
GDN_NB = 32


def _gdn_kernel(q_ref, k_ref, v_ref, z_ref, a_ref, b_ref, ar_ref, cwq_ref, cwk_ref, cwv_ref,
                alog_ref, dtb_ref, alogr_ref, dtbr_ref, ng_ref, o_ref,
                qs, ks, kbs, xs, gcs, grs, ps, qq, qks, os_, sst, *, S):
    C = GDN_CHUNK
    N = S // C
    W = GDN_HPB * HEAD_DIM
    row = lax.broadcasted_iota(jnp.int32, (S, W), 0)
    row8 = lax.broadcasted_iota(jnp.int32, (8, W), 0)

    def conv_silu(x, w):
        y = x * w[GDN_CONV - 1:GDN_CONV, :]
        for s in range(1, GDN_CONV):
            xs = pltpu.roll(x, s, axis=0)
            xs = jnp.concatenate([jnp.where(row8 >= s, xs[:8], 0.0), xs[8:]], axis=0)
            y = y + xs * w[GDN_CONV - 1 - s:GDN_CONV - s, :]
        return _silu(y)

    bd = jnp.where(lax.broadcasted_iota(jnp.int32, (W, W), 0) // HEAD_DIM
                   == lax.broadcasted_iota(jnp.int32, (W, W), 1) // HEAD_DIM, 1.0, 0.0).astype(BF16)

    def head_sumsq(x):
        hi, lo = _split2(x * x)
        return jnp.dot(hi, bd, preferred_element_type=F32) + jnp.dot(lo, bd, preferred_element_type=F32)

    def head_l2(x):
        return x * lax.rsqrt(head_sumsq(x) + RMS_EPS)

    q = head_l2(conv_silu(q_ref[...], cwq_ref[...])) * (HEAD_DIM ** -0.5)
    k = head_l2(conv_silu(k_ref[...], cwk_ref[...]))
    v = conv_silu(v_ref[...], cwv_ref[...])
    beta = _sigmoid(b_ref[...])

    def log_decay(a, alog, dtb):
        xx = a + dtb
        sp = jnp.maximum(xx, 0.0) + jnp.log(1.0 + jnp.exp(-jnp.abs(xx)))
        return -jnp.exp(alog) * sp

    gc = log_decay(a_ref[...], alog_ref[...], dtb_ref[...])
    pos = row % C
    for s in (1, 2, 4, 8, 16, 32):
        gc = gc + jnp.where(pos >= s, pltpu.roll(gc, s, axis=0), 0.0)
    gr = log_decay(ar_ref[...], alogr_ref[...], dtbr_ref[...])
    lpos = lax.broadcasted_iota(jnp.int32, (N, W), 1) % C
    for s in (1, 2, 4, 8, 16, 32):
        gr = gr + jnp.where(lpos >= s, pltpu.roll(gr, s, axis=1), 0.0)

    kb = k * beta
    vb = v * beta
    kbe = kb * jnp.exp(gc)
    for h in range(GDN_HPB):
        sl = slice(h * HEAD_DIM, (h + 1) * HEAD_DIM)
        qs[h] = q[:, sl]
        ks[h] = k[:, sl]
        kbs[h] = kb[:, sl]
        gcs[h] = gc[:, sl]
        for n in range(N):
            grs[h, n] = jnp.broadcast_to(gr[n:n + 1, sl], (8, HEAD_DIM))
        xs[h] = jnp.concatenate([kbe[:, sl], vb[:, sl]], axis=1)

    NB = GDN_NB
    ci3 = lax.broadcasted_iota(jnp.int32, (1, C, C), 1)
    cj3 = lax.broadcasted_iota(jnp.int32, (1, C, C), 2)
    eye3 = jnp.where(ci3 == cj3, 1.0, 0.0)

    def bmm(a, b):
        return lax.dot_general(a.astype(BF16), b.astype(BF16), (((2,), (1,)), ((0,), (0,))),
                               preferred_element_type=F32)

    def bmm_nt(a, b):
        return lax.dot_general(a.astype(BF16), b.astype(BF16), (((2,), (2,)), ((0,), (0,))),
                               preferred_element_type=F32)

    def intra(nb, carry):
        rows = pl.ds(pl.multiple_of(nb * (NB * C), NB * C), NB * C)
        for h in range(GDN_HPB):
            kk = ks[h, rows, :].reshape(NB, C, HEAD_DIM)
            kbq = jnp.concatenate([kbs[h, rows, :].reshape(NB, C, HEAD_DIM),
                                   qs[h, rows, :].reshape(NB, C, HEAD_DIM)], axis=1)
            a2 = bmm_nt(kbq, kk)
            gcol = gcs[h, rows, :].reshape(NB, C, HEAD_DIM)
            grow = grs[h, pl.ds(nb * NB, NB), 0:1, :]
            decay = jnp.exp(jnp.where(ci3 >= cj3, gcol - grow, NEG_BIG))
            low = jnp.where(ci3 > cj3, a2[:, :C] * decay, 0.0)
            qks[h, rows, :] = (a2[:, C:] * decay).reshape(NB * C, C)
            tinv = eye3 - jnp.where(ci3 // 2 == cj3 // 2, low, 0.0)
            for lv in range(1, 6):
                s = 2 ** lv
                off = jnp.where((ci3 // (2 * s) == cj3 // (2 * s)) & (ci3 // s != cj3 // s), low, 0.0)
                tinv = tinv - bmm(tinv, bmm(off, tinv))
            x = bmm(tinv, xs[h, rows, :].reshape(NB, C, 2 * HEAD_DIM))
            xs[h, rows, :] = x.reshape(NB * C, 2 * HEAD_DIM)
            kd = kk * jnp.exp(gcol[:, C - 1:C, :] - gcol)
            pq = lax.dot_general(kd.astype(BF16), x.astype(BF16), (((1,), (1,)), ((0,), (0,))),
                                 preferred_element_type=F32)
            ps[h, rows, :] = pq[:, :, :HEAD_DIM].reshape(NB * C, HEAD_DIM)
            qq[h, rows, :] = pq[:, :, HEAD_DIM:].reshape(NB * C, HEAD_DIM)
        return carry

    lax.fori_loop(0, N // NB, intra, 0)

    def inter(n, states):
        sl = pl.ds(pl.multiple_of(n * C, C), C)
        new_states = []
        for h in range(GDN_HPB):
            st = states[h]
            sst[h, n] = st
            glast = gcs[h, pl.ds(n * C + C - 1, 1), :]
            new_states.append(st * jnp.exp(glast) - _dot_hi(ps[h, sl, :], st) + qq[h, sl, :])
        return tuple(new_states)

    lax.fori_loop(0, N, inter, tuple(jnp.zeros((HEAD_DIM, HEAD_DIM), F32) for _ in range(GDN_HPB)))

    def outputs(nb, carry):
        rows = pl.ds(pl.multiple_of(nb * (NB * C), NB * C), NB * C)
        for h in range(GDN_HPB):
            st = sst[h, pl.ds(nb * NB, NB)]
            x = xs[h, rows, :].reshape(NB, C, 2 * HEAD_DIM)
            gcol = gcs[h, rows, :].reshape(NB, C, HEAD_DIM)
            qd = qs[h, rows, :].reshape(NB, C, HEAD_DIM) * jnp.exp(gcol)
            t1 = bmm(jnp.concatenate([x[:, :, :HEAD_DIM], qd], axis=1), st)
            vn = x[:, :, HEAD_DIM:] - t1[:, :C]
            o = t1[:, C:] + bmm(qks[h, rows, :].reshape(NB, C, C), vn)
            os_[h, rows, :] = o.reshape(NB * C, HEAD_DIM)
        return carry

    lax.fori_loop(0, N // NB, outputs, 0)

    o = jnp.concatenate([os_[h] for h in range(GDN_HPB)], axis=1)
    o = o * lax.rsqrt(head_sumsq(o) * (1.0 / HEAD_DIM) + RMS_EPS) * ng_ref[...]
    o_ref[...] = o * _silu(z_ref[...])


def _gdn_call(qkv, z, aexp, bexp, a_rows, conv_w, alog_exp, dtb_exp, norm_g, B, S):
    T = B * S
    W = GDN_HPB * HEAD_DIM
    P = GDN_HEADS // GDN_HPB
    N = S // GDN_CHUNK
    col = lambda off: (lambda b, p: (b, off + p))
    par = lambda off: (lambda b, p: (0, off + p))
    hs = lambda n: pltpu.VMEM((GDN_HPB, S, n), F32)
    return pl.pallas_call(
        functools.partial(_gdn_kernel, S=S),
        grid=(B, P),
        in_specs=[pl.BlockSpec((S, W), col(0)), pl.BlockSpec((S, W), col(P)), pl.BlockSpec((S, W), col(2 * P)),
                  pl.BlockSpec((S, W), col(0)), pl.BlockSpec((S, W), col(0)), pl.BlockSpec((S, W), col(0)),
                  pl.BlockSpec((None, None, N, W), lambda b, p: (b, p, 0, 0)),
                  pl.BlockSpec((GDN_CONV, W), par(0)), pl.BlockSpec((GDN_CONV, W), par(P)),
                  pl.BlockSpec((GDN_CONV, W), par(2 * P)),
                  pl.BlockSpec((1, W), par(0)), pl.BlockSpec((1, W), par(0)),
                  pl.BlockSpec((1, W), par(0)), pl.BlockSpec((1, W), par(0)),
                  pl.BlockSpec((1, W), lambda b, p: (0, 0))],
        out_specs=pl.BlockSpec((S, W), col(0)),
        out_shape=jax.ShapeDtypeStruct((T, GDN_WIDTH), F32),
        scratch_shapes=[hs(HEAD_DIM), hs(HEAD_DIM), hs(HEAD_DIM), hs(2 * HEAD_DIM), hs(HEAD_DIM),
                        pltpu.VMEM((GDN_HPB, N, 8, HEAD_DIM), F32),
                        hs(HEAD_DIM), hs(HEAD_DIM), hs(HEAD_DIM), hs(HEAD_DIM),
                        pltpu.VMEM((GDN_HPB, N, HEAD_DIM, HEAD_DIM), F32)],
        compiler_params=_cparams("parallel", "parallel"),
        name="gdn",
    )(qkv, qkv, qkv, z, aexp, bexp, a_rows, conv_w, conv_w, conv_w,
      alog_exp, dtb_exp, alog_exp, dtb_exp, norm_g)


def _cmp_kernel(xk_ref, xv_ref, pe_ref, w1_ref, b1_ref, w2k_ref, w2vt_ref, b2k_ref, b2v_ref, kg_ref,
                kc_ref, vct_ref):
    half = (CMP_BLOCK // 2) * HEAD_DIM
    nrows = xk_ref.shape[0]

    def hidden(x, j):
        pe = pe_ref[j]
        p = _dot(x + pe[:, :half], w1_ref[j, :half, :])
        q = _dot(x + pe[:, half:], w1_ref[j, half:, :])
        return _silu(p + pltpu.roll(q, nrows - 1, axis=0) + b1_ref[j])

    kc = _dot(hidden(xk_ref[...], 0), w2k_ref[...]) + b2k_ref[...]
    ms = jnp.sum(kc * kc, axis=-1, keepdims=True) * (1.0 / HEAD_DIM)
    kcn = kc * lax.rsqrt(ms + RMS_EPS) * kg_ref[...]
    lane = lax.broadcasted_iota(jnp.int32, kc.shape, 1)
    cend = lax.broadcasted_iota(jnp.int32, kc.shape, 0) * CMP_STRIDE + (CMP_BLOCK - 1)
    ext = jnp.where(lane == HEAD_DIM, cend // LANES, jnp.where(lane == HEAD_DIM + 1, cend % LANES, 0)).astype(F32)
    kc_ref[...] = jnp.where(lane < HEAD_DIM, kcn, ext)
    vct_ref[...] = _dot_nt(w2vt_ref[...], hidden(xv_ref[...], 1)) + b2v_ref[...]


def _cmp_call(xk, xv, pe, w1, b1, w2k, w2vt, b2k, b2v, kg):
    BG, NC, half = xk.shape
    blk = pl.BlockSpec((None, NC, half), lambda i: (i, 0, 0))
    full = lambda a: pl.BlockSpec(a.shape, lambda i: (0,) * a.ndim)
    return pl.pallas_call(
        _cmp_kernel,
        grid=(BG,),
        in_specs=[blk, blk] + [full(a) for a in (pe, w1, b1, w2k, w2vt, b2k, b2v, kg)],
        out_specs=[pl.BlockSpec((None, NC, LANES), lambda i: (i, 0, 0)),
                   pl.BlockSpec((None, HEAD_DIM, NC), lambda i: (i, 0, 0))],
        out_shape=[jax.ShapeDtypeStruct((BG, NC, LANES), F32), jax.ShapeDtypeStruct((BG, HEAD_DIM, NC), F32)],
        compiler_params=_cparams("parallel"),
        name="nsa_compress",
    )(xk, xv, pe, w1, b1, w2k, w2vt, b2k, b2v, kg)


NSA_TQ = 128
NSA_TK = 128
NSA_M = NSA_GROUP * NSA_TQ
NSA_MS = NSA_M
MASK_BIG = 2.0 ** 100
SEL_LANE0 = HEAD_DIM + 2


def _nsa_kernel(q_ref, kv_ref, vt_ref, kcx_ref, vct_ref, misct_ref, cbias_ref, tbl_ref, qg_ref, kg_ref,
                ovt_ref, ish_ref, o_ref, ks_s, kw_s, *, S):
    tq, tk, M = NSA_TQ, NSA_TK, NSA_M
    MS = NSA_MS
    G, R = NSA_KV_HEADS, NSA_GROUP
    NS = S // SLC_BLOCK
    n_top = min(SLC_TOPN, NS)
    qi = pl.program_id(1)
    t0 = qi * tq
    nt_dims = (((1,), (1,)), ((), ()))

    @pl.when(qi == 0)
    def _():
        rowi = lax.broadcasted_iota(jnp.int32, (S, LANES), 0)
        lane = lax.broadcasted_iota(jnp.int32, (S, LANES), 1)
        ext = jnp.where(lane == HEAD_DIM, rowi // LANES, jnp.where(lane == HEAD_DIM + 1, rowi % LANES, 0)).astype(F32)
        ext_slc = jnp.where(lane - SEL_LANE0 == rowi // SLC_BLOCK, -MASK_BIG, ext)

        def prep(c0, g, gain, extra):
            kraw = kv_ref[:, c0:c0 + LANES]
            if g == 1:
                kraw = pltpu.roll(kraw, HEAD_DIM, axis=1)
            ms = jnp.sum(jnp.where(lane < HEAD_DIM, kraw * kraw, 0.0), axis=-1, keepdims=True) * (1.0 / HEAD_DIM)
            return jnp.where(lane < HEAD_DIM, kraw * lax.rsqrt(ms + RMS_EPS) * gain, extra).astype(BF16)

        for g in range(G):
            ks_s[g] = prep(2 * NSA_KV_WIDTH, g, kg_ref[1:2, :], ext_slc)
            kw_s[g] = prep(4 * NSA_KV_WIDTH, g, kg_ref[2:3, :], ext)

    lane_q = lax.broadcasted_iota(jnp.int32, (tq, LANES), 1)
    gate_sig = _sigmoid(misct_ref[MISC_GATE0:MISC_GATE0 + 3 * NSA_HEADS, :])
    kt_d = t0 // tk

    def score_tiles(streams, kt, biased):
        rows = pl.ds(pl.multiple_of(kt * tk, tk), tk)
        out = []
        for qpp, k_s, g, _, idx_fn in streams:
            s = lax.dot_general(k_s[g, rows, :], qpp, nt_dims, preferred_element_type=F32)
            out.append(s + tbl_ref[idx_fn(kt)] if biased else s)
        return tuple(out)

    def softmax_pv(streams, kt, scores, states):
        soft = []
        for s, (m, l, _) in zip(scores, states):
            m_new = jnp.maximum(m, jnp.max(s, axis=0, keepdims=True))
            alpha = jnp.exp(m - m_new)
            p = jnp.exp(s - m_new)
            soft.append((m_new, alpha, alpha * l + jnp.sum(p, axis=0, keepdims=True), p.astype(BF16)))
        pvs = [jnp.dot(vt_ref[kt, st[3]:st[3] + HEAD_DIM, :], sf[3], preferred_element_type=F32)
               for st, sf in zip(streams, soft)]
        return tuple((m_new, l, alpha * acc + pv)
                     for (m_new, alpha, l, _), (_, _, acc), pv in zip(soft, states, pvs))

    flash_init = (jnp.full((1, M), NEG_BIG, F32), jnp.zeros((1, M), F32), jnp.zeros((HEAD_DIM, M), F32))

    qpps, o_cmps = [], []
    for g in range(G):
        qrows = []
        for r in range(R):
            hh = g * R + r
            qpair = q_ref[:, (hh // 2) * LANES:(hh // 2 + 1) * LANES]
            if hh % 2 == 1:
                qpair = pltpu.roll(qpair, HEAD_DIM, axis=1)
            ms = jnp.sum(jnp.where(lane_q < HEAD_DIM, qpair * qpair, 0.0), axis=-1, keepdims=True) * (1.0 / HEAD_DIM)
            qn = qpair * lax.rsqrt(ms + RMS_EPS) * qg_ref[...] * (HEAD_DIM ** -0.5)
            slope = 2.0 ** (-8.0 * (hh + 1) / NSA_HEADS)
            ex = jnp.where(lane_q == HEAD_DIM, slope * LANES, jnp.where(lane_q == HEAD_DIM + 1, slope, 0.0))
            qrows.append(jnp.where(lane_q < HEAD_DIM, qn, ex))
        qbase = jnp.concatenate(qrows, axis=0)

        cb = cbias_ref[...]
        s = (lax.dot_general(kcx_ref[g].astype(BF16), qbase.astype(BF16), nt_dims, preferred_element_type=F32)
             + jnp.concatenate([cb] * R, axis=1))
        mx = jnp.maximum(jnp.max(s, axis=0, keepdims=True), NEG_BIG)
        e = jnp.exp(s - mx)
        l = jnp.sum(e, axis=0, keepdims=True)
        p_cmp = e / jnp.where(l > 0.0, l, 1.0)
        o_cmps.append(_dot(vct_ref[g], p_cmp))

        psum = p_cmp[:, 0:tq]
        for r in range(1, R):
            psum = psum + p_cmp[:, r * tq:(r + 1) * tq]
        hi, mid, lo = _split3(psum)
        ovt = ovt_ref[...]
        imp = (jnp.dot(ovt, hi, preferred_element_type=F32) + jnp.dot(ovt, mid, preferred_element_type=F32)
               + jnp.dot(ovt, lo, preferred_element_type=F32))
        blk = lax.broadcasted_iota(jnp.int32, (NS, tq), 0)
        cur = (t0 + lax.broadcasted_iota(jnp.int32, (NS, tq), 1)) // SLC_BLOCK
        valid = blk <= cur
        forced = (blk == 0) | (blk == cur) | (blk == cur - 1)
        score = jnp.where(forced, SEL_BIG, jnp.where(valid, imp, -SEL_BIG))
        rank = jnp.zeros((NS, tq), F32)
        for j in range(NS):
            cj = score[j:j + 1, :]
            beats = (cj > score) | ((cj == score) & (blk > j))
            rank = rank + jnp.where(beats, 1.0, 0.0)
        nsel = jnp.where((rank < n_top) & valid, 0.0, 1.0).astype(BF16)
        nsel_q = lax.dot_general(nsel, ish_ref[...], (((0,), (0,)), ((), ())),
                                 preferred_element_type=F32)
        in_sel = (lane_q >= SEL_LANE0) & (lane_q < SEL_LANE0 + NS)
        qpps.append(jnp.concatenate([jnp.where(in_sel, nsel_q, qrows[r]) for r in range(R)], axis=0).astype(BF16))

    slc_idx = lambda kt: jnp.where(kt == kt_d, 1, 0)
    win_lo = kt_d - WINDOW // tk
    win_idx = lambda kt: jnp.where(kt == kt_d, 1, jnp.where(kt == win_lo, 2, 0))
    slc_streams = [(qpps[g], ks_s, g, g * HEAD_DIM, slc_idx) for g in range(G)]
    win_streams = [(qpps[g], kw_s, g, (G + g) * HEAD_DIM, win_idx) for g in range(G)]
    kt_w = jnp.maximum(win_lo, 0)
    def slc_only(kt, carry):
        scores, states = carry
        nxt = score_tiles(slc_streams, jnp.minimum(kt + 1, kt_w), False)
        return nxt, softmax_pv(slc_streams, kt, scores, states)

    _, st_slc = lax.fori_loop(0, kt_w, slc_only, (score_tiles(slc_streams, 0, False), (flash_init,) * G))
    both = slc_streams + win_streams
    st_all = lax.fori_loop(kt_w, kt_d + 1, lambda kt, st: softmax_pv(both, kt, score_tiles(both, kt, True), st),
                           st_slc + (flash_init,) * G)
    o_all = [acc / jnp.where(l > 0.0, l, 1.0) for _, l, acc in st_all]
    o_slcs, o_wins = o_all[:G], o_all[G:]

    for g in range(G):
        for r in range(R):
            hh = g * R + r
            cs = slice(r * tq, (r + 1) * tq)
            o_ref[hh * HEAD_DIM:(hh + 1) * HEAD_DIM, :] = (
                gate_sig[hh:hh + 1, :] * o_cmps[g][:, cs]
                + gate_sig[NSA_HEADS + hh:NSA_HEADS + hh + 1, :] * o_slcs[g][:, cs]
                + gate_sig[2 * NSA_HEADS + hh:2 * NSA_HEADS + hh + 1, :] * o_wins[g][:, cs])


def _nsa_consts(S):
    tq, tk, M = NSA_TQ, NSA_TK, NSA_M
    NC = S // CMP_STRIDE
    NS = S // SLC_BLOCK
    cs = np.arange(NC)[:, None] * CMP_STRIDE
    ss = np.arange(NS)[None, :] * SLC_BLOCK
    ov = np.clip(np.minimum(cs + CMP_BLOCK, ss + SLC_BLOCK) - np.maximum(cs, ss), 0, None) / CMP_BLOCK
    ov[NC - 1:] = 0.0
    ish = np.zeros((NS, LANES), np.float32)
    ish[np.arange(NS), SEL_LANE0 + np.arange(NS)] = 1.0
    t = np.arange(S).reshape(S // tq, 1, tq)
    n = np.arange(NC).reshape(1, NC, 1)
    cbias = np.where((t >= n * CMP_STRIDE + CMP_BLOCK - 1) & (n < NC - 1), 0.0, -MASK_BIG).astype(np.float32)
    i = np.arange(tk)[:, None]
    j = np.arange(M)[None, :] % tq
    tbl = np.stack([np.zeros((tk, M)), np.where(i <= j, 0.0, -MASK_BIG), np.where(i > j, 0.0, -MASK_BIG)])
    return (jnp.asarray(ov.T, BF16), jnp.asarray(ish, BF16), jnp.asarray(cbias), jnp.asarray(tbl, F32))


def _nsa_call(nq, kv, vt, kcx, vct, misct, q_norm_g, k_norm_g, B, S):
    T = B * S
    tq, tk, M = NSA_TQ, NSA_TK, NSA_M
    G = NSA_KV_HEADS
    nqt = S // tq
    NC = S // CMP_STRIDE
    NS = S // SLC_BLOCK
    ovt, ish, cbias, tbl = _nsa_consts(S)
    tile2 = lambda a: jnp.concatenate([a, a], axis=-1)
    row = lambda b, i: (b * nqt + i, 0)
    cst = lambda b, i: (0, 0)
    return pl.pallas_call(
        functools.partial(_nsa_kernel, S=S),
        grid=(B, nqt),
        in_specs=[pl.BlockSpec((tq, NSA_WIDTH), row),
                  pl.BlockSpec((S, 6 * NSA_KV_WIDTH), lambda b, i: (b, 0)),
                  pl.BlockSpec((None, S // tk, 2 * NSA_KV_WIDTH, tk), lambda b, i: (b, 0, 0, 0)),
                  pl.BlockSpec((None, G, NC, LANES), lambda b, i: (b, 0, 0, 0)),
                  pl.BlockSpec((None, G, HEAD_DIM, NC), lambda b, i: (b, 0, 0, 0)),
                  pl.BlockSpec((LANES, tq), lambda b, i: (0, b * nqt + i)),
                  pl.BlockSpec((None, NC, tq), lambda b, i: (i, 0, 0)),
                  pl.BlockSpec((3, tk, M), lambda b, i: (0, 0, 0)),
                  pl.BlockSpec((1, LANES), cst),
                  pl.BlockSpec((3, LANES), cst),
                  pl.BlockSpec((NS, NC), cst),
                  pl.BlockSpec((NS, LANES), cst)],
        out_specs=pl.BlockSpec((NSA_WIDTH, tq), lambda b, i: (0, b * nqt + i)),
        out_shape=jax.ShapeDtypeStruct((NSA_WIDTH, T), F32),
        scratch_shapes=[pltpu.VMEM((G, S, LANES), BF16), pltpu.VMEM((G, S, LANES), BF16)],
        compiler_params=_cparams("parallel", "arbitrary"),
        name="nsa_attn",
    )(nq, kv, vt, kcx, vct, misct, cbias, tbl, tile2(q_norm_g), tile2(k_norm_g), ovt, ish)


def _out_proj_kernel(yg_ref, yn_ref, x_ref, g1_ref, sc_ref, sh_ref, ng_ref, wo_ref, rwt_ref, rb_ref,
                     x1_ref, h2_ref, gate_ref, gate_t_ref):
    mix = _dot(yg_ref[...], wo_ref[:GDN_WIDTH, :]) + _dot(yn_ref[...], wo_ref[GDN_WIDTH:, :])
    x1 = x_ref[...] + g1_ref[...] * mix
    x1_ref[...] = x1
    h2 = _rms(x1, ng_ref[...]) * (1.0 + sc_ref[...]) + sh_ref[...]
    h2_ref[...] = h2.astype(BF16)

    tm = h2.shape[0]
    logits = _dot_nt_hi(rwt_ref[...], h2)
    score = _sigmoid(logits)
    biased = score + rb_ref[...]
    b = [biased[e:e + 1, :] for e in range(N_EXPERTS)]
    n_groups = N_EXPERTS // EXPERTS_PER_GROUP
    gscore = []
    for gi in range(n_groups):
        vals = b[gi * EXPERTS_PER_GROUP:(gi + 1) * EXPERTS_PER_GROUP]
        best = None
        for i in range(EXPERTS_PER_GROUP):
            for j in range(i + 1, EXPERTS_PER_GROUP):
                pair = vals[i] + vals[j]
                best = pair if best is None else jnp.maximum(best, pair)
        gscore.append(best)
    gbest = jnp.zeros((1, tm), jnp.int32)
    top = gscore[0]
    for gi in range(1, n_groups):
        better = gscore[gi] > top
        gbest = jnp.where(better, gi, gbest)
        top = jnp.where(better, gscore[gi], top)
    erow = lax.broadcasted_iota(jnp.int32, (N_EXPERTS, tm), 0)
    gates = jnp.zeros((N_EXPERTS, tm), F32)
    for e in range(N_EXPERTS):
        gi = e // EXPERTS_PER_GROUP
        rank = jnp.zeros((1, tm), F32)
        for e2 in range(gi * EXPERTS_PER_GROUP, (gi + 1) * EXPERTS_PER_GROUP):
            if e2 == e:
                continue
            beats = (b[e2] > b[e]) | ((b[e2] == b[e]) & (e2 < e))
            rank = rank + jnp.where(beats, 1.0, 0.0)
        chosen = (gbest == gi) & (rank < 2.0)
        ge = jnp.where(chosen, score[e:e + 1, :], 0.0)
        gates = gates + jnp.where(erow == e, ge, 0.0)
    gates = gates / jnp.sum(gates, axis=0, keepdims=True)
    pad_rows = jnp.zeros((8 - EXPERTS_PER_GROUP, tm), F32)
    for gi in range(n_groups):
        gate_t_ref[gi] = jnp.concatenate(
            [gates[gi * EXPERTS_PER_GROUP:(gi + 1) * EXPERTS_PER_GROUP, :], pad_rows], axis=0)
    ident = (lax.broadcasted_iota(jnp.int32, (N_EXPERTS, LANES), 0)
             == lax.broadcasted_iota(jnp.int32, (N_EXPERTS, LANES), 1))
    ident = jnp.where(ident, 1.0, 0.0).astype(BF16)
    hi, mid, lo = _split3(gates)
    dn = (((0,), (0,)), ((), ()))
    gate_ref[...] = (lax.dot_general(hi, ident, dn, preferred_element_type=F32)
                     + lax.dot_general(mid, ident, dn, preferred_element_type=F32)
                     + lax.dot_general(lo, ident, dn, preferred_element_type=F32))


def _out_proj_call(yg, yn, x2, g1, sc2, sh2, ng, wo, rwt, rb, S):
    T, D = x2.shape
    tm = 256
    per_b = S // tm
    row = lambda i: (i, 0)
    bat = lambda i: (i // per_b, 0, 0)
    cst = lambda i: (0, 0)
    return pl.pallas_call(
        _out_proj_kernel,
        grid=(T // tm,),
        in_specs=[pl.BlockSpec((tm, GDN_WIDTH), row), pl.BlockSpec((tm, NSA_WIDTH), row), pl.BlockSpec((tm, D), row),
                  pl.BlockSpec((None, 1, D), bat), pl.BlockSpec((None, 1, D), bat), pl.BlockSpec((None, 1, D), bat),
                  pl.BlockSpec((1, D), cst), pl.BlockSpec((D, D), cst),
                  pl.BlockSpec((N_EXPERTS, D), cst), pl.BlockSpec((N_EXPERTS, 1), cst)],
        out_specs=[pl.BlockSpec((tm, D), row), pl.BlockSpec((tm, D), row), pl.BlockSpec((tm, LANES), row),
                   pl.BlockSpec((N_EXPERTS // EXPERTS_PER_GROUP, 8, tm), lambda i: (0, 0, i))],
        out_shape=[jax.ShapeDtypeStruct((T, D), F32), jax.ShapeDtypeStruct((T, D), BF16),
                   jax.ShapeDtypeStruct((T, LANES), F32),
                   jax.ShapeDtypeStruct((N_EXPERTS // EXPERTS_PER_GROUP, 8, T), F32)],
        compiler_params=_cparams("parallel"),
        name="out_proj_router",
    )(yg, yn, x2, g1, sc2, sh2, ng, wo, rwt, rb)


MOE_TM = 1024
MOE_RB = 288


def _moe_kernel(h_ref, gates_ref, gt8_ref, tri_ref, trit_ref, x1_ref, g2_ref, wg_ref, wu_ref, wd_ref, o_ref):
    g = pl.program_id(1)
    tm, rb = MOE_TM, MOE_RB
    nlb = tm // LANES

    @pl.when(g == 0)
    def _():
        o_ref[...] = jnp.zeros_like(o_ref)

    gt8 = gt8_ref[...]
    in_row = jnp.sum(gt8, axis=0, keepdims=True) > 0.0
    first = lax.broadcasted_iota(jnp.int32, (8, LANES), 0) == 0
    off = jnp.zeros((1, 1), F32)
    keys = []
    for b in range(nlb):
        blk = in_row[:, b * LANES:(b + 1) * LANES]
        one = jnp.where(blk & first, 1.0, 0.0).astype(BF16)
        within = jnp.dot(one, tri_ref[...], preferred_element_type=F32)[0:1, :]
        keys.append(jnp.where(blk, within + off, -1.0))
        off = off + jnp.sum(jnp.where(blk, 1.0, 0.0), axis=1, keepdims=True)
    key_row = jnp.concatenate(keys, axis=1)
    n_g = jnp.sum(off).astype(jnp.int32)
    lane = lax.broadcasted_iota(jnp.int32, (tm, LANES), 1)
    mine = (lane >= g * EXPERTS_PER_GROUP) & (lane < (g + 1) * EXPERTS_PER_GROUP)
    in_col = jnp.sum(jnp.where(mine, gates_ref[...], 0.0), axis=1, keepdims=True) > 0.0
    off = jnp.zeros((1, 1), F32)
    keys = []
    for b in range(nlb):
        blk = in_col[b * LANES:(b + 1) * LANES, :]
        one = jnp.broadcast_to(jnp.where(blk, 1.0, 0.0), (LANES, LANES)).astype(BF16)
        within = jnp.dot(trit_ref[...], one, preferred_element_type=F32)[:, 0:1]
        keys.append(jnp.where(blk, within + off, -1.0))
        off = off + jnp.sum(jnp.where(blk, 1.0, 0.0), axis=0, keepdims=True)
    key_col = jnp.concatenate(keys, axis=0)

    gt_hi, gt_mid, gt_lo = _split3(gt8)
    nt_dims = (((1,), (1,)), ((), ()))

    def block(j, carry):
        r0 = (j * rb).astype(F32)
        sel = jnp.where(key_row - r0 == lax.broadcasted_iota(jnp.int32, (rb, tm), 0).astype(F32), 1.0, 0.0)
        sel = sel.astype(BF16)
        xb = jnp.dot(sel, h_ref[...], preferred_element_type=F32).astype(BF16)
        gb = (lax.dot_general(sel, gt_hi, nt_dims, preferred_element_type=F32)
              + lax.dot_general(sel, gt_mid, nt_dims, preferred_element_type=F32)
              + lax.dot_general(sel, gt_lo, nt_dims, preferred_element_type=F32))
        y = jnp.zeros((rb, o_ref.shape[1]), F32)
        for k in range(EXPERTS_PER_GROUP):
            act = (_silu(jnp.dot(xb, wg_ref[k], preferred_element_type=F32))
                   * jnp.dot(xb, wu_ref[k], preferred_element_type=F32))
            y = y + _dot(act * gb[:, k:k + 1], wd_ref[k])
        sel_t = jnp.where(key_col - r0 == lax.broadcasted_iota(jnp.int32, (tm, rb), 1).astype(F32), 1.0, 0.0)
        o_ref[...] += jnp.dot(sel_t.astype(BF16), y.astype(BF16), preferred_element_type=F32)
        return carry

    lax.fori_loop(0, (n_g + rb - 1) // rb, block, 0)

    @pl.when(g == N_EXPERTS // EXPERTS_PER_GROUP - 1)
    def _():
        o_ref[...] = x1_ref[...] + g2_ref[...] * o_ref[...]


def _moe_call(h2, gates, gates_t, x1, g2, wg, wu, wd, S):
    T, D = x1.shape
    tm = MOE_TM
    per_b = S // tm
    n_groups = N_EXPERTS // EXPERTS_PER_GROUP
    i = np.arange(LANES)
    tri = jnp.asarray(i[:, None] < i[None, :], BF16)
    row = lambda i, g: (i, 0)
    cst = lambda i, g: (0, 0)
    wsel = lambda i, g: (g, 0, 0)
    return pl.pallas_call(
        _moe_kernel,
        grid=(T // tm, n_groups),
        in_specs=[pl.BlockSpec((tm, D), row), pl.BlockSpec((tm, LANES), row),
                  pl.BlockSpec((None, 8, tm), lambda i, g: (g, 0, i)),
                  pl.BlockSpec((LANES, LANES), cst), pl.BlockSpec((LANES, LANES), cst),
                  pl.BlockSpec((tm, D), row),
                  pl.BlockSpec((None, 1, D), lambda i, g: (i // per_b, 0, 0)),
                  pl.BlockSpec((EXPERTS_PER_GROUP, D, EXPERT_FF), wsel),
                  pl.BlockSpec((EXPERTS_PER_GROUP, D, EXPERT_FF), wsel),
                  pl.BlockSpec((EXPERTS_PER_GROUP, EXPERT_FF, D), wsel)],
        out_specs=pl.BlockSpec((tm, D), row),
        out_shape=jax.ShapeDtypeStruct((T, D), F32),
        compiler_params=_cparams("parallel", "arbitrary"),
        name="moe",
    )(h2, gates, gates_t, tri, tri.T, x1, g2, wg, wu, wd)


def kernel(x, c, ada_w, ada_b, norm1_g, norm2_g, w_in, gdn_conv_w, gdn_a_log, gdn_dt_bias, gdn_norm_g,
           nsa_q_norm_g, nsa_k_norm_g, cmp_pe, cmp_w1, cmp_b1, cmp_w2, cmp_b2, w_out, router_w, router_bias,
           exp_w_gate, exp_w_up, exp_w_down):
    B, S, D = x.shape
    L = ada_w.shape[0]
    T = B * S
    G = NSA_KV_HEADS
    NC = S // CMP_STRIDE
    N = S // GDN_CHUNK
    P = GDN_HEADS // GDN_HPB
    W = GDN_HPB * HEAD_DIM

    mod = _ada_call(c, ada_w, ada_b)
    rwt = router_w.T
    rb = router_bias.reshape(N_EXPERTS, 1)
    x2 = x.reshape(T, D)
    for l in range(L):
        m6 = mod[l].reshape(B, 6, 1, D)
        sh1, sc1, g1, sh2, sc2, g2 = (m6[:, i] for i in range(6))
        qkv, z, nq, kv, aexp, bexp, misc, misct = _in_proj_call(
            x2, sc1, sh1, norm1_g[l].reshape(1, D), _pad_w_in(w_in[l]), S)

        a_rows = misc[:, :GDN_HEADS].reshape(B, N, GDN_CHUNK, P, GDN_HPB).transpose(0, 3, 1, 4, 2).reshape(B, P, N, W)
        rep = lambda t: jnp.repeat(t, HEAD_DIM).reshape(1, GDN_WIDTH)
        y_gdn = _gdn_call(qkv, z, aexp, bexp, a_rows, gdn_conv_w[l], rep(gdn_a_log[l]), rep(gdn_dt_bias[l]),
                          jnp.tile(gdn_norm_g[l], GDN_HPB).reshape(1, W), B, S)

        xkv = kv[:, :2 * NSA_KV_WIDTH].reshape(B, S, 2, G, HEAD_DIM).transpose(2, 0, 3, 1, 4)
        xkv = xkv.reshape(2, B * G, NC, CMP_STRIDE * HEAD_DIM)
        w2 = cmp_w2[l]
        kcx, vct = _cmp_call(
            xkv[0], xkv[1], cmp_pe[l].reshape(2, 1, CMP_BLOCK * HEAD_DIM), cmp_w1[l].astype(BF16),
            cmp_b1[l].reshape(2, 1, CMP_HIDDEN), jnp.pad(w2[0], ((0, 0), (0, LANES - HEAD_DIM))).astype(BF16),
            w2[1].T.astype(BF16), jnp.pad(cmp_b2[l, 0], (0, LANES - HEAD_DIM)).reshape(1, LANES),
            cmp_b2[l, 1].reshape(HEAD_DIM, 1), jnp.pad(nsa_k_norm_g[l, 0], (0, LANES - HEAD_DIM)).reshape(1, LANES))
        vsw = jnp.concatenate([kv[:, 3 * NSA_KV_WIDTH:4 * NSA_KV_WIDTH], kv[:, 5 * NSA_KV_WIDTH:]], axis=1)
        vt = vsw.astype(BF16).reshape(B, S // NSA_TK, NSA_TK, 2 * NSA_KV_WIDTH).transpose(0, 1, 3, 2)
        y_nsa_t = _nsa_call(nq, kv, vt, kcx.reshape(B, G, NC, LANES), vct.reshape(B, G, HEAD_DIM, NC), misct,
                            nsa_q_norm_g[l].reshape(1, HEAD_DIM), nsa_k_norm_g[l], B, S)
        y_nsa = y_nsa_t.T

        x1, h2, gates, gates_t = _out_proj_call(y_gdn, y_nsa, x2, g1, sc2, sh2, norm2_g[l].reshape(1, D),
                                       w_out[l].astype(BF16), rwt, rb, S)
        x2 = _moe_call(h2, gates, gates_t, x1, g2, exp_w_gate[l].astype(BF16), exp_w_up[l].astype(BF16),
                       exp_w_down[l].astype(BF16), S)
    return x2.reshape(B, S, D)
```

```python
import functools

import numpy as np
import jax
import jax.numpy as jnp
from jax import lax
from jax.experimental import pallas as pl
from jax.experimental.pallas import tpu as pltpu

F32 = jnp.float32
BF16 = jnp.bfloat16

HEAD_DIM = 64
GDN_HEADS = 8
GDN_WIDTH = GDN_HEADS * HEAD_DIM
GDN_CONV = 4
GDN_CHUNK = 64
NSA_HEADS = 8
NSA_KV_HEADS = 2
NSA_GROUP = NSA_HEADS // NSA_KV_HEADS
NSA_WIDTH = NSA_HEADS * HEAD_DIM
NSA_KV_WIDTH = NSA_KV_HEADS * HEAD_DIM
CMP_BLOCK = 32
CMP_STRIDE = 16
CMP_HIDDEN = 256
SLC_BLOCK = 64
SLC_TOPN = 8
WINDOW = 512
N_EXPERTS = 16
EXPERTS_PER_GROUP = 4
EXPERT_FF = 512
RMS_EPS = 1e-6
NEG_BIG = -1e30
SEL_BIG = 1e9

LANES = 128
VMEM_LIMIT_BYTES = 56 * 1024 * 1024

C_QKV = 0
C_Z = 3 * GDN_WIDTH
C_NQ = C_Z + GDN_WIDTH
C_KV = C_NQ + NSA_WIDTH
C_MISC = C_KV + 6 * NSA_KV_WIDTH
IN_PAD_COLS = C_MISC + LANES
MISC_B0 = GDN_HEADS
MISC_GATE0 = 16


def _cparams(*sem):
    return pltpu.CompilerParams(dimension_semantics=sem, vmem_limit_bytes=VMEM_LIMIT_BYTES)


def _dot(a, b):
    return jnp.dot(a.astype(BF16), b.astype(BF16), preferred_element_type=F32)


def _dot_nt(a, b):
    return lax.dot_general(a.astype(BF16), b.astype(BF16), (((1,), (1,)), ((), ())),
                           preferred_element_type=F32)


def _dot_tn(a, b):
    return lax.dot_general(a.astype(BF16), b.astype(BF16), (((0,), (0,)), ((), ())),
                           preferred_element_type=F32)


def _split2(x):
    hi = x.astype(BF16)
    lo = (x - hi.astype(F32)).astype(BF16)
    return hi, lo


def _split3(x):
    hi = x.astype(BF16)
    r = x - hi.astype(F32)
    mid = r.astype(BF16)
    lo = (r - mid.astype(F32)).astype(BF16)
    return hi, mid, lo


def _dot_hi(a, b):
    ah, al = _split2(a)
    bh, bl = _split2(b)
    return (jnp.dot(ah, bh, preferred_element_type=F32) + jnp.dot(al, bh, preferred_element_type=F32)
            + jnp.dot(ah, bl, preferred_element_type=F32))


def _dot_nt_hi(a, b):
    ah, al = _split2(a)
    bh, bl = _split2(b)
    dn = (((1,), (1,)), ((), ()))
    return (lax.dot_general(ah, bh, dn, preferred_element_type=F32)
            + lax.dot_general(al, bh, dn, preferred_element_type=F32)
            + lax.dot_general(ah, bl, dn, preferred_element_type=F32))


def _dot_exact_rhs(a, b_bf16):
    hi, mid, lo = _split3(a)
    return (jnp.dot(hi, b_bf16, preferred_element_type=F32) + jnp.dot(mid, b_bf16, preferred_element_type=F32)
            + jnp.dot(lo, b_bf16, preferred_element_type=F32))


def _rms(x, g):
    return x * lax.rsqrt(jnp.mean(x * x, axis=-1, keepdims=True) + RMS_EPS) * g


def _sigmoid(x):
    return 0.5 * (jnp.tanh(0.5 * x) + 1.0)


def _silu(x):
    return x * _sigmoid(x)


def _ada_kernel(c_ref, w_ref, b_ref, o_ref):
    o_ref[...] = _dot_hi(_silu(c_ref[...]), w_ref[...]) + b_ref[...]


def _ada_call(c, ada_w, ada_b):
    L, D, N = ada_w.shape
    B = c.shape[0]
    tn = 1024
    return pl.pallas_call(
        _ada_kernel,
        grid=(L, N // tn),
        in_specs=[pl.BlockSpec((B, D), lambda l, j: (0, 0)),
                  pl.BlockSpec((None, D, tn), lambda l, j: (l, 0, j)),
                  pl.BlockSpec((None, 1, tn), lambda l, j: (l, 0, j))],
        out_specs=pl.BlockSpec((None, B, tn), lambda l, j: (l, 0, j)),
        out_shape=jax.ShapeDtypeStruct((L, B, N), F32),
        compiler_params=_cparams("parallel", "parallel"),
        name="ada_mod",
    )(c, ada_w, ada_b.reshape(L, 1, N))


IN_SEGS = ((C_QKV, 3 * GDN_WIDTH), (C_Z, GDN_WIDTH), (C_NQ, NSA_WIDTH), (C_KV, 6 * NSA_KV_WIDTH), (C_MISC, LANES))


def _in_proj_kernel(x_ref, sc_ref, sh_ref, g_ref, w_ref, wmt_ref, *o_refs):
    h = _rms(x_ref[...], g_ref[...]) * (1.0 + sc_ref[...]) + sh_ref[...]
    hb = h.astype(BF16)
    for (c0, n), o_ref in zip(IN_SEGS, o_refs[:-1]):
        o_ref[...] = jnp.dot(hb, w_ref[:, c0:c0 + n], preferred_element_type=F32)
    o_refs[-1][...] = lax.dot_general(wmt_ref[...], hb, (((1,), (1,)), ((), ())), preferred_element_type=F32)


def _in_proj_call(x2, sc, sh, g, w_pad, S):
    T, D = x2.shape
    tm = 256
    per_b = S // tm
    row = lambda i: (i, 0)
    bat = lambda i: (i // per_b, 0, 0)
    return pl.pallas_call(
        _in_proj_kernel,
        grid=(T // tm,),
        in_specs=[pl.BlockSpec((tm, D), row),
                  pl.BlockSpec((None, 1, D), bat),
                  pl.BlockSpec((None, 1, D), bat),
                  pl.BlockSpec((1, D), lambda i: (0, 0)),
                  pl.BlockSpec((D, IN_PAD_COLS), lambda i: (0, 0)),
                  pl.BlockSpec((LANES, D), lambda i: (0, 0))],
        out_specs=[pl.BlockSpec((tm, n), row) for _, n in IN_SEGS] + [pl.BlockSpec((LANES, tm), lambda i: (0, i))],
        out_shape=[jax.ShapeDtypeStruct((T, n), F32) for _, n in IN_SEGS] + [jax.ShapeDtypeStruct((LANES, T), F32)],
        compiler_params=_cparams("parallel"),
        name="in_proj",
    )(x2, sc, sh, g, w_pad, w_pad[:, C_MISC:].T)


def _pad_w_in(w_in):
    D = w_in.shape[0]
    W = GDN_WIDTH
    o = 0
    gq, gk, gv, gz = (w_in[:, o + i * W:o + (i + 1) * W] for i in range(4))
    o += 4 * W
    ga = w_in[:, o:o + GDN_HEADS]
    gb = w_in[:, o + GDN_HEADS:o + 2 * GDN_HEADS]
    o += 2 * GDN_HEADS
    nq = w_in[:, o:o + NSA_WIDTH]
    o += NSA_WIDTH
    kv = w_in[:, o:o + 6 * NSA_KV_WIDTH]
    o += 6 * NSA_KV_WIDTH
    gates = w_in[:, o:o + 3 * NSA_HEADS]
    misc = jnp.zeros((D, LANES), w_in.dtype)
    misc = misc.at[:, 0:GDN_HEADS].set(ga).at[:, MISC_B0:MISC_B0 + GDN_HEADS].set(gb)
    misc = misc.at[:, MISC_GATE0:MISC_GATE0 + 3 * NSA_HEADS].set(gates)
    cols = [gq, gk, gv, gz, nq, kv, misc]
    return jnp.concatenate(cols, axis=1).astype(BF16)


GDN_HPB = 2
GDN_NB = 32


def _gdn_kernel(q_ref, k_ref, v_ref, z_ref, misc_ref, ar_ref, cwq_ref, cwk_ref, cwv_ref,
                alog_ref, dtb_ref, alogr_ref, dtbr_ref, ng_ref, o_ref,
                qs, ks, kbs, xs, gcs, grs, ps, qq, qks, os_, rs, q2s, sev, *, S):
    C = GDN_CHUNK
    N = S // C
    W = GDN_HPB * HEAD_DIM
    row = lax.broadcasted_iota(jnp.int32, (S, W), 0)
    row8 = lax.broadcasted_iota(jnp.int32, (8, W), 0)

    def conv_silu(x, w):
        y = x * w[GDN_CONV - 1:GDN_CONV, :]
        for s in range(1, GDN_CONV):
            xs = pltpu.roll(x, s, axis=0)
            xs = jnp.concatenate([jnp.where(row8 >= s, xs[:8], 0.0), xs[8:]], axis=0)
            y = y + xs * w[GDN_CONV - 1 - s:GDN_CONV - s, :]
        return _silu(y)

    bd = jnp.where(lax.broadcasted_iota(jnp.int32, (W, W), 0) // HEAD_DIM
                   == lax.broadcasted_iota(jnp.int32, (W, W), 1) // HEAD_DIM, 1.0, 0.0).astype(BF16)

    def head_sumsq(x):
        hi, lo = _split2(x * x)
        return jnp.dot(hi, bd, preferred_element_type=F32) + jnp.dot(lo, bd, preferred_element_type=F32)

    def head_l2(x):
        return x * lax.rsqrt(head_sumsq(x) + RMS_EPS)

    q = head_l2(conv_silu(q_ref[...], cwq_ref[...])) * (HEAD_DIM ** -0.5)
    k = head_l2(conv_silu(k_ref[...], cwk_ref[...]))
    v = conv_silu(v_ref[...], cwv_ref[...])

    hd0 = pl.program_id(1) * GDN_HPB
    src = lax.broadcasted_iota(jnp.int32, (LANES, W), 0)
    dst = lax.broadcasted_iota(jnp.int32, (LANES, W), 1) // HEAD_DIM + hd0
    a_exp = _dot_exact_rhs(misc_ref[...], jnp.where(src == dst, 1.0, 0.0).astype(BF16))
    b_exp = _dot_exact_rhs(misc_ref[...], jnp.where(src == dst + MISC_B0, 1.0, 0.0).astype(BF16))
    beta = _sigmoid(b_exp)

    def log_decay(a, alog, dtb):
        xx = a + dtb
        sp = jnp.maximum(xx, 0.0) + jnp.log(1.0 + jnp.exp(-jnp.abs(xx)))
        return -jnp.exp(alog) * sp

    gc = log_decay(a_exp, alog_ref[...], dtb_ref[...])
    pos = row % C
    for s in (1, 2, 4, 8, 16, 32):
        gc = gc + jnp.where(pos >= s, pltpu.roll(gc, s, axis=0), 0.0)
    gr = log_decay(ar_ref[...], alogr_ref[...], dtbr_ref[...])
    lpos = lax.broadcasted_iota(jnp.int32, (N, W), 1) % C
    for s in (1, 2, 4, 8, 16, 32):
        gr = gr + jnp.where(lpos >= s, pltpu.roll(gr, s, axis=1), 0.0)

    kb = k * beta
    vb = v * beta
    kbe = kb * jnp.exp(gc)
    for h in range(GDN_HPB):
        sl = slice(h * HEAD_DIM, (h + 1) * HEAD_DIM)
        qs[h] = q[:, sl]
        ks[h] = k[:, sl]
        kbs[h] = kb[:, sl]
        gcs[h] = gc[:, sl]
        for n in range(N):
            grs[h, n] = jnp.broadcast_to(gr[n:n + 1, sl], (8, HEAD_DIM))
        xs[h] = jnp.concatenate([kbe[:, sl], vb[:, sl]], axis=1)

    NB = GDN_NB
    ci3 = lax.broadcasted_iota(jnp.int32, (1, C, C), 1)
    cj3 = lax.broadcasted_iota(jnp.int32, (1, C, C), 2)
    eye3 = jnp.where(ci3 == cj3, 1.0, 0.0)

    def bmm(a, b):
        return lax.dot_general(a.astype(BF16), b.astype(BF16), (((2,), (1,)), ((0,), (0,))),
                               preferred_element_type=F32)

    def bmm_nt(a, b):
        return lax.dot_general(a.astype(BF16), b.astype(BF16), (((2,), (2,)), ((0,), (0,))),
                               preferred_element_type=F32)

    def intra(nb, carry):
        rows = pl.ds(pl.multiple_of(nb * (NB * C), NB * C), NB * C)
        for h in range(GDN_HPB):
            kk = ks[h, rows, :].reshape(NB, C, HEAD_DIM)
            kbq = jnp.concatenate([kbs[h, rows, :].reshape(NB, C, HEAD_DIM),
                                   qs[h, rows, :].reshape(NB, C, HEAD_DIM)], axis=1)
            a2 = bmm_nt(kbq, kk)
            gcol = gcs[h, rows, :].reshape(NB, C, HEAD_DIM)
            grow = grs[h, pl.ds(nb * NB, NB), 0:1, :]
            decay = jnp.exp(jnp.where(ci3 >= cj3, gcol - grow, NEG_BIG))
            low = jnp.where(ci3 > cj3, a2[:, :C] * decay, 0.0)
            qks[h, rows, :] = (a2[:, C:] * decay).reshape(NB * C, C)
            tinv = eye3 - jnp.where(ci3 // 2 == cj3 // 2, low, 0.0)
            for lv in range(1, 6):
                s = 2 ** lv
                off = jnp.where((ci3 // (2 * s) == cj3 // (2 * s)) & (ci3 // s != cj3 // s), low, 0.0)
                tinv = tinv - bmm(tinv, bmm(off, tinv))
            x = bmm(tinv, xs[h, rows, :].reshape(NB, C, 2 * HEAD_DIM))
            xs[h, rows, :] = x.reshape(NB * C, 2 * HEAD_DIM)
            kd = kk * jnp.exp(gcol[:, C - 1:C, :] - gcol)
            pq = lax.dot_general(kd.astype(BF16), x.astype(BF16), (((1,), (1,)), ((0,), (0,))),
                                 preferred_element_type=F32)
            pn = pq[:, :, :HEAD_DIM]
            qn = pq[:, :, HEAD_DIM:]
            ps[h, rows, :] = pn.reshape(NB * C, HEAD_DIM)
            qq[h, rows, :] = qn.reshape(NB * C, HEAD_DIM)
            cd = jnp.exp(gcol[:, C - 1:C, :]).reshape(NB // 2, 2, 1, HEAD_DIM)
            pn = pn.reshape(NB // 2, 2, HEAD_DIM, HEAD_DIM)
            qn = qn.reshape(NB // 2, 2, HEAD_DIM, HEAD_DIM)
            p0, p1, q0, q1, c0, c1 = pn[:, 0], pn[:, 1], qn[:, 0], qn[:, 1], cd[:, 0], cd[:, 1]
            t = bmm(p1, pq.reshape(NB // 2, 2, HEAD_DIM, 2 * HEAD_DIM)[:, 0])
            prow = pl.ds(pl.multiple_of(nb * (NB // 2 * C), NB // 2 * C), NB // 2 * C)
            rs[h, prow, :] = (c1 * p0 + c0 * p1 - t[:, :, :HEAD_DIM]).reshape(NB // 2 * C, HEAD_DIM)
            q2s[h, prow, :] = (c1 * q0 - t[:, :, HEAD_DIM:] + q1).reshape(NB // 2 * C, HEAD_DIM)
        return carry

    lax.fori_loop(0, N // NB, intra, 0)

    def inter(k, states):
        sl = pl.ds(pl.multiple_of(k * C, C), C)
        new_states = []
        for h in range(GDN_HPB):
            st = states[h]
            sev[h, k] = st
            gsum = gcs[h, pl.ds(2 * k * C + C - 1, 1), :] + gcs[h, pl.ds(2 * k * C + 2 * C - 1, 1), :]
            new_states.append(st * jnp.exp(gsum) - _dot_hi(rs[h, sl, :], st) + q2s[h, sl, :])
        return tuple(new_states)

    lax.fori_loop(0, N // 2, inter, tuple(jnp.zeros((HEAD_DIM, HEAD_DIM), F32) for _ in range(GDN_HPB)))

    def outputs(nb, carry):
        rows = pl.ds(pl.multiple_of(nb * (NB * C), NB * C), NB * C)
        for h in range(GDN_HPB):
            gcol = gcs[h, rows, :].reshape(NB, C, HEAD_DIM)
            s0 = sev[h, pl.ds(nb * (NB // 2), NB // 2)]
            p0 = ps[h, rows, :].reshape(NB // 2, 2, HEAD_DIM, HEAD_DIM)[:, 0]
            q0 = qq[h, rows, :].reshape(NB // 2, 2, HEAD_DIM, HEAD_DIM)[:, 0]
            c0 = jnp.exp(gcol[:, C - 1:C, :]).reshape(NB // 2, 2, 1, HEAD_DIM)[:, 0]
            s1 = c0 * s0 - bmm(p0, s0) + q0
            st = jnp.stack([s0, s1], axis=1).reshape(NB, HEAD_DIM, HEAD_DIM)
            x = xs[h, rows, :].reshape(NB, C, 2 * HEAD_DIM)
            qd = qs[h, rows, :].reshape(NB, C, HEAD_DIM) * jnp.exp(gcol)
            t1 = bmm(jnp.concatenate([x[:, :, :HEAD_DIM], qd], axis=1), st)
            vn = x[:, :, HEAD_DIM:] - t1[:, :C]
            o = t1[:, C:] + bmm(qks[h, rows, :].reshape(NB, C, C), vn)
            os_[h, rows, :] = o.reshape(NB * C, HEAD_DIM)
        return carry

    lax.fori_loop(0, N // NB, outputs, 0)

    o = jnp.concatenate([os_[h] for h in range(GDN_HPB)], axis=1)
    o = o * lax.rsqrt(head_sumsq(o) * (1.0 / HEAD_DIM) + RMS_EPS) * ng_ref[...]
    o_ref[...] = o * _silu(z_ref[...])


def _gdn_call(qkv, z, misc, a_rows, conv_w, alog_exp, dtb_exp, norm_g, B, S):
    T = B * S
    W = GDN_HPB * HEAD_DIM
    P = GDN_HEADS // GDN_HPB
    N = S // GDN_CHUNK
    col = lambda off: (lambda b, p: (b, off + p))
    par = lambda off: (lambda b, p: (0, off + p))
    hs = lambda n: pltpu.VMEM((GDN_HPB, S, n), F32)
    return pl.pallas_call(
        functools.partial(_gdn_kernel, S=S),
        grid=(B, P),
        in_specs=[pl.BlockSpec((S, W), col(0)), pl.BlockSpec((S, W), col(P)), pl.BlockSpec((S, W), col(2 * P)),
                  pl.BlockSpec((S, W), col(0)), pl.BlockSpec((S, LANES), lambda b, p: (b, 0)),
                  pl.BlockSpec((None, None, N, W), lambda b, p: (b, p, 0, 0)),
                  pl.BlockSpec((GDN_CONV, W), par(0)), pl.BlockSpec((GDN_CONV, W), par(P)),
                  pl.BlockSpec((GDN_CONV, W), par(2 * P)),
                  pl.BlockSpec((1, W), par(0)), pl.BlockSpec((1, W), par(0)),
                  pl.BlockSpec((1, W), par(0)), pl.BlockSpec((1, W), par(0)),
                  pl.BlockSpec((1, W), lambda b, p: (0, 0))],
        out_specs=pl.BlockSpec((S, W), col(0)),
        out_shape=jax.ShapeDtypeStruct((T, GDN_WIDTH), F32),
        scratch_shapes=[hs(HEAD_DIM), hs(HEAD_DIM), hs(HEAD_DIM), hs(2 * HEAD_DIM), hs(HEAD_DIM),
                        pltpu.VMEM((GDN_HPB, N, 8, HEAD_DIM), F32),
                        hs(HEAD_DIM), hs(HEAD_DIM), hs(HEAD_DIM), hs(HEAD_DIM),
                        pltpu.VMEM((GDN_HPB, S // 2, HEAD_DIM), F32), pltpu.VMEM((GDN_HPB, S // 2, HEAD_DIM), F32),
                        pltpu.VMEM((GDN_HPB, N // 2, HEAD_DIM, HEAD_DIM), F32)],
        compiler_params=_cparams("parallel", "parallel"),
        name="gdn",
    )(qkv, qkv, qkv, z, misc, a_rows, conv_w, conv_w, conv_w,
      alog_exp, dtb_exp, alog_exp, dtb_exp, norm_g)


def _cmp_kernel(xk_ref, xv_ref, pe_ref, w1_ref, b1_ref, w2k_ref, w2vt_ref, b2k_ref, b2v_ref, kg_ref,
                kc_ref, vct_ref):
    half = (CMP_BLOCK // 2) * HEAD_DIM
    nrows = xk_ref.shape[0]

    def hidden(x, j):
        pe = pe_ref[j]
        p = _dot(x + pe[:, :half], w1_ref[j, :half, :])
        q = _dot(x + pe[:, half:], w1_ref[j, half:, :])
        return _silu(p + pltpu.roll(q, nrows - 1, axis=0) + b1_ref[j])

    kc = _dot(hidden(xk_ref[...], 0), w2k_ref[...]) + b2k_ref[...]
    ms = jnp.sum(kc * kc, axis=-1, keepdims=True) * (1.0 / HEAD_DIM)
    kcn = kc * lax.rsqrt(ms + RMS_EPS) * kg_ref[...]
    lane = lax.broadcasted_iota(jnp.int32, kc.shape, 1)
    cend = lax.broadcasted_iota(jnp.int32, kc.shape, 0) * CMP_STRIDE + (CMP_BLOCK - 1)
    ext = jnp.where(lane == HEAD_DIM, cend // LANES, jnp.where(lane == HEAD_DIM + 1, cend % LANES, 0)).astype(F32)
    kc_ref[...] = jnp.where(lane < HEAD_DIM, kcn, ext)
    vct_ref[...] = _dot_nt(w2vt_ref[...], hidden(xv_ref[...], 1)) + b2v_ref[...]


def _cmp_call(xk, xv, pe, w1, b1, w2k, w2vt, b2k, b2v, kg):
    BG, NC, half = xk.shape
    blk = pl.BlockSpec((None, NC, half), lambda i: (i, 0, 0))
    full = lambda a: pl.BlockSpec(a.shape, lambda i: (0,) * a.ndim)
    return pl.pallas_call(
        _cmp_kernel,
        grid=(BG,),
        in_specs=[blk, blk] + [full(a) for a in (pe, w1, b1, w2k, w2vt, b2k, b2v, kg)],
        out_specs=[pl.BlockSpec((None, NC, LANES), lambda i: (i, 0, 0)),
                   pl.BlockSpec((None, HEAD_DIM, NC), lambda i: (i, 0, 0))],
        out_shape=[jax.ShapeDtypeStruct((BG, NC, LANES), F32), jax.ShapeDtypeStruct((BG, HEAD_DIM, NC), F32)],
        compiler_params=_cparams("parallel"),
        name="nsa_compress",
    )(xk, xv, pe, w1, b1, w2k, w2vt, b2k, b2v, kg)


NSA_TQ = 128
NSA_TK = 128
NSA_M = NSA_GROUP * NSA_TQ
NSA_MS = NSA_M
MASK_BIG = 2.0 ** 100
SEL_LANE0 = HEAD_DIM + 2


def _nsa_kernel(q_ref, kv_ref, vt_ref, kcx_ref, vct_ref, misct_ref, cbias_ref, tbl_ref, qg_ref, kg_ref,
                ovt_ref, ish_ref, o_ref, ks_s, kw_s, *, S):
    tq, tk, M = NSA_TQ, NSA_TK, NSA_M
    MS = NSA_MS
    G, R = NSA_KV_HEADS, NSA_GROUP
    NS = S // SLC_BLOCK
    n_top = min(SLC_TOPN, NS)
    qi = pl.program_id(1)
    t0 = qi * tq
    nt_dims = (((1,), (1,)), ((), ()))

    @pl.when(qi == 0)
    def _():
        rowi = lax.broadcasted_iota(jnp.int32, (S, LANES), 0)
        lane = lax.broadcasted_iota(jnp.int32, (S, LANES), 1)
        ext = jnp.where(lane == HEAD_DIM, rowi // LANES, jnp.where(lane == HEAD_DIM + 1, rowi % LANES, 0)).astype(F32)
        ext_slc = jnp.where(lane - SEL_LANE0 == rowi // SLC_BLOCK, -MASK_BIG, ext)

        def prep(c0, g, gain, extra):
            kraw = kv_ref[:, c0:c0 + LANES]
            if g == 1:
                kraw = pltpu.roll(kraw, HEAD_DIM, axis=1)
            ms = jnp.sum(jnp.where(lane < HEAD_DIM, kraw * kraw, 0.0), axis=-1, keepdims=True) * (1.0 / HEAD_DIM)
            return jnp.where(lane < HEAD_DIM, kraw * lax.rsqrt(ms + RMS_EPS) * gain, extra).astype(BF16)

        for g in range(G):
            ks_s[g] = prep(2 * NSA_KV_WIDTH, g, kg_ref[1:2, :], ext_slc)
            kw_s[g] = prep(4 * NSA_KV_WIDTH, g, kg_ref[2:3, :], ext)

    lane_q = lax.broadcasted_iota(jnp.int32, (tq, LANES), 1)
    gate_sig = _sigmoid(misct_ref[MISC_GATE0:MISC_GATE0 + 3 * NSA_HEADS, :])
    kt_d = t0 // tk

    def score_tiles(streams, kt, biased):
        rows = pl.ds(pl.multiple_of(kt * tk, tk), tk)
        out = []
        for qpp, k_s, g, _, idx_fn in streams:
            s = lax.dot_general(k_s[g, rows, :], qpp, nt_dims, preferred_element_type=F32)
            out.append(s + tbl_ref[idx_fn(kt)] if biased else s)
        return tuple(out)

    def softmax_pv(streams, kt, scores, states):
        soft = []
        for s, (m, l, _) in zip(scores, states):
            m_new = jnp.maximum(m, jnp.max(s, axis=0, keepdims=True))
            alpha = jnp.exp(m - m_new)
            p = jnp.exp(s - m_new)
            soft.append((m_new, alpha, alpha * l + jnp.sum(p, axis=0, keepdims=True), p.astype(BF16)))
        pvs = [jnp.dot(vt_ref[kt, st[3]:st[3] + HEAD_DIM, :], sf[3], preferred_element_type=F32)
               for st, sf in zip(streams, soft)]
        return tuple((m_new, l, alpha * acc + pv)
                     for (m_new, alpha, l, _), (_, _, acc), pv in zip(soft, states, pvs))

    flash_init = (jnp.full((1, M), NEG_BIG, F32), jnp.zeros((1, M), F32), jnp.zeros((HEAD_DIM, M), F32))

    qpps, o_cmps = [], []
    for g in range(G):
        qrows = []
        for r in range(R):
            hh = g * R + r
            qpair = q_ref[:, (hh // 2) * LANES:(hh // 2 + 1) * LANES]
            if hh % 2 == 1:
                qpair = pltpu.roll(qpair, HEAD_DIM, axis=1)
            ms = jnp.sum(jnp.where(lane_q < HEAD_DIM, qpair * qpair, 0.0), axis=-1, keepdims=True) * (1.0 / HEAD_DIM)
            qn = qpair * lax.rsqrt(ms + RMS_EPS) * qg_ref[...] * (HEAD_DIM ** -0.5)
            slope = 2.0 ** (-8.0 * (hh + 1) / NSA_HEADS)
            ex = jnp.where(lane_q == HEAD_DIM, slope * LANES, jnp.where(lane_q == HEAD_DIM + 1, slope, 0.0))
            qrows.append(jnp.where(lane_q < HEAD_DIM, qn, ex))
        qbase = jnp.concatenate(qrows, axis=0)

        cb = cbias_ref[...]
        s = (lax.dot_general(kcx_ref[g].astype(BF16), qbase.astype(BF16), nt_dims, preferred_element_type=F32)
             + jnp.concatenate([cb] * R, axis=1))
        mx = jnp.maximum(jnp.max(s, axis=0, keepdims=True), NEG_BIG)
        e = jnp.exp(s - mx)
        l = jnp.sum(e, axis=0, keepdims=True)
        p_cmp = e / jnp.where(l > 0.0, l, 1.0)
        o_cmps.append(_dot(vct_ref[g], p_cmp))

        psum = p_cmp[:, 0:tq]
        for r in range(1, R):
            psum = psum + p_cmp[:, r * tq:(r + 1) * tq]
        hi, mid, lo = _split3(psum)
        ovt = ovt_ref[...]
        imp = (jnp.dot(ovt, hi, preferred_element_type=F32) + jnp.dot(ovt, mid, preferred_element_type=F32)
               + jnp.dot(ovt, lo, preferred_element_type=F32))
        blk = lax.broadcasted_iota(jnp.int32, (NS, tq), 0)
        cur = (t0 + lax.broadcasted_iota(jnp.int32, (NS, tq), 1)) // SLC_BLOCK
        valid = blk <= cur
        forced = (blk == 0) | (blk == cur) | (blk == cur - 1)
        score = jnp.where(forced, SEL_BIG, jnp.where(valid, imp, -SEL_BIG))
        rank = jnp.zeros((NS, tq), F32)
        for j in range(NS):
            cj = score[j:j + 1, :]
            beats = (cj > score) | ((cj == score) & (blk > j))
            rank = rank + jnp.where(beats, 1.0, 0.0)
        nsel = jnp.where((rank < n_top) & valid, 0.0, 1.0).astype(BF16)
        nsel_q = lax.dot_general(nsel, ish_ref[...], (((0,), (0,)), ((), ())),
                                 preferred_element_type=F32)
        in_sel = (lane_q >= SEL_LANE0) & (lane_q < SEL_LANE0 + NS)
        qpps.append(jnp.concatenate([jnp.where(in_sel, nsel_q, qrows[r]) for r in range(R)], axis=0).astype(BF16))

    slc_idx = lambda kt: jnp.where(kt == kt_d, 1, 0)
    win_lo = kt_d - WINDOW // tk
    win_idx = lambda kt: jnp.where(kt == kt_d, 1, jnp.where(kt == win_lo, 2, 0))
    slc_streams = [(qpps[g], ks_s, g, g * HEAD_DIM, slc_idx) for g in range(G)]
    win_streams = [(qpps[g], kw_s, g, (G + g) * HEAD_DIM, win_idx) for g in range(G)]
    kt_w = jnp.maximum(win_lo, 0)
    def slc_only(kt, carry):
        scores, states = carry
        nxt = score_tiles(slc_streams, jnp.minimum(kt + 1, kt_w), False)
        return nxt, softmax_pv(slc_streams, kt, scores, states)

    _, st_slc = lax.fori_loop(0, kt_w, slc_only, (score_tiles(slc_streams, 0, False), (flash_init,) * G))
    both = slc_streams + win_streams
    st_all = lax.fori_loop(kt_w, kt_d + 1, lambda kt, st: softmax_pv(both, kt, score_tiles(both, kt, True), st),
                           st_slc + (flash_init,) * G)
    o_all = [acc / jnp.where(l > 0.0, l, 1.0) for _, l, acc in st_all]
    o_slcs, o_wins = o_all[:G], o_all[G:]

    for g in range(G):
        for r in range(R):
            hh = g * R + r
            cs = slice(r * tq, (r + 1) * tq)
            o_ref[hh * HEAD_DIM:(hh + 1) * HEAD_DIM, :] = (
                gate_sig[hh:hh + 1, :] * o_cmps[g][:, cs]
                + gate_sig[NSA_HEADS + hh:NSA_HEADS + hh + 1, :] * o_slcs[g][:, cs]
                + gate_sig[2 * NSA_HEADS + hh:2 * NSA_HEADS + hh + 1, :] * o_wins[g][:, cs])


def _nsa_consts(S):
    tq, tk, M = NSA_TQ, NSA_TK, NSA_M
    NC = S // CMP_STRIDE
    NS = S // SLC_BLOCK
    cs = np.arange(NC)[:, None] * CMP_STRIDE
    ss = np.arange(NS)[None, :] * SLC_BLOCK
    ov = np.clip(np.minimum(cs + CMP_BLOCK, ss + SLC_BLOCK) - np.maximum(cs, ss), 0, None) / CMP_BLOCK
    ov[NC - 1:] = 0.0
    ish = np.zeros((NS, LANES), np.float32)
    ish[np.arange(NS), SEL_LANE0 + np.arange(NS)] = 1.0
    t = np.arange(S).reshape(S // tq, 1, tq)
    n = np.arange(NC).reshape(1, NC, 1)
    cbias = np.where((t >= n * CMP_STRIDE + CMP_BLOCK - 1) & (n < NC - 1), 0.0, -MASK_BIG).astype(np.float32)
    i = np.arange(tk)[:, None]
    j = np.arange(M)[None, :] % tq
    tbl = np.stack([np.zeros((tk, M)), np.where(i <= j, 0.0, -MASK_BIG), np.where(i > j, 0.0, -MASK_BIG)])
    return (jnp.asarray(ov.T, BF16), jnp.asarray(ish, BF16), jnp.asarray(cbias), jnp.asarray(tbl, F32))


def _nsa_call(nq, kv, vt, kcx, vct, misct, q_norm_g, k_norm_g, B, S):
    T = B * S
    tq, tk, M = NSA_TQ, NSA_TK, NSA_M
    G = NSA_KV_HEADS
    nqt = S // tq
    NC = S // CMP_STRIDE
    NS = S // SLC_BLOCK
    ovt, ish, cbias, tbl = _nsa_consts(S)
    tile2 = lambda a: jnp.concatenate([a, a], axis=-1)
    row = lambda b, i: (b * nqt + i, 0)
    cst = lambda b, i: (0, 0)
    return pl.pallas_call(
        functools.partial(_nsa_kernel, S=S),
        grid=(B, nqt),
        in_specs=[pl.BlockSpec((tq, NSA_WIDTH), row),
                  pl.BlockSpec((S, 6 * NSA_KV_WIDTH), lambda b, i: (b, 0)),
                  pl.BlockSpec((None, S // tk, 2 * NSA_KV_WIDTH, tk), lambda b, i: (b, 0, 0, 0)),
                  pl.BlockSpec((None, G, NC, LANES), lambda b, i: (b, 0, 0, 0)),
                  pl.BlockSpec((None, G, HEAD_DIM, NC), lambda b, i: (b, 0, 0, 0)),
                  pl.BlockSpec((LANES, tq), lambda b, i: (0, b * nqt + i)),
                  pl.BlockSpec((None, NC, tq), lambda b, i: (i, 0, 0)),
                  pl.BlockSpec((3, tk, M), lambda b, i: (0, 0, 0)),
                  pl.BlockSpec((1, LANES), cst),
                  pl.BlockSpec((3, LANES), cst),
                  pl.BlockSpec((NS, NC), cst),
                  pl.BlockSpec((NS, LANES), cst)],
        out_specs=pl.BlockSpec((NSA_WIDTH, tq), lambda b, i: (0, b * nqt + i)),
        out_shape=jax.ShapeDtypeStruct((NSA_WIDTH, T), F32),
        scratch_shapes=[pltpu.VMEM((G, S, LANES), BF16), pltpu.VMEM((G, S, LANES), BF16)],
        compiler_params=_cparams("parallel", "arbitrary"),
        name="nsa_attn",
    )(nq, kv, vt, kcx, vct, misct, cbias, tbl, tile2(q_norm_g), tile2(k_norm_g), ovt, ish)


def _out_proj_kernel(yg_ref, yn_ref, x_ref, g1_ref, sc_ref, sh_ref, ng_ref, wo_ref, rwt_ref, rb_ref,
                     x1_ref, h2_ref, gate_ref, gate_t_ref):
    mix = _dot(yg_ref[...], wo_ref[:GDN_WIDTH, :]) + _dot(yn_ref[...], wo_ref[GDN_WIDTH:, :])
    x1 = x_ref[...] + g1_ref[...] * mix
    x1_ref[...] = x1
    h2 = _rms(x1, ng_ref[...]) * (1.0 + sc_ref[...]) + sh_ref[...]
    h2_ref[...] = h2.astype(BF16)

    tm = h2.shape[0]
    logits = _dot_nt_hi(rwt_ref[...], h2)
    score = _sigmoid(logits)
    biased = score + rb_ref[...]
    b = [biased[e:e + 1, :] for e in range(N_EXPERTS)]
    n_groups = N_EXPERTS // EXPERTS_PER_GROUP
    gscore = []
    for gi in range(n_groups):
        vals = b[gi * EXPERTS_PER_GROUP:(gi + 1) * EXPERTS_PER_GROUP]
        best = None
        for i in range(EXPERTS_PER_GROUP):
            for j in range(i + 1, EXPERTS_PER_GROUP):
                pair = vals[i] + vals[j]
                best = pair if best is None else jnp.maximum(best, pair)
        gscore.append(best)
    gbest = jnp.zeros((1, tm), jnp.int32)
    top = gscore[0]
    for gi in range(1, n_groups):
        better = gscore[gi] > top
        gbest = jnp.where(better, gi, gbest)
        top = jnp.where(better, gscore[gi], top)
    erow = lax.broadcasted_iota(jnp.int32, (N_EXPERTS, tm), 0)
    gates = jnp.zeros((N_EXPERTS, tm), F32)
    for e in range(N_EXPERTS):
        gi = e // EXPERTS_PER_GROUP
        rank = jnp.zeros((1, tm), F32)
        for e2 in range(gi * EXPERTS_PER_GROUP, (gi + 1) * EXPERTS_PER_GROUP):
            if e2 == e:
                continue
            beats = (b[e2] > b[e]) | ((b[e2] == b[e]) & (e2 < e))
            rank = rank + jnp.where(beats, 1.0, 0.0)
        chosen = (gbest == gi) & (rank < 2.0)
        ge = jnp.where(chosen, score[e:e + 1, :], 0.0)
        gates = gates + jnp.where(erow == e, ge, 0.0)
    gates = gates / jnp.sum(gates, axis=0, keepdims=True)
    pad_rows = jnp.zeros((8 - EXPERTS_PER_GROUP, tm), F32)
    for gi in range(n_groups):
        gate_t_ref[gi] = jnp.concatenate(
            [gates[gi * EXPERTS_PER_GROUP:(gi + 1) * EXPERTS_PER_GROUP, :], pad_rows], axis=0)
    ident = (lax.broadcasted_iota(jnp.int32, (N_EXPERTS, LANES), 0)
             == lax.broadcasted_iota(jnp.int32, (N_EXPERTS, LANES), 1))
    ident = jnp.where(ident, 1.0, 0.0).astype(BF16)
    hi, mid, lo = _split3(gates)
    dn = (((0,), (0,)), ((), ()))
    gate_ref[...] = (lax.dot_general(hi, ident, dn, preferred_element_type=F32)
                     + lax.dot_general(mid, ident, dn, preferred_element_type=F32)
                     + lax.dot_general(lo, ident, dn, preferred_element_type=F32))


def _out_proj_call(yg, yn, x2, g1, sc2, sh2, ng, wo, rwt, rb, S):
    T, D = x2.shape
    tm = 256
    per_b = S // tm
    row = lambda i: (i, 0)
    bat = lambda i: (i // per_b, 0, 0)
    cst = lambda i: (0, 0)
    return pl.pallas_call(
        _out_proj_kernel,
        grid=(T // tm,),
        in_specs=[pl.BlockSpec((tm, GDN_WIDTH), row), pl.BlockSpec((tm, NSA_WIDTH), row), pl.BlockSpec((tm, D), row),
                  pl.BlockSpec((None, 1, D), bat), pl.BlockSpec((None, 1, D), bat), pl.BlockSpec((None, 1, D), bat),
                  pl.BlockSpec((1, D), cst), pl.BlockSpec((D, D), cst),
                  pl.BlockSpec((N_EXPERTS, D), cst), pl.BlockSpec((N_EXPERTS, 1), cst)],
        out_specs=[pl.BlockSpec((tm, D), row), pl.BlockSpec((tm, D), row), pl.BlockSpec((tm, LANES), row),
                   pl.BlockSpec((N_EXPERTS // EXPERTS_PER_GROUP, 8, tm), lambda i: (0, 0, i))],
        out_shape=[jax.ShapeDtypeStruct((T, D), F32), jax.ShapeDtypeStruct((T, D), BF16),
                   jax.ShapeDtypeStruct((T, LANES), F32),
                   jax.ShapeDtypeStruct((N_EXPERTS // EXPERTS_PER_GROUP, 8, T), F32)],
        compiler_params=_cparams("parallel"),
        name="out_proj_router",
    )(yg, yn, x2, g1, sc2, sh2, ng, wo, rwt, rb)


MOE_TM = 1024
MOE_RB = 288


def _moe_kernel(h_ref, gates_ref, gt8_ref, tri_ref, trit_ref, x1_ref, g2_ref, wg_ref, wu_ref, wd_ref, o_ref):
    g = pl.program_id(1)
    tm, rb = MOE_TM, MOE_RB
    nlb = tm // LANES

    @pl.when(g == 0)
    def _():
        o_ref[...] = jnp.zeros_like(o_ref)

    gt8 = gt8_ref[...]
    in_row = jnp.sum(gt8, axis=0, keepdims=True) > 0.0
    first = lax.broadcasted_iota(jnp.int32, (8, LANES), 0) == 0
    off = jnp.zeros((1, 1), F32)
    keys = []
    for b in range(nlb):
        blk = in_row[:, b * LANES:(b + 1) * LANES]
        one = jnp.where(blk & first, 1.0, 0.0).astype(BF16)
        within = jnp.dot(one, tri_ref[...], preferred_element_type=F32)[0:1, :]
        keys.append(jnp.where(blk, within + off, -1.0))
        off = off + jnp.sum(jnp.where(blk, 1.0, 0.0), axis=1, keepdims=True)
    key_row = jnp.concatenate(keys, axis=1)
    n_g = jnp.sum(off).astype(jnp.int32)
    lane = lax.broadcasted_iota(jnp.int32, (tm, LANES), 1)
    mine = (lane >= g * EXPERTS_PER_GROUP) & (lane < (g + 1) * EXPERTS_PER_GROUP)
    in_col = jnp.sum(jnp.where(mine, gates_ref[...], 0.0), axis=1, keepdims=True) > 0.0
    off = jnp.zeros((1, 1), F32)
    keys = []
    for b in range(nlb):
        blk = in_col[b * LANES:(b + 1) * LANES, :]
        one = jnp.broadcast_to(jnp.where(blk, 1.0, 0.0), (LANES, LANES)).astype(BF16)
        within = jnp.dot(trit_ref[...], one, preferred_element_type=F32)[:, 0:1]
        keys.append(jnp.where(blk, within + off, -1.0))
        off = off + jnp.sum(jnp.where(blk, 1.0, 0.0), axis=0, keepdims=True)
    key_col = jnp.concatenate(keys, axis=0)

    gt_hi, gt_mid, gt_lo = _split3(gt8)
    nt_dims = (((1,), (1,)), ((), ()))

    def block(j, carry):
        r0 = (j * rb).astype(F32)
        sel = jnp.where(key_row - r0 == lax.broadcasted_iota(jnp.int32, (rb, tm), 0).astype(F32), 1.0, 0.0)
        sel = sel.astype(BF16)
        xb = jnp.dot(sel, h_ref[...], preferred_element_type=F32).astype(BF16)
        gb = (lax.dot_general(sel, gt_hi, nt_dims, preferred_element_type=F32)
              + lax.dot_general(sel, gt_mid, nt_dims, preferred_element_type=F32)
              + lax.dot_general(sel, gt_lo, nt_dims, preferred_element_type=F32))
        y = jnp.zeros((rb, o_ref.shape[1]), F32)
        for k in range(EXPERTS_PER_GROUP):
            act = (_silu(jnp.dot(xb, wg_ref[k], preferred_element_type=F32))
                   * jnp.dot(xb, wu_ref[k], preferred_element_type=F32))
            y = y + _dot(act * gb[:, k:k + 1], wd_ref[k])
        sel_t = jnp.where(key_col - r0 == lax.broadcasted_iota(jnp.int32, (tm, rb), 1).astype(F32), 1.0, 0.0)
        o_ref[...] += jnp.dot(sel_t.astype(BF16), y.astype(BF16), preferred_element_type=F32)
        return carry

    lax.fori_loop(0, (n_g + rb - 1) // rb, block, 0)

    @pl.when(g == N_EXPERTS // EXPERTS_PER_GROUP - 1)
    def _():
        o_ref[...] = x1_ref[...] + g2_ref[...] * o_ref[...]


def _moe_call(h2, gates, gates_t, x1, g2, wg, wu, wd, S):
    T, D = x1.shape
    tm = MOE_TM
    per_b = S // tm
    n_groups = N_EXPERTS // EXPERTS_PER_GROUP
    i = np.arange(LANES)
    tri = jnp.asarray(i[:, None] < i[None, :], BF16)
    row = lambda i, g: (i, 0)
    cst = lambda i, g: (0, 0)
    wsel = lambda i, g: (g, 0, 0)
    return pl.pallas_call(
        _moe_kernel,
        grid=(T // tm, n_groups),
        in_specs=[pl.BlockSpec((tm, D), row), pl.BlockSpec((tm, LANES), row),
                  pl.BlockSpec((None, 8, tm), lambda i, g: (g, 0, i)),
                  pl.BlockSpec((LANES, LANES), cst), pl.BlockSpec((LANES, LANES), cst),
                  pl.BlockSpec((tm, D), row),
                  pl.BlockSpec((None, 1, D), lambda i, g: (i // per_b, 0, 0)),
                  pl.BlockSpec((EXPERTS_PER_GROUP, D, EXPERT_FF), wsel),
                  pl.BlockSpec((EXPERTS_PER_GROUP, D, EXPERT_FF), wsel),
                  pl.BlockSpec((EXPERTS_PER_GROUP, EXPERT_FF, D), wsel)],
        out_specs=pl.BlockSpec((tm, D), row),
        out_shape=jax.ShapeDtypeStruct((T, D), F32),
        compiler_params=_cparams("parallel", "arbitrary"),
        name="moe",
    )(h2, gates, gates_t, tri, tri.T, x1, g2, wg, wu, wd)


def kernel(x, c, ada_w, ada_b, norm1_g, norm2_g, w_in, gdn_conv_w, gdn_a_log, gdn_dt_bias, gdn_norm_g,
           nsa_q_norm_g, nsa_k_norm_g, cmp_pe, cmp_w1, cmp_b1, cmp_w2, cmp_b2, w_out, router_w, router_bias,
           exp_w_gate, exp_w_up, exp_w_down):
    B, S, D = x.shape
    L = ada_w.shape[0]
    T = B * S
    G = NSA_KV_HEADS
    NC = S // CMP_STRIDE
    N = S // GDN_CHUNK
    P = GDN_HEADS // GDN_HPB
    W = GDN_HPB * HEAD_DIM

    mod = _ada_call(c, ada_w, ada_b)
    rwt = router_w.T
    rb = router_bias.reshape(N_EXPERTS, 1)
    x2 = x.reshape(T, D)
    for l in range(L):
        m6 = mod[l].reshape(B, 6, 1, D)
        sh1, sc1, g1, sh2, sc2, g2 = (m6[:, i] for i in range(6))
        qkv, z, nq, kv, misc, misct = _in_proj_call(
            x2, sc1, sh1, norm1_g[l].reshape(1, D), _pad_w_in(w_in[l]), S)

        a_rows = misc[:, :GDN_HEADS].reshape(B, N, GDN_CHUNK, P, GDN_HPB).transpose(0, 3, 1, 4, 2).reshape(B, P, N, W)
        rep = lambda t: jnp.repeat(t, HEAD_DIM).reshape(1, GDN_WIDTH)
        y_gdn = _gdn_call(qkv, z, misc, a_rows, gdn_conv_w[l], rep(gdn_a_log[l]), rep(gdn_dt_bias[l]),
                          jnp.tile(gdn_norm_g[l], GDN_HPB).reshape(1, W), B, S)

        xkv = kv[:, :2 * NSA_KV_WIDTH].reshape(B, S, 2, G, HEAD_DIM).transpose(2, 0, 3, 1, 4)
        xkv = xkv.reshape(2, B * G, NC, CMP_STRIDE * HEAD_DIM)
        w2 = cmp_w2[l]
        kcx, vct = _cmp_call(
            xkv[0], xkv[1], cmp_pe[l].reshape(2, 1, CMP_BLOCK * HEAD_DIM), cmp_w1[l].astype(BF16),
            cmp_b1[l].reshape(2, 1, CMP_HIDDEN), jnp.pad(w2[0], ((0, 0), (0, LANES - HEAD_DIM))).astype(BF16),
            w2[1].T.astype(BF16), jnp.pad(cmp_b2[l, 0], (0, LANES - HEAD_DIM)).reshape(1, LANES),
            cmp_b2[l, 1].reshape(HEAD_DIM, 1), jnp.pad(nsa_k_norm_g[l, 0], (0, LANES - HEAD_DIM)).reshape(1, LANES))
        vsw = jnp.concatenate([kv[:, 3 * NSA_KV_WIDTH:4 * NSA_KV_WIDTH], kv[:, 5 * NSA_KV_WIDTH:]], axis=1)
        vt = vsw.astype(BF16).reshape(B, S // NSA_TK, NSA_TK, 2 * NSA_KV_WIDTH).transpose(0, 1, 3, 2)
        y_nsa_t = _nsa_call(nq, kv, vt, kcx.reshape(B, G, NC, LANES), vct.reshape(B, G, HEAD_DIM, NC), misct,
                            nsa_q_norm_g[l].reshape(1, HEAD_DIM), nsa_k_norm_g[l], B, S)
        y_nsa = y_nsa_t.T

        x1, h2, gates, gates_t = _out_proj_call(y_gdn, y_nsa, x2, g1, sc2, sh2, norm2_g[l].reshape(1, D),
                                       w_out[l].astype(BF16), rwt, rb, S)
        x2 = _moe_call(h2, gates, gates_t, x1, g2, exp_w_gate[l].astype(BF16), exp_w_up[l].astype(BF16),
                       exp_w_down[l].astype(BF16), S)
    return x2.reshape(B, S, D)
```

```python
import functools

import numpy as np
import jax
import jax.numpy as jnp
from jax import lax
from jax.experimental import pallas as pl
from jax.experimental.pallas import tpu as pltpu

F32 = jnp.float32
BF16 = jnp.bfloat16

HEAD_DIM = 64
GDN_HEADS = 8
GDN_WIDTH = GDN_HEADS * HEAD_DIM
GDN_CONV = 4
GDN_CHUNK = 64
NSA_HEADS = 8
NSA_KV_HEADS = 2
NSA_GROUP = NSA_HEADS // NSA_KV_HEADS
NSA_WIDTH = NSA_HEADS * HEAD_DIM
NSA_KV_WIDTH = NSA_KV_HEADS * HEAD_DIM
CMP_BLOCK = 32
CMP_STRIDE = 16
CMP_HIDDEN = 256
SLC_BLOCK = 64
SLC_TOPN = 8
WINDOW = 512
N_EXPERTS = 16
EXPERTS_PER_GROUP = 4
EXPERT_FF = 512
RMS_EPS = 1e-6
NEG_BIG = -1e30
SEL_BIG = 1e9

LANES = 128
VMEM_LIMIT_BYTES = 56 * 1024 * 1024

C_QKV = 0
C_Z = 3 * GDN_WIDTH
C_NQ = C_Z + GDN_WIDTH
C_KV = C_NQ + NSA_WIDTH
KV_COLS = 4 * NSA_KV_WIDTH
C_MISC = C_KV + KV_COLS
IN_PAD_COLS = C_MISC + LANES
MISC_B0 = GDN_HEADS
MISC_GATE0 = 16


def _cparams(*sem):
    return pltpu.CompilerParams(dimension_semantics=sem, vmem_limit_bytes=VMEM_LIMIT_BYTES)


def _dot(a, b):
    return jnp.dot(a.astype(BF16), b.astype(BF16), preferred_element_type=F32)


def _dot_nt(a, b):
    return lax.dot_general(a.astype(BF16), b.astype(BF16), (((1,), (1,)), ((), ())),
                           preferred_element_type=F32)


def _dot_tn(a, b):
    return lax.dot_general(a.astype(BF16), b.astype(BF16), (((0,), (0,)), ((), ())),
                           preferred_element_type=F32)


def _split2(x):
    hi = x.astype(BF16)
    lo = (x - hi.astype(F32)).astype(BF16)
    return hi, lo


def _split3(x):
    hi = x.astype(BF16)
    r = x - hi.astype(F32)
    mid = r.astype(BF16)
    lo = (r - mid.astype(F32)).astype(BF16)
    return hi, mid, lo


def _dot_hi(a, b):
    ah, al = _split2(a)
    bh, bl = _split2(b)
    return (jnp.dot(ah, bh, preferred_element_type=F32) + jnp.dot(al, bh, preferred_element_type=F32)
            + jnp.dot(ah, bl, preferred_element_type=F32))


def _dot_nt_hi(a, b):
    ah, al = _split2(a)
    bh, bl = _split2(b)
    dn = (((1,), (1,)), ((), ()))
    return (lax.dot_general(ah, bh, dn, preferred_element_type=F32)
            + lax.dot_general(al, bh, dn, preferred_element_type=F32)
            + lax.dot_general(ah, bl, dn, preferred_element_type=F32))


def _dot_exact_rhs(a, b_bf16):
    hi, mid, lo = _split3(a)
    return (jnp.dot(hi, b_bf16, preferred_element_type=F32) + jnp.dot(mid, b_bf16, preferred_element_type=F32)
            + jnp.dot(lo, b_bf16, preferred_element_type=F32))


def _rms(x, g):
    return x * lax.rsqrt(jnp.mean(x * x, axis=-1, keepdims=True) + RMS_EPS) * g


def _sigmoid(x):
    return 0.5 * (jnp.tanh(0.5 * x) + 1.0)


def _silu(x):
    return x * _sigmoid(x)


def _ada_kernel(c_ref, w_ref, b_ref, o_ref):
    o_ref[...] = _dot_hi(_silu(c_ref[...]), w_ref[...]) + b_ref[...]


def _ada_call(c, ada_w, ada_b):
    L, D, N = ada_w.shape
    B = c.shape[0]
    tn = 1024
    return pl.pallas_call(
        _ada_kernel,
        grid=(L, N // tn),
        in_specs=[pl.BlockSpec((B, D), lambda l, j: (0, 0)),
                  pl.BlockSpec((None, D, tn), lambda l, j: (l, 0, j)),
                  pl.BlockSpec((None, 1, tn), lambda l, j: (l, 0, j))],
        out_specs=pl.BlockSpec((None, B, tn), lambda l, j: (l, 0, j)),
        out_shape=jax.ShapeDtypeStruct((L, B, N), F32),
        compiler_params=_cparams("parallel", "parallel"),
        name="ada_mod",
    )(c, ada_w, ada_b.reshape(L, 1, N))


IN_SEGS = ((C_QKV, 3 * GDN_WIDTH), (C_Z, GDN_WIDTH), (C_NQ, NSA_WIDTH), (C_KV, KV_COLS), (C_MISC, LANES))


def _in_proj_kernel(x_ref, sc_ref, sh_ref, g_ref, w_ref, wmt_ref, wvt_ref, *o_refs):
    h = _rms(x_ref[...], g_ref[...]) * (1.0 + sc_ref[...]) + sh_ref[...]
    hb = h.astype(BF16)
    for (c0, n), o_ref in zip(IN_SEGS, o_refs[:-2]):
        o_ref[...] = jnp.dot(hb, w_ref[:, c0:c0 + n], preferred_element_type=F32)
    nt_dims = (((1,), (1,)), ((), ()))
    o_refs[-2][...] = lax.dot_general(wmt_ref[...], hb, nt_dims, preferred_element_type=F32)
    vt = lax.dot_general(wvt_ref[...], hb, nt_dims, preferred_element_type=F32).astype(BF16)
    for j in range(o_refs[-1].shape[0]):
        o_refs[-1][j] = vt[:, j * LANES:(j + 1) * LANES]


def _in_proj_call(x2, sc, sh, g, w_pad, wvt, S):
    T, D = x2.shape
    tm = 256
    per_b = S // tm
    row = lambda i: (i, 0)
    bat = lambda i: (i // per_b, 0, 0)
    return pl.pallas_call(
        _in_proj_kernel,
        grid=(T // tm,),
        in_specs=[pl.BlockSpec((tm, D), row),
                  pl.BlockSpec((None, 1, D), bat),
                  pl.BlockSpec((None, 1, D), bat),
                  pl.BlockSpec((1, D), lambda i: (0, 0)),
                  pl.BlockSpec((D, IN_PAD_COLS), lambda i: (0, 0)),
                  pl.BlockSpec((LANES, D), lambda i: (0, 0)),
                  pl.BlockSpec((2 * NSA_KV_WIDTH, D), lambda i: (0, 0))],
        out_specs=([pl.BlockSpec((tm, n), row) for _, n in IN_SEGS]
                   + [pl.BlockSpec((LANES, tm), lambda i: (0, i)),
                      pl.BlockSpec((tm // LANES, 2 * NSA_KV_WIDTH, LANES), lambda i: (i, 0, 0))]),
        out_shape=([jax.ShapeDtypeStruct((T, n), F32) for _, n in IN_SEGS]
                   + [jax.ShapeDtypeStruct((LANES, T), F32),
                      jax.ShapeDtypeStruct((T // LANES, 2 * NSA_KV_WIDTH, LANES), BF16)]),
        compiler_params=_cparams("parallel"),
        name="in_proj",
    )(x2, sc, sh, g, w_pad, w_pad[:, C_MISC:].T, wvt)


def _pad_w_in(w_in):
    D = w_in.shape[0]
    W = GDN_WIDTH
    o = 0
    gq, gk, gv, gz = (w_in[:, o + i * W:o + (i + 1) * W] for i in range(4))
    o += 4 * W
    ga = w_in[:, o:o + GDN_HEADS]
    gb = w_in[:, o + GDN_HEADS:o + 2 * GDN_HEADS]
    o += 2 * GDN_HEADS
    nq = w_in[:, o:o + NSA_WIDTH]
    o += NSA_WIDTH
    kc, vc, ks, vs, kw, vw = (w_in[:, o + i * NSA_KV_WIDTH:o + (i + 1) * NSA_KV_WIDTH] for i in range(6))
    o += 6 * NSA_KV_WIDTH
    gates = w_in[:, o:o + 3 * NSA_HEADS]
    misc = jnp.zeros((D, LANES), w_in.dtype)
    misc = misc.at[:, 0:GDN_HEADS].set(ga).at[:, MISC_B0:MISC_B0 + GDN_HEADS].set(gb)
    misc = misc.at[:, MISC_GATE0:MISC_GATE0 + 3 * NSA_HEADS].set(gates)
    cols = [gq, gk, gv, gz, nq, kc, vc, ks, kw, misc]
    return jnp.concatenate(cols, axis=1).astype(BF16), jnp.concatenate([vs, vw], axis=1).T.astype(BF16)


GDN_HPB = 2
GDN_NB = 32


def _gdn_kernel(q_ref, k_ref, v_ref, z_ref, misc_ref, ar_ref, cwq_ref, cwk_ref, cwv_ref,
                alog_ref, dtb_ref, alogr_ref, dtbr_ref, ng_ref, o_ref,
                qs, ks, kbs, xs, gcs, grs, ps, qq, qks, os_, rs, q2s, sev, *, S):
    C = GDN_CHUNK
    N = S // C
    W = GDN_HPB * HEAD_DIM
    row = lax.broadcasted_iota(jnp.int32, (S, W), 0)
    row8 = lax.broadcasted_iota(jnp.int32, (8, W), 0)

    def conv_silu(x, w):
        y = x * w[GDN_CONV - 1:GDN_CONV, :]
        for s in range(1, GDN_CONV):
            xs = pltpu.roll(x, s, axis=0)
            xs = jnp.concatenate([jnp.where(row8 >= s, xs[:8], 0.0), xs[8:]], axis=0)
            y = y + xs * w[GDN_CONV - 1 - s:GDN_CONV - s, :]
        return _silu(y)

    bd = jnp.where(lax.broadcasted_iota(jnp.int32, (W, W), 0) // HEAD_DIM
                   == lax.broadcasted_iota(jnp.int32, (W, W), 1) // HEAD_DIM, 1.0, 0.0).astype(BF16)

    def head_sumsq(x):
        hi, lo = _split2(x * x)
        return jnp.dot(hi, bd, preferred_element_type=F32) + jnp.dot(lo, bd, preferred_element_type=F32)

    def head_l2(x):
        return x * lax.rsqrt(head_sumsq(x) + RMS_EPS)

    q = head_l2(conv_silu(q_ref[...], cwq_ref[...])) * (HEAD_DIM ** -0.5)
    k = head_l2(conv_silu(k_ref[...], cwk_ref[...]))
    v = conv_silu(v_ref[...], cwv_ref[...])

    hd0 = pl.program_id(1) * GDN_HPB
    src = lax.broadcasted_iota(jnp.int32, (LANES, W), 0)
    dst = lax.broadcasted_iota(jnp.int32, (LANES, W), 1) // HEAD_DIM + hd0
    a_exp = _dot_exact_rhs(misc_ref[...], jnp.where(src == dst, 1.0, 0.0).astype(BF16))
    b_exp = _dot_exact_rhs(misc_ref[...], jnp.where(src == dst + MISC_B0, 1.0, 0.0).astype(BF16))
    beta = _sigmoid(b_exp)

    def log_decay(a, alog, dtb):
        xx = a + dtb
        sp = jnp.maximum(xx, 0.0) + jnp.log(1.0 + jnp.exp(-jnp.abs(xx)))
        return -jnp.exp(alog) * sp

    gc = log_decay(a_exp, alog_ref[...], dtb_ref[...])
    pos = row % C
    for s in (1, 2, 4, 8, 16, 32):
        gc = gc + jnp.where(pos >= s, pltpu.roll(gc, s, axis=0), 0.0)
    gr = log_decay(ar_ref[...], alogr_ref[...], dtbr_ref[...])
    lpos = lax.broadcasted_iota(jnp.int32, (N, W), 1) % C
    for s in (1, 2, 4, 8, 16, 32):
        gr = gr + jnp.where(lpos >= s, pltpu.roll(gr, s, axis=1), 0.0)

    kb = k * beta
    vb = v * beta
    kbe = kb * jnp.exp(gc)
    for h in range(GDN_HPB):
        sl = slice(h * HEAD_DIM, (h + 1) * HEAD_DIM)
        qs[h] = q[:, sl]
        ks[h] = k[:, sl]
        kbs[h] = kb[:, sl]
        gcs[h] = gc[:, sl]
        for n in range(N):
            grs[h, n] = jnp.broadcast_to(gr[n:n + 1, sl], (8, HEAD_DIM))
        xs[h] = jnp.concatenate([kbe[:, sl], vb[:, sl]], axis=1)

    NB = GDN_NB
    ci3 = lax.broadcasted_iota(jnp.int32, (1, C, C), 1)
    cj3 = lax.broadcasted_iota(jnp.int32, (1, C, C), 2)
    eye3 = jnp.where(ci3 == cj3, 1.0, 0.0)

    def bmm(a, b):
        return lax.dot_general(a.astype(BF16), b.astype(BF16), (((2,), (1,)), ((0,), (0,))),
                               preferred_element_type=F32)

    def bmm_nt(a, b):
        return lax.dot_general(a.astype(BF16), b.astype(BF16), (((2,), (2,)), ((0,), (0,))),
                               preferred_element_type=F32)

    def intra(nb, carry):
        rows = pl.ds(pl.multiple_of(nb * (NB * C), NB * C), NB * C)
        for h in range(GDN_HPB):
            kk = ks[h, rows, :].reshape(NB, C, HEAD_DIM)
            kbq = jnp.concatenate([kbs[h, rows, :].reshape(NB, C, HEAD_DIM),
                                   qs[h, rows, :].reshape(NB, C, HEAD_DIM)], axis=1)
            a2 = bmm_nt(kbq, kk)
            gcol = gcs[h, rows, :].reshape(NB, C, HEAD_DIM)
            grow = grs[h, pl.ds(nb * NB, NB), 0:1, :]
            decay = jnp.exp(jnp.where(ci3 >= cj3, gcol - grow, NEG_BIG))
            low = jnp.where(ci3 > cj3, a2[:, :C] * decay, 0.0)
            qks[h, rows, :] = (a2[:, C:] * decay).reshape(NB * C, C)
            tinv = eye3 - jnp.where(ci3 // 2 == cj3 // 2, low, 0.0)
            for lv in range(1, 6):
                s = 2 ** lv
                off = jnp.where((ci3 // (2 * s) == cj3 // (2 * s)) & (ci3 // s != cj3 // s), low, 0.0)
                tinv = tinv - bmm(tinv, bmm(off, tinv))
            x = bmm(tinv, xs[h, rows, :].reshape(NB, C, 2 * HEAD_DIM))
            xs[h, rows, :] = x.reshape(NB * C, 2 * HEAD_DIM)
            kd = kk * jnp.exp(gcol[:, C - 1:C, :] - gcol)
            pq = lax.dot_general(kd.astype(BF16), x.astype(BF16), (((1,), (1,)), ((0,), (0,))),
                                 preferred_element_type=F32)
            pn = pq[:, :, :HEAD_DIM]
            qn = pq[:, :, HEAD_DIM:]
            ps[h, rows, :] = pn.reshape(NB * C, HEAD_DIM)
            qq[h, rows, :] = qn.reshape(NB * C, HEAD_DIM)
            cd = jnp.exp(gcol[:, C - 1:C, :]).reshape(NB // 2, 2, 1, HEAD_DIM)
            pn = pn.reshape(NB // 2, 2, HEAD_DIM, HEAD_DIM)
            qn = qn.reshape(NB // 2, 2, HEAD_DIM, HEAD_DIM)
            p0, p1, q0, q1, c0, c1 = pn[:, 0], pn[:, 1], qn[:, 0], qn[:, 1], cd[:, 0], cd[:, 1]
            t = bmm(p1, pq.reshape(NB // 2, 2, HEAD_DIM, 2 * HEAD_DIM)[:, 0])
            prow = pl.ds(pl.multiple_of(nb * (NB // 2 * C), NB // 2 * C), NB // 2 * C)
            rs[h, prow, :] = (c1 * p0 + c0 * p1 - t[:, :, :HEAD_DIM]).reshape(NB // 2 * C, HEAD_DIM)
            q2s[h, prow, :] = (c1 * q0 - t[:, :, HEAD_DIM:] + q1).reshape(NB // 2 * C, HEAD_DIM)
        return carry

    lax.fori_loop(0, N // NB, intra, 0)

    def inter(k, states):
        sl = pl.ds(pl.multiple_of(k * C, C), C)
        new_states = []
        for h in range(GDN_HPB):
            st = states[h]
            sev[h, k] = st
            gsum = gcs[h, pl.ds(2 * k * C + C - 1, 1), :] + gcs[h, pl.ds(2 * k * C + 2 * C - 1, 1), :]
            new_states.append(st * jnp.exp(gsum) - _dot_hi(rs[h, sl, :], st) + q2s[h, sl, :])
        return tuple(new_states)

    lax.fori_loop(0, N // 2, inter, tuple(jnp.zeros((HEAD_DIM, HEAD_DIM), F32) for _ in range(GDN_HPB)))

    def outputs(nb, carry):
        rows = pl.ds(pl.multiple_of(nb * (NB * C), NB * C), NB * C)
        for h in range(GDN_HPB):
            gcol = gcs[h, rows, :].reshape(NB, C, HEAD_DIM)
            s0 = sev[h, pl.ds(nb * (NB // 2), NB // 2)]
            p0 = ps[h, rows, :].reshape(NB // 2, 2, HEAD_DIM, HEAD_DIM)[:, 0]
            q0 = qq[h, rows, :].reshape(NB // 2, 2, HEAD_DIM, HEAD_DIM)[:, 0]
            c0 = jnp.exp(gcol[:, C - 1:C, :]).reshape(NB // 2, 2, 1, HEAD_DIM)[:, 0]
            s1 = c0 * s0 - bmm(p0, s0) + q0
            st = jnp.stack([s0, s1], axis=1).reshape(NB, HEAD_DIM, HEAD_DIM)
            x = xs[h, rows, :].reshape(NB, C, 2 * HEAD_DIM)
            qd = qs[h, rows, :].reshape(NB, C, HEAD_DIM) * jnp.exp(gcol)
            t1 = bmm(jnp.concatenate([x[:, :, :HEAD_DIM], qd], axis=1), st)
            vn = x[:, :, HEAD_DIM:] - t1[:, :C]
            o = t1[:, C:] + bmm(qks[h, rows, :].reshape(NB, C, C), vn)
            os_[h, rows, :] = o.reshape(NB * C, HEAD_DIM)
        return carry

    lax.fori_loop(0, N // NB, outputs, 0)

    o = jnp.concatenate([os_[h] for h in range(GDN_HPB)], axis=1)
    o = o * lax.rsqrt(head_sumsq(o) * (1.0 / HEAD_DIM) + RMS_EPS) * ng_ref[...]
    o_ref[...] = o * _silu(z_ref[...])


def _gdn_call(qkv, z, misc, a_rows, conv_w, alog_exp, dtb_exp, norm_g, B, S):
    T = B * S
    W = GDN_HPB * HEAD_DIM
    P = GDN_HEADS // GDN_HPB
    N = S // GDN_CHUNK
    col = lambda off: (lambda b, p: (b, off + p))
    par = lambda off: (lambda b, p: (0, off + p))
    hs = lambda n: pltpu.VMEM((GDN_HPB, S, n), F32)
    return pl.pallas_call(
        functools.partial(_gdn_kernel, S=S),
        grid=(B, P),
        in_specs=[pl.BlockSpec((S, W), col(0)), pl.BlockSpec((S, W), col(P)), pl.BlockSpec((S, W), col(2 * P)),
                  pl.BlockSpec((S, W), col(0)), pl.BlockSpec((S, LANES), lambda b, p: (b, 0)),
                  pl.BlockSpec((None, None, N, W), lambda b, p: (b, p, 0, 0)),
                  pl.BlockSpec((GDN_CONV, W), par(0)), pl.BlockSpec((GDN_CONV, W), par(P)),
                  pl.BlockSpec((GDN_CONV, W), par(2 * P)),
                  pl.BlockSpec((1, W), par(0)), pl.BlockSpec((1, W), par(0)),
                  pl.BlockSpec((1, W), par(0)), pl.BlockSpec((1, W), par(0)),
                  pl.BlockSpec((1, W), lambda b, p: (0, 0))],
        out_specs=pl.BlockSpec((S, W), col(0)),
        out_shape=jax.ShapeDtypeStruct((T, GDN_WIDTH), F32),
        scratch_shapes=[hs(HEAD_DIM), hs(HEAD_DIM), hs(HEAD_DIM), hs(2 * HEAD_DIM), hs(HEAD_DIM),
                        pltpu.VMEM((GDN_HPB, N, 8, HEAD_DIM), F32),
                        hs(HEAD_DIM), hs(HEAD_DIM), hs(HEAD_DIM), hs(HEAD_DIM),
                        pltpu.VMEM((GDN_HPB, S // 2, HEAD_DIM), F32), pltpu.VMEM((GDN_HPB, S // 2, HEAD_DIM), F32),
                        pltpu.VMEM((GDN_HPB, N // 2, HEAD_DIM, HEAD_DIM), F32)],
        compiler_params=_cparams("parallel", "parallel"),
        name="gdn",
    )(qkv, qkv, qkv, z, misc, a_rows, conv_w, conv_w, conv_w,
      alog_exp, dtb_exp, alog_exp, dtb_exp, norm_g)


def _cmp_kernel(xk_ref, xv_ref, pe_ref, w1_ref, b1_ref, w2k_ref, w2vt_ref, b2k_ref, b2v_ref, kg_ref,
                kc_ref, vct_ref):
    half = (CMP_BLOCK // 2) * HEAD_DIM
    nrows = xk_ref.shape[0]

    def hidden(x, j):
        pe = pe_ref[j]
        p = _dot(x + pe[:, :half], w1_ref[j, :half, :])
        q = _dot(x + pe[:, half:], w1_ref[j, half:, :])
        return _silu(p + pltpu.roll(q, nrows - 1, axis=0) + b1_ref[j])

    kc = _dot(hidden(xk_ref[...], 0), w2k_ref[...]) + b2k_ref[...]
    ms = jnp.sum(kc * kc, axis=-1, keepdims=True) * (1.0 / HEAD_DIM)
    kcn = kc * lax.rsqrt(ms + RMS_EPS) * kg_ref[...]
    lane = lax.broadcasted_iota(jnp.int32, kc.shape, 1)
    cend = lax.broadcasted_iota(jnp.int32, kc.shape, 0) * CMP_STRIDE + (CMP_BLOCK - 1)
    ext = jnp.where(lane == HEAD_DIM, cend // LANES, jnp.where(lane == HEAD_DIM + 1, cend % LANES, 0)).astype(F32)
    kc_ref[...] = jnp.where(lane < HEAD_DIM, kcn, ext)
    vct_ref[...] = _dot_nt(w2vt_ref[...], hidden(xv_ref[...], 1)) + b2v_ref[...]


def _cmp_call(xk, xv, pe, w1, b1, w2k, w2vt, b2k, b2v, kg):
    BG, NC, half = xk.shape
    blk = pl.BlockSpec((None, NC, half), lambda i: (i, 0, 0))
    full = lambda a: pl.BlockSpec(a.shape, lambda i: (0,) * a.ndim)
    return pl.pallas_call(
        _cmp_kernel,
        grid=(BG,),
        in_specs=[blk, blk] + [full(a) for a in (pe, w1, b1, w2k, w2vt, b2k, b2v, kg)],
        out_specs=[pl.BlockSpec((None, NC, LANES), lambda i: (i, 0, 0)),
                   pl.BlockSpec((None, HEAD_DIM, NC), lambda i: (i, 0, 0))],
        out_shape=[jax.ShapeDtypeStruct((BG, NC, LANES), F32), jax.ShapeDtypeStruct((BG, HEAD_DIM, NC), F32)],
        compiler_params=_cparams("parallel"),
        name="nsa_compress",
    )(xk, xv, pe, w1, b1, w2k, w2vt, b2k, b2v, kg)


NSA_TQ = 128
NSA_TK = 128
NSA_M = NSA_GROUP * NSA_TQ
NSA_MS = NSA_M
MASK_BIG = 2.0 ** 100
SEL_LANE0 = HEAD_DIM + 2


def _nsa_kernel(q_ref, kv_ref, vt_ref, kcx_ref, vct_ref, misct_ref, cbias_ref, tbl_ref, qg_ref, kg_ref,
                ovt_ref, ish_ref, o_ref, ks_s, kw_s, *, S):
    tq, tk, M = NSA_TQ, NSA_TK, NSA_M
    MS = NSA_MS
    G, R = NSA_KV_HEADS, NSA_GROUP
    NS = S // SLC_BLOCK
    n_top = min(SLC_TOPN, NS)
    qi = pl.program_id(1)
    t0 = qi * tq
    nt_dims = (((1,), (1,)), ((), ()))

    @pl.when(qi == 0)
    def _():
        rowi = lax.broadcasted_iota(jnp.int32, (S, LANES), 0)
        lane = lax.broadcasted_iota(jnp.int32, (S, LANES), 1)
        ext = jnp.where(lane == HEAD_DIM, rowi // LANES, jnp.where(lane == HEAD_DIM + 1, rowi % LANES, 0)).astype(F32)
        ext_slc = jnp.where(lane - SEL_LANE0 == rowi // SLC_BLOCK, -MASK_BIG, ext)

        def prep(c0, g, gain, extra):
            kraw = kv_ref[:, c0:c0 + LANES]
            if g == 1:
                kraw = pltpu.roll(kraw, HEAD_DIM, axis=1)
            ms = jnp.sum(jnp.where(lane < HEAD_DIM, kraw * kraw, 0.0), axis=-1, keepdims=True) * (1.0 / HEAD_DIM)
            return jnp.where(lane < HEAD_DIM, kraw * lax.rsqrt(ms + RMS_EPS) * gain, extra).astype(BF16)

        for g in range(G):
            ks_s[g] = prep(2 * NSA_KV_WIDTH, g, kg_ref[1:2, :], ext_slc)
            kw_s[g] = prep(3 * NSA_KV_WIDTH, g, kg_ref[2:3, :], ext)

    lane_q = lax.broadcasted_iota(jnp.int32, (tq, LANES), 1)
    gate_sig = _sigmoid(misct_ref[MISC_GATE0:MISC_GATE0 + 3 * NSA_HEADS, :])
    kt_d = t0 // tk

    def score_tiles(streams, kt, biased):
        rows = pl.ds(pl.multiple_of(kt * tk, tk), tk)
        out = []
        for qpp, k_s, g, _, idx_fn in streams:
            s = lax.dot_general(k_s[g, rows, :], qpp, nt_dims, preferred_element_type=F32)
            out.append(s + tbl_ref[idx_fn(kt)] if biased else s)
        return tuple(out)

    def softmax_pv(streams, kt, scores, states):
        soft = []
        for s, (m, l, _) in zip(scores, states):
            m_new = jnp.maximum(m, jnp.max(s, axis=0, keepdims=True))
            alpha = jnp.exp(m - m_new)
            p = jnp.exp(s - m_new)
            soft.append((m_new, alpha, alpha * l + jnp.sum(p, axis=0, keepdims=True), p.astype(BF16)))
        pvs = [jnp.dot(vt_ref[kt, st[3]:st[3] + HEAD_DIM, :], sf[3], preferred_element_type=F32)
               for st, sf in zip(streams, soft)]
        return tuple((m_new, l, alpha * acc + pv)
                     for (m_new, alpha, l, _), (_, _, acc), pv in zip(soft, states, pvs))

    flash_init = (jnp.full((1, M), NEG_BIG, F32), jnp.zeros((1, M), F32), jnp.zeros((HEAD_DIM, M), F32))

    qpps, o_cmps = [], []
    for g in range(G):
        qrows = []
        for r in range(R):
            hh = g * R + r
            qpair = q_ref[:, (hh // 2) * LANES:(hh // 2 + 1) * LANES]
            if hh % 2 == 1:
                qpair = pltpu.roll(qpair, HEAD_DIM, axis=1)
            ms = jnp.sum(jnp.where(lane_q < HEAD_DIM, qpair * qpair, 0.0), axis=-1, keepdims=True) * (1.0 / HEAD_DIM)
            qn = qpair * lax.rsqrt(ms + RMS_EPS) * qg_ref[...] * (HEAD_DIM ** -0.5)
            slope = 2.0 ** (-8.0 * (hh + 1) / NSA_HEADS)
            ex = jnp.where(lane_q == HEAD_DIM, slope * LANES, jnp.where(lane_q == HEAD_DIM + 1, slope, 0.0))
            qrows.append(jnp.where(lane_q < HEAD_DIM, qn, ex))
        qbase = jnp.concatenate(qrows, axis=0)

        cb = cbias_ref[...]
        s = (lax.dot_general(kcx_ref[g].astype(BF16), qbase.astype(BF16), nt_dims, preferred_element_type=F32)
             + jnp.concatenate([cb] * R, axis=1))
        mx = jnp.maximum(jnp.max(s, axis=0, keepdims=True), NEG_BIG)
        e = jnp.exp(s - mx)
        l = jnp.sum(e, axis=0, keepdims=True)
        p_cmp = e / jnp.where(l > 0.0, l, 1.0)
        o_cmps.append(_dot(vct_ref[g], p_cmp))

        psum = p_cmp[:, 0:tq]
        for r in range(1, R):
            psum = psum + p_cmp[:, r * tq:(r + 1) * tq]
        hi, mid, lo = _split3(psum)
        ovt = ovt_ref[...]
        imp = (jnp.dot(ovt, hi, preferred_element_type=F32) + jnp.dot(ovt, mid, preferred_element_type=F32)
               + jnp.dot(ovt, lo, preferred_element_type=F32))
        blk = lax.broadcasted_iota(jnp.int32, (NS, tq), 0)
        cur = (t0 + lax.broadcasted_iota(jnp.int32, (NS, tq), 1)) // SLC_BLOCK
        valid = blk <= cur
        forced = (blk == 0) | (blk == cur) | (blk == cur - 1)
        score = jnp.where(forced, SEL_BIG, jnp.where(valid, imp, -SEL_BIG))
        rank = jnp.zeros((NS, tq), F32)
        for j in range(NS):
            cj = score[j:j + 1, :]
            beats = (cj > score) | ((cj == score) & (blk > j))
            rank = rank + jnp.where(beats, 1.0, 0.0)
        nsel = jnp.where((rank < n_top) & valid, 0.0, 1.0).astype(BF16)
        nsel_q = lax.dot_general(nsel, ish_ref[...], (((0,), (0,)), ((), ())),
                                 preferred_element_type=F32)
        in_sel = (lane_q >= SEL_LANE0) & (lane_q < SEL_LANE0 + NS)
        qpps.append(jnp.concatenate([jnp.where(in_sel, nsel_q, qrows[r]) for r in range(R)], axis=0).astype(BF16))

    slc_idx = lambda kt: jnp.where(kt == kt_d, 1, 0)
    win_lo = kt_d - WINDOW // tk
    win_idx = lambda kt: jnp.where(kt == kt_d, 1, jnp.where(kt == win_lo, 2, 0))
    slc_streams = [(qpps[g], ks_s, g, g * HEAD_DIM, slc_idx) for g in range(G)]
    win_streams = [(qpps[g], kw_s, g, (G + g) * HEAD_DIM, win_idx) for g in range(G)]
    kt_w = jnp.maximum(win_lo, 0)
    def slc_only(kt, carry):
        scores, states = carry
        nxt = score_tiles(slc_streams, jnp.minimum(kt + 1, kt_w), False)
        return nxt, softmax_pv(slc_streams, kt, scores, states)

    _, st_slc = lax.fori_loop(0, kt_w, slc_only, (score_tiles(slc_streams, 0, False), (flash_init,) * G))
    both = slc_streams + win_streams
    st_all = lax.fori_loop(kt_w, kt_d + 1, lambda kt, st: softmax_pv(both, kt, score_tiles(both, kt, True), st),
                           st_slc + (flash_init,) * G)
    o_all = [acc / jnp.where(l > 0.0, l, 1.0) for _, l, acc in st_all]
    o_slcs, o_wins = o_all[:G], o_all[G:]

    for g in range(G):
        for r in range(R):
            hh = g * R + r
            cs = slice(r * tq, (r + 1) * tq)
            o_ref[hh * HEAD_DIM:(hh + 1) * HEAD_DIM, :] = (
                gate_sig[hh:hh + 1, :] * o_cmps[g][:, cs]
                + gate_sig[NSA_HEADS + hh:NSA_HEADS + hh + 1, :] * o_slcs[g][:, cs]
                + gate_sig[2 * NSA_HEADS + hh:2 * NSA_HEADS + hh + 1, :] * o_wins[g][:, cs])


def _nsa_consts(S):
    tq, tk, M = NSA_TQ, NSA_TK, NSA_M
    NC = S // CMP_STRIDE
    NS = S // SLC_BLOCK
    cs = np.arange(NC)[:, None] * CMP_STRIDE
    ss = np.arange(NS)[None, :] * SLC_BLOCK
    ov = np.clip(np.minimum(cs + CMP_BLOCK, ss + SLC_BLOCK) - np.maximum(cs, ss), 0, None) / CMP_BLOCK
    ov[NC - 1:] = 0.0
    ish = np.zeros((NS, LANES), np.float32)
    ish[np.arange(NS), SEL_LANE0 + np.arange(NS)] = 1.0
    t = np.arange(S).reshape(S // tq, 1, tq)
    n = np.arange(NC).reshape(1, NC, 1)
    cbias = np.where((t >= n * CMP_STRIDE + CMP_BLOCK - 1) & (n < NC - 1), 0.0, -MASK_BIG).astype(np.float32)
    i = np.arange(tk)[:, None]
    j = np.arange(M)[None, :] % tq
    tbl = np.stack([np.zeros((tk, M)), np.where(i <= j, 0.0, -MASK_BIG), np.where(i > j, 0.0, -MASK_BIG)])
    return (jnp.asarray(ov.T, BF16), jnp.asarray(ish, BF16), jnp.asarray(cbias), jnp.asarray(tbl, F32))


def _nsa_call(nq, kv, vt, kcx, vct, misct, q_norm_g, k_norm_g, B, S):
    T = B * S
    tq, tk, M = NSA_TQ, NSA_TK, NSA_M
    G = NSA_KV_HEADS
    nqt = S // tq
    NC = S // CMP_STRIDE
    NS = S // SLC_BLOCK
    ovt, ish, cbias, tbl = _nsa_consts(S)
    tile2 = lambda a: jnp.concatenate([a, a], axis=-1)
    row = lambda b, i: (b * nqt + i, 0)
    cst = lambda b, i: (0, 0)
    return pl.pallas_call(
        functools.partial(_nsa_kernel, S=S),
        grid=(B, nqt),
        in_specs=[pl.BlockSpec((tq, NSA_WIDTH), row),
                  pl.BlockSpec((S, KV_COLS), lambda b, i: (b, 0)),
                  pl.BlockSpec((None, S // tk, 2 * NSA_KV_WIDTH, tk), lambda b, i: (b, 0, 0, 0)),
                  pl.BlockSpec((None, G, NC, LANES), lambda b, i: (b, 0, 0, 0)),
                  pl.BlockSpec((None, G, HEAD_DIM, NC), lambda b, i: (b, 0, 0, 0)),
                  pl.BlockSpec((LANES, tq), lambda b, i: (0, b * nqt + i)),
                  pl.BlockSpec((None, NC, tq), lambda b, i: (i, 0, 0)),
                  pl.BlockSpec((3, tk, M), lambda b, i: (0, 0, 0)),
                  pl.BlockSpec((1, LANES), cst),
                  pl.BlockSpec((3, LANES), cst),
                  pl.BlockSpec((NS, NC), cst),
                  pl.BlockSpec((NS, LANES), cst)],
        out_specs=pl.BlockSpec((NSA_WIDTH, tq), lambda b, i: (0, b * nqt + i)),
        out_shape=jax.ShapeDtypeStruct((NSA_WIDTH, T), F32),
        scratch_shapes=[pltpu.VMEM((G, S, LANES), BF16), pltpu.VMEM((G, S, LANES), BF16)],
        compiler_params=_cparams("parallel", "arbitrary"),
        name="nsa_attn",
    )(nq, kv, vt, kcx, vct, misct, cbias, tbl, tile2(q_norm_g), tile2(k_norm_g), ovt, ish)


def _out_proj_kernel(yg_ref, yn_ref, x_ref, g1_ref, sc_ref, sh_ref, ng_ref, wo_ref, rwt_ref, rb_ref,
                     x1_ref, h2_ref, gate_ref, gate_t_ref):
    mix = _dot(yg_ref[...], wo_ref[:GDN_WIDTH, :]) + _dot_tn(yn_ref[...], wo_ref[GDN_WIDTH:, :])
    x1 = x_ref[...] + g1_ref[...] * mix
    x1_ref[...] = x1
    h2 = _rms(x1, ng_ref[...]) * (1.0 + sc_ref[...]) + sh_ref[...]
    h2_ref[...] = h2.astype(BF16)

    tm = h2.shape[0]
    logits = _dot_nt_hi(rwt_ref[...], h2)
    score = _sigmoid(logits)
    biased = score + rb_ref[...]
    b = [biased[e:e + 1, :] for e in range(N_EXPERTS)]
    n_groups = N_EXPERTS // EXPERTS_PER_GROUP
    gscore = []
    for gi in range(n_groups):
        vals = b[gi * EXPERTS_PER_GROUP:(gi + 1) * EXPERTS_PER_GROUP]
        best = None
        for i in range(EXPERTS_PER_GROUP):
            for j in range(i + 1, EXPERTS_PER_GROUP):
                pair = vals[i] + vals[j]
                best = pair if best is None else jnp.maximum(best, pair)
        gscore.append(best)
    gbest = jnp.zeros((1, tm), jnp.int32)
    top = gscore[0]
    for gi in range(1, n_groups):
        better = gscore[gi] > top
        gbest = jnp.where(better, gi, gbest)
        top = jnp.where(better, gscore[gi], top)
    erow = lax.broadcasted_iota(jnp.int32, (N_EXPERTS, tm), 0)
    gates = jnp.zeros((N_EXPERTS, tm), F32)
    for e in range(N_EXPERTS):
        gi = e // EXPERTS_PER_GROUP
        rank = jnp.zeros((1, tm), F32)
        for e2 in range(gi * EXPERTS_PER_GROUP, (gi + 1) * EXPERTS_PER_GROUP):
            if e2 == e:
                continue
            beats = (b[e2] > b[e]) | ((b[e2] == b[e]) & (e2 < e))
            rank = rank + jnp.where(beats, 1.0, 0.0)
        chosen = (gbest == gi) & (rank < 2.0)
        ge = jnp.where(chosen, score[e:e + 1, :], 0.0)
        gates = gates + jnp.where(erow == e, ge, 0.0)
    gates = gates / jnp.sum(gates, axis=0, keepdims=True)
    pad_rows = jnp.zeros((8 - EXPERTS_PER_GROUP, tm), F32)
    for gi in range(n_groups):
        gate_t_ref[gi] = jnp.concatenate(
            [gates[gi * EXPERTS_PER_GROUP:(gi + 1) * EXPERTS_PER_GROUP, :], pad_rows], axis=0)
    ident = (lax.broadcasted_iota(jnp.int32, (N_EXPERTS, LANES), 0)
             == lax.broadcasted_iota(jnp.int32, (N_EXPERTS, LANES), 1))
    ident = jnp.where(ident, 1.0, 0.0).astype(BF16)
    hi, mid, lo = _split3(gates)
    dn = (((0,), (0,)), ((), ()))
    gate_ref[...] = (lax.dot_general(hi, ident, dn, preferred_element_type=F32)
                     + lax.dot_general(mid, ident, dn, preferred_element_type=F32)
                     + lax.dot_general(lo, ident, dn, preferred_element_type=F32))


def _out_proj_call(yg, yn, x2, g1, sc2, sh2, ng, wo, rwt, rb, S):
    T, D = x2.shape
    tm = 512
    per_b = S // tm
    row = lambda i: (i, 0)
    bat = lambda i: (i // per_b, 0, 0)
    cst = lambda i: (0, 0)
    return pl.pallas_call(
        _out_proj_kernel,
        grid=(T // tm,),
        in_specs=[pl.BlockSpec((tm, GDN_WIDTH), row), pl.BlockSpec((NSA_WIDTH, tm), lambda i: (0, i)),
                  pl.BlockSpec((tm, D), row),
                  pl.BlockSpec((None, 1, D), bat), pl.BlockSpec((None, 1, D), bat), pl.BlockSpec((None, 1, D), bat),
                  pl.BlockSpec((1, D), cst), pl.BlockSpec((D, D), cst),
                  pl.BlockSpec((N_EXPERTS, D), cst), pl.BlockSpec((N_EXPERTS, 1), cst)],
        out_specs=[pl.BlockSpec((tm, D), row), pl.BlockSpec((tm, D), row), pl.BlockSpec((tm, LANES), row),
                   pl.BlockSpec((N_EXPERTS // EXPERTS_PER_GROUP, 8, tm), lambda i: (0, 0, i))],
        out_shape=[jax.ShapeDtypeStruct((T, D), F32), jax.ShapeDtypeStruct((T, D), BF16),
                   jax.ShapeDtypeStruct((T, LANES), F32),
                   jax.ShapeDtypeStruct((N_EXPERTS // EXPERTS_PER_GROUP, 8, T), F32)],
        compiler_params=_cparams("parallel"),
        name="out_proj_router",
    )(yg, yn, x2, g1, sc2, sh2, ng, wo, rwt, rb)


MOE_TM = 1024
MOE_RB = 288


def _moe_kernel(h_ref, gates_ref, gt8_ref, tri_ref, trit_ref, x1_ref, g2_ref, wg_ref, wu_ref, wd_ref, o_ref):
    g = pl.program_id(1)
    tm, rb = MOE_TM, MOE_RB
    nlb = tm // LANES

    @pl.when(g == 0)
    def _():
        o_ref[...] = jnp.zeros_like(o_ref)

    gt8 = gt8_ref[...]
    in_row = jnp.sum(gt8, axis=0, keepdims=True) > 0.0
    first = lax.broadcasted_iota(jnp.int32, (8, LANES), 0) == 0
    off = jnp.zeros((1, 1), F32)
    keys = []
    for b in range(nlb):
        blk = in_row[:, b * LANES:(b + 1) * LANES]
        one = jnp.where(blk & first, 1.0, 0.0).astype(BF16)
        within = jnp.dot(one, tri_ref[...], preferred_element_type=F32)[0:1, :]
        keys.append(jnp.where(blk, within + off, -1.0))
        off = off + jnp.sum(jnp.where(blk, 1.0, 0.0), axis=1, keepdims=True)
    key_row = jnp.concatenate(keys, axis=1)
    n_g = jnp.sum(off).astype(jnp.int32)
    lane = lax.broadcasted_iota(jnp.int32, (tm, LANES), 1)
    mine = (lane >= g * EXPERTS_PER_GROUP) & (lane < (g + 1) * EXPERTS_PER_GROUP)
    in_col = jnp.sum(jnp.where(mine, gates_ref[...], 0.0), axis=1, keepdims=True) > 0.0
    off = jnp.zeros((1, 1), F32)
    keys = []
    for b in range(nlb):
        blk = in_col[b * LANES:(b + 1) * LANES, :]
        one = jnp.broadcast_to(jnp.where(blk, 1.0, 0.0), (LANES, LANES)).astype(BF16)
        within = jnp.dot(trit_ref[...], one, preferred_element_type=F32)[:, 0:1]
        keys.append(jnp.where(blk, within + off, -1.0))
        off = off + jnp.sum(jnp.where(blk, 1.0, 0.0), axis=0, keepdims=True)
    key_col = jnp.concatenate(keys, axis=0)

    gt_hi, gt_mid, gt_lo = _split3(gt8)
    nt_dims = (((1,), (1,)), ((), ()))

    def block(j, carry):
        r0 = (j * rb).astype(F32)
        sel = jnp.where(key_row - r0 == lax.broadcasted_iota(jnp.int32, (rb, tm), 0).astype(F32), 1.0, 0.0)
        sel = sel.astype(BF16)
        xb = jnp.dot(sel, h_ref[...], preferred_element_type=F32).astype(BF16)
        gb = (lax.dot_general(sel, gt_hi, nt_dims, preferred_element_type=F32)
              + lax.dot_general(sel, gt_mid, nt_dims, preferred_element_type=F32)
              + lax.dot_general(sel, gt_lo, nt_dims, preferred_element_type=F32))
        y = jnp.zeros((rb, o_ref.shape[1]), F32)
        for k in range(EXPERTS_PER_GROUP):
            act = (_silu(jnp.dot(xb, wg_ref[k], preferred_element_type=F32))
                   * jnp.dot(xb, wu_ref[k], preferred_element_type=F32))
            y = y + _dot(act * gb[:, k:k + 1], wd_ref[k])
        sel_t = jnp.where(key_col - r0 == lax.broadcasted_iota(jnp.int32, (tm, rb), 1).astype(F32), 1.0, 0.0)
        o_ref[...] += jnp.dot(sel_t.astype(BF16), y.astype(BF16), preferred_element_type=F32)
        return carry

    lax.fori_loop(0, (n_g + rb - 1) // rb, block, 0)

    @pl.when(g == N_EXPERTS // EXPERTS_PER_GROUP - 1)
    def _():
        o_ref[...] = x1_ref[...] + g2_ref[...] * o_ref[...]


def _moe_call(h2, gates, gates_t, x1, g2, wg, wu, wd, S):
    T, D = x1.shape
    tm = MOE_TM
    per_b = S // tm
    n_groups = N_EXPERTS // EXPERTS_PER_GROUP
    i = np.arange(LANES)
    tri = jnp.asarray(i[:, None] < i[None, :], BF16)
    row = lambda i, g: (i, 0)
    cst = lambda i, g: (0, 0)
    wsel = lambda i, g: (g, 0, 0)
    return pl.pallas_call(
        _moe_kernel,
        grid=(T // tm, n_groups),
        in_specs=[pl.BlockSpec((tm, D), row), pl.BlockSpec((tm, LANES), row),
                  pl.BlockSpec((None, 8, tm), lambda i, g: (g, 0, i)),
                  pl.BlockSpec((LANES, LANES), cst), pl.BlockSpec((LANES, LANES), cst),
                  pl.BlockSpec((tm, D), row),
                  pl.BlockSpec((None, 1, D), lambda i, g: (i // per_b, 0, 0)),
                  pl.BlockSpec((EXPERTS_PER_GROUP, D, EXPERT_FF), wsel),
                  pl.BlockSpec((EXPERTS_PER_GROUP, D, EXPERT_FF), wsel),
                  pl.BlockSpec((EXPERTS_PER_GROUP, EXPERT_FF, D), wsel)],
        out_specs=pl.BlockSpec((tm, D), row),
        out_shape=jax.ShapeDtypeStruct((T, D), F32),
        compiler_params=_cparams("parallel", "arbitrary"),
        name="moe",
    )(h2, gates, gates_t, tri, tri.T, x1, g2, wg, wu, wd)


def kernel(x, c, ada_w, ada_b, norm1_g, norm2_g, w_in, gdn_conv_w, gdn_a_log, gdn_dt_bias, gdn_norm_g,
           nsa_q_norm_g, nsa_k_norm_g, cmp_pe, cmp_w1, cmp_b1, cmp_w2, cmp_b2, w_out, router_w, router_bias,
           exp_w_gate, exp_w_up, exp_w_down):
    B, S, D = x.shape
    L = ada_w.shape[0]
    T = B * S
    G = NSA_KV_HEADS
    NC = S // CMP_STRIDE
    N = S // GDN_CHUNK
    P = GDN_HEADS // GDN_HPB
    W = GDN_HPB * HEAD_DIM

    mod = _ada_call(c, ada_w, ada_b)
    rwt = router_w.T
    rb = router_bias.reshape(N_EXPERTS, 1)
    x2 = x.reshape(T, D)
    for l in range(L):
        m6 = mod[l].reshape(B, 6, 1, D)
        sh1, sc1, g1, sh2, sc2, g2 = (m6[:, i] for i in range(6))
        qkv, z, nq, kv, misc, misct, vt = _in_proj_call(
            x2, sc1, sh1, norm1_g[l].reshape(1, D), *_pad_w_in(w_in[l]), S)

        a_rows = misc[:, :GDN_HEADS].reshape(B, N, GDN_CHUNK, P, GDN_HPB).transpose(0, 3, 1, 4, 2).reshape(B, P, N, W)
        rep = lambda t: jnp.repeat(t, HEAD_DIM).reshape(1, GDN_WIDTH)
        y_gdn = _gdn_call(qkv, z, misc, a_rows, gdn_conv_w[l], rep(gdn_a_log[l]), rep(gdn_dt_bias[l]),
                          jnp.tile(gdn_norm_g[l], GDN_HPB).reshape(1, W), B, S)

        xkv = kv[:, :2 * NSA_KV_WIDTH].reshape(B, S, 2, G, HEAD_DIM).transpose(2, 0, 3, 1, 4)
        xkv = xkv.reshape(2, B * G, NC, CMP_STRIDE * HEAD_DIM)
        w2 = cmp_w2[l]
        kcx, vct = _cmp_call(
            xkv[0], xkv[1], cmp_pe[l].reshape(2, 1, CMP_BLOCK * HEAD_DIM), cmp_w1[l].astype(BF16),
            cmp_b1[l].reshape(2, 1, CMP_HIDDEN), jnp.pad(w2[0], ((0, 0), (0, LANES - HEAD_DIM))).astype(BF16),
            w2[1].T.astype(BF16), jnp.pad(cmp_b2[l, 0], (0, LANES - HEAD_DIM)).reshape(1, LANES),
            cmp_b2[l, 1].reshape(HEAD_DIM, 1), jnp.pad(nsa_k_norm_g[l, 0], (0, LANES - HEAD_DIM)).reshape(1, LANES))
        y_nsa_t = _nsa_call(nq, kv, vt.reshape(B, S // NSA_TK, 2 * NSA_KV_WIDTH, NSA_TK), kcx.reshape(B, G, NC, LANES), vct.reshape(B, G, HEAD_DIM, NC), misct,
                            nsa_q_norm_g[l].reshape(1, HEAD_DIM), nsa_k_norm_g[l], B, S)

        x1, h2, gates, gates_t = _out_proj_call(y_gdn, y_nsa_t, x2, g1, sc2, sh2, norm2_g[l].reshape(1, D),
                                       w_out[l].astype(BF16), rwt, rb, S)
        x2 = _moe_call(h2, gates, gates_t, x1, g2, exp_w_gate[l].astype(BF16), exp_w_up[l].astype(BF16),
                       exp_w_down[l].astype(BF16), S)
    return x2.reshape(B, S, D)
```

```python
import functools

import numpy as np
import jax
import jax.numpy as jnp
from jax import lax
from jax.experimental import pallas as pl
from jax.experimental.pallas import tpu as pltpu

F32 = jnp.float32
BF16 = jnp.bfloat16

HEAD_DIM = 64
GDN_HEADS = 8
GDN_WIDTH = GDN_HEADS * HEAD_DIM
GDN_CONV = 4
GDN_CHUNK = 64
NSA_HEADS = 8
NSA_KV_HEADS = 2
NSA_GROUP = NSA_HEADS // NSA_KV_HEADS
NSA_WIDTH = NSA_HEADS * HEAD_DIM
NSA_KV_WIDTH = NSA_KV_HEADS * HEAD_DIM
CMP_BLOCK = 32
CMP_STRIDE = 16
CMP_HIDDEN = 256
SLC_BLOCK = 64
SLC_TOPN = 8
WINDOW = 512
N_EXPERTS = 16
EXPERTS_PER_GROUP = 4
EXPERT_FF = 512
RMS_EPS = 1e-6
NEG_BIG = -1e30
SEL_BIG = 1e9

LANES = 128
VMEM_LIMIT_BYTES = 56 * 1024 * 1024

C_QKV = 0
C_Z = 3 * GDN_WIDTH
C_NQ = C_Z + GDN_WIDTH
C_KV = C_NQ + NSA_WIDTH
KV_COLS = 4 * NSA_KV_WIDTH
C_MISC = C_KV + KV_COLS
IN_PAD_COLS = C_MISC + LANES
MISC_B0 = GDN_HEADS
MISC_GATE0 = 16


def _cparams(*sem):
    return pltpu.CompilerParams(dimension_semantics=sem, vmem_limit_bytes=VMEM_LIMIT_BYTES)


def _dot(a, b):
    return jnp.dot(a.astype(BF16), b.astype(BF16), preferred_element_type=F32)


def _dot_nt(a, b):
    return lax.dot_general(a.astype(BF16), b.astype(BF16), (((1,), (1,)), ((), ())),
                           preferred_element_type=F32)


def _dot_tn(a, b):
    return lax.dot_general(a.astype(BF16), b.astype(BF16), (((0,), (0,)), ((), ())),
                           preferred_element_type=F32)


def _split2(x):
    hi = x.astype(BF16)
    lo = (x - hi.astype(F32)).astype(BF16)
    return hi, lo


def _split3(x):
    hi = x.astype(BF16)
    r = x - hi.astype(F32)
    mid = r.astype(BF16)
    lo = (r - mid.astype(F32)).astype(BF16)
    return hi, mid, lo


def _dot_hi(a, b):
    ah, al = _split2(a)
    bh, bl = _split2(b)
    return (jnp.dot(ah, bh, preferred_element_type=F32) + jnp.dot(al, bh, preferred_element_type=F32)
            + jnp.dot(ah, bl, preferred_element_type=F32))


def _dot_nt_hi(a, b):
    ah, al = _split2(a)
    bh, bl = _split2(b)
    dn = (((1,), (1,)), ((), ()))
    return (lax.dot_general(ah, bh, dn, preferred_element_type=F32)
            + lax.dot_general(al, bh, dn, preferred_element_type=F32)
            + lax.dot_general(ah, bl, dn, preferred_element_type=F32))


def _dot_exact_rhs(a, b_bf16):
    hi, mid, lo = _split3(a)
    return (jnp.dot(hi, b_bf16, preferred_element_type=F32) + jnp.dot(mid, b_bf16, preferred_element_type=F32)
            + jnp.dot(lo, b_bf16, preferred_element_type=F32))


def _rms(x, g):
    return x * lax.rsqrt(jnp.mean(x * x, axis=-1, keepdims=True) + RMS_EPS) * g


def _sigmoid(x):
    return 0.5 * (jnp.tanh(0.5 * x) + 1.0)


def _silu(x):
    return x * _sigmoid(x)


def _ada_kernel(c_ref, w_ref, b_ref, o_ref):
    o_ref[...] = _dot_hi(_silu(c_ref[...]), w_ref[...]) + b_ref[...]


def _ada_call(c, ada_w, ada_b):
    L, D, N = ada_w.shape
    B = c.shape[0]
    tn = 1024
    return pl.pallas_call(
        _ada_kernel,
        grid=(L, N // tn),
        in_specs=[pl.BlockSpec((B, D), lambda l, j: (0, 0)),
                  pl.BlockSpec((None, D, tn), lambda l, j: (l, 0, j)),
                  pl.BlockSpec((None, 1, tn), lambda l, j: (l, 0, j))],
        out_specs=pl.BlockSpec((None, B, tn), lambda l, j: (l, 0, j)),
        out_shape=jax.ShapeDtypeStruct((L, B, N), F32),
        compiler_params=_cparams("parallel", "parallel"),
        name="ada_mod",
    )(c, ada_w, ada_b.reshape(L, 1, N))


IN_SEGS = ((C_QKV, 3 * GDN_WIDTH), (C_Z, GDN_WIDTH), (C_NQ, NSA_WIDTH), (C_KV, KV_COLS), (C_MISC, LANES))


def _in_proj_kernel(x_ref, sc_ref, sh_ref, g_ref, w_ref, wmt_ref, wvt_ref, *o_refs):
    h = _rms(x_ref[...], g_ref[...]) * (1.0 + sc_ref[...]) + sh_ref[...]
    hb = h.astype(BF16)
    for (c0, n), o_ref in zip(IN_SEGS, o_refs[:-2]):
        o_ref[...] = jnp.dot(hb, w_ref[:, c0:c0 + n], preferred_element_type=F32)
    nt_dims = (((1,), (1,)), ((), ()))
    o_refs[-2][...] = lax.dot_general(wmt_ref[...], hb, nt_dims, preferred_element_type=F32)
    vt = lax.dot_general(wvt_ref[...], hb, nt_dims, preferred_element_type=F32).astype(BF16)
    for j in range(o_refs[-1].shape[0]):
        o_refs[-1][j] = vt[:, j * LANES:(j + 1) * LANES]


def _in_proj_call(x2, sc, sh, g, w_pad, wvt, S):
    T, D = x2.shape
    tm = 256
    per_b = S // tm
    row = lambda i: (i, 0)
    bat = lambda i: (i // per_b, 0, 0)
    return pl.pallas_call(
        _in_proj_kernel,
        grid=(T // tm,),
        in_specs=[pl.BlockSpec((tm, D), row),
                  pl.BlockSpec((None, 1, D), bat),
                  pl.BlockSpec((None, 1, D), bat),
                  pl.BlockSpec((1, D), lambda i: (0, 0)),
                  pl.BlockSpec((D, IN_PAD_COLS), lambda i: (0, 0)),
                  pl.BlockSpec((LANES, D), lambda i: (0, 0)),
                  pl.BlockSpec((2 * NSA_KV_WIDTH, D), lambda i: (0, 0))],
        out_specs=([pl.BlockSpec((tm, n), row) for _, n in IN_SEGS]
                   + [pl.BlockSpec((LANES, tm), lambda i: (0, i)),
                      pl.BlockSpec((tm // LANES, 2 * NSA_KV_WIDTH, LANES), lambda i: (i, 0, 0))]),
        out_shape=([jax.ShapeDtypeStruct((T, n), F32) for _, n in IN_SEGS]
                   + [jax.ShapeDtypeStruct((LANES, T), F32),
                      jax.ShapeDtypeStruct((T // LANES, 2 * NSA_KV_WIDTH, LANES), BF16)]),
        compiler_params=_cparams("parallel"),
        name="in_proj",
    )(x2, sc, sh, g, w_pad, w_pad[:, C_MISC:].T, wvt)


def _pad_w_in(w_in):
    D = w_in.shape[0]
    W = GDN_WIDTH
    o = 0
    gq, gk, gv, gz = (w_in[:, o + i * W:o + (i + 1) * W] for i in range(4))
    o += 4 * W
    ga = w_in[:, o:o + GDN_HEADS]
    gb = w_in[:, o + GDN_HEADS:o + 2 * GDN_HEADS]
    o += 2 * GDN_HEADS
    nq = w_in[:, o:o + NSA_WIDTH]
    o += NSA_WIDTH
    kc, vc, ks, vs, kw, vw = (w_in[:, o + i * NSA_KV_WIDTH:o + (i + 1) * NSA_KV_WIDTH] for i in range(6))
    o += 6 * NSA_KV_WIDTH
    gates = w_in[:, o:o + 3 * NSA_HEADS]
    misc = jnp.zeros((D, LANES), w_in.dtype)
    misc = misc.at[:, 0:GDN_HEADS].set(ga).at[:, MISC_B0:MISC_B0 + GDN_HEADS].set(gb)
    misc = misc.at[:, MISC_GATE0:MISC_GATE0 + 3 * NSA_HEADS].set(gates)
    cols = [gq, gk, gv, gz, nq, kc, vc, ks, kw, misc]
    return jnp.concatenate(cols, axis=1).astype(BF16), jnp.concatenate([vs, vw], axis=1).T.astype(BF16)


GDN_HPB = 2
GDN_NB = 32


def _gdn_kernel(q_ref, k_ref, v_ref, z_ref, misc_ref, ar_ref, cwq_ref, cwk_ref, cwv_ref,
                alog_ref, dtb_ref, alogr_ref, dtbr_ref, ng_ref, o_ref,
                qs, ks, kbs, xs, gcs, grs, ps, qq, qks, os_, rs, q2s, sev, *, S):
    C = GDN_CHUNK
    N = S // C
    W = GDN_HPB * HEAD_DIM
    row = lax.broadcasted_iota(jnp.int32, (S, W), 0)
    row8 = lax.broadcasted_iota(jnp.int32, (8, W), 0)

    def conv_silu(x, w):
        y = x * w[GDN_CONV - 1:GDN_CONV, :]
        for s in range(1, GDN_CONV):
            xs = pltpu.roll(x, s, axis=0)
            xs = jnp.concatenate([jnp.where(row8 >= s, xs[:8], 0.0), xs[8:]], axis=0)
            y = y + xs * w[GDN_CONV - 1 - s:GDN_CONV - s, :]
        return _silu(y)

    bd = jnp.where(lax.broadcasted_iota(jnp.int32, (W, W), 0) // HEAD_DIM
                   == lax.broadcasted_iota(jnp.int32, (W, W), 1) // HEAD_DIM, 1.0, 0.0).astype(BF16)

    def head_sumsq(x):
        hi, lo = _split2(x * x)
        return jnp.dot(hi, bd, preferred_element_type=F32) + jnp.dot(lo, bd, preferred_element_type=F32)

    def head_l2(x):
        return x * lax.rsqrt(head_sumsq(x) + RMS_EPS)

    q = head_l2(conv_silu(q_ref[...], cwq_ref[...])) * (HEAD_DIM ** -0.5)
    k = head_l2(conv_silu(k_ref[...], cwk_ref[...]))
    v = conv_silu(v_ref[...], cwv_ref[...])

    hd0 = pl.program_id(1) * GDN_HPB
    src = lax.broadcasted_iota(jnp.int32, (LANES, W), 0)
    dst = lax.broadcasted_iota(jnp.int32, (LANES, W), 1) // HEAD_DIM + hd0
    a_exp = _dot_exact_rhs(misc_ref[...], jnp.where(src == dst, 1.0, 0.0).astype(BF16))
    b_exp = _dot_exact_rhs(misc_ref[...], jnp.where(src == dst + MISC_B0, 1.0, 0.0).astype(BF16))
    beta = _sigmoid(b_exp)

    def log_decay(a, alog, dtb):
        xx = a + dtb
        sp = jnp.maximum(xx, 0.0) + jnp.log(1.0 + jnp.exp(-jnp.abs(xx)))
        return -jnp.exp(alog) * sp

    gc = log_decay(a_exp, alog_ref[...], dtb_ref[...])
    pos = row % C
    for s in (1, 2, 4, 8, 16, 32):
        gc = gc + jnp.where(pos >= s, pltpu.roll(gc, s, axis=0), 0.0)
    gr = log_decay(ar_ref[...], alogr_ref[...], dtbr_ref[...])
    lpos = lax.broadcasted_iota(jnp.int32, (N, W), 1) % C
    for s in (1, 2, 4, 8, 16, 32):
        gr = gr + jnp.where(lpos >= s, pltpu.roll(gr, s, axis=1), 0.0)

    kb = k * beta
    vb = v * beta
    kbe = kb * jnp.exp(gc)
    for h in range(GDN_HPB):
        sl = slice(h * HEAD_DIM, (h + 1) * HEAD_DIM)
        qs[h] = q[:, sl]
        ks[h] = k[:, sl]
        kbs[h] = kb[:, sl]
        gcs[h] = gc[:, sl]
        for n in range(N):
            grs[h, n] = jnp.broadcast_to(gr[n:n + 1, sl], (8, HEAD_DIM))
        xs[h] = jnp.concatenate([kbe[:, sl], vb[:, sl]], axis=1)

    NB = GDN_NB
    ci3 = lax.broadcasted_iota(jnp.int32, (1, C, C), 1)
    cj3 = lax.broadcasted_iota(jnp.int32, (1, C, C), 2)
    eye3 = jnp.where(ci3 == cj3, 1.0, 0.0)

    def bmm(a, b):
        return lax.dot_general(a.astype(BF16), b.astype(BF16), (((2,), (1,)), ((0,), (0,))),
                               preferred_element_type=F32)

    def bmm_nt(a, b):
        return lax.dot_general(a.astype(BF16), b.astype(BF16), (((2,), (2,)), ((0,), (0,))),
                               preferred_element_type=F32)

    def intra(nb, carry):
        rows = pl.ds(pl.multiple_of(nb * (NB * C), NB * C), NB * C)
        for h in range(GDN_HPB):
            kk = ks[h, rows, :].reshape(NB, C, HEAD_DIM)
            kbq = jnp.concatenate([kbs[h, rows, :].reshape(NB, C, HEAD_DIM),
                                   qs[h, rows, :].reshape(NB, C, HEAD_DIM)], axis=1)
            a2 = bmm_nt(kbq, kk)
            gcol = gcs[h, rows, :].reshape(NB, C, HEAD_DIM)
            grow = grs[h, pl.ds(nb * NB, NB), 0:1, :]
            decay = jnp.exp(jnp.where(ci3 >= cj3, gcol - grow, NEG_BIG))
            low = jnp.where(ci3 > cj3, a2[:, :C] * decay, 0.0)
            qks[h, rows, :] = (a2[:, C:] * decay).reshape(NB * C, C)
            tinv = eye3 - jnp.where(ci3 // 2 == cj3 // 2, low, 0.0)
            for lv in range(1, 6):
                s = 2 ** lv
                off = jnp.where((ci3 // (2 * s) == cj3 // (2 * s)) & (ci3 // s != cj3 // s), low, 0.0)
                tinv = tinv - bmm(tinv, bmm(off, tinv))
            x = bmm(tinv, xs[h, rows, :].reshape(NB, C, 2 * HEAD_DIM))
            xs[h, rows, :] = x.reshape(NB * C, 2 * HEAD_DIM)
            kd = kk * jnp.exp(gcol[:, C - 1:C, :] - gcol)
            pq = lax.dot_general(kd.astype(BF16), x.astype(BF16), (((1,), (1,)), ((0,), (0,))),
                                 preferred_element_type=F32)
            pn = pq[:, :, :HEAD_DIM]
            qn = pq[:, :, HEAD_DIM:]
            ps[h, rows, :] = pn.reshape(NB * C, HEAD_DIM)
            qq[h, rows, :] = qn.reshape(NB * C, HEAD_DIM)
            cd = jnp.exp(gcol[:, C - 1:C, :]).reshape(NB // 2, 2, 1, HEAD_DIM)
            pn = pn.reshape(NB // 2, 2, HEAD_DIM, HEAD_DIM)
            qn = qn.reshape(NB // 2, 2, HEAD_DIM, HEAD_DIM)
            p0, p1, q0, q1, c0, c1 = pn[:, 0], pn[:, 1], qn[:, 0], qn[:, 1], cd[:, 0], cd[:, 1]
            t = bmm(p1, pq.reshape(NB // 2, 2, HEAD_DIM, 2 * HEAD_DIM)[:, 0])
            prow = pl.ds(pl.multiple_of(nb * (NB // 2 * C), NB // 2 * C), NB // 2 * C)
            rs[h, prow, :] = (c1 * p0 + c0 * p1 - t[:, :, :HEAD_DIM]).reshape(NB // 2 * C, HEAD_DIM)
            q2s[h, prow, :] = (c1 * q0 - t[:, :, HEAD_DIM:] + q1).reshape(NB // 2 * C, HEAD_DIM)
        return carry

    lax.fori_loop(0, N // NB, intra, 0)

    def inter(k, states):
        sl = pl.ds(pl.multiple_of(k * C, C), C)
        new_states = []
        for h in range(GDN_HPB):
            st = states[h]
            sev[h, k] = st
            gsum = gcs[h, pl.ds(2 * k * C + C - 1, 1), :] + gcs[h, pl.ds(2 * k * C + 2 * C - 1, 1), :]
            new_states.append(st * jnp.exp(gsum) - _dot_hi(rs[h, sl, :], st) + q2s[h, sl, :])
        return tuple(new_states)

    lax.fori_loop(0, N // 2, inter, tuple(jnp.zeros((HEAD_DIM, HEAD_DIM), F32) for _ in range(GDN_HPB)))

    def outputs(nb, carry):
        rows = pl.ds(pl.multiple_of(nb * (NB * C), NB * C), NB * C)
        for h in range(GDN_HPB):
            gcol = gcs[h, rows, :].reshape(NB, C, HEAD_DIM)
            s0 = sev[h, pl.ds(nb * (NB // 2), NB // 2)]
            p0 = ps[h, rows, :].reshape(NB // 2, 2, HEAD_DIM, HEAD_DIM)[:, 0]
            q0 = qq[h, rows, :].reshape(NB // 2, 2, HEAD_DIM, HEAD_DIM)[:, 0]
            c0 = jnp.exp(gcol[:, C - 1:C, :]).reshape(NB // 2, 2, 1, HEAD_DIM)[:, 0]
            s1 = c0 * s0 - bmm(p0, s0) + q0
            st = jnp.stack([s0, s1], axis=1).reshape(NB, HEAD_DIM, HEAD_DIM)
            x = xs[h, rows, :].reshape(NB, C, 2 * HEAD_DIM)
            qd = qs[h, rows, :].reshape(NB, C, HEAD_DIM) * jnp.exp(gcol)
            t1 = bmm(jnp.concatenate([x[:, :, :HEAD_DIM], qd], axis=1), st)
            vn = x[:, :, HEAD_DIM:] - t1[:, :C]
            o = t1[:, C:] + bmm(qks[h, rows, :].reshape(NB, C, C), vn)
            os_[h, rows, :] = o.reshape(NB * C, HEAD_DIM)
        return carry

    lax.fori_loop(0, N // NB, outputs, 0)

    o = jnp.concatenate([os_[h] for h in range(GDN_HPB)], axis=1)
    o = o * lax.rsqrt(head_sumsq(o) * (1.0 / HEAD_DIM) + RMS_EPS) * ng_ref[...]
    o_ref[...] = o * _silu(z_ref[...])


def _gdn_call(qkv, z, misc, a_rows, conv_w, alog_exp, dtb_exp, norm_g, B, S):
    T = B * S
    W = GDN_HPB * HEAD_DIM
    P = GDN_HEADS // GDN_HPB
    N = S // GDN_CHUNK
    col = lambda off: (lambda b, p: (b, off + p))
    par = lambda off: (lambda b, p: (0, off + p))
    hs = lambda n: pltpu.VMEM((GDN_HPB, S, n), F32)
    return pl.pallas_call(
        functools.partial(_gdn_kernel, S=S),
        grid=(B, P),
        in_specs=[pl.BlockSpec((S, W), col(0)), pl.BlockSpec((S, W), col(P)), pl.BlockSpec((S, W), col(2 * P)),
                  pl.BlockSpec((S, W), col(0)), pl.BlockSpec((S, LANES), lambda b, p: (b, 0)),
                  pl.BlockSpec((None, None, N, W), lambda b, p: (b, p, 0, 0)),
                  pl.BlockSpec((GDN_CONV, W), par(0)), pl.BlockSpec((GDN_CONV, W), par(P)),
                  pl.BlockSpec((GDN_CONV, W), par(2 * P)),
                  pl.BlockSpec((1, W), par(0)), pl.BlockSpec((1, W), par(0)),
                  pl.BlockSpec((1, W), par(0)), pl.BlockSpec((1, W), par(0)),
                  pl.BlockSpec((1, W), lambda b, p: (0, 0))],
        out_specs=pl.BlockSpec((S, W), col(0)),
        out_shape=jax.ShapeDtypeStruct((T, GDN_WIDTH), F32),
        scratch_shapes=[hs(HEAD_DIM), hs(HEAD_DIM), hs(HEAD_DIM), hs(2 * HEAD_DIM), hs(HEAD_DIM),
                        pltpu.VMEM((GDN_HPB, N, 8, HEAD_DIM), F32),
                        hs(HEAD_DIM), hs(HEAD_DIM), hs(HEAD_DIM), hs(HEAD_DIM),
                        pltpu.VMEM((GDN_HPB, S // 2, HEAD_DIM), F32), pltpu.VMEM((GDN_HPB, S // 2, HEAD_DIM), F32),
                        pltpu.VMEM((GDN_HPB, N // 2, HEAD_DIM, HEAD_DIM), F32)],
        compiler_params=_cparams("parallel", "parallel"),
        name="gdn",
    )(qkv, qkv, qkv, z, misc, a_rows, conv_w, conv_w, conv_w,
      alog_exp, dtb_exp, alog_exp, dtb_exp, norm_g)


def _cmp_kernel(xk_ref, xv_ref, pe_ref, w1_ref, b1_ref, w2k_ref, w2vt_ref, b2k_ref, b2v_ref, kg_ref,
                kc_ref, vct_ref):
    half = (CMP_BLOCK // 2) * HEAD_DIM
    nrows = xk_ref.shape[0]

    def hidden(x, j):
        pe = pe_ref[j]
        p = _dot(x + pe[:, :half], w1_ref[j, :half, :])
        q = _dot(x + pe[:, half:], w1_ref[j, half:, :])
        return _silu(p + pltpu.roll(q, nrows - 1, axis=0) + b1_ref[j])

    kc = _dot(hidden(xk_ref[...], 0), w2k_ref[...]) + b2k_ref[...]
    ms = jnp.sum(kc * kc, axis=-1, keepdims=True) * (1.0 / HEAD_DIM)
    kcn = kc * lax.rsqrt(ms + RMS_EPS) * kg_ref[...]
    lane = lax.broadcasted_iota(jnp.int32, kc.shape, 1)
    cend = lax.broadcasted_iota(jnp.int32, kc.shape, 0) * CMP_STRIDE + (CMP_BLOCK - 1)
    ext = jnp.where(lane == HEAD_DIM, cend // LANES, jnp.where(lane == HEAD_DIM + 1, cend % LANES, 0)).astype(F32)
    kc_ref[...] = jnp.where(lane < HEAD_DIM, kcn, ext)
    vct_ref[...] = _dot_nt(w2vt_ref[...], hidden(xv_ref[...], 1)) + b2v_ref[...]


def _cmp_call(xk, xv, pe, w1, b1, w2k, w2vt, b2k, b2v, kg):
    BG, NC, half = xk.shape
    blk = pl.BlockSpec((None, NC, half), lambda i: (i, 0, 0))
    full = lambda a: pl.BlockSpec(a.shape, lambda i: (0,) * a.ndim)
    return pl.pallas_call(
        _cmp_kernel,
        grid=(BG,),
        in_specs=[blk, blk] + [full(a) for a in (pe, w1, b1, w2k, w2vt, b2k, b2v, kg)],
        out_specs=[pl.BlockSpec((None, NC, LANES), lambda i: (i, 0, 0)),
                   pl.BlockSpec((None, HEAD_DIM, NC), lambda i: (i, 0, 0))],
        out_shape=[jax.ShapeDtypeStruct((BG, NC, LANES), F32), jax.ShapeDtypeStruct((BG, HEAD_DIM, NC), F32)],
        compiler_params=_cparams("parallel"),
        name="nsa_compress",
    )(xk, xv, pe, w1, b1, w2k, w2vt, b2k, b2v, kg)


NSA_TQ = 128
NSA_TK = 128
NSA_M = NSA_GROUP * NSA_TQ
NSA_MS = NSA_M
MASK_BIG = 2.0 ** 100
SEL_LANE0 = HEAD_DIM + 2


def _nsa_kernel(q_ref, kv_ref, vt_ref, kcx_ref, vct_ref, misct_ref, cbias_ref, tbl_ref, qg_ref, kg_ref,
                ovt_ref, ish_ref, o_ref, ks_s, kw_s, live, n_live, *, S):
    tq, tk, M = NSA_TQ, NSA_TK, NSA_M
    MS = NSA_MS
    G, R = NSA_KV_HEADS, NSA_GROUP
    NS = S // SLC_BLOCK
    n_top = min(SLC_TOPN, NS)
    qi = pl.program_id(1)
    t0 = qi * tq
    nt_dims = (((1,), (1,)), ((), ()))

    @pl.when(qi == 0)
    def _():
        rowi = lax.broadcasted_iota(jnp.int32, (S, LANES), 0)
        lane = lax.broadcasted_iota(jnp.int32, (S, LANES), 1)
        ext = jnp.where(lane == HEAD_DIM, rowi // LANES, jnp.where(lane == HEAD_DIM + 1, rowi % LANES, 0)).astype(F32)
        ext_slc = jnp.where(lane - SEL_LANE0 == rowi // SLC_BLOCK, -MASK_BIG, ext)

        def prep(c0, g, gain, extra):
            kraw = kv_ref[:, c0:c0 + LANES]
            if g == 1:
                kraw = pltpu.roll(kraw, HEAD_DIM, axis=1)
            ms = jnp.sum(jnp.where(lane < HEAD_DIM, kraw * kraw, 0.0), axis=-1, keepdims=True) * (1.0 / HEAD_DIM)
            return jnp.where(lane < HEAD_DIM, kraw * lax.rsqrt(ms + RMS_EPS) * gain, extra).astype(BF16)

        for g in range(G):
            ks_s[g] = prep(2 * NSA_KV_WIDTH, g, kg_ref[1:2, :], ext_slc)
            kw_s[g] = prep(3 * NSA_KV_WIDTH, g, kg_ref[2:3, :], ext)

    lane_q = lax.broadcasted_iota(jnp.int32, (tq, LANES), 1)
    gate_sig = _sigmoid(misct_ref[MISC_GATE0:MISC_GATE0 + 3 * NSA_HEADS, :])
    kt_d = t0 // tk

    def score_tiles(streams, kt, biased):
        rows = pl.ds(pl.multiple_of(kt * tk, tk), tk)
        out = []
        for qpp, k_s, g, _, idx_fn in streams:
            s = lax.dot_general(k_s[g, rows, :], qpp, nt_dims, preferred_element_type=F32)
            out.append(s + tbl_ref[idx_fn(kt)] if biased else s)
        return tuple(out)

    def softmax_pv(streams, kt, scores, states):
        soft = []
        for s, (m, l, _) in zip(scores, states):
            m_new = jnp.maximum(m, jnp.max(s, axis=0, keepdims=True))
            alpha = jnp.exp(m - m_new)
            p = jnp.exp(s - m_new)
            soft.append((m_new, alpha, alpha * l + jnp.sum(p, axis=0, keepdims=True), p.astype(BF16)))
        pvs = [jnp.dot(vt_ref[kt, st[3]:st[3] + HEAD_DIM, :], sf[3], preferred_element_type=F32)
               for st, sf in zip(streams, soft)]
        return tuple((m_new, l, alpha * acc + pv)
                     for (m_new, alpha, l, _), (_, _, acc), pv in zip(soft, states, pvs))

    flash_init = (jnp.full((1, M), NEG_BIG, F32), jnp.zeros((1, M), F32), jnp.zeros((HEAD_DIM, M), F32))

    qpps, o_cmps, unsel = [], [], []
    for g in range(G):
        qrows = []
        for r in range(R):
            hh = g * R + r
            qpair = q_ref[:, (hh // 2) * LANES:(hh // 2 + 1) * LANES]
            if hh % 2 == 1:
                qpair = pltpu.roll(qpair, HEAD_DIM, axis=1)
            ms = jnp.sum(jnp.where(lane_q < HEAD_DIM, qpair * qpair, 0.0), axis=-1, keepdims=True) * (1.0 / HEAD_DIM)
            qn = qpair * lax.rsqrt(ms + RMS_EPS) * qg_ref[...] * (HEAD_DIM ** -0.5)
            slope = 2.0 ** (-8.0 * (hh + 1) / NSA_HEADS)
            ex = jnp.where(lane_q == HEAD_DIM, slope * LANES, jnp.where(lane_q == HEAD_DIM + 1, slope, 0.0))
            qrows.append(jnp.where(lane_q < HEAD_DIM, qn, ex))
        qbase = jnp.concatenate(qrows, axis=0)

        cb = cbias_ref[...]
        s = (lax.dot_general(kcx_ref[g].astype(BF16), qbase.astype(BF16), nt_dims, preferred_element_type=F32)
             + jnp.concatenate([cb] * R, axis=1))
        mx = jnp.maximum(jnp.max(s, axis=0, keepdims=True), NEG_BIG)
        e = jnp.exp(s - mx)
        l = jnp.sum(e, axis=0, keepdims=True)
        p_cmp = e / jnp.where(l > 0.0, l, 1.0)
        o_cmps.append(_dot(vct_ref[g], p_cmp))

        psum = p_cmp[:, 0:tq]
        for r in range(1, R):
            psum = psum + p_cmp[:, r * tq:(r + 1) * tq]
        hi, mid, lo = _split3(psum)
        ovt = ovt_ref[...]
        imp = (jnp.dot(ovt, hi, preferred_element_type=F32) + jnp.dot(ovt, mid, preferred_element_type=F32)
               + jnp.dot(ovt, lo, preferred_element_type=F32))
        blk = lax.broadcasted_iota(jnp.int32, (NS, tq), 0)
        cur = (t0 + lax.broadcasted_iota(jnp.int32, (NS, tq), 1)) // SLC_BLOCK
        valid = blk <= cur
        forced = (blk == 0) | (blk == cur) | (blk == cur - 1)
        score = jnp.where(forced, SEL_BIG, jnp.where(valid, imp, -SEL_BIG))
        rank = jnp.zeros((NS, tq), F32)
        for j in range(NS):
            cj = score[j:j + 1, :]
            beats = (cj > score) | ((cj == score) & (blk > j))
            rank = rank + jnp.where(beats, 1.0, 0.0)
        nsel = jnp.where((rank < n_top) & valid, 0.0, 1.0)
        unsel.append(jnp.min(nsel, axis=1, keepdims=True))
        nsel = nsel.astype(BF16)
        nsel_q = lax.dot_general(nsel, ish_ref[...], (((0,), (0,)), ((), ())),
                                 preferred_element_type=F32)
        in_sel = (lane_q >= SEL_LANE0) & (lane_q < SEL_LANE0 + NS)
        qpps.append(jnp.concatenate([jnp.where(in_sel, nsel_q, qrows[r]) for r in range(R)], axis=0).astype(BF16))

    slc_idx = lambda kt: jnp.where(kt == kt_d, 1, 0)
    win_lo = kt_d - WINDOW // tk
    win_idx = lambda kt: jnp.where(kt == kt_d, 1, jnp.where(kt == win_lo, 2, 0))
    slc_streams = [(qpps[g], ks_s, g, g * HEAD_DIM, slc_idx) for g in range(G)]
    win_streams = [(qpps[g], kw_s, g, (G + g) * HEAD_DIM, win_idx) for g in range(G)]
    kt_w = jnp.maximum(win_lo, 0)

    blocks_per_tile = tk // SLC_BLOCK
    unsel_any = unsel[0]
    for g in range(1, G):
        unsel_any = jnp.minimum(unsel_any, unsel[g])
    n_live[0] = 0
    for kt in range(S // tk - WINDOW // tk):
        dead = jnp.min(unsel_any[kt * blocks_per_tile:(kt + 1) * blocks_per_tile, :])

        @pl.when((dead < 0.5) & (kt < kt_w))
        def _():
            live[n_live[0]] = kt
            n_live[0] = n_live[0] + 1

    n_slc = n_live[0]

    def slc_only(i, carry):
        scores, states = carry
        nxt = score_tiles(slc_streams, live[jnp.minimum(i + 1, n_slc - 1)], False)
        return nxt, softmax_pv(slc_streams, live[i], scores, states)

    first = jnp.where(n_slc > 0, live[0], 0)
    _, st_slc = lax.fori_loop(0, n_slc, slc_only, (score_tiles(slc_streams, first, False), (flash_init,) * G))
    both = slc_streams + win_streams
    st_all = lax.fori_loop(kt_w, kt_d + 1, lambda kt, st: softmax_pv(both, kt, score_tiles(both, kt, True), st),
                           st_slc + (flash_init,) * G)
    o_all = [acc / jnp.where(l > 0.0, l, 1.0) for _, l, acc in st_all]
    o_slcs, o_wins = o_all[:G], o_all[G:]

    for g in range(G):
        for r in range(R):
            hh = g * R + r
            cs = slice(r * tq, (r + 1) * tq)
            o_ref[hh * HEAD_DIM:(hh + 1) * HEAD_DIM, :] = (
                gate_sig[hh:hh + 1, :] * o_cmps[g][:, cs]
                + gate_sig[NSA_HEADS + hh:NSA_HEADS + hh + 1, :] * o_slcs[g][:, cs]
                + gate_sig[2 * NSA_HEADS + hh:2 * NSA_HEADS + hh + 1, :] * o_wins[g][:, cs])


def _nsa_consts(S):
    tq, tk, M = NSA_TQ, NSA_TK, NSA_M
    NC = S // CMP_STRIDE
    NS = S // SLC_BLOCK
    cs = np.arange(NC)[:, None] * CMP_STRIDE
    ss = np.arange(NS)[None, :] * SLC_BLOCK
    ov = np.clip(np.minimum(cs + CMP_BLOCK, ss + SLC_BLOCK) - np.maximum(cs, ss), 0, None) / CMP_BLOCK
    ov[NC - 1:] = 0.0
    ish = np.zeros((NS, LANES), np.float32)
    ish[np.arange(NS), SEL_LANE0 + np.arange(NS)] = 1.0
    t = np.arange(S).reshape(S // tq, 1, tq)
    n = np.arange(NC).reshape(1, NC, 1)
    cbias = np.where((t >= n * CMP_STRIDE + CMP_BLOCK - 1) & (n < NC - 1), 0.0, -MASK_BIG).astype(np.float32)
    i = np.arange(tk)[:, None]
    j = np.arange(M)[None, :] % tq
    tbl = np.stack([np.zeros((tk, M)), np.where(i <= j, 0.0, -MASK_BIG), np.where(i > j, 0.0, -MASK_BIG)])
    return (jnp.asarray(ov.T, BF16), jnp.asarray(ish, BF16), jnp.asarray(cbias), jnp.asarray(tbl, F32))


def _nsa_call(nq, kv, vt, kcx, vct, misct, q_norm_g, k_norm_g, B, S):
    T = B * S
    tq, tk, M = NSA_TQ, NSA_TK, NSA_M
    G = NSA_KV_HEADS
    nqt = S // tq
    NC = S // CMP_STRIDE
    NS = S // SLC_BLOCK
    ovt, ish, cbias, tbl = _nsa_consts(S)
    tile2 = lambda a: jnp.concatenate([a, a], axis=-1)
    row = lambda b, i: (b * nqt + i, 0)
    cst = lambda b, i: (0, 0)
    return pl.pallas_call(
        functools.partial(_nsa_kernel, S=S),
        grid=(B, nqt),
        in_specs=[pl.BlockSpec((tq, NSA_WIDTH), row),
                  pl.BlockSpec((S, KV_COLS), lambda b, i: (b, 0)),
                  pl.BlockSpec((None, S // tk, 2 * NSA_KV_WIDTH, tk), lambda b, i: (b, 0, 0, 0)),
                  pl.BlockSpec((None, G, NC, LANES), lambda b, i: (b, 0, 0, 0)),
                  pl.BlockSpec((None, G, HEAD_DIM, NC), lambda b, i: (b, 0, 0, 0)),
                  pl.BlockSpec((LANES, tq), lambda b, i: (0, b * nqt + i)),
                  pl.BlockSpec((None, NC, tq), lambda b, i: (i, 0, 0)),
                  pl.BlockSpec((3, tk, M), lambda b, i: (0, 0, 0)),
                  pl.BlockSpec((1, LANES), cst),
                  pl.BlockSpec((3, LANES), cst),
                  pl.BlockSpec((NS, NC), cst),
                  pl.BlockSpec((NS, LANES), cst)],
        out_specs=pl.BlockSpec((NSA_WIDTH, tq), lambda b, i: (0, b * nqt + i)),
        out_shape=jax.ShapeDtypeStruct((NSA_WIDTH, T), F32),
        scratch_shapes=[pltpu.VMEM((G, S, LANES), BF16), pltpu.VMEM((G, S, LANES), BF16),
                        pltpu.SMEM((S // tk,), jnp.int32), pltpu.SMEM((1,), jnp.int32)],
        compiler_params=_cparams("parallel", "arbitrary"),
        name="nsa_attn",
    )(nq, kv, vt, kcx, vct, misct, cbias, tbl, tile2(q_norm_g), tile2(k_norm_g), ovt, ish)


def _out_proj_kernel(yg_ref, yn_ref, x_ref, g1_ref, sc_ref, sh_ref, ng_ref, wo_ref, rwt_ref, rb_ref,
                     x1_ref, h2_ref, gate_ref, gate_t_ref):
    mix = _dot(yg_ref[...], wo_ref[:GDN_WIDTH, :]) + _dot_tn(yn_ref[...], wo_ref[GDN_WIDTH:, :])
    x1 = x_ref[...] + g1_ref[...] * mix
    x1_ref[...] = x1
    h2 = _rms(x1, ng_ref[...]) * (1.0 + sc_ref[...]) + sh_ref[...]
    h2_ref[...] = h2.astype(BF16)

    tm = h2.shape[0]
    logits = _dot_nt_hi(rwt_ref[...], h2)
    score = _sigmoid(logits)
    biased = score + rb_ref[...]
    b = [biased[e:e + 1, :] for e in range(N_EXPERTS)]
    n_groups = N_EXPERTS // EXPERTS_PER_GROUP
    gscore = []
    for gi in range(n_groups):
        vals = b[gi * EXPERTS_PER_GROUP:(gi + 1) * EXPERTS_PER_GROUP]
        best = None
        for i in range(EXPERTS_PER_GROUP):
            for j in range(i + 1, EXPERTS_PER_GROUP):
                pair = vals[i] + vals[j]
                best = pair if best is None else jnp.maximum(best, pair)
        gscore.append(best)
    gbest = jnp.zeros((1, tm), jnp.int32)
    top = gscore[0]
    for gi in range(1, n_groups):
        better = gscore[gi] > top
        gbest = jnp.where(better, gi, gbest)
        top = jnp.where(better, gscore[gi], top)
    erow = lax.broadcasted_iota(jnp.int32, (N_EXPERTS, tm), 0)
    gates = jnp.zeros((N_EXPERTS, tm), F32)
    for e in range(N_EXPERTS):
        gi = e // EXPERTS_PER_GROUP
        rank = jnp.zeros((1, tm), F32)
        for e2 in range(gi * EXPERTS_PER_GROUP, (gi + 1) * EXPERTS_PER_GROUP):
            if e2 == e:
                continue
            beats = (b[e2] > b[e]) | ((b[e2] == b[e]) & (e2 < e))
            rank = rank + jnp.where(beats, 1.0, 0.0)
        chosen = (gbest == gi) & (rank < 2.0)
        ge = jnp.where(chosen, score[e:e + 1, :], 0.0)
        gates = gates + jnp.where(erow == e, ge, 0.0)
    gates = gates / jnp.sum(gates, axis=0, keepdims=True)
    pad_rows = jnp.zeros((8 - EXPERTS_PER_GROUP, tm), F32)
    for gi in range(n_groups):
        gate_t_ref[gi] = jnp.concatenate(
            [gates[gi * EXPERTS_PER_GROUP:(gi + 1) * EXPERTS_PER_GROUP, :], pad_rows], axis=0)
    ident = (lax.broadcasted_iota(jnp.int32, (N_EXPERTS, LANES), 0)
             == lax.broadcasted_iota(jnp.int32, (N_EXPERTS, LANES), 1))
    ident = jnp.where(ident, 1.0, 0.0).astype(BF16)
    hi, mid, lo = _split3(gates)
    dn = (((0,), (0,)), ((), ()))
    gate_ref[...] = (lax.dot_general(hi, ident, dn, preferred_element_type=F32)
                     + lax.dot_general(mid, ident, dn, preferred_element_type=F32)
                     + lax.dot_general(lo, ident, dn, preferred_element_type=F32))


def _out_proj_call(yg, yn, x2, g1, sc2, sh2, ng, wo, rwt, rb, S):
    T, D = x2.shape
    tm = 512
    per_b = S // tm
    row = lambda i: (i, 0)
    bat = lambda i: (i // per_b, 0, 0)
    cst = lambda i: (0, 0)
    return pl.pallas_call(
        _out_proj_kernel,
        grid=(T // tm,),
        in_specs=[pl.BlockSpec((tm, GDN_WIDTH), row), pl.BlockSpec((NSA_WIDTH, tm), lambda i: (0, i)),
                  pl.BlockSpec((tm, D), row),
                  pl.BlockSpec((None, 1, D), bat), pl.BlockSpec((None, 1, D), bat), pl.BlockSpec((None, 1, D), bat),
                  pl.BlockSpec((1, D), cst), pl.BlockSpec((D, D), cst),
                  pl.BlockSpec((N_EXPERTS, D), cst), pl.BlockSpec((N_EXPERTS, 1), cst)],
        out_specs=[pl.BlockSpec((tm, D), row), pl.BlockSpec((tm, D), row), pl.BlockSpec((tm, LANES), row),
                   pl.BlockSpec((N_EXPERTS // EXPERTS_PER_GROUP, 8, tm), lambda i: (0, 0, i))],
        out_shape=[jax.ShapeDtypeStruct((T, D), F32), jax.ShapeDtypeStruct((T, D), BF16),
                   jax.ShapeDtypeStruct((T, LANES), F32),
                   jax.ShapeDtypeStruct((N_EXPERTS // EXPERTS_PER_GROUP, 8, T), F32)],
        compiler_params=_cparams("parallel"),
        name="out_proj_router",
    )(yg, yn, x2, g1, sc2, sh2, ng, wo, rwt, rb)


MOE_TM = 1024
MOE_RB = 288


def _moe_kernel(h_ref, gates_ref, gt8_ref, tri_ref, trit_ref, x1_ref, g2_ref, wg_ref, wu_ref, wd_ref, o_ref):
    g = pl.program_id(1)
    tm, rb = MOE_TM, MOE_RB
    nlb = tm // LANES

    @pl.when(g == 0)
    def _():
        o_ref[...] = jnp.zeros_like(o_ref)

    gt8 = gt8_ref[...]
    in_row = jnp.sum(gt8, axis=0, keepdims=True) > 0.0
    first = lax.broadcasted_iota(jnp.int32, (8, LANES), 0) == 0
    off = jnp.zeros((1, 1), F32)
    keys = []
    for b in range(nlb):
        blk = in_row[:, b * LANES:(b + 1) * LANES]
        one = jnp.where(blk & first, 1.0, 0.0).astype(BF16)
        within = jnp.dot(one, tri_ref[...], preferred_element_type=F32)[0:1, :]
        keys.append(jnp.where(blk, within + off, -1.0))
        off = off + jnp.sum(jnp.where(blk, 1.0, 0.0), axis=1, keepdims=True)
    key_row = jnp.concatenate(keys, axis=1)
    n_g = jnp.sum(off).astype(jnp.int32)
    lane = lax.broadcasted_iota(jnp.int32, (tm, LANES), 1)
    mine = (lane >= g * EXPERTS_PER_GROUP) & (lane < (g + 1) * EXPERTS_PER_GROUP)
    in_col = jnp.sum(jnp.where(mine, gates_ref[...], 0.0), axis=1, keepdims=True) > 0.0
    off = jnp.zeros((1, 1), F32)
    keys = []
    for b in range(nlb):
        blk = in_col[b * LANES:(b + 1) * LANES, :]
        one = jnp.broadcast_to(jnp.where(blk, 1.0, 0.0), (LANES, LANES)).astype(BF16)
        within = jnp.dot(trit_ref[...], one, preferred_element_type=F32)[:, 0:1]
        keys.append(jnp.where(blk, within + off, -1.0))
        off = off + jnp.sum(jnp.where(blk, 1.0, 0.0), axis=0, keepdims=True)
    key_col = jnp.concatenate(keys, axis=0)

    gt_hi, gt_mid, gt_lo = _split3(gt8)
    nt_dims = (((1,), (1,)), ((), ()))

    def block(j, carry):
        r0 = (j * rb).astype(F32)
        sel = jnp.where(key_row - r0 == lax.broadcasted_iota(jnp.int32, (rb, tm), 0).astype(F32), 1.0, 0.0)
        sel = sel.astype(BF16)
        xb = jnp.dot(sel, h_ref[...], preferred_element_type=F32).astype(BF16)
        gb = (lax.dot_general(sel, gt_hi, nt_dims, preferred_element_type=F32)
              + lax.dot_general(sel, gt_mid, nt_dims, preferred_element_type=F32)
              + lax.dot_general(sel, gt_lo, nt_dims, preferred_element_type=F32))
        y = jnp.zeros((rb, o_ref.shape[1]), F32)
        for k in range(EXPERTS_PER_GROUP):
            act = (_silu(jnp.dot(xb, wg_ref[k], preferred_element_type=F32))
                   * jnp.dot(xb, wu_ref[k], preferred_element_type=F32))
            y = y + _dot(act * gb[:, k:k + 1], wd_ref[k])
        sel_t = jnp.where(key_col - r0 == lax.broadcasted_iota(jnp.int32, (tm, rb), 1).astype(F32), 1.0, 0.0)
        o_ref[...] += jnp.dot(sel_t.astype(BF16), y.astype(BF16), preferred_element_type=F32)
        return carry

    lax.fori_loop(0, (n_g + rb - 1) // rb, block, 0)

    @pl.when(g == N_EXPERTS // EXPERTS_PER_GROUP - 1)
    def _():
        o_ref[...] = x1_ref[...] + g2_ref[...] * o_ref[...]


def _moe_call(h2, gates, gates_t, x1, g2, wg, wu, wd, S):
    T, D = x1.shape
    tm = MOE_TM
    per_b = S // tm
    n_groups = N_EXPERTS // EXPERTS_PER_GROUP
    i = np.arange(LANES)
    tri = jnp.asarray(i[:, None] < i[None, :], BF16)
    row = lambda i, g: (i, 0)
    cst = lambda i, g: (0, 0)
    wsel = lambda i, g: (g, 0, 0)
    return pl.pallas_call(
        _moe_kernel,
        grid=(T // tm, n_groups),
        in_specs=[pl.BlockSpec((tm, D), row), pl.BlockSpec((tm, LANES), row),
                  pl.BlockSpec((None, 8, tm), lambda i, g: (g, 0, i)),
                  pl.BlockSpec((LANES, LANES), cst), pl.BlockSpec((LANES, LANES), cst),
                  pl.BlockSpec((tm, D), row),
                  pl.BlockSpec((None, 1, D), lambda i, g: (i // per_b, 0, 0)),
                  pl.BlockSpec((EXPERTS_PER_GROUP, D, EXPERT_FF), wsel),
                  pl.BlockSpec((EXPERTS_PER_GROUP, D, EXPERT_FF), wsel),
                  pl.BlockSpec((EXPERTS_PER_GROUP, EXPERT_FF, D), wsel)],
        out_specs=pl.BlockSpec((tm, D), row),
        out_shape=jax.ShapeDtypeStruct((T, D), F32),
        compiler_params=_cparams("parallel", "arbitrary"),
        name="moe",
    )(h2, gates, gates_t, tri, tri.T, x1, g2, wg, wu, wd)


def kernel(x, c, ada_w, ada_b, norm1_g, norm2_g, w_in, gdn_conv_w, gdn_a_log, gdn_dt_bias, gdn_norm_g,
           nsa_q_norm_g, nsa_k_norm_g, cmp_pe, cmp_w1, cmp_b1, cmp_w2, cmp_b2, w_out, router_w, router_bias,
           exp_w_gate, exp_w_up, exp_w_down):
    B, S, D = x.shape
    L = ada_w.shape[0]
    T = B * S
    G = NSA_KV_HEADS
    NC = S // CMP_STRIDE
    N = S // GDN_CHUNK
    P = GDN_HEADS // GDN_HPB
    W = GDN_HPB * HEAD_DIM

    mod = _ada_call(c, ada_w, ada_b)
    rwt = router_w.T
    rb = router_bias.reshape(N_EXPERTS, 1)
    x2 = x.reshape(T, D)
    for l in range(L):
        m6 = mod[l].reshape(B, 6, 1, D)
        sh1, sc1, g1, sh2, sc2, g2 = (m6[:, i] for i in range(6))
        qkv, z, nq, kv, misc, misct, vt = _in_proj_call(
            x2, sc1, sh1, norm1_g[l].reshape(1, D), *_pad_w_in(w_in[l]), S)

        a_rows = misc[:, :GDN_HEADS].reshape(B, N, GDN_CHUNK, P, GDN_HPB).transpose(0, 3, 1, 4, 2).reshape(B, P, N, W)
        rep = lambda t: jnp.repeat(t, HEAD_DIM).reshape(1, GDN_WIDTH)
        y_gdn = _gdn_call(qkv, z, misc, a_rows, gdn_conv_w[l], rep(gdn_a_log[l]), rep(gdn_dt_bias[l]),
                          jnp.tile(gdn_norm_g[l], GDN_HPB).reshape(1, W), B, S)

        xkv = kv[:, :2 * NSA_KV_WIDTH].reshape(B, S, 2, G, HEAD_DIM).transpose(2, 0, 3, 1, 4)
        xkv = xkv.reshape(2, B * G, NC, CMP_STRIDE * HEAD_DIM)
        w2 = cmp_w2[l]
        kcx, vct = _cmp_call(
            xkv[0], xkv[1], cmp_pe[l].reshape(2, 1, CMP_BLOCK * HEAD_DIM), cmp_w1[l].astype(BF16),
            cmp_b1[l].reshape(2, 1, CMP_HIDDEN), jnp.pad(w2[0], ((0, 0), (0, LANES - HEAD_DIM))).astype(BF16),
            w2[1].T.astype(BF16), jnp.pad(cmp_b2[l, 0], (0, LANES - HEAD_DIM)).reshape(1, LANES),
            cmp_b2[l, 1].reshape(HEAD_DIM, 1), jnp.pad(nsa_k_norm_g[l, 0], (0, LANES - HEAD_DIM)).reshape(1, LANES))
        y_nsa_t = _nsa_call(nq, kv, vt.reshape(B, S // NSA_TK, 2 * NSA_KV_WIDTH, NSA_TK), kcx.reshape(B, G, NC, LANES), vct.reshape(B, G, HEAD_DIM, NC), misct,
                            nsa_q_norm_g[l].reshape(1, HEAD_DIM), nsa_k_norm_g[l], B, S)

        x1, h2, gates, gates_t = _out_proj_call(y_gdn, y_nsa_t, x2, g1, sc2, sh2, norm2_g[l].reshape(1, D),
                                       w_out[l].astype(BF16), rwt, rb, S)
        x2 = _moe_call(h2, gates, gates_t, x1, g2, exp_w_gate[l].astype(BF16), exp_w_up[l].astype(BF16),
                       exp_w_down[l].astype(BF16), S)
    return x2.reshape(B, S, D)
```

```python
import functools

import numpy as np
import jax
import jax.numpy as jnp
from jax import lax
from jax.experimental import pallas as pl
from jax.experimental.pallas import tpu as pltpu

F32 = jnp.float32
BF16 = jnp.bfloat16

HEAD_DIM = 64
GDN_HEADS = 8
GDN_WIDTH = GDN_HEADS * HEAD_DIM
GDN_CONV = 4
GDN_CHUNK = 64
NSA_HEADS = 8
NSA_KV_HEADS = 2
NSA_GROUP = NSA_HEADS // NSA_KV_HEADS
NSA_WIDTH = NSA_HEADS * HEAD_DIM
NSA_KV_WIDTH = NSA_KV_HEADS * HEAD_DIM
CMP_BLOCK = 32
CMP_STRIDE = 16
CMP_HIDDEN = 256
SLC_BLOCK = 64
SLC_TOPN = 8
WINDOW = 512
N_EXPERTS = 16
EXPERTS_PER_GROUP = 4
EXPERT_FF = 512
RMS_EPS = 1e-6
NEG_BIG = -1e30
SEL_BIG = 1e9

LANES = 128
VMEM_LIMIT_BYTES = 56 * 1024 * 1024

C_QKV = 0
C_Z = 3 * GDN_WIDTH
C_NQ = C_Z + GDN_WIDTH
C_KV = C_NQ + NSA_WIDTH
KV_COLS = 4 * NSA_KV_WIDTH
C_MISC = C_KV + KV_COLS
IN_PAD_COLS = C_MISC + LANES
MISC_B0 = GDN_HEADS
MISC_GATE0 = 16


def _cparams(*sem):
    return pltpu.CompilerParams(dimension_semantics=sem, vmem_limit_bytes=VMEM_LIMIT_BYTES)


def _dot(a, b):
    return jnp.dot(a.astype(BF16), b.astype(BF16), preferred_element_type=F32)


def _dot_nt(a, b):
    return lax.dot_general(a.astype(BF16), b.astype(BF16), (((1,), (1,)), ((), ())),
                           preferred_element_type=F32)


def _dot_tn(a, b):
    return lax.dot_general(a.astype(BF16), b.astype(BF16), (((0,), (0,)), ((), ())),
                           preferred_element_type=F32)


def _split2(x):
    hi = x.astype(BF16)
    lo = (x - hi.astype(F32)).astype(BF16)
    return hi, lo


def _split3(x):
    hi = x.astype(BF16)
    r = x - hi.astype(F32)
    mid = r.astype(BF16)
    lo = (r - mid.astype(F32)).astype(BF16)
    return hi, mid, lo


def _dot_hi(a, b):
    ah, al = _split2(a)
    bh, bl = _split2(b)
    return (jnp.dot(ah, bh, preferred_element_type=F32) + jnp.dot(al, bh, preferred_element_type=F32)
            + jnp.dot(ah, bl, preferred_element_type=F32))


def _dot_nt_hi(a, b):
    ah, al = _split2(a)
    bh, bl = _split2(b)
    dn = (((1,), (1,)), ((), ()))
    return (lax.dot_general(ah, bh, dn, preferred_element_type=F32)
            + lax.dot_general(al, bh, dn, preferred_element_type=F32)
            + lax.dot_general(ah, bl, dn, preferred_element_type=F32))


def _dot_exact_rhs(a, b_bf16):
    hi, mid, lo = _split3(a)
    return (jnp.dot(hi, b_bf16, preferred_element_type=F32) + jnp.dot(mid, b_bf16, preferred_element_type=F32)
            + jnp.dot(lo, b_bf16, preferred_element_type=F32))


def _rms(x, g):
    return x * lax.rsqrt(jnp.mean(x * x, axis=-1, keepdims=True) + RMS_EPS) * g


def _sigmoid(x):
    return 0.5 * (jnp.tanh(0.5 * x) + 1.0)


def _silu(x):
    return x * _sigmoid(x)


def _ada_kernel(c_ref, w_ref, b_ref, o_ref):
    o_ref[...] = _dot_hi(_silu(c_ref[...]), w_ref[...]) + b_ref[...]


def _ada_call(c, ada_w, ada_b):
    L, D, N = ada_w.shape
    B = c.shape[0]
    tn = 1024
    return pl.pallas_call(
        _ada_kernel,
        grid=(L, N // tn),
        in_specs=[pl.BlockSpec((B, D), lambda l, j: (0, 0)),
                  pl.BlockSpec((None, D, tn), lambda l, j: (l, 0, j)),
                  pl.BlockSpec((None, 1, tn), lambda l, j: (l, 0, j))],
        out_specs=pl.BlockSpec((None, B, tn), lambda l, j: (l, 0, j)),
        out_shape=jax.ShapeDtypeStruct((L, B, N), F32),
        compiler_params=_cparams("parallel", "parallel"),
        name="ada_mod",
    )(c, ada_w, ada_b.reshape(L, 1, N))


IN_SEGS = ((C_QKV, 3 * GDN_WIDTH), (C_Z, GDN_WIDTH), (C_NQ, NSA_WIDTH), (C_KV, KV_COLS), (C_MISC, LANES))


def _in_proj_kernel(x_ref, sc_ref, sh_ref, g_ref, w_ref, wmt_ref, wvt_ref, *o_refs):
    h = _rms(x_ref[...], g_ref[...]) * (1.0 + sc_ref[...]) + sh_ref[...]
    hb = h.astype(BF16)
    for (c0, n), o_ref in zip(IN_SEGS, o_refs[:-2]):
        o_ref[...] = jnp.dot(hb, w_ref[:, c0:c0 + n], preferred_element_type=F32)
    nt_dims = (((1,), (1,)), ((), ()))
    o_refs[-2][...] = lax.dot_general(wmt_ref[...], hb, nt_dims, preferred_element_type=F32)
    vt = lax.dot_general(wvt_ref[...], hb, nt_dims, preferred_element_type=F32).astype(BF16)
    for j in range(o_refs[-1].shape[0]):
        o_refs[-1][j] = vt[:, j * LANES:(j + 1) * LANES]


def _in_proj_call(x2, sc, sh, g, w_pad, wvt, S):
    T, D = x2.shape
    tm = 256
    per_b = S // tm
    row = lambda i: (i, 0)
    bat = lambda i: (i // per_b, 0, 0)
    return pl.pallas_call(
        _in_proj_kernel,
        grid=(T // tm,),
        in_specs=[pl.BlockSpec((tm, D), row),
                  pl.BlockSpec((None, 1, D), bat),
                  pl.BlockSpec((None, 1, D), bat),
                  pl.BlockSpec((1, D), lambda i: (0, 0)),
                  pl.BlockSpec((D, IN_PAD_COLS), lambda i: (0, 0)),
                  pl.BlockSpec((LANES, D), lambda i: (0, 0)),
                  pl.BlockSpec((2 * NSA_KV_WIDTH, D), lambda i: (0, 0))],
        out_specs=([pl.BlockSpec((tm, n), row) for _, n in IN_SEGS]
                   + [pl.BlockSpec((LANES, tm), lambda i: (0, i)),
                      pl.BlockSpec((tm // LANES, 2 * NSA_KV_WIDTH, LANES), lambda i: (i, 0, 0))]),
        out_shape=([jax.ShapeDtypeStruct((T, n), F32) for _, n in IN_SEGS]
                   + [jax.ShapeDtypeStruct((LANES, T), F32),
                      jax.ShapeDtypeStruct((T // LANES, 2 * NSA_KV_WIDTH, LANES), BF16)]),
        compiler_params=_cparams("parallel"),
        name="in_proj",
    )(x2, sc, sh, g, w_pad, w_pad[:, C_MISC:].T, wvt)


def _pad_w_in(w_in):
    D = w_in.shape[0]
    W = GDN_WIDTH
    o = 0
    gq, gk, gv, gz = (w_in[:, o + i * W:o + (i + 1) * W] for i in range(4))
    o += 4 * W
    ga = w_in[:, o:o + GDN_HEADS]
    gb = w_in[:, o + GDN_HEADS:o + 2 * GDN_HEADS]
    o += 2 * GDN_HEADS
    nq = w_in[:, o:o + NSA_WIDTH]
    o += NSA_WIDTH
    kc, vc, ks, vs, kw, vw = (w_in[:, o + i * NSA_KV_WIDTH:o + (i + 1) * NSA_KV_WIDTH] for i in range(6))
    o += 6 * NSA_KV_WIDTH
    gates = w_in[:, o:o + 3 * NSA_HEADS]
    misc = jnp.zeros((D, LANES), w_in.dtype)
    misc = misc.at[:, 0:GDN_HEADS].set(ga).at[:, MISC_B0:MISC_B0 + GDN_HEADS].set(gb)
    misc = misc.at[:, MISC_GATE0:MISC_GATE0 + 3 * NSA_HEADS].set(gates)
    cols = [gq, gk, gv, gz, nq, kc, vc, ks, kw, misc]
    return jnp.concatenate(cols, axis=1).astype(BF16), jnp.concatenate([vs, vw], axis=1).T.astype(BF16)


GDN_HPB = 2
GDN_NB = 32


def _gdn_kernel(q_ref, k_ref, v_ref, z_ref, misc_ref, ar_ref, cwq_ref, cwk_ref, cwv_ref,
                alog_ref, dtb_ref, alogr_ref, dtbr_ref, ng_ref, o_ref,
                qs, ks, kbs, xs, gcs, grs, ps, qq, qks, os_, rs, q2s, sev, *, S):
    C = GDN_CHUNK
    N = S // C
    W = GDN_HPB * HEAD_DIM
    row = lax.broadcasted_iota(jnp.int32, (S, W), 0)
    row8 = lax.broadcasted_iota(jnp.int32, (8, W), 0)

    def conv_silu(x, w):
        y = x * w[GDN_CONV - 1:GDN_CONV, :]
        for s in range(1, GDN_CONV):
            xs = pltpu.roll(x, s, axis=0)
            xs = jnp.concatenate([jnp.where(row8 >= s, xs[:8], 0.0), xs[8:]], axis=0)
            y = y + xs * w[GDN_CONV - 1 - s:GDN_CONV - s, :]
        return _silu(y)

    bd = jnp.where(lax.broadcasted_iota(jnp.int32, (W, W), 0) // HEAD_DIM
                   == lax.broadcasted_iota(jnp.int32, (W, W), 1) // HEAD_DIM, 1.0, 0.0).astype(BF16)

    def head_sumsq(x):
        hi, lo = _split2(x * x)
        return jnp.dot(hi, bd, preferred_element_type=F32) + jnp.dot(lo, bd, preferred_element_type=F32)

    def head_l2(x):
        return x * lax.rsqrt(head_sumsq(x) + RMS_EPS)

    q = head_l2(conv_silu(q_ref[...], cwq_ref[...])) * (HEAD_DIM ** -0.5)
    k = head_l2(conv_silu(k_ref[...], cwk_ref[...]))
    v = conv_silu(v_ref[...], cwv_ref[...])

    hd0 = pl.program_id(1) * GDN_HPB
    src = lax.broadcasted_iota(jnp.int32, (LANES, W), 0)
    dst = lax.broadcasted_iota(jnp.int32, (LANES, W), 1) // HEAD_DIM + hd0
    a_exp = _dot_exact_rhs(misc_ref[...], jnp.where(src == dst, 1.0, 0.0).astype(BF16))
    b_exp = _dot_exact_rhs(misc_ref[...], jnp.where(src == dst + MISC_B0, 1.0, 0.0).astype(BF16))
    beta = _sigmoid(b_exp)

    def log_decay(a, alog, dtb):
        xx = a + dtb
        sp = jnp.maximum(xx, 0.0) + jnp.log(1.0 + jnp.exp(-jnp.abs(xx)))
        return -jnp.exp(alog) * sp

    gc = log_decay(a_exp, alog_ref[...], dtb_ref[...])
    pos = row % C
    for s in (1, 2, 4, 8, 16, 32):
        gc = gc + jnp.where(pos >= s, pltpu.roll(gc, s, axis=0), 0.0)
    gr = log_decay(ar_ref[...], alogr_ref[...], dtbr_ref[...])
    lpos = lax.broadcasted_iota(jnp.int32, (N, W), 1) % C
    for s in (1, 2, 4, 8, 16, 32):
        gr = gr + jnp.where(lpos >= s, pltpu.roll(gr, s, axis=1), 0.0)

    kb = k * beta
    vb = v * beta
    kbe = kb * jnp.exp(gc)
    for h in range(GDN_HPB):
        sl = slice(h * HEAD_DIM, (h + 1) * HEAD_DIM)
        qs[h] = q[:, sl]
        ks[h] = k[:, sl]
        kbs[h] = kb[:, sl]
        gcs[h] = gc[:, sl]
        for n in range(N):
            grs[h, n] = jnp.broadcast_to(gr[n:n + 1, sl], (8, HEAD_DIM))
        xs[h] = jnp.concatenate([kbe[:, sl], vb[:, sl]], axis=1)

    NB = GDN_NB
    ci3 = lax.broadcasted_iota(jnp.int32, (1, C, C), 1)
    cj3 = lax.broadcasted_iota(jnp.int32, (1, C, C), 2)
    eye3 = jnp.where(ci3 == cj3, 1.0, 0.0)

    def bmm(a, b):
        return lax.dot_general(a.astype(BF16), b.astype(BF16), (((2,), (1,)), ((0,), (0,))),
                               preferred_element_type=F32)

    def bmm_nt(a, b):
        return lax.dot_general(a.astype(BF16), b.astype(BF16), (((2,), (2,)), ((0,), (0,))),
                               preferred_element_type=F32)

    def intra(nb, carry):
        rows = pl.ds(pl.multiple_of(nb * (NB * C), NB * C), NB * C)
        for h in range(GDN_HPB):
            kk = ks[h, rows, :].reshape(NB, C, HEAD_DIM)
            kbq = jnp.concatenate([kbs[h, rows, :].reshape(NB, C, HEAD_DIM),
                                   qs[h, rows, :].reshape(NB, C, HEAD_DIM)], axis=1)
            a2 = bmm_nt(kbq, kk)
            gcol = gcs[h, rows, :].reshape(NB, C, HEAD_DIM)
            grow = grs[h, pl.ds(nb * NB, NB), 0:1, :]
            decay = jnp.exp(jnp.where(ci3 >= cj3, gcol - grow, NEG_BIG))
            low = jnp.where(ci3 > cj3, a2[:, :C] * decay, 0.0)
            qks[h, rows, :] = (a2[:, C:] * decay).reshape(NB * C, C)
            tinv = eye3 - jnp.where(ci3 // 2 == cj3 // 2, low, 0.0)
            for lv in range(1, 6):
                s = 2 ** lv
                off = jnp.where((ci3 // (2 * s) == cj3 // (2 * s)) & (ci3 // s != cj3 // s), low, 0.0)
                tinv = tinv - bmm(tinv, bmm(off, tinv))
            x = bmm(tinv, xs[h, rows, :].reshape(NB, C, 2 * HEAD_DIM))
            xs[h, rows, :] = x.reshape(NB * C, 2 * HEAD_DIM)
            kd = kk * jnp.exp(gcol[:, C - 1:C, :] - gcol)
            pq = lax.dot_general(kd.astype(BF16), x.astype(BF16), (((1,), (1,)), ((0,), (0,))),
                                 preferred_element_type=F32)
            pn = pq[:, :, :HEAD_DIM]
            qn = pq[:, :, HEAD_DIM:]
            ps[h, rows, :] = pn.reshape(NB * C, HEAD_DIM)
            qq[h, rows, :] = qn.reshape(NB * C, HEAD_DIM)
            cd = jnp.exp(gcol[:, C - 1:C, :]).reshape(NB // 2, 2, 1, HEAD_DIM)
            pn = pn.reshape(NB // 2, 2, HEAD_DIM, HEAD_DIM)
            qn = qn.reshape(NB // 2, 2, HEAD_DIM, HEAD_DIM)
            p0, p1, q0, q1, c0, c1 = pn[:, 0], pn[:, 1], qn[:, 0], qn[:, 1], cd[:, 0], cd[:, 1]
            t = bmm(p1, pq.reshape(NB // 2, 2, HEAD_DIM, 2 * HEAD_DIM)[:, 0])
            prow = pl.ds(pl.multiple_of(nb * (NB // 2 * C), NB // 2 * C), NB // 2 * C)
            rs[h, prow, :] = (c1 * p0 + c0 * p1 - t[:, :, :HEAD_DIM]).reshape(NB // 2 * C, HEAD_DIM)
            q2s[h, prow, :] = (c1 * q0 - t[:, :, HEAD_DIM:] + q1).reshape(NB // 2 * C, HEAD_DIM)
        return carry

    lax.fori_loop(0, N // NB, intra, 0)

    def inter(k, states):
        sl = pl.ds(pl.multiple_of(k * C, C), C)
        new_states = []
        for h in range(GDN_HPB):
            st = states[h]
            sev[h, k] = st
            gsum = gcs[h, pl.ds(2 * k * C + C - 1, 1), :] + gcs[h, pl.ds(2 * k * C + 2 * C - 1, 1), :]
            new_states.append(st * jnp.exp(gsum) - _dot_hi(rs[h, sl, :], st) + q2s[h, sl, :])
        return tuple(new_states)

    lax.fori_loop(0, N // 2, inter, tuple(jnp.zeros((HEAD_DIM, HEAD_DIM), F32) for _ in range(GDN_HPB)))

    def outputs(nb, carry):
        rows = pl.ds(pl.multiple_of(nb * (NB * C), NB * C), NB * C)
        for h in range(GDN_HPB):
            gcol = gcs[h, rows, :].reshape(NB, C, HEAD_DIM)
            s0 = sev[h, pl.ds(nb * (NB // 2), NB // 2)]
            p0 = ps[h, rows, :].reshape(NB // 2, 2, HEAD_DIM, HEAD_DIM)[:, 0]
            q0 = qq[h, rows, :].reshape(NB // 2, 2, HEAD_DIM, HEAD_DIM)[:, 0]
            c0 = jnp.exp(gcol[:, C - 1:C, :]).reshape(NB // 2, 2, 1, HEAD_DIM)[:, 0]
            s1 = c0 * s0 - bmm(p0, s0) + q0
            st = jnp.stack([s0, s1], axis=1).reshape(NB, HEAD_DIM, HEAD_DIM)
            x = xs[h, rows, :].reshape(NB, C, 2 * HEAD_DIM)
            qd = qs[h, rows, :].reshape(NB, C, HEAD_DIM) * jnp.exp(gcol)
            t1 = bmm(jnp.concatenate([x[:, :, :HEAD_DIM], qd], axis=1), st)
            vn = x[:, :, HEAD_DIM:] - t1[:, :C]
            o = t1[:, C:] + bmm(qks[h, rows, :].reshape(NB, C, C), vn)
            os_[h, rows, :] = o.reshape(NB * C, HEAD_DIM)
        return carry

    lax.fori_loop(0, N // NB, outputs, 0)

    o = jnp.concatenate([os_[h] for h in range(GDN_HPB)], axis=1)
    o = o * lax.rsqrt(head_sumsq(o) * (1.0 / HEAD_DIM) + RMS_EPS) * ng_ref[...]
    o_ref[...] = o * _silu(z_ref[...])


def _gdn_call(qkv, z, misc, a_rows, conv_w, alog_exp, dtb_exp, norm_g, B, S):
    T = B * S
    W = GDN_HPB * HEAD_DIM
    P = GDN_HEADS // GDN_HPB
    N = S // GDN_CHUNK
    col = lambda off: (lambda b, p: (b, off + p))
    par = lambda off: (lambda b, p: (0, off + p))
    hs = lambda n: pltpu.VMEM((GDN_HPB, S, n), F32)
    return pl.pallas_call(
        functools.partial(_gdn_kernel, S=S),
        grid=(B, P),
        in_specs=[pl.BlockSpec((S, W), col(0)), pl.BlockSpec((S, W), col(P)), pl.BlockSpec((S, W), col(2 * P)),
                  pl.BlockSpec((S, W), col(0)), pl.BlockSpec((S, LANES), lambda b, p: (b, 0)),
                  pl.BlockSpec((None, None, N, W), lambda b, p: (b, p, 0, 0)),
                  pl.BlockSpec((GDN_CONV, W), par(0)), pl.BlockSpec((GDN_CONV, W), par(P)),
                  pl.BlockSpec((GDN_CONV, W), par(2 * P)),
                  pl.BlockSpec((1, W), par(0)), pl.BlockSpec((1, W), par(0)),
                  pl.BlockSpec((1, W), par(0)), pl.BlockSpec((1, W), par(0)),
                  pl.BlockSpec((1, W), lambda b, p: (0, 0))],
        out_specs=pl.BlockSpec((S, W), col(0)),
        out_shape=jax.ShapeDtypeStruct((T, GDN_WIDTH), F32),
        scratch_shapes=[hs(HEAD_DIM), hs(HEAD_DIM), hs(HEAD_DIM), hs(2 * HEAD_DIM), hs(HEAD_DIM),
                        pltpu.VMEM((GDN_HPB, N, 8, HEAD_DIM), F32),
                        hs(HEAD_DIM), hs(HEAD_DIM), hs(HEAD_DIM), hs(HEAD_DIM),
                        pltpu.VMEM((GDN_HPB, S // 2, HEAD_DIM), F32), pltpu.VMEM((GDN_HPB, S // 2, HEAD_DIM), F32),
                        pltpu.VMEM((GDN_HPB, N // 2, HEAD_DIM, HEAD_DIM), F32)],
        compiler_params=_cparams("parallel", "parallel"),
        name="gdn",
    )(qkv, qkv, qkv, z, misc, a_rows, conv_w, conv_w, conv_w,
      alog_exp, dtb_exp, alog_exp, dtb_exp, norm_g)


def _cmp_kernel(kc_in_ref, vc_in_ref, pe_ref, w1_ref, b1_ref, w2k_ref, w2vt_ref, b2k_ref, b2v_ref, kg_ref, kc_ref, vct_ref):
    half = (CMP_BLOCK // 2) * HEAD_DIM
    nrows = kc_in_ref.shape[0] // CMP_STRIDE
    lane = lax.broadcasted_iota(jnp.int32, (nrows, LANES), 1)
    cend = lax.broadcasted_iota(jnp.int32, (nrows, LANES), 0) * CMP_STRIDE + (CMP_BLOCK - 1)
    ext = jnp.where(lane == HEAD_DIM, cend // LANES, jnp.where(lane == HEAD_DIM + 1, cend % LANES, 0)).astype(F32)

    def token_groups(x_ref):
        cols = [[] for _ in range(NSA_KV_HEADS)]
        for m in range(CMP_STRIDE // 2):
            ev = x_ref[pl.ds(2 * m, nrows, stride=CMP_STRIDE), :]
            od = x_ref[pl.ds(2 * m + 1, nrows, stride=CMP_STRIDE), :]
            cols[0].append(jnp.where(lane < HEAD_DIM, ev, pltpu.roll(od, HEAD_DIM, axis=1)))
            cols[1].append(jnp.where(lane < HEAD_DIM, pltpu.roll(ev, HEAD_DIM, axis=1), od))
        return [jnp.concatenate(c, axis=1) for c in cols]

    def hidden(x, j):
        pe = pe_ref[j]
        p = _dot(x + pe[:, :half], w1_ref[j, :half, :])
        q = _dot(x + pe[:, half:], w1_ref[j, half:, :])
        return _silu(p + pltpu.roll(q, nrows - 1, axis=0) + b1_ref[j])

    for g, x in enumerate(token_groups(kc_in_ref)):
        kc = _dot(hidden(x, 0), w2k_ref[...]) + b2k_ref[...]
        ms = jnp.sum(kc * kc, axis=-1, keepdims=True) * (1.0 / HEAD_DIM)
        kc_ref[g] = jnp.where(lane < HEAD_DIM, kc * lax.rsqrt(ms + RMS_EPS) * kg_ref[...], ext)
    for g, x in enumerate(token_groups(vc_in_ref)):
        vct_ref[g] = _dot_nt(w2vt_ref[...], hidden(x, 1)) + b2v_ref[...]


def _cmp_call(kv, pe, w1, b1, w2k, w2vt, b2k, b2v, kg, B, S):
    G = NSA_KV_HEADS
    NC = S // CMP_STRIDE
    full = lambda a: pl.BlockSpec(a.shape, lambda b: (0,) * a.ndim)
    return pl.pallas_call(
        _cmp_kernel,
        grid=(B,),
        in_specs=[pl.BlockSpec((S, NSA_KV_WIDTH), lambda b: (b, 0)), pl.BlockSpec((S, NSA_KV_WIDTH), lambda b: (b, 1))]
        + [full(a) for a in (pe, w1, b1, w2k, w2vt, b2k, b2v, kg)],
        out_specs=[pl.BlockSpec((None, G, NC, LANES), lambda b: (b, 0, 0, 0)),
                   pl.BlockSpec((None, G, HEAD_DIM, NC), lambda b: (b, 0, 0, 0))],
        out_shape=[jax.ShapeDtypeStruct((B, G, NC, LANES), F32), jax.ShapeDtypeStruct((B, G, HEAD_DIM, NC), F32)],
        compiler_params=_cparams("parallel"),
        name="nsa_compress",
    )(kv, kv, pe, w1, b1, w2k, w2vt, b2k, b2v, kg)


NSA_TQ = 128
NSA_TK = 128
NSA_M = NSA_GROUP * NSA_TQ
NSA_MS = NSA_M
MASK_BIG = 2.0 ** 100
SEL_LANE0 = HEAD_DIM + 2


def _nsa_kernel(q_ref, kv_ref, vt_ref, kcx_ref, vct_ref, misct_ref, cbias_ref, tbl_ref, qg_ref, kg_ref,
                ovt_ref, ish_ref, o_ref, ks_s, kw_s, live, n_live, *, S):
    tq, tk, M = NSA_TQ, NSA_TK, NSA_M
    MS = NSA_MS
    G, R = NSA_KV_HEADS, NSA_GROUP
    NS = S // SLC_BLOCK
    n_top = min(SLC_TOPN, NS)
    qi = pl.program_id(1)
    t0 = qi * tq
    nt_dims = (((1,), (1,)), ((), ()))

    @pl.when(qi == 0)
    def _():
        rowi = lax.broadcasted_iota(jnp.int32, (S, LANES), 0)
        lane = lax.broadcasted_iota(jnp.int32, (S, LANES), 1)
        ext = jnp.where(lane == HEAD_DIM, rowi // LANES, jnp.where(lane == HEAD_DIM + 1, rowi % LANES, 0)).astype(F32)
        ext_slc = jnp.where(lane - SEL_LANE0 == rowi // SLC_BLOCK, -MASK_BIG, ext)

        def prep(c0, g, gain, extra):
            kraw = kv_ref[:, c0:c0 + LANES]
            if g == 1:
                kraw = pltpu.roll(kraw, HEAD_DIM, axis=1)
            ms = jnp.sum(jnp.where(lane < HEAD_DIM, kraw * kraw, 0.0), axis=-1, keepdims=True) * (1.0 / HEAD_DIM)
            return jnp.where(lane < HEAD_DIM, kraw * lax.rsqrt(ms + RMS_EPS) * gain, extra).astype(BF16)

        for g in range(G):
            ks_s[g] = prep(2 * NSA_KV_WIDTH, g, kg_ref[1:2, :], ext_slc)
            kw_s[g] = prep(3 * NSA_KV_WIDTH, g, kg_ref[2:3, :], ext)

    lane_q = lax.broadcasted_iota(jnp.int32, (tq, LANES), 1)
    gate_sig = _sigmoid(misct_ref[MISC_GATE0:MISC_GATE0 + 3 * NSA_HEADS, :])
    kt_d = t0 // tk

    def score_tiles(streams, kt, biased):
        rows = pl.ds(pl.multiple_of(kt * tk, tk), tk)
        out = []
        for qpp, k_s, g, _, idx_fn in streams:
            s = lax.dot_general(k_s[g, rows, :], qpp, nt_dims, preferred_element_type=F32)
            out.append(s + tbl_ref[idx_fn(kt)] if biased else s)
        return tuple(out)

    def softmax_pv(streams, kt, scores, states):
        soft = []
        for s, (m, l, _) in zip(scores, states):
            m_new = jnp.maximum(m, jnp.max(s, axis=0, keepdims=True))
            alpha = jnp.exp(m - m_new)
            p = jnp.exp(s - m_new)
            soft.append((m_new, alpha, alpha * l + jnp.sum(p, axis=0, keepdims=True), p.astype(BF16)))
        pvs = [jnp.dot(vt_ref[kt, st[3]:st[3] + HEAD_DIM, :], sf[3], preferred_element_type=F32)
               for st, sf in zip(streams, soft)]
        return tuple((m_new, l, alpha * acc + pv)
                     for (m_new, alpha, l, _), (_, _, acc), pv in zip(soft, states, pvs))

    flash_init = (jnp.full((1, M), NEG_BIG, F32), jnp.zeros((1, M), F32), jnp.zeros((HEAD_DIM, M), F32))

    qpps, o_cmps, unsel = [], [], []
    for g in range(G):
        qrows = []
        for r in range(R):
            hh = g * R + r
            qpair = q_ref[:, (hh // 2) * LANES:(hh // 2 + 1) * LANES]
            if hh % 2 == 1:
                qpair = pltpu.roll(qpair, HEAD_DIM, axis=1)
            ms = jnp.sum(jnp.where(lane_q < HEAD_DIM, qpair * qpair, 0.0), axis=-1, keepdims=True) * (1.0 / HEAD_DIM)
            qn = qpair * lax.rsqrt(ms + RMS_EPS) * qg_ref[...] * (HEAD_DIM ** -0.5)
            slope = 2.0 ** (-8.0 * (hh + 1) / NSA_HEADS)
            ex = jnp.where(lane_q == HEAD_DIM, slope * LANES, jnp.where(lane_q == HEAD_DIM + 1, slope, 0.0))
            qrows.append(jnp.where(lane_q < HEAD_DIM, qn, ex))
        qbase = jnp.concatenate(qrows, axis=0)

        cb = cbias_ref[...]
        s = (lax.dot_general(kcx_ref[g].astype(BF16), qbase.astype(BF16), nt_dims, preferred_element_type=F32)
             + jnp.concatenate([cb] * R, axis=1))
        mx = jnp.maximum(jnp.max(s, axis=0, keepdims=True), NEG_BIG)
        e = jnp.exp(s - mx)
        l = jnp.sum(e, axis=0, keepdims=True)
        p_cmp = e / jnp.where(l > 0.0, l, 1.0)
        o_cmps.append(_dot(vct_ref[g], p_cmp))

        psum = p_cmp[:, 0:tq]
        for r in range(1, R):
            psum = psum + p_cmp[:, r * tq:(r + 1) * tq]
        hi, mid, lo = _split3(psum)
        ovt = ovt_ref[...]
        imp = (jnp.dot(ovt, hi, preferred_element_type=F32) + jnp.dot(ovt, mid, preferred_element_type=F32)
               + jnp.dot(ovt, lo, preferred_element_type=F32))
        blk = lax.broadcasted_iota(jnp.int32, (NS, tq), 0)
        cur = (t0 + lax.broadcasted_iota(jnp.int32, (NS, tq), 1)) // SLC_BLOCK
        valid = blk <= cur
        forced = (blk == 0) | (blk == cur) | (blk == cur - 1)
        score = jnp.where(forced, SEL_BIG, jnp.where(valid, imp, -SEL_BIG))
        rank = jnp.zeros((NS, tq), F32)
        for j in range(NS):
            cj = score[j:j + 1, :]
            beats = (cj > score) | ((cj == score) & (blk > j))
            rank = rank + jnp.where(beats, 1.0, 0.0)
        nsel = jnp.where((rank < n_top) & valid, 0.0, 1.0)
        unsel.append(jnp.min(nsel, axis=1, keepdims=True))
        nsel = nsel.astype(BF16)
        nsel_q = lax.dot_general(nsel, ish_ref[...], (((0,), (0,)), ((), ())),
                                 preferred_element_type=F32)
        in_sel = (lane_q >= SEL_LANE0) & (lane_q < SEL_LANE0 + NS)
        qpps.append(jnp.concatenate([jnp.where(in_sel, nsel_q, qrows[r]) for r in range(R)], axis=0).astype(BF16))

    slc_idx = lambda kt: jnp.where(kt == kt_d, 1, 0)
    win_lo = kt_d - WINDOW // tk
    win_idx = lambda kt: jnp.where(kt == kt_d, 1, jnp.where(kt == win_lo, 2, 0))
    slc_streams = [(qpps[g], ks_s, g, g * HEAD_DIM, slc_idx) for g in range(G)]
    win_streams = [(qpps[g], kw_s, g, (G + g) * HEAD_DIM, win_idx) for g in range(G)]
    kt_w = jnp.maximum(win_lo, 0)

    blocks_per_tile = tk // SLC_BLOCK
    unsel_any = unsel[0]
    for g in range(1, G):
        unsel_any = jnp.minimum(unsel_any, unsel[g])
    n_live[0] = 0
    for kt in range(S // tk - WINDOW // tk):
        dead = jnp.min(unsel_any[kt * blocks_per_tile:(kt + 1) * blocks_per_tile, :])

        @pl.when((dead < 0.5) & (kt < kt_w))
        def _():
            live[n_live[0]] = kt
            n_live[0] = n_live[0] + 1

    n_slc = n_live[0]

    def slc_only(i, carry):
        scores, states = carry
        nxt = score_tiles(slc_streams, live[jnp.minimum(i + 1, n_slc - 1)], False)
        return nxt, softmax_pv(slc_streams, live[i], scores, states)

    first = jnp.where(n_slc > 0, live[0], 0)
    _, st_slc = lax.fori_loop(0, n_slc, slc_only, (score_tiles(slc_streams, first, False), (flash_init,) * G))
    both = slc_streams + win_streams
    st_all = lax.fori_loop(kt_w, kt_d + 1, lambda kt, st: softmax_pv(both, kt, score_tiles(both, kt, True), st),
                           st_slc + (flash_init,) * G)
    o_all = [acc / jnp.where(l > 0.0, l, 1.0) for _, l, acc in st_all]
    o_slcs, o_wins = o_all[:G], o_all[G:]

    for g in range(G):
        for r in range(R):
            hh = g * R + r
            cs = slice(r * tq, (r + 1) * tq)
            o_ref[hh * HEAD_DIM:(hh + 1) * HEAD_DIM, :] = (
                gate_sig[hh:hh + 1, :] * o_cmps[g][:, cs]
                + gate_sig[NSA_HEADS + hh:NSA_HEADS + hh + 1, :] * o_slcs[g][:, cs]
                + gate_sig[2 * NSA_HEADS + hh:2 * NSA_HEADS + hh + 1, :] * o_wins[g][:, cs])


def _nsa_consts(S):
    tq, tk, M = NSA_TQ, NSA_TK, NSA_M
    NC = S // CMP_STRIDE
    NS = S // SLC_BLOCK
    cs = np.arange(NC)[:, None] * CMP_STRIDE
    ss = np.arange(NS)[None, :] * SLC_BLOCK
    ov = np.clip(np.minimum(cs + CMP_BLOCK, ss + SLC_BLOCK) - np.maximum(cs, ss), 0, None) / CMP_BLOCK
    ov[NC - 1:] = 0.0
    ish = np.zeros((NS, LANES), np.float32)
    ish[np.arange(NS), SEL_LANE0 + np.arange(NS)] = 1.0
    t = np.arange(S).reshape(S // tq, 1, tq)
    n = np.arange(NC).reshape(1, NC, 1)
    cbias = np.where((t >= n * CMP_STRIDE + CMP_BLOCK - 1) & (n < NC - 1), 0.0, -MASK_BIG).astype(np.float32)
    i = np.arange(tk)[:, None]
    j = np.arange(M)[None, :] % tq
    tbl = np.stack([np.zeros((tk, M)), np.where(i <= j, 0.0, -MASK_BIG), np.where(i > j, 0.0, -MASK_BIG)])
    return (jnp.asarray(ov.T, BF16), jnp.asarray(ish, BF16), jnp.asarray(cbias), jnp.asarray(tbl, F32))


def _nsa_call(nq, kv, vt, kcx, vct, misct, q_norm_g, k_norm_g, B, S):
    T = B * S
    tq, tk, M = NSA_TQ, NSA_TK, NSA_M
    G = NSA_KV_HEADS
    nqt = S // tq
    NC = S // CMP_STRIDE
    NS = S // SLC_BLOCK
    ovt, ish, cbias, tbl = _nsa_consts(S)
    tile2 = lambda a: jnp.concatenate([a, a], axis=-1)
    row = lambda b, i: (b * nqt + i, 0)
    cst = lambda b, i: (0, 0)
    return pl.pallas_call(
        functools.partial(_nsa_kernel, S=S),
        grid=(B, nqt),
        in_specs=[pl.BlockSpec((tq, NSA_WIDTH), row),
                  pl.BlockSpec((S, KV_COLS), lambda b, i: (b, 0)),
                  pl.BlockSpec((None, S // tk, 2 * NSA_KV_WIDTH, tk), lambda b, i: (b, 0, 0, 0)),
                  pl.BlockSpec((None, G, NC, LANES), lambda b, i: (b, 0, 0, 0)),
                  pl.BlockSpec((None, G, HEAD_DIM, NC), lambda b, i: (b, 0, 0, 0)),
                  pl.BlockSpec((LANES, tq), lambda b, i: (0, b * nqt + i)),
                  pl.BlockSpec((None, NC, tq), lambda b, i: (i, 0, 0)),
                  pl.BlockSpec((3, tk, M), lambda b, i: (0, 0, 0)),
                  pl.BlockSpec((1, LANES), cst),
                  pl.BlockSpec((3, LANES), cst),
                  pl.BlockSpec((NS, NC), cst),
                  pl.BlockSpec((NS, LANES), cst)],
        out_specs=pl.BlockSpec((NSA_WIDTH, tq), lambda b, i: (0, b * nqt + i)),
        out_shape=jax.ShapeDtypeStruct((NSA_WIDTH, T), F32),
        scratch_shapes=[pltpu.VMEM((G, S, LANES), BF16), pltpu.VMEM((G, S, LANES), BF16),
                        pltpu.SMEM((S // tk,), jnp.int32), pltpu.SMEM((1,), jnp.int32)],
        compiler_params=_cparams("parallel", "arbitrary"),
        name="nsa_attn",
    )(nq, kv, vt, kcx, vct, misct, cbias, tbl, tile2(q_norm_g), tile2(k_norm_g), ovt, ish)


def _out_proj_kernel(yg_ref, yn_ref, x_ref, g1_ref, sc_ref, sh_ref, ng_ref, wo_ref, rwt_ref, rb_ref,
                     x1_ref, h2_ref, gate_ref, gate_t_ref):
    mix = _dot(yg_ref[...], wo_ref[:GDN_WIDTH, :]) + _dot_tn(yn_ref[...], wo_ref[GDN_WIDTH:, :])
    x1 = x_ref[...] + g1_ref[...] * mix
    x1_ref[...] = x1
    h2 = _rms(x1, ng_ref[...]) * (1.0 + sc_ref[...]) + sh_ref[...]
    h2_ref[...] = h2.astype(BF16)

    tm = h2.shape[0]
    logits = _dot_nt_hi(rwt_ref[...], h2)
    score = _sigmoid(logits)
    biased = score + rb_ref[...]
    b = [biased[e:e + 1, :] for e in range(N_EXPERTS)]
    n_groups = N_EXPERTS // EXPERTS_PER_GROUP
    gscore = []
    for gi in range(n_groups):
        vals = b[gi * EXPERTS_PER_GROUP:(gi + 1) * EXPERTS_PER_GROUP]
        best = None
        for i in range(EXPERTS_PER_GROUP):
            for j in range(i + 1, EXPERTS_PER_GROUP):
                pair = vals[i] + vals[j]
                best = pair if best is None else jnp.maximum(best, pair)
        gscore.append(best)
    gbest = jnp.zeros((1, tm), jnp.int32)
    top = gscore[0]
    for gi in range(1, n_groups):
        better = gscore[gi] > top
        gbest = jnp.where(better, gi, gbest)
        top = jnp.where(better, gscore[gi], top)
    erow = lax.broadcasted_iota(jnp.int32, (N_EXPERTS, tm), 0)
    gates = jnp.zeros((N_EXPERTS, tm), F32)
    for e in range(N_EXPERTS):
        gi = e // EXPERTS_PER_GROUP
        rank = jnp.zeros((1, tm), F32)
        for e2 in range(gi * EXPERTS_PER_GROUP, (gi + 1) * EXPERTS_PER_GROUP):
            if e2 == e:
                continue
            beats = (b[e2] > b[e]) | ((b[e2] == b[e]) & (e2 < e))
            rank = rank + jnp.where(beats, 1.0, 0.0)
        chosen = (gbest == gi) & (rank < 2.0)
        ge = jnp.where(chosen, score[e:e + 1, :], 0.0)
        gates = gates + jnp.where(erow == e, ge, 0.0)
    gates = gates / jnp.sum(gates, axis=0, keepdims=True)
    pad_rows = jnp.zeros((8 - EXPERTS_PER_GROUP, tm), F32)
    for gi in range(n_groups):
        gate_t_ref[gi] = jnp.concatenate(
            [gates[gi * EXPERTS_PER_GROUP:(gi + 1) * EXPERTS_PER_GROUP, :], pad_rows], axis=0)
    ident = (lax.broadcasted_iota(jnp.int32, (N_EXPERTS, LANES), 0)
             == lax.broadcasted_iota(jnp.int32, (N_EXPERTS, LANES), 1))
    ident = jnp.where(ident, 1.0, 0.0).astype(BF16)
    hi, mid, lo = _split3(gates)
    dn = (((0,), (0,)), ((), ()))
    gate_ref[...] = (lax.dot_general(hi, ident, dn, preferred_element_type=F32)
                     + lax.dot_general(mid, ident, dn, preferred_element_type=F32)
                     + lax.dot_general(lo, ident, dn, preferred_element_type=F32))


def _out_proj_call(yg, yn, x2, g1, sc2, sh2, ng, wo, rwt, rb, S):
    T, D = x2.shape
    tm = 512
    per_b = S // tm
    row = lambda i: (i, 0)
    bat = lambda i: (i // per_b, 0, 0)
    cst = lambda i: (0, 0)
    return pl.pallas_call(
        _out_proj_kernel,
        grid=(T // tm,),
        in_specs=[pl.BlockSpec((tm, GDN_WIDTH), row), pl.BlockSpec((NSA_WIDTH, tm), lambda i: (0, i)),
                  pl.BlockSpec((tm, D), row),
                  pl.BlockSpec((None, 1, D), bat), pl.BlockSpec((None, 1, D), bat), pl.BlockSpec((None, 1, D), bat),
                  pl.BlockSpec((1, D), cst), pl.BlockSpec((D, D), cst),
                  pl.BlockSpec((N_EXPERTS, D), cst), pl.BlockSpec((N_EXPERTS, 1), cst)],
        out_specs=[pl.BlockSpec((tm, D), row), pl.BlockSpec((tm, D), row), pl.BlockSpec((tm, LANES), row),
                   pl.BlockSpec((N_EXPERTS // EXPERTS_PER_GROUP, 8, tm), lambda i: (0, 0, i))],
        out_shape=[jax.ShapeDtypeStruct((T, D), F32), jax.ShapeDtypeStruct((T, D), BF16),
                   jax.ShapeDtypeStruct((T, LANES), F32),
                   jax.ShapeDtypeStruct((N_EXPERTS // EXPERTS_PER_GROUP, 8, T), F32)],
        compiler_params=_cparams("parallel"),
        name="out_proj_router",
    )(yg, yn, x2, g1, sc2, sh2, ng, wo, rwt, rb)


MOE_TM = 1024
MOE_RB = 288


def _moe_kernel(h_ref, gates_ref, gt8_ref, tri_ref, trit_ref, x1_ref, g2_ref, wg_ref, wu_ref, wd_ref, o_ref):
    g = pl.program_id(1)
    tm, rb = MOE_TM, MOE_RB
    nlb = tm // LANES

    @pl.when(g == 0)
    def _():
        o_ref[...] = jnp.zeros_like(o_ref)

    gt8 = gt8_ref[...]
    in_row = jnp.sum(gt8, axis=0, keepdims=True) > 0.0
    first = lax.broadcasted_iota(jnp.int32, (8, LANES), 0) == 0
    off = jnp.zeros((1, 1), F32)
    keys = []
    for b in range(nlb):
        blk = in_row[:, b * LANES:(b + 1) * LANES]
        one = jnp.where(blk & first, 1.0, 0.0).astype(BF16)
        within = jnp.dot(one, tri_ref[...], preferred_element_type=F32)[0:1, :]
        keys.append(jnp.where(blk, within + off, -1.0))
        off = off + jnp.sum(jnp.where(blk, 1.0, 0.0), axis=1, keepdims=True)
    key_row = jnp.concatenate(keys, axis=1)
    n_g = jnp.sum(off).astype(jnp.int32)
    lane = lax.broadcasted_iota(jnp.int32, (tm, LANES), 1)
    mine = (lane >= g * EXPERTS_PER_GROUP) & (lane < (g + 1) * EXPERTS_PER_GROUP)
    in_col = jnp.sum(jnp.where(mine, gates_ref[...], 0.0), axis=1, keepdims=True) > 0.0
    off = jnp.zeros((1, 1), F32)
    keys = []
    for b in range(nlb):
        blk = in_col[b * LANES:(b + 1) * LANES, :]
        one = jnp.broadcast_to(jnp.where(blk, 1.0, 0.0), (LANES, LANES)).astype(BF16)
        within = jnp.dot(trit_ref[...], one, preferred_element_type=F32)[:, 0:1]
        keys.append(jnp.where(blk, within + off, -1.0))
        off = off + jnp.sum(jnp.where(blk, 1.0, 0.0), axis=0, keepdims=True)
    key_col = jnp.concatenate(keys, axis=0)

    gt_hi, gt_mid, gt_lo = _split3(gt8)
    nt_dims = (((1,), (1,)), ((), ()))

    def block(j, carry):
        r0 = (j * rb).astype(F32)
        sel = jnp.where(key_row - r0 == lax.broadcasted_iota(jnp.int32, (rb, tm), 0).astype(F32), 1.0, 0.0)
        sel = sel.astype(BF16)
        xb = jnp.dot(sel, h_ref[...], preferred_element_type=F32).astype(BF16)
        gb = (lax.dot_general(sel, gt_hi, nt_dims, preferred_element_type=F32)
              + lax.dot_general(sel, gt_mid, nt_dims, preferred_element_type=F32)
              + lax.dot_general(sel, gt_lo, nt_dims, preferred_element_type=F32))
        y = jnp.zeros((rb, o_ref.shape[1]), F32)
        for k in range(EXPERTS_PER_GROUP):
            act = (_silu(jnp.dot(xb, wg_ref[k], preferred_element_type=F32))
                   * jnp.dot(xb, wu_ref[k], preferred_element_type=F32))
            y = y + _dot(act * gb[:, k:k + 1], wd_ref[k])
        sel_t = jnp.where(key_col - r0 == lax.broadcasted_iota(jnp.int32, (tm, rb), 1).astype(F32), 1.0, 0.0)
        o_ref[...] += jnp.dot(sel_t.astype(BF16), y.astype(BF16), preferred_element_type=F32)
        return carry

    lax.fori_loop(0, (n_g + rb - 1) // rb, block, 0)

    @pl.when(g == N_EXPERTS // EXPERTS_PER_GROUP - 1)
    def _():
        o_ref[...] = x1_ref[...] + g2_ref[...] * o_ref[...]


def _moe_call(h2, gates, gates_t, x1, g2, wg, wu, wd, S):
    T, D = x1.shape
    tm = MOE_TM
    per_b = S // tm
    n_groups = N_EXPERTS // EXPERTS_PER_GROUP
    i = np.arange(LANES)
    tri = jnp.asarray(i[:, None] < i[None, :], BF16)
    row = lambda i, g: (i, 0)
    cst = lambda i, g: (0, 0)
    wsel = lambda i, g: (g, 0, 0)
    return pl.pallas_call(
        _moe_kernel,
        grid=(T // tm, n_groups),
        in_specs=[pl.BlockSpec((tm, D), row), pl.BlockSpec((tm, LANES), row),
                  pl.BlockSpec((None, 8, tm), lambda i, g: (g, 0, i)),
                  pl.BlockSpec((LANES, LANES), cst), pl.BlockSpec((LANES, LANES), cst),
                  pl.BlockSpec((tm, D), row),
                  pl.BlockSpec((None, 1, D), lambda i, g: (i // per_b, 0, 0)),
                  pl.BlockSpec((EXPERTS_PER_GROUP, D, EXPERT_FF), wsel),
                  pl.BlockSpec((EXPERTS_PER_GROUP, D, EXPERT_FF), wsel),
                  pl.BlockSpec((EXPERTS_PER_GROUP, EXPERT_FF, D), wsel)],
        out_specs=pl.BlockSpec((tm, D), row),
        out_shape=jax.ShapeDtypeStruct((T, D), F32),
        compiler_params=_cparams("parallel", "arbitrary"),
        name="moe",
    )(h2, gates, gates_t, tri, tri.T, x1, g2, wg, wu, wd)


def kernel(x, c, ada_w, ada_b, norm1_g, norm2_g, w_in, gdn_conv_w, gdn_a_log, gdn_dt_bias, gdn_norm_g,
           nsa_q_norm_g, nsa_k_norm_g, cmp_pe, cmp_w1, cmp_b1, cmp_w2, cmp_b2, w_out, router_w, router_bias,
           exp_w_gate, exp_w_up, exp_w_down):
    B, S, D = x.shape
    L = ada_w.shape[0]
    T = B * S
    G = NSA_KV_HEADS
    NC = S // CMP_STRIDE
    N = S // GDN_CHUNK
    P = GDN_HEADS // GDN_HPB
    W = GDN_HPB * HEAD_DIM

    mod = _ada_call(c, ada_w, ada_b)
    rwt = router_w.T
    rb = router_bias.reshape(N_EXPERTS, 1)
    x2 = x.reshape(T, D)
    for l in range(L):
        m6 = mod[l].reshape(B, 6, 1, D)
        sh1, sc1, g1, sh2, sc2, g2 = (m6[:, i] for i in range(6))
        qkv, z, nq, kv, misc, misct, vt = _in_proj_call(
            x2, sc1, sh1, norm1_g[l].reshape(1, D), *_pad_w_in(w_in[l]), S)

        a_rows = misc[:, :GDN_HEADS].reshape(B, N, GDN_CHUNK, P, GDN_HPB).transpose(0, 3, 1, 4, 2).reshape(B, P, N, W)
        rep = lambda t: jnp.repeat(t, HEAD_DIM).reshape(1, GDN_WIDTH)
        y_gdn = _gdn_call(qkv, z, misc, a_rows, gdn_conv_w[l], rep(gdn_a_log[l]), rep(gdn_dt_bias[l]),
                          jnp.tile(gdn_norm_g[l], GDN_HPB).reshape(1, W), B, S)

        w2 = cmp_w2[l]
        kcx, vct = _cmp_call(
            kv, cmp_pe[l].reshape(2, 1, CMP_BLOCK * HEAD_DIM), cmp_w1[l].astype(BF16),
            cmp_b1[l].reshape(2, 1, CMP_HIDDEN), jnp.pad(w2[0], ((0, 0), (0, LANES - HEAD_DIM))).astype(BF16),
            w2[1].T.astype(BF16), jnp.pad(cmp_b2[l, 0], (0, LANES - HEAD_DIM)).reshape(1, LANES),
            cmp_b2[l, 1].reshape(HEAD_DIM, 1), jnp.pad(nsa_k_norm_g[l, 0], (0, LANES - HEAD_DIM)).reshape(1, LANES),
            B, S)
        y_nsa_t = _nsa_call(nq, kv, vt.reshape(B, S // NSA_TK, 2 * NSA_KV_WIDTH, NSA_TK), kcx, vct, misct,
                            nsa_q_norm_g[l].reshape(1, HEAD_DIM), nsa_k_norm_g[l], B, S)

        x1, h2, gates, gates_t = _out_proj_call(y_gdn, y_nsa_t, x2, g1, sc2, sh2, norm2_g[l].reshape(1, D),
                                       w_out[l].astype(BF16), rwt, rb, S)
        x2 = _moe_call(h2, gates, gates_t, x1, g2, exp_w_gate[l].astype(BF16), exp_w_up[l].astype(BF16),
                       exp_w_down[l].astype(BF16), S)
    return x2.reshape(B, S, D)
```

```python
import functools

import numpy as np
import jax
import jax.numpy as jnp
from jax import lax
from jax.experimental import pallas as pl
from jax.experimental.pallas import tpu as pltpu

F32 = jnp.float32
BF16 = jnp.bfloat16

HEAD_DIM = 64
GDN_HEADS = 8
GDN_WIDTH = GDN_HEADS * HEAD_DIM
GDN_CONV = 4
GDN_CHUNK = 64
NSA_HEADS = 8
NSA_KV_HEADS = 2
NSA_GROUP = NSA_HEADS // NSA_KV_HEADS
NSA_WIDTH = NSA_HEADS * HEAD_DIM
NSA_KV_WIDTH = NSA_KV_HEADS * HEAD_DIM
CMP_BLOCK = 32
CMP_STRIDE = 16
CMP_HIDDEN = 256
SLC_BLOCK = 64
SLC_TOPN = 8
WINDOW = 512
N_EXPERTS = 16
EXPERTS_PER_GROUP = 4
EXPERT_FF = 512
RMS_EPS = 1e-6
NEG_BIG = -1e30
SEL_BIG = 1e9

LANES = 128
VMEM_LIMIT_BYTES = 56 * 1024 * 1024

C_QKV = 0
C_Z = 3 * GDN_WIDTH
C_NQ = C_Z + GDN_WIDTH
C_KV = C_NQ + NSA_WIDTH
KV_COLS = 4 * NSA_KV_WIDTH
C_MISC = C_KV + KV_COLS
IN_PAD_COLS = C_MISC + LANES
MISC_B0 = GDN_HEADS
MISC_GATE0 = 16
VT_SLAB = HEAD_DIM + 16
VT_ROWS = 2 * NSA_KV_HEADS * VT_SLAB


def _cparams(*sem):
    return pltpu.CompilerParams(dimension_semantics=sem, vmem_limit_bytes=VMEM_LIMIT_BYTES)


def _dot(a, b):
    return jnp.dot(a.astype(BF16), b.astype(BF16), preferred_element_type=F32)


def _dot_nt(a, b):
    return lax.dot_general(a.astype(BF16), b.astype(BF16), (((1,), (1,)), ((), ())),
                           preferred_element_type=F32)


def _dot_tn(a, b):
    return lax.dot_general(a.astype(BF16), b.astype(BF16), (((0,), (0,)), ((), ())),
                           preferred_element_type=F32)


def _split2(x):
    hi = x.astype(BF16)
    lo = (x - hi.astype(F32)).astype(BF16)
    return hi, lo


def _split3(x):
    hi = x.astype(BF16)
    r = x - hi.astype(F32)
    mid = r.astype(BF16)
    lo = (r - mid.astype(F32)).astype(BF16)
    return hi, mid, lo


def _dot_hi(a, b):
    ah, al = _split2(a)
    bh, bl = _split2(b)
    return (jnp.dot(ah, bh, preferred_element_type=F32) + jnp.dot(al, bh, preferred_element_type=F32)
            + jnp.dot(ah, bl, preferred_element_type=F32))


def _dot_nt_hi(a, b):
    ah, al = _split2(a)
    bh, bl = _split2(b)
    dn = (((1,), (1,)), ((), ()))
    return (lax.dot_general(ah, bh, dn, preferred_element_type=F32)
            + lax.dot_general(al, bh, dn, preferred_element_type=F32)
            + lax.dot_general(ah, bl, dn, preferred_element_type=F32))


def _dot_exact_rhs(a, b_bf16):
    hi, mid, lo = _split3(a)
    return (jnp.dot(hi, b_bf16, preferred_element_type=F32) + jnp.dot(mid, b_bf16, preferred_element_type=F32)
            + jnp.dot(lo, b_bf16, preferred_element_type=F32))


def _rms(x, g):
    return x * lax.rsqrt(jnp.mean(x * x, axis=-1, keepdims=True) + RMS_EPS) * g


def _sigmoid(x):
    return 0.5 * (jnp.tanh(0.5 * x) + 1.0)


def _silu(x):
    return x * _sigmoid(x)


def _ada_kernel(c_ref, w_ref, b_ref, o_ref):
    o_ref[...] = _dot_hi(_silu(c_ref[...]), w_ref[...]) + b_ref[...]


def _ada_call(c, ada_w, ada_b):
    L, D, N = ada_w.shape
    B = c.shape[0]
    tn = 1024
    return pl.pallas_call(
        _ada_kernel,
        grid=(L, N // tn),
        in_specs=[pl.BlockSpec((B, D), lambda l, j: (0, 0)),
                  pl.BlockSpec((None, D, tn), lambda l, j: (l, 0, j)),
                  pl.BlockSpec((None, 1, tn), lambda l, j: (l, 0, j))],
        out_specs=pl.BlockSpec((None, B, tn), lambda l, j: (l, 0, j)),
        out_shape=jax.ShapeDtypeStruct((L, B, N), F32),
        compiler_params=_cparams("parallel", "parallel"),
        name="ada_mod",
    )(c, ada_w, ada_b.reshape(L, 1, N))


IN_SEGS = ((C_QKV, 3 * GDN_WIDTH), (C_Z, GDN_WIDTH), (C_NQ, NSA_WIDTH), (C_KV, KV_COLS), (C_MISC, LANES))


def _in_proj_kernel(x_ref, sc_ref, sh_ref, g_ref, w_ref, wmt_ref, wvt_ref, *o_refs):
    h = _rms(x_ref[...], g_ref[...]) * (1.0 + sc_ref[...]) + sh_ref[...]
    hb = h.astype(BF16)
    for (c0, n), o_ref in zip(IN_SEGS, o_refs[:-2]):
        o_ref[...] = jnp.dot(hb, w_ref[:, c0:c0 + n], preferred_element_type=F32)
    nt_dims = (((1,), (1,)), ((), ()))
    o_refs[-2][...] = lax.dot_general(wmt_ref[...], hb, nt_dims, preferred_element_type=F32)
    vt = lax.dot_general(wvt_ref[...], hb, nt_dims, preferred_element_type=F32).astype(BF16)
    ones = jnp.ones((VT_SLAB - HEAD_DIM, LANES), BF16)
    for j in range(o_refs[-1].shape[0]):
        for st in range(2 * NSA_KV_HEADS):
            o_refs[-1][j, st * VT_SLAB:st * VT_SLAB + HEAD_DIM, :] = vt[st * HEAD_DIM:(st + 1) * HEAD_DIM,
                                                                         j * LANES:(j + 1) * LANES]
            o_refs[-1][j, st * VT_SLAB + HEAD_DIM:(st + 1) * VT_SLAB, :] = ones


def _in_proj_call(x2, sc, sh, g, w_pad, wvt, S):
    T, D = x2.shape
    tm = 256
    per_b = S // tm
    row = lambda i: (i, 0)
    bat = lambda i: (i // per_b, 0, 0)
    return pl.pallas_call(
        _in_proj_kernel,
        grid=(T // tm,),
        in_specs=[pl.BlockSpec((tm, D), row),
                  pl.BlockSpec((None, 1, D), bat),
                  pl.BlockSpec((None, 1, D), bat),
                  pl.BlockSpec((1, D), lambda i: (0, 0)),
                  pl.BlockSpec((D, IN_PAD_COLS), lambda i: (0, 0)),
                  pl.BlockSpec((LANES, D), lambda i: (0, 0)),
                  pl.BlockSpec((2 * NSA_KV_WIDTH, D), lambda i: (0, 0))],
        out_specs=([pl.BlockSpec((tm, n), row) for _, n in IN_SEGS]
                   + [pl.BlockSpec((LANES, tm), lambda i: (0, i)),
                      pl.BlockSpec((tm // LANES, VT_ROWS, LANES), lambda i: (i, 0, 0))]),
        out_shape=([jax.ShapeDtypeStruct((T, n), F32) for _, n in IN_SEGS]
                   + [jax.ShapeDtypeStruct((LANES, T), F32),
                      jax.ShapeDtypeStruct((T // LANES, VT_ROWS, LANES), BF16)]),
        compiler_params=_cparams("parallel"),
        name="in_proj",
    )(x2, sc, sh, g, w_pad, w_pad[:, C_MISC:].T, wvt)


def _pad_w_in(w_in):
    D = w_in.shape[0]
    W = GDN_WIDTH
    o = 0
    gq, gk, gv, gz = (w_in[:, o + i * W:o + (i + 1) * W] for i in range(4))
    o += 4 * W
    ga = w_in[:, o:o + GDN_HEADS]
    gb = w_in[:, o + GDN_HEADS:o + 2 * GDN_HEADS]
    o += 2 * GDN_HEADS
    nq = w_in[:, o:o + NSA_WIDTH]
    o += NSA_WIDTH
    kc, vc, ks, vs, kw, vw = (w_in[:, o + i * NSA_KV_WIDTH:o + (i + 1) * NSA_KV_WIDTH] for i in range(6))
    o += 6 * NSA_KV_WIDTH
    gates = w_in[:, o:o + 3 * NSA_HEADS]
    misc = jnp.zeros((D, LANES), w_in.dtype)
    misc = misc.at[:, 0:GDN_HEADS].set(ga).at[:, MISC_B0:MISC_B0 + GDN_HEADS].set(gb)
    misc = misc.at[:, MISC_GATE0:MISC_GATE0 + 3 * NSA_HEADS].set(gates)
    cols = [gq, gk, gv, gz, nq, kc, vc, ks, kw, misc]
    return jnp.concatenate(cols, axis=1).astype(BF16), jnp.concatenate([vs, vw], axis=1).T.astype(BF16)


GDN_HPB = 2
GDN_NB = 32


def _gdn_kernel(q_ref, k_ref, v_ref, z_ref, misc_ref, ar_ref, cwq_ref, cwk_ref, cwv_ref,
                alog_ref, dtb_ref, alogr_ref, dtbr_ref, ng_ref, o_ref,
                qs, ks, kbs, xs, gcs, grs, ps, qq, qks, os_, rs, q2s, sev, *, S):
    C = GDN_CHUNK
    N = S // C
    W = GDN_HPB * HEAD_DIM
    row = lax.broadcasted_iota(jnp.int32, (S, W), 0)
    row8 = lax.broadcasted_iota(jnp.int32, (8, W), 0)

    def conv_silu(x_ref, w):
        x = x_ref[...]
        y = x * w[GDN_CONV - 1:GDN_CONV, :]
        for s in range(1, GDN_CONV):
            head = jnp.where(row8 >= s, pltpu.roll(x[:8], s, axis=0), 0.0)
            xs = jnp.concatenate([head, x_ref[pl.ds(8 - s, S - 8), :]], axis=0)
            y = y + xs * w[GDN_CONV - 1 - s:GDN_CONV - s, :]
        return _silu(y)

    bd = jnp.where(lax.broadcasted_iota(jnp.int32, (W, W), 0) // HEAD_DIM
                   == lax.broadcasted_iota(jnp.int32, (W, W), 1) // HEAD_DIM, 1.0, 0.0).astype(BF16)

    def head_sumsq(x):
        hi, lo = _split2(x * x)
        return jnp.dot(hi, bd, preferred_element_type=F32) + jnp.dot(lo, bd, preferred_element_type=F32)

    def head_l2(x):
        return x * lax.rsqrt(head_sumsq(x) + RMS_EPS)

    q = head_l2(conv_silu(q_ref, cwq_ref[...])) * (HEAD_DIM ** -0.5)
    k = head_l2(conv_silu(k_ref, cwk_ref[...]))
    v = conv_silu(v_ref, cwv_ref[...])

    hd0 = pl.program_id(1) * GDN_HPB
    src = lax.broadcasted_iota(jnp.int32, (LANES, W), 0)
    dst = lax.broadcasted_iota(jnp.int32, (LANES, W), 1) // HEAD_DIM + hd0
    a_exp = _dot_exact_rhs(misc_ref[...], jnp.where(src == dst, 1.0, 0.0).astype(BF16))
    b_exp = _dot_exact_rhs(misc_ref[...], jnp.where(src == dst + MISC_B0, 1.0, 0.0).astype(BF16))
    beta = _sigmoid(b_exp)

    def log_decay(a, alog, dtb):
        xx = a + dtb
        sp = jnp.maximum(xx, 0.0) + jnp.log(1.0 + jnp.exp(-jnp.abs(xx)))
        return -jnp.exp(alog) * sp

    gc = log_decay(a_exp, alog_ref[...], dtb_ref[...])
    pos = row % C
    for s in (1, 2, 4, 8, 16, 32):
        gc = gc + jnp.where(pos >= s, pltpu.roll(gc, s, axis=0), 0.0)
    gr = log_decay(ar_ref[...], alogr_ref[...], dtbr_ref[...])
    lpos = lax.broadcasted_iota(jnp.int32, (N, W), 1) % C
    for s in (1, 2, 4, 8, 16, 32):
        gr = gr + jnp.where(lpos >= s, pltpu.roll(gr, s, axis=1), 0.0)

    kb = k * beta
    vb = v * beta
    kbe = kb * jnp.exp(gc)
    for h in range(GDN_HPB):
        sl = slice(h * HEAD_DIM, (h + 1) * HEAD_DIM)
        qs[h] = q[:, sl]
        ks[h] = k[:, sl]
        kbs[h] = kb[:, sl]
        gcs[h] = gc[:, sl]
        for n in range(N):
            grs[h, n] = jnp.broadcast_to(gr[n:n + 1, sl], (8, HEAD_DIM))
        xs[h] = jnp.concatenate([kbe[:, sl], vb[:, sl]], axis=1)

    NB = GDN_NB
    ci3 = lax.broadcasted_iota(jnp.int32, (1, C, C), 1)
    cj3 = lax.broadcasted_iota(jnp.int32, (1, C, C), 2)
    eye3 = jnp.where(ci3 == cj3, 1.0, 0.0)

    def bmm(a, b):
        return lax.dot_general(a.astype(BF16), b.astype(BF16), (((2,), (1,)), ((0,), (0,))),
                               preferred_element_type=F32)

    def bmm_nt(a, b):
        return lax.dot_general(a.astype(BF16), b.astype(BF16), (((2,), (2,)), ((0,), (0,))),
                               preferred_element_type=F32)

    def intra(nb, carry):
        rows = pl.ds(pl.multiple_of(nb * (NB * C), NB * C), NB * C)
        for h in range(GDN_HPB):
            kk = ks[h, rows, :].reshape(NB, C, HEAD_DIM)
            kbq = jnp.concatenate([kbs[h, rows, :].reshape(NB, C, HEAD_DIM),
                                   qs[h, rows, :].reshape(NB, C, HEAD_DIM)], axis=1)
            a2 = bmm_nt(kbq, kk)
            gcol = gcs[h, rows, :].reshape(NB, C, HEAD_DIM)
            grow = grs[h, pl.ds(nb * NB, NB), 0:1, :]
            decay = jnp.exp(jnp.where(ci3 >= cj3, gcol - grow, NEG_BIG))
            low = jnp.where(ci3 > cj3, a2[:, :C] * decay, 0.0)
            qks[h, rows, :] = (a2[:, C:] * decay).reshape(NB * C, C)
            tinv = eye3 - jnp.where(ci3 // 2 == cj3 // 2, low, 0.0)
            for lv in range(1, 6):
                s = 2 ** lv
                off = jnp.where((ci3 // (2 * s) == cj3 // (2 * s)) & (ci3 // s != cj3 // s), low, 0.0)
                tinv = tinv - bmm(tinv, bmm(off, tinv))
            x = bmm(tinv, xs[h, rows, :].reshape(NB, C, 2 * HEAD_DIM))
            xs[h, rows, :] = x.reshape(NB * C, 2 * HEAD_DIM)
            kd = kk * jnp.exp(gcol[:, C - 1:C, :] - gcol)
            pq = lax.dot_general(kd.astype(BF16), x.astype(BF16), (((1,), (1,)), ((0,), (0,))),
                                 preferred_element_type=F32)
            pn = pq[:, :, :HEAD_DIM]
            qn = pq[:, :, HEAD_DIM:]
            ps[h, rows, :] = pn.reshape(NB * C, HEAD_DIM)
            qq[h, rows, :] = qn.reshape(NB * C, HEAD_DIM)
            cd = jnp.exp(gcol[:, C - 1:C, :]).reshape(NB // 2, 2, 1, HEAD_DIM)
            pn = pn.reshape(NB // 2, 2, HEAD_DIM, HEAD_DIM)
            qn = qn.reshape(NB // 2, 2, HEAD_DIM, HEAD_DIM)
            p0, p1, q0, q1, c0, c1 = pn[:, 0], pn[:, 1], qn[:, 0], qn[:, 1], cd[:, 0], cd[:, 1]
            t = bmm(p1, pq.reshape(NB // 2, 2, HEAD_DIM, 2 * HEAD_DIM)[:, 0])
            prow = pl.ds(pl.multiple_of(nb * (NB // 2 * C), NB // 2 * C), NB // 2 * C)
            rs[h, prow, :] = (c1 * p0 + c0 * p1 - t[:, :, :HEAD_DIM]).reshape(NB // 2 * C, HEAD_DIM)
            q2s[h, prow, :] = (c1 * q0 - t[:, :, HEAD_DIM:] + q1).reshape(NB // 2 * C, HEAD_DIM)
        return carry

    lax.fori_loop(0, N // NB, intra, 0)

    def inter(k, states):
        sl = pl.ds(pl.multiple_of(k * C, C), C)
        new_states = []
        for h in range(GDN_HPB):
            st = states[h]
            sev[h, k] = st
            gsum = gcs[h, pl.ds(2 * k * C + C - 1, 1), :] + gcs[h, pl.ds(2 * k * C + 2 * C - 1, 1), :]
            new_states.append(st * jnp.exp(gsum) - _dot_hi(rs[h, sl, :], st) + q2s[h, sl, :])
        return tuple(new_states)

    lax.fori_loop(0, N // 2, inter, tuple(jnp.zeros((HEAD_DIM, HEAD_DIM), F32) for _ in range(GDN_HPB)))

    def outputs(nb, carry):
        rows = pl.ds(pl.multiple_of(nb * (NB * C), NB * C), NB * C)
        for h in range(GDN_HPB):
            gcol = gcs[h, rows, :].reshape(NB, C, HEAD_DIM)
            s0 = sev[h, pl.ds(nb * (NB // 2), NB // 2)]
            p0 = ps[h, rows, :].reshape(NB // 2, 2, HEAD_DIM, HEAD_DIM)[:, 0]
            q0 = qq[h, rows, :].reshape(NB // 2, 2, HEAD_DIM, HEAD_DIM)[:, 0]
            c0 = jnp.exp(gcol[:, C - 1:C, :]).reshape(NB // 2, 2, 1, HEAD_DIM)[:, 0]
            s1 = c0 * s0 - bmm(p0, s0) + q0
            st = jnp.stack([s0, s1], axis=1).reshape(NB, HEAD_DIM, HEAD_DIM)
            x = xs[h, rows, :].reshape(NB, C, 2 * HEAD_DIM)
            qd = qs[h, rows, :].reshape(NB, C, HEAD_DIM) * jnp.exp(gcol)
            t1 = bmm(jnp.concatenate([x[:, :, :HEAD_DIM], qd], axis=1), st)
            vn = x[:, :, HEAD_DIM:] - t1[:, :C]
            o = t1[:, C:] + bmm(qks[h, rows, :].reshape(NB, C, C), vn)
            os_[h, rows, :] = o.reshape(NB * C, HEAD_DIM)
        return carry

    lax.fori_loop(0, N // NB, outputs, 0)

    o = jnp.concatenate([os_[h] for h in range(GDN_HPB)], axis=1)
    o = o * lax.rsqrt(head_sumsq(o) * (1.0 / HEAD_DIM) + RMS_EPS) * ng_ref[...]
    o_ref[...] = o * _silu(z_ref[...])


def _gdn_call(qkv, z, misc, a_rows, conv_w, alog_exp, dtb_exp, norm_g, B, S):
    T = B * S
    W = GDN_HPB * HEAD_DIM
    P = GDN_HEADS // GDN_HPB
    N = S // GDN_CHUNK
    col = lambda off: (lambda b, p: (b, off + p))
    par = lambda off: (lambda b, p: (0, off + p))
    hs = lambda n: pltpu.VMEM((GDN_HPB, S, n), F32)
    return pl.pallas_call(
        functools.partial(_gdn_kernel, S=S),
        grid=(B, P),
        in_specs=[pl.BlockSpec((S, W), col(0)), pl.BlockSpec((S, W), col(P)), pl.BlockSpec((S, W), col(2 * P)),
                  pl.BlockSpec((S, W), col(0)), pl.BlockSpec((S, LANES), lambda b, p: (b, 0)),
                  pl.BlockSpec((None, None, N, W), lambda b, p: (b, p, 0, 0)),
                  pl.BlockSpec((GDN_CONV, W), par(0)), pl.BlockSpec((GDN_CONV, W), par(P)),
                  pl.BlockSpec((GDN_CONV, W), par(2 * P)),
                  pl.BlockSpec((1, W), par(0)), pl.BlockSpec((1, W), par(0)),
                  pl.BlockSpec((1, W), par(0)), pl.BlockSpec((1, W), par(0)),
                  pl.BlockSpec((1, W), lambda b, p: (0, 0))],
        out_specs=pl.BlockSpec((S, W), col(0)),
        out_shape=jax.ShapeDtypeStruct((T, GDN_WIDTH), F32),
        scratch_shapes=[hs(HEAD_DIM), hs(HEAD_DIM), hs(HEAD_DIM), hs(2 * HEAD_DIM), hs(HEAD_DIM),
                        pltpu.VMEM((GDN_HPB, N, 8, HEAD_DIM), F32),
                        hs(HEAD_DIM), hs(HEAD_DIM), hs(HEAD_DIM), hs(HEAD_DIM),
                        pltpu.VMEM((GDN_HPB, S // 2, HEAD_DIM), F32), pltpu.VMEM((GDN_HPB, S // 2, HEAD_DIM), F32),
                        pltpu.VMEM((GDN_HPB, N // 2, HEAD_DIM, HEAD_DIM), F32)],
        compiler_params=_cparams("parallel", "parallel"),
        name="gdn",
    )(qkv, qkv, qkv, z, misc, a_rows, conv_w, conv_w, conv_w,
      alog_exp, dtb_exp, alog_exp, dtb_exp, norm_g)


def _cmp_kernel(kc_in_ref, vc_in_ref, pe_ref, w1_ref, b1_ref, w2k_ref, w2vt_ref, b2k_ref, b2v_ref, kg_ref, kc_ref, vct_ref):
    half = (CMP_BLOCK // 2) * HEAD_DIM
    nrows = kc_in_ref.shape[0] // CMP_STRIDE
    lane = lax.broadcasted_iota(jnp.int32, (nrows, LANES), 1)
    cend = lax.broadcasted_iota(jnp.int32, (nrows, LANES), 0) * CMP_STRIDE + (CMP_BLOCK - 1)
    ext = jnp.where(lane == HEAD_DIM, cend // LANES, jnp.where(lane == HEAD_DIM + 1, cend % LANES, 0)).astype(F32)

    def token_groups(x_ref):
        cols = [[] for _ in range(NSA_KV_HEADS)]
        for m in range(CMP_STRIDE // 2):
            ev = x_ref[pl.ds(2 * m, nrows, stride=CMP_STRIDE), :]
            od = x_ref[pl.ds(2 * m + 1, nrows, stride=CMP_STRIDE), :]
            cols[0].append(jnp.where(lane < HEAD_DIM, ev, pltpu.roll(od, HEAD_DIM, axis=1)))
            cols[1].append(jnp.where(lane < HEAD_DIM, pltpu.roll(ev, HEAD_DIM, axis=1), od))
        return [jnp.concatenate(c, axis=1) for c in cols]

    def hidden(x, j):
        pe = pe_ref[j]
        p = _dot(x + pe[:, :half], w1_ref[j, :half, :])
        q = _dot(x + pe[:, half:], w1_ref[j, half:, :])
        return _silu(p + pltpu.roll(q, nrows - 1, axis=0) + b1_ref[j])

    for g, x in enumerate(token_groups(kc_in_ref)):
        kc = _dot(hidden(x, 0), w2k_ref[...]) + b2k_ref[...]
        ms = jnp.sum(kc * kc, axis=-1, keepdims=True) * (1.0 / HEAD_DIM)
        kc_ref[g] = jnp.where(lane < HEAD_DIM, kc * lax.rsqrt(ms + RMS_EPS) * kg_ref[...], ext)
    for g, x in enumerate(token_groups(vc_in_ref)):
        vct_ref[g] = _dot_nt(w2vt_ref[...], hidden(x, 1)) + b2v_ref[...]


def _cmp_call(kv, pe, w1, b1, w2k, w2vt, b2k, b2v, kg, B, S):
    G = NSA_KV_HEADS
    NC = S // CMP_STRIDE
    full = lambda a: pl.BlockSpec(a.shape, lambda b: (0,) * a.ndim)
    return pl.pallas_call(
        _cmp_kernel,
        grid=(B,),
        in_specs=[pl.BlockSpec((S, NSA_KV_WIDTH), lambda b: (b, 0)), pl.BlockSpec((S, NSA_KV_WIDTH), lambda b: (b, 1))]
        + [full(a) for a in (pe, w1, b1, w2k, w2vt, b2k, b2v, kg)],
        out_specs=[pl.BlockSpec((None, G, NC, LANES), lambda b: (b, 0, 0, 0)),
                   pl.BlockSpec((None, G, HEAD_DIM, NC), lambda b: (b, 0, 0, 0))],
        out_shape=[jax.ShapeDtypeStruct((B, G, NC, LANES), F32), jax.ShapeDtypeStruct((B, G, HEAD_DIM, NC), F32)],
        compiler_params=_cparams("parallel"),
        name="nsa_compress",
    )(kv, kv, pe, w1, b1, w2k, w2vt, b2k, b2v, kg)


NSA_TQ = 128
NSA_TK = 128
NSA_M = NSA_GROUP * NSA_TQ
NSA_MS = NSA_M
MASK_BIG = 2.0 ** 100
SEL_LANE0 = HEAD_DIM + 2


def _nsa_kernel(q_ref, kv_ref, vt_ref, kcx_ref, vct_ref, misct_ref, cbias_ref, tbl_ref, qg_ref, kg_ref,
                ovt_ref, ish_ref, o_ref, ks_s, kw_s, live, n_live, *, S):
    tq, tk, M = NSA_TQ, NSA_TK, NSA_M
    MS = NSA_MS
    G, R = NSA_KV_HEADS, NSA_GROUP
    NS = S // SLC_BLOCK
    n_top = min(SLC_TOPN, NS)
    qi = pl.program_id(1)
    t0 = qi * tq
    nt_dims = (((1,), (1,)), ((), ()))

    @pl.when(qi == 0)
    def _():
        rowi = lax.broadcasted_iota(jnp.int32, (S, LANES), 0)
        lane = lax.broadcasted_iota(jnp.int32, (S, LANES), 1)
        ext = jnp.where(lane == HEAD_DIM, rowi // LANES, jnp.where(lane == HEAD_DIM + 1, rowi % LANES, 0)).astype(F32)
        ext_slc = jnp.where(lane - SEL_LANE0 == rowi // SLC_BLOCK, -MASK_BIG, ext)

        def prep(c0, g, gain, extra):
            kraw = kv_ref[:, c0:c0 + LANES]
            if g == 1:
                kraw = pltpu.roll(kraw, HEAD_DIM, axis=1)
            ms = jnp.sum(jnp.where(lane < HEAD_DIM, kraw * kraw, 0.0), axis=-1, keepdims=True) * (1.0 / HEAD_DIM)
            return jnp.where(lane < HEAD_DIM, kraw * lax.rsqrt(ms + RMS_EPS) * gain, extra).astype(BF16)

        for g in range(G):
            ks_s[g] = prep(2 * NSA_KV_WIDTH, g, kg_ref[1:2, :], ext_slc)
            kw_s[g] = prep(3 * NSA_KV_WIDTH, g, kg_ref[2:3, :], ext)

    lane_q = lax.broadcasted_iota(jnp.int32, (tq, LANES), 1)
    gate_sig = _sigmoid(misct_ref[MISC_GATE0:MISC_GATE0 + 3 * NSA_HEADS, :])
    kt_d = t0 // tk

    def score_tiles(streams, kt, biased):
        rows = pl.ds(pl.multiple_of(kt * tk, tk), tk)
        out = []
        for qpp, k_s, g, _, idx_fn in streams:
            s = lax.dot_general(k_s[g, rows, :], qpp, nt_dims, preferred_element_type=F32)
            out.append(s + tbl_ref[idx_fn(kt)] if biased else s)
        return tuple(out)

    def softmax_pv(streams, kt, scores, states):
        soft = []
        for s, (m, _) in zip(scores, states):
            m_new = jnp.maximum(m, jnp.max(s, axis=0, keepdims=True))
            soft.append((m_new, jnp.exp(m - m_new), jnp.exp(s - m_new).astype(BF16)))
        pvs = [jnp.dot(vt_ref[kt, st[3]:st[3] + VT_SLAB, :], sf[2], preferred_element_type=F32)
               for st, sf in zip(streams, soft)]
        return tuple((m_new, alpha * acc + pv) for (m_new, alpha, _), (_, acc), pv in zip(soft, states, pvs))

    flash_init = (jnp.full((1, M), NEG_BIG, F32), jnp.zeros((VT_SLAB, M), F32))

    qpps, o_cmps, unsel = [], [], []
    for g in range(G):
        qrows = []
        for r in range(R):
            hh = g * R + r
            qpair = q_ref[:, (hh // 2) * LANES:(hh // 2 + 1) * LANES]
            if hh % 2 == 1:
                qpair = pltpu.roll(qpair, HEAD_DIM, axis=1)
            ms = jnp.sum(jnp.where(lane_q < HEAD_DIM, qpair * qpair, 0.0), axis=-1, keepdims=True) * (1.0 / HEAD_DIM)
            qn = qpair * lax.rsqrt(ms + RMS_EPS) * qg_ref[...] * (HEAD_DIM ** -0.5)
            slope = 2.0 ** (-8.0 * (hh + 1) / NSA_HEADS)
            ex = jnp.where(lane_q == HEAD_DIM, slope * LANES, jnp.where(lane_q == HEAD_DIM + 1, slope, 0.0))
            qrows.append(jnp.where(lane_q < HEAD_DIM, qn, ex))
        qbase = jnp.concatenate(qrows, axis=0)

        cb = cbias_ref[...]
        s = (lax.dot_general(kcx_ref[g].astype(BF16), qbase.astype(BF16), nt_dims, preferred_element_type=F32)
             + jnp.concatenate([cb] * R, axis=1))
        mx = jnp.maximum(jnp.max(s, axis=0, keepdims=True), NEG_BIG)
        e = jnp.exp(s - mx)
        l = jnp.sum(e, axis=0, keepdims=True)
        p_cmp = e / jnp.where(l > 0.0, l, 1.0)
        o_cmps.append(_dot(vct_ref[g], p_cmp))

        psum = p_cmp[:, 0:tq]
        for r in range(1, R):
            psum = psum + p_cmp[:, r * tq:(r + 1) * tq]
        hi, mid, lo = _split3(psum)
        ovt = ovt_ref[...]
        imp = (jnp.dot(ovt, hi, preferred_element_type=F32) + jnp.dot(ovt, mid, preferred_element_type=F32)
               + jnp.dot(ovt, lo, preferred_element_type=F32))
        blk = lax.broadcasted_iota(jnp.int32, (NS, tq), 0)
        cur = (t0 + lax.broadcasted_iota(jnp.int32, (NS, tq), 1)) // SLC_BLOCK
        valid = blk <= cur
        forced = (blk == 0) | (blk == cur) | (blk == cur - 1)
        score = jnp.where(forced, SEL_BIG, jnp.where(valid, imp, -SEL_BIG))
        rank = jnp.zeros((NS, tq), F32)
        for j in range(NS):
            cj = score[j:j + 1, :]
            beats = (cj > score) | ((cj == score) & (blk > j))
            rank = rank + jnp.where(beats, 1.0, 0.0)
        nsel = jnp.where((rank < n_top) & valid, 0.0, 1.0)
        unsel.append(jnp.min(nsel, axis=1, keepdims=True))
        nsel = nsel.astype(BF16)
        nsel_q = lax.dot_general(nsel, ish_ref[...], (((0,), (0,)), ((), ())),
                                 preferred_element_type=F32)
        in_sel = (lane_q >= SEL_LANE0) & (lane_q < SEL_LANE0 + NS)
        qpps.append(jnp.concatenate([jnp.where(in_sel, nsel_q, qrows[r]) for r in range(R)], axis=0).astype(BF16))

    slc_idx = lambda kt: jnp.where(kt == kt_d, 1, 0)
    win_lo = kt_d - WINDOW // tk
    win_idx = lambda kt: jnp.where(kt == kt_d, 1, jnp.where(kt == win_lo, 2, 0))
    slc_streams = [(qpps[g], ks_s, g, g * VT_SLAB, slc_idx) for g in range(G)]
    win_streams = [(qpps[g], kw_s, g, (G + g) * VT_SLAB, win_idx) for g in range(G)]
    kt_w = jnp.maximum(win_lo, 0)

    blocks_per_tile = tk // SLC_BLOCK
    unsel_any = unsel[0]
    for g in range(1, G):
        unsel_any = jnp.minimum(unsel_any, unsel[g])
    n_live[0] = 0
    for kt in range(S // tk - WINDOW // tk):
        dead = jnp.min(unsel_any[kt * blocks_per_tile:(kt + 1) * blocks_per_tile, :])

        @pl.when((dead < 0.5) & (kt < kt_w))
        def _():
            live[n_live[0]] = kt
            n_live[0] = n_live[0] + 1

    n_slc = n_live[0]

    def slc_only(i, carry):
        scores, states = carry
        nxt = score_tiles(slc_streams, live[jnp.minimum(i + 1, n_slc - 1)], False)
        return nxt, softmax_pv(slc_streams, live[i], scores, states)

    first = jnp.where(n_slc > 0, live[0], 0)
    _, st_slc = lax.fori_loop(0, n_slc, slc_only, (score_tiles(slc_streams, first, False), (flash_init,) * G))
    both = slc_streams + win_streams
    st_all = lax.fori_loop(kt_w, kt_d + 1, lambda kt, st: softmax_pv(both, kt, score_tiles(both, kt, True), st),
                           st_slc + (flash_init,) * G)
    o_all = []
    for _, acc in st_all:
        den = acc[HEAD_DIM:HEAD_DIM + 1, :]
        o_all.append(acc[:HEAD_DIM] / jnp.where(den > 0.0, den, 1.0))
    o_slcs, o_wins = o_all[:G], o_all[G:]

    for g in range(G):
        for r in range(R):
            hh = g * R + r
            cs = slice(r * tq, (r + 1) * tq)
            o_ref[hh * HEAD_DIM:(hh + 1) * HEAD_DIM, :] = (
                gate_sig[hh:hh + 1, :] * o_cmps[g][:, cs]
                + gate_sig[NSA_HEADS + hh:NSA_HEADS + hh + 1, :] * o_slcs[g][:, cs]
                + gate_sig[2 * NSA_HEADS + hh:2 * NSA_HEADS + hh + 1, :] * o_wins[g][:, cs])


def _nsa_consts(S):
    tq, tk, M = NSA_TQ, NSA_TK, NSA_M
    NC = S // CMP_STRIDE
    NS = S // SLC_BLOCK
    cs = np.arange(NC)[:, None] * CMP_STRIDE
    ss = np.arange(NS)[None, :] * SLC_BLOCK
    ov = np.clip(np.minimum(cs + CMP_BLOCK, ss + SLC_BLOCK) - np.maximum(cs, ss), 0, None) / CMP_BLOCK
    ov[NC - 1:] = 0.0
    ish = np.zeros((NS, LANES), np.float32)
    ish[np.arange(NS), SEL_LANE0 + np.arange(NS)] = 1.0
    t = np.arange(S).reshape(S // tq, 1, tq)
    n = np.arange(NC).reshape(1, NC, 1)
    cbias = np.where((t >= n * CMP_STRIDE + CMP_BLOCK - 1) & (n < NC - 1), 0.0, -MASK_BIG).astype(np.float32)
    i = np.arange(tk)[:, None]
    j = np.arange(M)[None, :] % tq
    tbl = np.stack([np.zeros((tk, M)), np.where(i <= j, 0.0, -MASK_BIG), np.where(i > j, 0.0, -MASK_BIG)])
    return (jnp.asarray(ov.T, BF16), jnp.asarray(ish, BF16), jnp.asarray(cbias), jnp.asarray(tbl, F32))


def _nsa_call(nq, kv, vt, kcx, vct, misct, q_norm_g, k_norm_g, B, S):
    T = B * S
    tq, tk, M = NSA_TQ, NSA_TK, NSA_M
    G = NSA_KV_HEADS
    nqt = S // tq
    NC = S // CMP_STRIDE
    NS = S // SLC_BLOCK
    ovt, ish, cbias, tbl = _nsa_consts(S)
    tile2 = lambda a: jnp.concatenate([a, a], axis=-1)
    row = lambda b, i: (b * nqt + i, 0)
    cst = lambda b, i: (0, 0)
    return pl.pallas_call(
        functools.partial(_nsa_kernel, S=S),
        grid=(B, nqt),
        in_specs=[pl.BlockSpec((tq, NSA_WIDTH), row),
                  pl.BlockSpec((S, KV_COLS), lambda b, i: (b, 0)),
                  pl.BlockSpec((None, S // tk, VT_ROWS, tk), lambda b, i: (b, 0, 0, 0)),
                  pl.BlockSpec((None, G, NC, LANES), lambda b, i: (b, 0, 0, 0)),
                  pl.BlockSpec((None, G, HEAD_DIM, NC), lambda b, i: (b, 0, 0, 0)),
                  pl.BlockSpec((LANES, tq), lambda b, i: (0, b * nqt + i)),
                  pl.BlockSpec((None, NC, tq), lambda b, i: (i, 0, 0)),
                  pl.BlockSpec((3, tk, M), lambda b, i: (0, 0, 0)),
                  pl.BlockSpec((1, LANES), cst),
                  pl.BlockSpec((3, LANES), cst),
                  pl.BlockSpec((NS, NC), cst),
                  pl.BlockSpec((NS, LANES), cst)],
        out_specs=pl.BlockSpec((NSA_WIDTH, tq), lambda b, i: (0, b * nqt + i)),
        out_shape=jax.ShapeDtypeStruct((NSA_WIDTH, T), F32),
        scratch_shapes=[pltpu.VMEM((G, S, LANES), BF16), pltpu.VMEM((G, S, LANES), BF16),
                        pltpu.SMEM((S // tk,), jnp.int32), pltpu.SMEM((1,), jnp.int32)],
        compiler_params=_cparams("parallel", "arbitrary"),
        name="nsa_attn",
    )(nq, kv, vt, kcx, vct, misct, cbias, tbl, tile2(q_norm_g), tile2(k_norm_g), ovt, ish)


def _out_proj_kernel(yg_ref, yn_ref, x_ref, g1_ref, sc_ref, sh_ref, ng_ref, wo_ref, rwt_ref, rb_ref,
                     x1_ref, h2_ref, gate_ref, gate_t_ref):
    mix = _dot(yg_ref[...], wo_ref[:GDN_WIDTH, :]) + _dot_tn(yn_ref[...], wo_ref[GDN_WIDTH:, :])
    x1 = x_ref[...] + g1_ref[...] * mix
    x1_ref[...] = x1
    h2 = _rms(x1, ng_ref[...]) * (1.0 + sc_ref[...]) + sh_ref[...]
    h2_ref[...] = h2.astype(BF16)

    tm = h2.shape[0]
    logits = _dot_nt_hi(rwt_ref[...], h2)
    score = _sigmoid(logits)
    biased = score + rb_ref[...]
    b = [biased[e:e + 1, :] for e in range(N_EXPERTS)]
    n_groups = N_EXPERTS // EXPERTS_PER_GROUP
    gscore = []
    for gi in range(n_groups):
        vals = b[gi * EXPERTS_PER_GROUP:(gi + 1) * EXPERTS_PER_GROUP]
        best = None
        for i in range(EXPERTS_PER_GROUP):
            for j in range(i + 1, EXPERTS_PER_GROUP):
                pair = vals[i] + vals[j]
                best = pair if best is None else jnp.maximum(best, pair)
        gscore.append(best)
    gbest = jnp.zeros((1, tm), jnp.int32)
    top = gscore[0]
    for gi in range(1, n_groups):
        better = gscore[gi] > top
        gbest = jnp.where(better, gi, gbest)
        top = jnp.where(better, gscore[gi], top)
    erow = lax.broadcasted_iota(jnp.int32, (N_EXPERTS, tm), 0)
    gates = jnp.zeros((N_EXPERTS, tm), F32)
    for e in range(N_EXPERTS):
        gi = e // EXPERTS_PER_GROUP
        rank = jnp.zeros((1, tm), F32)
        for e2 in range(gi * EXPERTS_PER_GROUP, (gi + 1) * EXPERTS_PER_GROUP):
            if e2 == e:
                continue
            beats = (b[e2] > b[e]) | ((b[e2] == b[e]) & (e2 < e))
            rank = rank + jnp.where(beats, 1.0, 0.0)
        chosen = (gbest == gi) & (rank < 2.0)
        ge = jnp.where(chosen, score[e:e + 1, :], 0.0)
        gates = gates + jnp.where(erow == e, ge, 0.0)
    gates = gates / jnp.sum(gates, axis=0, keepdims=True)
    pad_rows = jnp.zeros((8 - EXPERTS_PER_GROUP, tm), F32)
    for gi in range(n_groups):
        gate_t_ref[gi] = jnp.concatenate(
            [gates[gi * EXPERTS_PER_GROUP:(gi + 1) * EXPERTS_PER_GROUP, :], pad_rows], axis=0)
    ident = (lax.broadcasted_iota(jnp.int32, (N_EXPERTS, LANES), 0)
             == lax.broadcasted_iota(jnp.int32, (N_EXPERTS, LANES), 1))
    ident = jnp.where(ident, 1.0, 0.0).astype(BF16)
    hi, mid, lo = _split3(gates)
    dn = (((0,), (0,)), ((), ()))
    gate_ref[...] = (lax.dot_general(hi, ident, dn, preferred_element_type=F32)
                     + lax.dot_general(mid, ident, dn, preferred_element_type=F32)
                     + lax.dot_general(lo, ident, dn, preferred_element_type=F32))


def _out_proj_call(yg, yn, x2, g1, sc2, sh2, ng, wo, rwt, rb, S):
    T, D = x2.shape
    tm = 512
    per_b = S // tm
    row = lambda i: (i, 0)
    bat = lambda i: (i // per_b, 0, 0)
    cst = lambda i: (0, 0)
    return pl.pallas_call(
        _out_proj_kernel,
        grid=(T // tm,),
        in_specs=[pl.BlockSpec((tm, GDN_WIDTH), row), pl.BlockSpec((NSA_WIDTH, tm), lambda i: (0, i)),
                  pl.BlockSpec((tm, D), row),
                  pl.BlockSpec((None, 1, D), bat), pl.BlockSpec((None, 1, D), bat), pl.BlockSpec((None, 1, D), bat),
                  pl.BlockSpec((1, D), cst), pl.BlockSpec((D, D), cst),
                  pl.BlockSpec((N_EXPERTS, D), cst), pl.BlockSpec((N_EXPERTS, 1), cst)],
        out_specs=[pl.BlockSpec((tm, D), row), pl.BlockSpec((tm, D), row), pl.BlockSpec((tm, LANES), row),
                   pl.BlockSpec((N_EXPERTS // EXPERTS_PER_GROUP, 8, tm), lambda i: (0, 0, i))],
        out_shape=[jax.ShapeDtypeStruct((T, D), F32), jax.ShapeDtypeStruct((T, D), BF16),
                   jax.ShapeDtypeStruct((T, LANES), F32),
                   jax.ShapeDtypeStruct((N_EXPERTS // EXPERTS_PER_GROUP, 8, T), F32)],
        compiler_params=_cparams("parallel"),
        name="out_proj_router",
    )(yg, yn, x2, g1, sc2, sh2, ng, wo, rwt, rb)


MOE_TM = 1024
MOE_RB = 288


def _moe_kernel(h_ref, gates_ref, gt8_ref, tri_ref, trit_ref, x1_ref, g2_ref, wg_ref, wu_ref, wd_ref, o_ref):
    g = pl.program_id(1)
    tm, rb = MOE_TM, MOE_RB
    nlb = tm // LANES

    @pl.when(g == 0)
    def _():
        o_ref[...] = jnp.zeros_like(o_ref)

    gt8 = gt8_ref[...]
    in_row = jnp.sum(gt8, axis=0, keepdims=True) > 0.0
    first = lax.broadcasted_iota(jnp.int32, (8, LANES), 0) == 0
    off = jnp.zeros((1, 1), F32)
    keys = []
    for b in range(nlb):
        blk = in_row[:, b * LANES:(b + 1) * LANES]
        one = jnp.where(blk & first, 1.0, 0.0).astype(BF16)
        within = jnp.dot(one, tri_ref[...], preferred_element_type=F32)[0:1, :]
        keys.append(jnp.where(blk, within + off, -1.0))
        off = off + jnp.sum(jnp.where(blk, 1.0, 0.0), axis=1, keepdims=True)
    key_row = jnp.concatenate(keys, axis=1)
    n_g = jnp.sum(off).astype(jnp.int32)
    lane = lax.broadcasted_iota(jnp.int32, (tm, LANES), 1)
    mine = (lane >= g * EXPERTS_PER_GROUP) & (lane < (g + 1) * EXPERTS_PER_GROUP)
    in_col = jnp.sum(jnp.where(mine, gates_ref[...], 0.0), axis=1, keepdims=True) > 0.0
    off = jnp.zeros((1, 1), F32)
    keys = []
    for b in range(nlb):
        blk = in_col[b * LANES:(b + 1) * LANES, :]
        one = jnp.broadcast_to(jnp.where(blk, 1.0, 0.0), (LANES, LANES)).astype(BF16)
        within = jnp.dot(trit_ref[...], one, preferred_element_type=F32)[:, 0:1]
        keys.append(jnp.where(blk, within + off, -1.0))
        off = off + jnp.sum(jnp.where(blk, 1.0, 0.0), axis=0, keepdims=True)
    key_col = jnp.concatenate(keys, axis=0)

    gt_hi, gt_mid, gt_lo = _split3(gt8)
    nt_dims = (((1,), (1,)), ((), ()))

    def block(j, carry):
        r0 = (j * rb).astype(F32)
        sel = jnp.where(key_row - r0 == lax.broadcasted_iota(jnp.int32, (rb, tm), 0).astype(F32), 1.0, 0.0)
        sel = sel.astype(BF16)
        xb = jnp.dot(sel, h_ref[...], preferred_element_type=F32).astype(BF16)
        gb = (lax.dot_general(sel, gt_hi, nt_dims, preferred_element_type=F32)
              + lax.dot_general(sel, gt_mid, nt_dims, preferred_element_type=F32)
              + lax.dot_general(sel, gt_lo, nt_dims, preferred_element_type=F32))
        y = jnp.zeros((rb, o_ref.shape[1]), F32)
        for k in range(EXPERTS_PER_GROUP):
            act = (_silu(jnp.dot(xb, wg_ref[k], preferred_element_type=F32))
                   * jnp.dot(xb, wu_ref[k], preferred_element_type=F32))
            y = y + _dot(act * gb[:, k:k + 1], wd_ref[k])
        sel_t = jnp.where(key_col - r0 == lax.broadcasted_iota(jnp.int32, (tm, rb), 1).astype(F32), 1.0, 0.0)
        o_ref[...] += jnp.dot(sel_t.astype(BF16), y.astype(BF16), preferred_element_type=F32)
        return carry

    lax.fori_loop(0, (n_g + rb - 1) // rb, block, 0)

    @pl.when(g == N_EXPERTS // EXPERTS_PER_GROUP - 1)
    def _():
        o_ref[...] = x1_ref[...] + g2_ref[...] * o_ref[...]


def _moe_call(h2, gates, gates_t, x1, g2, wg, wu, wd, S):
    T, D = x1.shape
    tm = MOE_TM
    per_b = S // tm
    n_groups = N_EXPERTS // EXPERTS_PER_GROUP
    i = np.arange(LANES)
    tri = jnp.asarray(i[:, None] < i[None, :], BF16)
    row = lambda i, g: (i, 0)
    cst = lambda i, g: (0, 0)
    wsel = lambda i, g: (g, 0, 0)
    return pl.pallas_call(
        _moe_kernel,
        grid=(T // tm, n_groups),
        in_specs=[pl.BlockSpec((tm, D), row), pl.BlockSpec((tm, LANES), row),
                  pl.BlockSpec((None, 8, tm), lambda i, g: (g, 0, i)),
                  pl.BlockSpec((LANES, LANES), cst), pl.BlockSpec((LANES, LANES), cst),
                  pl.BlockSpec((tm, D), row),
                  pl.BlockSpec((None, 1, D), lambda i, g: (i // per_b, 0, 0)),
                  pl.BlockSpec((EXPERTS_PER_GROUP, D, EXPERT_FF), wsel),
                  pl.BlockSpec((EXPERTS_PER_GROUP, D, EXPERT_FF), wsel),
                  pl.BlockSpec((EXPERTS_PER_GROUP, EXPERT_FF, D), wsel)],
        out_specs=pl.BlockSpec((tm, D), row),
        out_shape=jax.ShapeDtypeStruct((T, D), F32),
        compiler_params=_cparams("parallel", "arbitrary"),
        name="moe",
    )(h2, gates, gates_t, tri, tri.T, x1, g2, wg, wu, wd)


def kernel(x, c, ada_w, ada_b, norm1_g, norm2_g, w_in, gdn_conv_w, gdn_a_log, gdn_dt_bias, gdn_norm_g,
           nsa_q_norm_g, nsa_k_norm_g, cmp_pe, cmp_w1, cmp_b1, cmp_w2, cmp_b2, w_out, router_w, router_bias,
           exp_w_gate, exp_w_up, exp_w_down):
    B, S, D = x.shape
    L = ada_w.shape[0]
    T = B * S
    G = NSA_KV_HEADS
    NC = S // CMP_STRIDE
    N = S // GDN_CHUNK
    P = GDN_HEADS // GDN_HPB
    W = GDN_HPB * HEAD_DIM

    mod = _ada_call(c, ada_w, ada_b)
    rwt = router_w.T
    rb = router_bias.reshape(N_EXPERTS, 1)
    x2 = x.reshape(T, D)
    for l in range(L):
        m6 = mod[l].reshape(B, 6, 1, D)
        sh1, sc1, g1, sh2, sc2, g2 = (m6[:, i] for i in range(6))
        qkv, z, nq, kv, misc, misct, vt = _in_proj_call(
            x2, sc1, sh1, norm1_g[l].reshape(1, D), *_pad_w_in(w_in[l]), S)

        a_rows = misc[:, :GDN_HEADS].reshape(B, N, GDN_CHUNK, P, GDN_HPB).transpose(0, 3, 1, 4, 2).reshape(B, P, N, W)
        rep = lambda t: jnp.repeat(t, HEAD_DIM).reshape(1, GDN_WIDTH)
        y_gdn = _gdn_call(qkv, z, misc, a_rows, gdn_conv_w[l], rep(gdn_a_log[l]), rep(gdn_dt_bias[l]),
                          jnp.tile(gdn_norm_g[l], GDN_HPB).reshape(1, W), B, S)

        w2 = cmp_w2[l]
        kcx, vct = _cmp_call(
            kv, cmp_pe[l].reshape(2, 1, CMP_BLOCK * HEAD_DIM), cmp_w1[l].astype(BF16),
            cmp_b1[l].reshape(2, 1, CMP_HIDDEN), jnp.pad(w2[0], ((0, 0), (0, LANES - HEAD_DIM))).astype(BF16),
            w2[1].T.astype(BF16), jnp.pad(cmp_b2[l, 0], (0, LANES - HEAD_DIM)).reshape(1, LANES),
            cmp_b2[l, 1].reshape(HEAD_DIM, 1), jnp.pad(nsa_k_norm_g[l, 0], (0, LANES - HEAD_DIM)).reshape(1, LANES),
            B, S)
        y_nsa_t = _nsa_call(nq, kv, vt.reshape(B, S // NSA_TK, VT_ROWS, NSA_TK), kcx, vct, misct,
                            nsa_q_norm_g[l].reshape(1, HEAD_DIM), nsa_k_norm_g[l], B, S)

        x1, h2, gates, gates_t = _out_proj_call(y_gdn, y_nsa_t, x2, g1, sc2, sh2, norm2_g[l].reshape(1, D),
                                       w_out[l].astype(BF16), rwt, rb, S)
        x2 = _moe_call(h2, gates, gates_t, x1, g2, exp_w_gate[l].astype(BF16), exp_w_up[l].astype(BF16),
                       exp_w_down[l].astype(BF16), S)
    return x2.reshape(B, S, D)
```

```python
import functools

import numpy as np
import jax
import jax.numpy as jnp
from jax import lax
from jax.experimental import pallas as pl
from jax.experimental.pallas import tpu as pltpu

F32 = jnp.float32
BF16 = jnp.bfloat16

HEAD_DIM = 64
GDN_HEADS = 8
GDN_WIDTH = GDN_HEADS * HEAD_DIM
GDN_CONV = 4
GDN_CHUNK = 64
NSA_HEADS = 8
NSA_KV_HEADS = 2
NSA_GROUP = NSA_HEADS // NSA_KV_HEADS
NSA_WIDTH = NSA_HEADS * HEAD_DIM
NSA_KV_WIDTH = NSA_KV_HEADS * HEAD_DIM
CMP_BLOCK = 32
CMP_STRIDE = 16
CMP_HIDDEN = 256
SLC_BLOCK = 64
SLC_TOPN = 8
WINDOW = 512
N_EXPERTS = 16
EXPERTS_PER_GROUP = 4
EXPERT_FF = 512
RMS_EPS = 1e-6
NEG_BIG = -1e30
SEL_BIG = 1e9

LANES = 128
VMEM_LIMIT_BYTES = 56 * 1024 * 1024

C_QKV = 0
C_Z = 3 * GDN_WIDTH
C_NQ = C_Z + GDN_WIDTH
C_KV = C_NQ + NSA_WIDTH
KV_COLS = 4 * NSA_KV_WIDTH
C_MISC = C_KV + KV_COLS
IN_PAD_COLS = C_MISC + LANES
MISC_B0 = GDN_HEADS
MISC_GATE0 = 16
VT_SLAB = HEAD_DIM + 16
VT_ROWS = 2 * NSA_KV_HEADS * VT_SLAB


def _cparams(*sem):
    return pltpu.CompilerParams(dimension_semantics=sem, vmem_limit_bytes=VMEM_LIMIT_BYTES)


def _dot(a, b):
    return jnp.dot(a.astype(BF16), b.astype(BF16), preferred_element_type=F32)


def _dot_nt(a, b):
    return lax.dot_general(a.astype(BF16), b.astype(BF16), (((1,), (1,)), ((), ())),
                           preferred_element_type=F32)


def _dot_tn(a, b):
    return lax.dot_general(a.astype(BF16), b.astype(BF16), (((0,), (0,)), ((), ())),
                           preferred_element_type=F32)


def _split2(x):
    hi = x.astype(BF16)
    lo = (x - hi.astype(F32)).astype(BF16)
    return hi, lo


def _split3(x):
    hi = x.astype(BF16)
    r = x - hi.astype(F32)
    mid = r.astype(BF16)
    lo = (r - mid.astype(F32)).astype(BF16)
    return hi, mid, lo


def _dot_hi(a, b):
    ah, al = _split2(a)
    bh, bl = _split2(b)
    return (jnp.dot(ah, bh, preferred_element_type=F32) + jnp.dot(al, bh, preferred_element_type=F32)
            + jnp.dot(ah, bl, preferred_element_type=F32))


def _dot_nt_hi(a, b):
    ah, al = _split2(a)
    bh, bl = _split2(b)
    dn = (((1,), (1,)), ((), ()))
    return (lax.dot_general(ah, bh, dn, preferred_element_type=F32)
            + lax.dot_general(al, bh, dn, preferred_element_type=F32)
            + lax.dot_general(ah, bl, dn, preferred_element_type=F32))


def _dot_exact_rhs(a, b_bf16):
    hi, mid, lo = _split3(a)
    return (jnp.dot(hi, b_bf16, preferred_element_type=F32) + jnp.dot(mid, b_bf16, preferred_element_type=F32)
            + jnp.dot(lo, b_bf16, preferred_element_type=F32))


def _rms(x, g):
    return x * lax.rsqrt(jnp.mean(x * x, axis=-1, keepdims=True) + RMS_EPS) * g


def _sigmoid(x):
    return 0.5 * (jnp.tanh(0.5 * x) + 1.0)


def _silu(x):
    return x * _sigmoid(x)


def _ada_kernel(c_ref, w_ref, b_ref, o_ref):
    o_ref[...] = _dot_hi(_silu(c_ref[...]), w_ref[...]) + b_ref[...]


def _ada_call(c, ada_w, ada_b):
    L, D, N = ada_w.shape
    B = c.shape[0]
    tn = 1024
    return pl.pallas_call(
        _ada_kernel,
        grid=(L, N // tn),
        in_specs=[pl.BlockSpec((B, D), lambda l, j: (0, 0)),
                  pl.BlockSpec((None, D, tn), lambda l, j: (l, 0, j)),
                  pl.BlockSpec((None, 1, tn), lambda l, j: (l, 0, j))],
        out_specs=pl.BlockSpec((None, B, tn), lambda l, j: (l, 0, j)),
        out_shape=jax.ShapeDtypeStruct((L, B, N), F32),
        compiler_params=_cparams("parallel", "parallel"),
        name="ada_mod",
    )(c, ada_w, ada_b.reshape(L, 1, N))


IN_SEGS = ((C_QKV, 3 * GDN_WIDTH), (C_Z, GDN_WIDTH), (C_NQ, NSA_WIDTH), (C_KV, KV_COLS), (C_MISC, LANES))


def _in_proj_kernel(x_ref, sc_ref, sh_ref, g_ref, w_ref, wmt_ref, wvt_ref, *o_refs):
    h = _rms(x_ref[...], g_ref[...]) * (1.0 + sc_ref[...]) + sh_ref[...]
    hb = h.astype(BF16)
    for (c0, n), o_ref in zip(IN_SEGS, o_refs[:-2]):
        o_ref[...] = jnp.dot(hb, w_ref[:, c0:c0 + n], preferred_element_type=F32)
    nt_dims = (((1,), (1,)), ((), ()))
    o_refs[-2][...] = lax.dot_general(wmt_ref[...], hb, nt_dims, preferred_element_type=F32)
    vt = lax.dot_general(wvt_ref[...], hb, nt_dims, preferred_element_type=F32).astype(BF16)
    ones = jnp.ones((VT_SLAB - HEAD_DIM, LANES), BF16)
    for j in range(o_refs[-1].shape[0]):
        for st in range(2 * NSA_KV_HEADS):
            o_refs[-1][j, st * VT_SLAB:st * VT_SLAB + HEAD_DIM, :] = vt[st * HEAD_DIM:(st + 1) * HEAD_DIM,
                                                                         j * LANES:(j + 1) * LANES]
            o_refs[-1][j, st * VT_SLAB + HEAD_DIM:(st + 1) * VT_SLAB, :] = ones


def _in_proj_call(x2, sc, sh, g, w_pad, wvt, S):
    T, D = x2.shape
    tm = 256
    per_b = S // tm
    row = lambda i: (i, 0)
    bat = lambda i: (i // per_b, 0, 0)
    return pl.pallas_call(
        _in_proj_kernel,
        grid=(T // tm,),
        in_specs=[pl.BlockSpec((tm, D), row),
                  pl.BlockSpec((None, 1, D), bat),
                  pl.BlockSpec((None, 1, D), bat),
                  pl.BlockSpec((1, D), lambda i: (0, 0)),
                  pl.BlockSpec((D, IN_PAD_COLS), lambda i: (0, 0)),
                  pl.BlockSpec((LANES, D), lambda i: (0, 0)),
                  pl.BlockSpec((2 * NSA_KV_WIDTH, D), lambda i: (0, 0))],
        out_specs=([pl.BlockSpec((tm, n), row) for _, n in IN_SEGS]
                   + [pl.BlockSpec((LANES, tm), lambda i: (0, i)),
                      pl.BlockSpec((tm // LANES, VT_ROWS, LANES), lambda i: (i, 0, 0))]),
        out_shape=([jax.ShapeDtypeStruct((T, n), F32) for _, n in IN_SEGS]
                   + [jax.ShapeDtypeStruct((LANES, T), F32),
                      jax.ShapeDtypeStruct((T // LANES, VT_ROWS, LANES), BF16)]),
        compiler_params=_cparams("parallel"),
        name="in_proj",
    )(x2, sc, sh, g, w_pad, w_pad[:, C_MISC:].T, wvt)


def _pad_w_in(w_in):
    D = w_in.shape[0]
    W = GDN_WIDTH
    o = 0
    gq, gk, gv, gz = (w_in[:, o + i * W:o + (i + 1) * W] for i in range(4))
    o += 4 * W
    ga = w_in[:, o:o + GDN_HEADS]
    gb = w_in[:, o + GDN_HEADS:o + 2 * GDN_HEADS]
    o += 2 * GDN_HEADS
    nq = w_in[:, o:o + NSA_WIDTH]
    o += NSA_WIDTH
    kc, vc, ks, vs, kw, vw = (w_in[:, o + i * NSA_KV_WIDTH:o + (i + 1) * NSA_KV_WIDTH] for i in range(6))
    o += 6 * NSA_KV_WIDTH
    gates = w_in[:, o:o + 3 * NSA_HEADS]
    misc = jnp.zeros((D, LANES), w_in.dtype)
    misc = misc.at[:, 0:GDN_HEADS].set(ga).at[:, MISC_B0:MISC_B0 + GDN_HEADS].set(gb)
    misc = misc.at[:, MISC_GATE0:MISC_GATE0 + 3 * NSA_HEADS].set(gates)
    cols = [gq, gk, gv, gz, nq, kc, vc, ks, kw, misc]
    return jnp.concatenate(cols, axis=1).astype(BF16), jnp.concatenate([vs, vw], axis=1).T.astype(BF16)


GDN_HPB = 2
GDN_NB = 32


def _gdn_kernel(q_ref, k_ref, v_ref, z_ref, misc_ref, ar_ref, cwq_ref, cwk_ref, cwv_ref,
                alog_ref, dtb_ref, alogr_ref, dtbr_ref, ng_ref, o_ref,
                qs, ks, kbs, xs, gcs, grs, ps, qq, qks, os_, rs, q2s, sev, *, S):
    C = GDN_CHUNK
    N = S // C
    W = GDN_HPB * HEAD_DIM
    row = lax.broadcasted_iota(jnp.int32, (S, W), 0)
    row8 = lax.broadcasted_iota(jnp.int32, (8, W), 0)

    def conv_silu(x_ref, w):
        x = x_ref[...]
        y = x * w[GDN_CONV - 1:GDN_CONV, :]
        for s in range(1, GDN_CONV):
            head = jnp.where(row8 >= s, pltpu.roll(x[:8], s, axis=0), 0.0)
            xs = jnp.concatenate([head, x_ref[pl.ds(8 - s, S - 8), :]], axis=0)
            y = y + xs * w[GDN_CONV - 1 - s:GDN_CONV - s, :]
        return _silu(y)

    bd = jnp.where(lax.broadcasted_iota(jnp.int32, (W, W), 0) // HEAD_DIM
                   == lax.broadcasted_iota(jnp.int32, (W, W), 1) // HEAD_DIM, 1.0, 0.0).astype(BF16)

    def head_sumsq(x):
        hi, lo = _split2(x * x)
        return jnp.dot(hi, bd, preferred_element_type=F32) + jnp.dot(lo, bd, preferred_element_type=F32)

    def head_l2(x):
        return x * lax.rsqrt(head_sumsq(x) + RMS_EPS)

    q = head_l2(conv_silu(q_ref, cwq_ref[...])) * (HEAD_DIM ** -0.5)
    k = head_l2(conv_silu(k_ref, cwk_ref[...]))
    v = conv_silu(v_ref, cwv_ref[...])

    hd0 = pl.program_id(1) * GDN_HPB
    src = lax.broadcasted_iota(jnp.int32, (LANES, W), 0)
    dst = lax.broadcasted_iota(jnp.int32, (LANES, W), 1) // HEAD_DIM + hd0
    a_exp = _dot_exact_rhs(misc_ref[...], jnp.where(src == dst, 1.0, 0.0).astype(BF16))
    b_exp = _dot_exact_rhs(misc_ref[...], jnp.where(src == dst + MISC_B0, 1.0, 0.0).astype(BF16))
    beta = _sigmoid(b_exp)

    def log_decay(a, alog, dtb):
        xx = a + dtb
        sp = jnp.maximum(xx, 0.0) + jnp.log(1.0 + jnp.exp(-jnp.abs(xx)))
        return -jnp.exp(alog) * sp

    gc = log_decay(a_exp, alog_ref[...], dtb_ref[...])
    pos = row % C
    for s in (1, 2, 4, 8, 16, 32):
        gc = gc + jnp.where(pos >= s, pltpu.roll(gc, s, axis=0), 0.0)
    gr = log_decay(ar_ref[...], alogr_ref[...], dtbr_ref[...])
    lpos = lax.broadcasted_iota(jnp.int32, (N, W), 1) % C
    for s in (1, 2, 4, 8, 16, 32):
        gr = gr + jnp.where(lpos >= s, pltpu.roll(gr, s, axis=1), 0.0)

    kb = k * beta
    vb = v * beta
    kbe = kb * jnp.exp(gc)
    for h in range(GDN_HPB):
        sl = slice(h * HEAD_DIM, (h + 1) * HEAD_DIM)
        qs[h] = q[:, sl]
        ks[h] = k[:, sl]
        kbs[h] = kb[:, sl]
        gcs[h] = gc[:, sl]
        for n in range(N):
            grs[h, n] = jnp.broadcast_to(gr[n:n + 1, sl], (8, HEAD_DIM))
        xs[h] = jnp.concatenate([kbe[:, sl], vb[:, sl]], axis=1)

    NB = GDN_NB
    ci3 = lax.broadcasted_iota(jnp.int32, (1, C, C), 1)
    cj3 = lax.broadcasted_iota(jnp.int32, (1, C, C), 2)
    eye3 = jnp.where(ci3 == cj3, 1.0, 0.0)

    def bmm(a, b):
        return lax.dot_general(a.astype(BF16), b.astype(BF16), (((2,), (1,)), ((0,), (0,))),
                               preferred_element_type=F32)

    def bmm_nt(a, b):
        return lax.dot_general(a.astype(BF16), b.astype(BF16), (((2,), (2,)), ((0,), (0,))),
                               preferred_element_type=F32)

    def intra(nb, carry):
        rows = pl.ds(pl.multiple_of(nb * (NB * C), NB * C), NB * C)
        for h in range(GDN_HPB):
            kk = ks[h, rows, :].reshape(NB, C, HEAD_DIM)
            kbq = jnp.concatenate([kbs[h, rows, :].reshape(NB, C, HEAD_DIM),
                                   qs[h, rows, :].reshape(NB, C, HEAD_DIM)], axis=1)
            a2 = bmm_nt(kbq, kk)
            gcol = gcs[h, rows, :].reshape(NB, C, HEAD_DIM)
            grow = grs[h, pl.ds(nb * NB, NB), 0:1, :]
            decay = jnp.exp(jnp.where(ci3 >= cj3, gcol - grow, NEG_BIG))
            low = jnp.where(ci3 > cj3, a2[:, :C] * decay, 0.0)
            qks[h, rows, :] = (a2[:, C:] * decay).reshape(NB * C, C)
            tinv = eye3 - jnp.where(ci3 // 2 == cj3 // 2, low, 0.0)
            for lv in range(1, 6):
                s = 2 ** lv
                off = jnp.where((ci3 // (2 * s) == cj3 // (2 * s)) & (ci3 // s != cj3 // s), low, 0.0)
                tinv = tinv - bmm(tinv, bmm(off, tinv))
            x = bmm(tinv, xs[h, rows, :].reshape(NB, C, 2 * HEAD_DIM))
            xs[h, rows, :] = x.reshape(NB * C, 2 * HEAD_DIM)
            kd = kk * jnp.exp(gcol[:, C - 1:C, :] - gcol)
            pq = lax.dot_general(kd.astype(BF16), x.astype(BF16), (((1,), (1,)), ((0,), (0,))),
                                 preferred_element_type=F32)
            pn = pq[:, :, :HEAD_DIM]
            qn = pq[:, :, HEAD_DIM:]
            ps[h, rows, :] = pn.reshape(NB * C, HEAD_DIM)
            qq[h, rows, :] = qn.reshape(NB * C, HEAD_DIM)
            cd = jnp.exp(gcol[:, C - 1:C, :]).reshape(NB // 2, 2, 1, HEAD_DIM)
            pn = pn.reshape(NB // 2, 2, HEAD_DIM, HEAD_DIM)
            qn = qn.reshape(NB // 2, 2, HEAD_DIM, HEAD_DIM)
            p0, p1, q0, q1, c0, c1 = pn[:, 0], pn[:, 1], qn[:, 0], qn[:, 1], cd[:, 0], cd[:, 1]
            t = bmm(p1, pq.reshape(NB // 2, 2, HEAD_DIM, 2 * HEAD_DIM)[:, 0])
            prow = pl.ds(pl.multiple_of(nb * (NB // 2 * C), NB // 2 * C), NB // 2 * C)
            rs[h, prow, :] = (c1 * p0 + c0 * p1 - t[:, :, :HEAD_DIM]).reshape(NB // 2 * C, HEAD_DIM)
            q2s[h, prow, :] = (c1 * q0 - t[:, :, HEAD_DIM:] + q1).reshape(NB // 2 * C, HEAD_DIM)
        return carry

    lax.fori_loop(0, N // NB, intra, 0)

    def inter(k, states):
        sl = pl.ds(pl.multiple_of(k * C, C), C)
        new_states = []
        for h in range(GDN_HPB):
            st = states[h]
            sev[h, k] = st
            gsum = gcs[h, pl.ds(2 * k * C + C - 1, 1), :] + gcs[h, pl.ds(2 * k * C + 2 * C - 1, 1), :]
            new_states.append(st * jnp.exp(gsum) - _dot_hi(rs[h, sl, :], st) + q2s[h, sl, :])
        return tuple(new_states)

    lax.fori_loop(0, N // 2, inter, tuple(jnp.zeros((HEAD_DIM, HEAD_DIM), F32) for _ in range(GDN_HPB)))

    def outputs(nb, carry):
        rows = pl.ds(pl.multiple_of(nb * (NB * C), NB * C), NB * C)
        for h in range(GDN_HPB):
            gcol = gcs[h, rows, :].reshape(NB, C, HEAD_DIM)
            s0 = sev[h, pl.ds(nb * (NB // 2), NB // 2)]
            p0 = ps[h, rows, :].reshape(NB // 2, 2, HEAD_DIM, HEAD_DIM)[:, 0]
            q0 = qq[h, rows, :].reshape(NB // 2, 2, HEAD_DIM, HEAD_DIM)[:, 0]
            c0 = jnp.exp(gcol[:, C - 1:C, :]).reshape(NB // 2, 2, 1, HEAD_DIM)[:, 0]
            s1 = c0 * s0 - bmm(p0, s0) + q0
            st = jnp.stack([s0, s1], axis=1).reshape(NB, HEAD_DIM, HEAD_DIM)
            x = xs[h, rows, :].reshape(NB, C, 2 * HEAD_DIM)
            qd = qs[h, rows, :].reshape(NB, C, HEAD_DIM) * jnp.exp(gcol)
            t1 = bmm(jnp.concatenate([x[:, :, :HEAD_DIM], qd], axis=1), st)
            vn = x[:, :, HEAD_DIM:] - t1[:, :C]
            o = t1[:, C:] + bmm(qks[h, rows, :].reshape(NB, C, C), vn)
            os_[h, rows, :] = o.reshape(NB * C, HEAD_DIM)
        return carry

    lax.fori_loop(0, N // NB, outputs, 0)

    o = jnp.concatenate([os_[h] for h in range(GDN_HPB)], axis=1)
    o = o * lax.rsqrt(head_sumsq(o) * (1.0 / HEAD_DIM) + RMS_EPS) * ng_ref[...]
    o_ref[...] = o * _silu(z_ref[...])


def _gdn_call(qkv, z, misc, a_rows, conv_w, alog_exp, dtb_exp, norm_g, B, S):
    T = B * S
    W = GDN_HPB * HEAD_DIM
    P = GDN_HEADS // GDN_HPB
    N = S // GDN_CHUNK
    col = lambda off: (lambda b, p: (b, off + p))
    par = lambda off: (lambda b, p: (0, off + p))
    hs = lambda n: pltpu.VMEM((GDN_HPB, S, n), F32)
    return pl.pallas_call(
        functools.partial(_gdn_kernel, S=S),
        grid=(B, P),
        in_specs=[pl.BlockSpec((S, W), col(0)), pl.BlockSpec((S, W), col(P)), pl.BlockSpec((S, W), col(2 * P)),
                  pl.BlockSpec((S, W), col(0)), pl.BlockSpec((S, LANES), lambda b, p: (b, 0)),
                  pl.BlockSpec((None, None, N, W), lambda b, p: (b, p, 0, 0)),
                  pl.BlockSpec((GDN_CONV, W), par(0)), pl.BlockSpec((GDN_CONV, W), par(P)),
                  pl.BlockSpec((GDN_CONV, W), par(2 * P)),
                  pl.BlockSpec((1, W), par(0)), pl.BlockSpec((1, W), par(0)),
                  pl.BlockSpec((1, W), par(0)), pl.BlockSpec((1, W), par(0)),
                  pl.BlockSpec((1, W), lambda b, p: (0, 0))],
        out_specs=pl.BlockSpec((S, W), col(0)),
        out_shape=jax.ShapeDtypeStruct((T, GDN_WIDTH), F32),
        scratch_shapes=[hs(HEAD_DIM), hs(HEAD_DIM), hs(HEAD_DIM), hs(2 * HEAD_DIM), hs(HEAD_DIM),
                        pltpu.VMEM((GDN_HPB, N, 8, HEAD_DIM), F32),
                        hs(HEAD_DIM), hs(HEAD_DIM), hs(HEAD_DIM), hs(HEAD_DIM),
                        pltpu.VMEM((GDN_HPB, S // 2, HEAD_DIM), F32), pltpu.VMEM((GDN_HPB, S // 2, HEAD_DIM), F32),
                        pltpu.VMEM((GDN_HPB, N // 2, HEAD_DIM, HEAD_DIM), F32)],
        compiler_params=_cparams("parallel", "parallel"),
        name="gdn",
    )(qkv, qkv, qkv, z, misc, a_rows, conv_w, conv_w, conv_w,
      alog_exp, dtb_exp, alog_exp, dtb_exp, norm_g)


def _cmp_kernel(kc_in_ref, vc_in_ref, pe_ref, w1_ref, b1_ref, w2k_ref, w2vt_ref, b2k_ref, b2v_ref, kg_ref, kc_ref, vct_ref):
    half = (CMP_BLOCK // 2) * HEAD_DIM
    nrows = kc_in_ref.shape[0] // CMP_STRIDE
    lane = lax.broadcasted_iota(jnp.int32, (nrows, LANES), 1)
    cend = lax.broadcasted_iota(jnp.int32, (nrows, LANES), 0) * CMP_STRIDE + (CMP_BLOCK - 1)
    ext = jnp.where(lane == HEAD_DIM, cend // LANES, jnp.where(lane == HEAD_DIM + 1, cend % LANES, 0)).astype(F32)

    def token_groups(x_ref):
        cols = [[] for _ in range(NSA_KV_HEADS)]
        for m in range(CMP_STRIDE // 2):
            ev = x_ref[pl.ds(2 * m, nrows, stride=CMP_STRIDE), :]
            od = x_ref[pl.ds(2 * m + 1, nrows, stride=CMP_STRIDE), :]
            cols[0].append(jnp.where(lane < HEAD_DIM, ev, pltpu.roll(od, HEAD_DIM, axis=1)))
            cols[1].append(jnp.where(lane < HEAD_DIM, pltpu.roll(ev, HEAD_DIM, axis=1), od))
        return [jnp.concatenate(c, axis=1) for c in cols]

    def hidden(x, j):
        pe = pe_ref[j]
        p = _dot(x + pe[:, :half], w1_ref[j, :half, :])
        q = _dot(x + pe[:, half:], w1_ref[j, half:, :])
        return _silu(p + pltpu.roll(q, nrows - 1, axis=0) + b1_ref[j])

    for g, x in enumerate(token_groups(kc_in_ref)):
        kc = _dot(hidden(x, 0), w2k_ref[...]) + b2k_ref[...]
        ms = jnp.sum(kc * kc, axis=-1, keepdims=True) * (1.0 / HEAD_DIM)
        kc_ref[g] = jnp.where(lane < HEAD_DIM, kc * lax.rsqrt(ms + RMS_EPS) * kg_ref[...], ext)
    for g, x in enumerate(token_groups(vc_in_ref)):
        vct_ref[g] = _dot_nt(w2vt_ref[...], hidden(x, 1)) + b2v_ref[...]


def _cmp_call(kv, pe, w1, b1, w2k, w2vt, b2k, b2v, kg, B, S):
    G = NSA_KV_HEADS
    NC = S // CMP_STRIDE
    full = lambda a: pl.BlockSpec(a.shape, lambda b: (0,) * a.ndim)
    return pl.pallas_call(
        _cmp_kernel,
        grid=(B,),
        in_specs=[pl.BlockSpec((S, NSA_KV_WIDTH), lambda b: (b, 0)), pl.BlockSpec((S, NSA_KV_WIDTH), lambda b: (b, 1))]
        + [full(a) for a in (pe, w1, b1, w2k, w2vt, b2k, b2v, kg)],
        out_specs=[pl.BlockSpec((None, G, NC, LANES), lambda b: (b, 0, 0, 0)),
                   pl.BlockSpec((None, G, HEAD_DIM, NC), lambda b: (b, 0, 0, 0))],
        out_shape=[jax.ShapeDtypeStruct((B, G, NC, LANES), F32), jax.ShapeDtypeStruct((B, G, HEAD_DIM, NC), F32)],
        compiler_params=_cparams("parallel"),
        name="nsa_compress",
    )(kv, kv, pe, w1, b1, w2k, w2vt, b2k, b2v, kg)


NSA_TQ = 128
NSA_TK = 128
NSA_M = NSA_GROUP * NSA_TQ
NSA_MS = NSA_M
MASK_BIG = 2.0 ** 100
SEL_LANE0 = HEAD_DIM + 2


def _nsa_kernel(q_ref, kv_ref, vt_ref, kcx_ref, vct_ref, misct_ref, cbias_ref, tbl_ref, qg_ref, kg_ref,
                ovt_ref, ish_ref, o_ref, ks_s, kw_s, live, n_live, *, S):
    tq, tk, M = NSA_TQ, NSA_TK, NSA_M
    MS = NSA_MS
    G, R = NSA_KV_HEADS, NSA_GROUP
    NS = S // SLC_BLOCK
    n_top = min(SLC_TOPN, NS)
    qi = pl.program_id(1)
    t0 = qi * tq
    nt_dims = (((1,), (1,)), ((), ()))

    @pl.when(qi == 0)
    def _():
        rowi = lax.broadcasted_iota(jnp.int32, (S, LANES), 0)
        lane = lax.broadcasted_iota(jnp.int32, (S, LANES), 1)
        ext = jnp.where(lane == HEAD_DIM, rowi // LANES, jnp.where(lane == HEAD_DIM + 1, rowi % LANES, 0)).astype(F32)
        ext_slc = jnp.where(lane - SEL_LANE0 == rowi // SLC_BLOCK, -MASK_BIG, ext)

        def prep(c0, g, gain, extra):
            kraw = kv_ref[:, c0:c0 + LANES]
            if g == 1:
                kraw = pltpu.roll(kraw, HEAD_DIM, axis=1)
            ms = jnp.sum(jnp.where(lane < HEAD_DIM, kraw * kraw, 0.0), axis=-1, keepdims=True) * (1.0 / HEAD_DIM)
            return jnp.where(lane < HEAD_DIM, kraw * lax.rsqrt(ms + RMS_EPS) * gain, extra).astype(BF16)

        for g in range(G):
            ks_s[g] = prep(2 * NSA_KV_WIDTH, g, kg_ref[1:2, :], ext_slc)
            kw_s[g] = prep(3 * NSA_KV_WIDTH, g, kg_ref[2:3, :], ext)

    lane_q = lax.broadcasted_iota(jnp.int32, (tq, LANES), 1)
    gate_sig = _sigmoid(misct_ref[MISC_GATE0:MISC_GATE0 + 3 * NSA_HEADS, :])
    kt_d = t0 // tk

    def score_tiles(streams, kt, biased):
        rows = pl.ds(pl.multiple_of(kt * tk, tk), tk)
        out = []
        for qpp, k_s, g, _, idx_fn in streams:
            s = lax.dot_general(k_s[g, rows, :], qpp, nt_dims, preferred_element_type=F32)
            out.append(s + tbl_ref[idx_fn(kt)] if biased else s)
        return tuple(out)

    def softmax_pv(streams, kt, scores, states):
        soft = []
        for s, (m, _) in zip(scores, states):
            m_new = jnp.maximum(m, jnp.max(s, axis=0, keepdims=True))
            soft.append((m_new, jnp.exp(m - m_new), jnp.exp(s - m_new).astype(BF16)))
        pvs = [jnp.dot(vt_ref[kt, st[3]:st[3] + VT_SLAB, :], sf[2], preferred_element_type=F32)
               for st, sf in zip(streams, soft)]
        return tuple((m_new, alpha * acc + pv) for (m_new, alpha, _), (_, acc), pv in zip(soft, states, pvs))

    flash_init = (jnp.full((1, M), NEG_BIG, F32), jnp.zeros((VT_SLAB, M), F32))

    qpps, o_cmps, unsel = [], [], []
    for g in range(G):
        qrows = []
        for r in range(R):
            hh = g * R + r
            qpair = q_ref[:, (hh // 2) * LANES:(hh // 2 + 1) * LANES]
            if hh % 2 == 1:
                qpair = pltpu.roll(qpair, HEAD_DIM, axis=1)
            ms = jnp.sum(jnp.where(lane_q < HEAD_DIM, qpair * qpair, 0.0), axis=-1, keepdims=True) * (1.0 / HEAD_DIM)
            qn = qpair * lax.rsqrt(ms + RMS_EPS) * qg_ref[...] * (HEAD_DIM ** -0.5)
            slope = 2.0 ** (-8.0 * (hh + 1) / NSA_HEADS)
            ex = jnp.where(lane_q == HEAD_DIM, slope * LANES, jnp.where(lane_q == HEAD_DIM + 1, slope, 0.0))
            qrows.append(jnp.where(lane_q < HEAD_DIM, qn, ex))
        qbase = jnp.concatenate(qrows, axis=0)

        cb = cbias_ref[...]
        s = (lax.dot_general(kcx_ref[g].astype(BF16), qbase.astype(BF16), nt_dims, preferred_element_type=F32)
             + jnp.concatenate([cb] * R, axis=1))
        mx = jnp.maximum(jnp.max(s, axis=0, keepdims=True), NEG_BIG)
        e = jnp.exp(s - mx)
        l = jnp.sum(e, axis=0, keepdims=True)
        p_cmp = e / jnp.where(l > 0.0, l, 1.0)
        o_cmps.append(_dot(vct_ref[g], p_cmp))

        psum = p_cmp[:, 0:tq]
        for r in range(1, R):
            psum = psum + p_cmp[:, r * tq:(r + 1) * tq]
        hi, mid, lo = _split3(psum)
        ovt = ovt_ref[...]
        imp = (jnp.dot(ovt, hi, preferred_element_type=F32) + jnp.dot(ovt, mid, preferred_element_type=F32)
               + jnp.dot(ovt, lo, preferred_element_type=F32))
        blk = lax.broadcasted_iota(jnp.int32, (NS, tq), 0)
        cur = (t0 + lax.broadcasted_iota(jnp.int32, (NS, tq), 1)) // SLC_BLOCK
        valid = blk <= cur
        forced = (blk == 0) | (blk == cur) | (blk == cur - 1)
        score = jnp.where(forced, SEL_BIG, jnp.where(valid, imp, -SEL_BIG))
        rank = jnp.zeros((NS, tq), F32)
        for j in range(NS):
            cj = score[j:j + 1, :]
            beats = (cj > score) | ((cj == score) & (blk > j))
            rank = rank + jnp.where(beats, 1.0, 0.0)
        nsel = jnp.where((rank < n_top) & valid, 0.0, 1.0)
        unsel.append(jnp.min(nsel, axis=1, keepdims=True))
        nsel = nsel.astype(BF16)
        nsel_q = lax.dot_general(nsel, ish_ref[...], (((0,), (0,)), ((), ())),
                                 preferred_element_type=F32)
        in_sel = (lane_q >= SEL_LANE0) & (lane_q < SEL_LANE0 + NS)
        qpps.append(jnp.concatenate([jnp.where(in_sel, nsel_q, qrows[r]) for r in range(R)], axis=0).astype(BF16))

    slc_idx = lambda kt: jnp.where(kt == kt_d, 1, 0)
    win_lo = kt_d - WINDOW // tk
    win_idx = lambda kt: jnp.where(kt == kt_d, 1, jnp.where(kt == win_lo, 2, 0))
    slc_streams = [(qpps[g], ks_s, g, g * VT_SLAB, slc_idx) for g in range(G)]
    win_streams = [(qpps[g], kw_s, g, (G + g) * VT_SLAB, win_idx) for g in range(G)]
    kt_w = jnp.maximum(win_lo, 0)

    blocks_per_tile = tk // SLC_BLOCK
    unsel_any = unsel[0]
    for g in range(1, G):
        unsel_any = jnp.minimum(unsel_any, unsel[g])
    blk_tile = lax.broadcasted_iota(jnp.int32, (NS, 1), 0) // blocks_per_tile
    bit = jnp.zeros((1, 1), F32)
    for kt in range(S // tk - WINDOW // tk):
        in_tile = jnp.min(jnp.where(blk_tile == kt, unsel_any, 1.0), axis=0, keepdims=True)
        bit = bit + jnp.where(in_tile < 0.5, float(2 ** kt), 0.0)
    live_bits = jnp.max(bit).astype(jnp.int32)
    n_live[0] = 0
    for kt in range(S // tk - WINDOW // tk):
        @pl.when((((live_bits >> kt) & 1) == 1) & (kt < kt_w))
        def _():
            live[n_live[0]] = kt
            n_live[0] = n_live[0] + 1

    n_slc = n_live[0]

    def slc_only(i, carry):
        scores, states = carry
        nxt = score_tiles(slc_streams, live[jnp.minimum(i + 1, n_slc - 1)], False)
        return nxt, softmax_pv(slc_streams, live[i], scores, states)

    first = jnp.where(n_slc > 0, live[0], 0)
    _, st_slc = lax.fori_loop(0, n_slc, slc_only, (score_tiles(slc_streams, first, False), (flash_init,) * G))
    both = slc_streams + win_streams
    st_all = lax.fori_loop(kt_w, kt_d + 1, lambda kt, st: softmax_pv(both, kt, score_tiles(both, kt, True), st),
                           st_slc + (flash_init,) * G)
    o_all = []
    for _, acc in st_all:
        den = acc[HEAD_DIM:HEAD_DIM + 1, :]
        o_all.append(acc[:HEAD_DIM] / jnp.where(den > 0.0, den, 1.0))
    o_slcs, o_wins = o_all[:G], o_all[G:]

    for g in range(G):
        for r in range(R):
            hh = g * R + r
            cs = slice(r * tq, (r + 1) * tq)
            o_ref[hh * HEAD_DIM:(hh + 1) * HEAD_DIM, :] = (
                gate_sig[hh:hh + 1, :] * o_cmps[g][:, cs]
                + gate_sig[NSA_HEADS + hh:NSA_HEADS + hh + 1, :] * o_slcs[g][:, cs]
                + gate_sig[2 * NSA_HEADS + hh:2 * NSA_HEADS + hh + 1, :] * o_wins[g][:, cs])


def _nsa_consts(S):
    tq, tk, M = NSA_TQ, NSA_TK, NSA_M
    NC = S // CMP_STRIDE
    NS = S // SLC_BLOCK
    cs = np.arange(NC)[:, None] * CMP_STRIDE
    ss = np.arange(NS)[None, :] * SLC_BLOCK
    ov = np.clip(np.minimum(cs + CMP_BLOCK, ss + SLC_BLOCK) - np.maximum(cs, ss), 0, None) / CMP_BLOCK
    ov[NC - 1:] = 0.0
    ish = np.zeros((NS, LANES), np.float32)
    ish[np.arange(NS), SEL_LANE0 + np.arange(NS)] = 1.0
    t = np.arange(S).reshape(S // tq, 1, tq)
    n = np.arange(NC).reshape(1, NC, 1)
    cbias = np.where((t >= n * CMP_STRIDE + CMP_BLOCK - 1) & (n < NC - 1), 0.0, -MASK_BIG).astype(np.float32)
    i = np.arange(tk)[:, None]
    j = np.arange(M)[None, :] % tq
    tbl = np.stack([np.zeros((tk, M)), np.where(i <= j, 0.0, -MASK_BIG), np.where(i > j, 0.0, -MASK_BIG)])
    return (jnp.asarray(ov.T, BF16), jnp.asarray(ish, BF16), jnp.asarray(cbias), jnp.asarray(tbl, F32))


def _nsa_call(nq, kv, vt, kcx, vct, misct, q_norm_g, k_norm_g, B, S):
    T = B * S
    tq, tk, M = NSA_TQ, NSA_TK, NSA_M
    G = NSA_KV_HEADS
    nqt = S // tq
    NC = S // CMP_STRIDE
    NS = S // SLC_BLOCK
    ovt, ish, cbias, tbl = _nsa_consts(S)
    tile2 = lambda a: jnp.concatenate([a, a], axis=-1)
    row = lambda b, i: (b * nqt + i, 0)
    cst = lambda b, i: (0, 0)
    return pl.pallas_call(
        functools.partial(_nsa_kernel, S=S),
        grid=(B, nqt),
        in_specs=[pl.BlockSpec((tq, NSA_WIDTH), row),
                  pl.BlockSpec((S, KV_COLS), lambda b, i: (b, 0)),
                  pl.BlockSpec((None, S // tk, VT_ROWS, tk), lambda b, i: (b, 0, 0, 0)),
                  pl.BlockSpec((None, G, NC, LANES), lambda b, i: (b, 0, 0, 0)),
                  pl.BlockSpec((None, G, HEAD_DIM, NC), lambda b, i: (b, 0, 0, 0)),
                  pl.BlockSpec((LANES, tq), lambda b, i: (0, b * nqt + i)),
                  pl.BlockSpec((None, NC, tq), lambda b, i: (i, 0, 0)),
                  pl.BlockSpec((3, tk, M), lambda b, i: (0, 0, 0)),
                  pl.BlockSpec((1, LANES), cst),
                  pl.BlockSpec((3, LANES), cst),
                  pl.BlockSpec((NS, NC), cst),
                  pl.BlockSpec((NS, LANES), cst)],
        out_specs=pl.BlockSpec((NSA_WIDTH, tq), lambda b, i: (0, b * nqt + i)),
        out_shape=jax.ShapeDtypeStruct((NSA_WIDTH, T), F32),
        scratch_shapes=[pltpu.VMEM((G, S, LANES), BF16), pltpu.VMEM((G, S, LANES), BF16),
                        pltpu.SMEM((S // tk,), jnp.int32), pltpu.SMEM((1,), jnp.int32)],
        compiler_params=_cparams("parallel", "arbitrary"),
        name="nsa_attn",
    )(nq, kv, vt, kcx, vct, misct, cbias, tbl, tile2(q_norm_g), tile2(k_norm_g), ovt, ish)


def _out_proj_kernel(yg_ref, yn_ref, x_ref, g1_ref, sc_ref, sh_ref, ng_ref, wo_ref, rwt_ref, rb_ref,
                     x1_ref, h2_ref, gate_ref, gate_t_ref):
    mix = _dot(yg_ref[...], wo_ref[:GDN_WIDTH, :]) + _dot_tn(yn_ref[...], wo_ref[GDN_WIDTH:, :])
    x1 = x_ref[...] + g1_ref[...] * mix
    x1_ref[...] = x1
    h2 = _rms(x1, ng_ref[...]) * (1.0 + sc_ref[...]) + sh_ref[...]
    h2_ref[...] = h2.astype(BF16)

    tm = h2.shape[0]
    logits = _dot_nt_hi(rwt_ref[...], h2)
    score = _sigmoid(logits)
    biased = score + rb_ref[...]
    b = [biased[e:e + 1, :] for e in range(N_EXPERTS)]
    n_groups = N_EXPERTS // EXPERTS_PER_GROUP
    gscore = []
    for gi in range(n_groups):
        vals = b[gi * EXPERTS_PER_GROUP:(gi + 1) * EXPERTS_PER_GROUP]
        best = None
        for i in range(EXPERTS_PER_GROUP):
            for j in range(i + 1, EXPERTS_PER_GROUP):
                pair = vals[i] + vals[j]
                best = pair if best is None else jnp.maximum(best, pair)
        gscore.append(best)
    gbest = jnp.zeros((1, tm), jnp.int32)
    top = gscore[0]
    for gi in range(1, n_groups):
        better = gscore[gi] > top
        gbest = jnp.where(better, gi, gbest)
        top = jnp.where(better, gscore[gi], top)
    erow = lax.broadcasted_iota(jnp.int32, (N_EXPERTS, tm), 0)
    gates = jnp.zeros((N_EXPERTS, tm), F32)
    for e in range(N_EXPERTS):
        gi = e // EXPERTS_PER_GROUP
        rank = jnp.zeros((1, tm), F32)
        for e2 in range(gi * EXPERTS_PER_GROUP, (gi + 1) * EXPERTS_PER_GROUP):
            if e2 == e:
                continue
            beats = (b[e2] > b[e]) | ((b[e2] == b[e]) & (e2 < e))
            rank = rank + jnp.where(beats, 1.0, 0.0)
        chosen = (gbest == gi) & (rank < 2.0)
        ge = jnp.where(chosen, score[e:e + 1, :], 0.0)
        gates = gates + jnp.where(erow == e, ge, 0.0)
    gates = gates / jnp.sum(gates, axis=0, keepdims=True)
    pad_rows = jnp.zeros((8 - EXPERTS_PER_GROUP, tm), F32)
    for gi in range(n_groups):
        gate_t_ref[gi] = jnp.concatenate(
            [gates[gi * EXPERTS_PER_GROUP:(gi + 1) * EXPERTS_PER_GROUP, :], pad_rows], axis=0)
    ident = (lax.broadcasted_iota(jnp.int32, (N_EXPERTS, LANES), 0)
             == lax.broadcasted_iota(jnp.int32, (N_EXPERTS, LANES), 1))
    ident = jnp.where(ident, 1.0, 0.0).astype(BF16)
    hi, mid, lo = _split3(gates)
    dn = (((0,), (0,)), ((), ()))
    gate_ref[...] = (lax.dot_general(hi, ident, dn, preferred_element_type=F32)
                     + lax.dot_general(mid, ident, dn, preferred_element_type=F32)
                     + lax.dot_general(lo, ident, dn, preferred_element_type=F32))


def _out_proj_call(yg, yn, x2, g1, sc2, sh2, ng, wo, rwt, rb, S):
    T, D = x2.shape
    tm = 512
    per_b = S // tm
    row = lambda i: (i, 0)
    bat = lambda i: (i // per_b, 0, 0)
    cst = lambda i: (0, 0)
    return pl.pallas_call(
        _out_proj_kernel,
        grid=(T // tm,),
        in_specs=[pl.BlockSpec((tm, GDN_WIDTH), row), pl.BlockSpec((NSA_WIDTH, tm), lambda i: (0, i)),
                  pl.BlockSpec((tm, D), row),
                  pl.BlockSpec((None, 1, D), bat), pl.BlockSpec((None, 1, D), bat), pl.BlockSpec((None, 1, D), bat),
                  pl.BlockSpec((1, D), cst), pl.BlockSpec((D, D), cst),
                  pl.BlockSpec((N_EXPERTS, D), cst), pl.BlockSpec((N_EXPERTS, 1), cst)],
        out_specs=[pl.BlockSpec((tm, D), row), pl.BlockSpec((tm, D), row), pl.BlockSpec((tm, LANES), row),
                   pl.BlockSpec((N_EXPERTS // EXPERTS_PER_GROUP, 8, tm), lambda i: (0, 0, i))],
        out_shape=[jax.ShapeDtypeStruct((T, D), F32), jax.ShapeDtypeStruct((T, D), BF16),
                   jax.ShapeDtypeStruct((T, LANES), F32),
                   jax.ShapeDtypeStruct((N_EXPERTS // EXPERTS_PER_GROUP, 8, T), F32)],
        compiler_params=_cparams("parallel"),
        name="out_proj_router",
    )(yg, yn, x2, g1, sc2, sh2, ng, wo, rwt, rb)


MOE_TM = 1024
MOE_RB = 288


def _moe_kernel(h_ref, gates_ref, gt8_ref, tri_ref, trit_ref, x1_ref, g2_ref, wg_ref, wu_ref, wd_ref, o_ref):
    g = pl.program_id(1)
    tm, rb = MOE_TM, MOE_RB
    nlb = tm // LANES

    @pl.when(g == 0)
    def _():
        o_ref[...] = jnp.zeros_like(o_ref)

    gt8 = gt8_ref[...]
    in_row = jnp.sum(gt8, axis=0, keepdims=True) > 0.0
    first = lax.broadcasted_iota(jnp.int32, (8, LANES), 0) == 0
    off = jnp.zeros((1, 1), F32)
    keys = []
    for b in range(nlb):
        blk = in_row[:, b * LANES:(b + 1) * LANES]
        one = jnp.where(blk & first, 1.0, 0.0).astype(BF16)
        within = jnp.dot(one, tri_ref[...], preferred_element_type=F32)[0:1, :]
        keys.append(jnp.where(blk, within + off, -1.0))
        off = off + jnp.sum(jnp.where(blk, 1.0, 0.0), axis=1, keepdims=True)
    key_row = jnp.concatenate(keys, axis=1)
    n_g = jnp.sum(off).astype(jnp.int32)
    lane = lax.broadcasted_iota(jnp.int32, (tm, LANES), 1)
    mine = (lane >= g * EXPERTS_PER_GROUP) & (lane < (g + 1) * EXPERTS_PER_GROUP)
    in_col = jnp.sum(jnp.where(mine, gates_ref[...], 0.0), axis=1, keepdims=True) > 0.0
    off = jnp.zeros((1, 1), F32)
    keys = []
    for b in range(nlb):
        blk = in_col[b * LANES:(b + 1) * LANES, :]
        one = jnp.broadcast_to(jnp.where(blk, 1.0, 0.0), (LANES, LANES)).astype(BF16)
        within = jnp.dot(trit_ref[...], one, preferred_element_type=F32)[:, 0:1]
        keys.append(jnp.where(blk, within + off, -1.0))
        off = off + jnp.sum(jnp.where(blk, 1.0, 0.0), axis=0, keepdims=True)
    key_col = jnp.concatenate(keys, axis=0)

    src = lax.broadcasted_iota(jnp.int32, (8, LANES), 0)
    dst = lax.broadcasted_iota(jnp.int32, (8, LANES), 1)
    tn_dims = (((0,), (0,)), ((), ()))
    gcols = jnp.zeros((tm, LANES), F32)
    for p, piece in enumerate(_split3(gt8)):
        place = jnp.where((dst == src + EXPERTS_PER_GROUP * p) & (src < EXPERTS_PER_GROUP), 1.0, 0.0).astype(BF16)
        gcols = gcols + lax.dot_general(piece, place, tn_dims, preferred_element_type=F32)
    gcols = gcols.astype(BF16)

    def block(j, carry):
        r0 = (j * rb).astype(F32)
        sel = jnp.where(key_row - r0 == lax.broadcasted_iota(jnp.int32, (rb, tm), 0).astype(F32), 1.0, 0.0)
        sel = sel.astype(BF16)
        xb = jnp.dot(sel, h_ref[...], preferred_element_type=F32).astype(BF16)
        gp = jnp.dot(sel, gcols, preferred_element_type=F32)
        y = jnp.zeros((rb, o_ref.shape[1]), F32)
        for k in range(EXPERTS_PER_GROUP):
            gate = (gp[:, k:k + 1] + gp[:, EXPERTS_PER_GROUP + k:EXPERTS_PER_GROUP + k + 1]
                    + gp[:, 2 * EXPERTS_PER_GROUP + k:2 * EXPERTS_PER_GROUP + k + 1])
            act = (_silu(jnp.dot(xb, wg_ref[k], preferred_element_type=F32))
                   * jnp.dot(xb, wu_ref[k], preferred_element_type=F32))
            y = y + _dot(act * gate, wd_ref[k])
        sel_t = jnp.where(key_col - r0 == lax.broadcasted_iota(jnp.int32, (tm, rb), 1).astype(F32), 1.0, 0.0)
        o_ref[...] += jnp.dot(sel_t.astype(BF16), y.astype(BF16), preferred_element_type=F32)
        return carry

    lax.fori_loop(0, (n_g + rb - 1) // rb, block, 0)

    @pl.when(g == N_EXPERTS // EXPERTS_PER_GROUP - 1)
    def _():
        o_ref[...] = x1_ref[...] + g2_ref[...] * o_ref[...]


def _moe_call(h2, gates, gates_t, x1, g2, wg, wu, wd, S):
    T, D = x1.shape
    tm = MOE_TM
    per_b = S // tm
    n_groups = N_EXPERTS // EXPERTS_PER_GROUP
    i = np.arange(LANES)
    tri = jnp.asarray(i[:, None] < i[None, :], BF16)
    row = lambda i, g: (i, 0)
    cst = lambda i, g: (0, 0)
    wsel = lambda i, g: (g, 0, 0)
    return pl.pallas_call(
        _moe_kernel,
        grid=(T // tm, n_groups),
        in_specs=[pl.BlockSpec((tm, D), row), pl.BlockSpec((tm, LANES), row),
                  pl.BlockSpec((None, 8, tm), lambda i, g: (g, 0, i)),
                  pl.BlockSpec((LANES, LANES), cst), pl.BlockSpec((LANES, LANES), cst),
                  pl.BlockSpec((tm, D), row),
                  pl.BlockSpec((None, 1, D), lambda i, g: (i // per_b, 0, 0)),
                  pl.BlockSpec((EXPERTS_PER_GROUP, D, EXPERT_FF), wsel),
                  pl.BlockSpec((EXPERTS_PER_GROUP, D, EXPERT_FF), wsel),
                  pl.BlockSpec((EXPERTS_PER_GROUP, EXPERT_FF, D), wsel)],
        out_specs=pl.BlockSpec((tm, D), row),
        out_shape=jax.ShapeDtypeStruct((T, D), F32),
        compiler_params=_cparams("parallel", "arbitrary"),
        name="moe",
    )(h2, gates, gates_t, tri, tri.T, x1, g2, wg, wu, wd)


def kernel(x, c, ada_w, ada_b, norm1_g, norm2_g, w_in, gdn_conv_w, gdn_a_log, gdn_dt_bias, gdn_norm_g,
           nsa_q_norm_g, nsa_k_norm_g, cmp_pe, cmp_w1, cmp_b1, cmp_w2, cmp_b2, w_out, router_w, router_bias,
           exp_w_gate, exp_w_up, exp_w_down):
    B, S, D = x.shape
    L = ada_w.shape[0]
    T = B * S
    G = NSA_KV_HEADS
    NC = S // CMP_STRIDE
    N = S // GDN_CHUNK
    P = GDN_HEADS // GDN_HPB
    W = GDN_HPB * HEAD_DIM

    mod = _ada_call(c, ada_w, ada_b)
    rwt = router_w.T
    rb = router_bias.reshape(N_EXPERTS, 1)
    x2 = x.reshape(T, D)
    for l in range(L):
        m6 = mod[l].reshape(B, 6, 1, D)
        sh1, sc1, g1, sh2, sc2, g2 = (m6[:, i] for i in range(6))
        qkv, z, nq, kv, misc, misct, vt = _in_proj_call(
            x2, sc1, sh1, norm1_g[l].reshape(1, D), *_pad_w_in(w_in[l]), S)

        a_rows = misct[:GDN_HEADS].reshape(P, GDN_HPB, B, N, GDN_CHUNK).transpose(2, 0, 3, 1, 4).reshape(B, P, N, W)
        rep = lambda t: jnp.repeat(t, HEAD_DIM).reshape(1, GDN_WIDTH)
        y_gdn = _gdn_call(qkv, z, misc, a_rows, gdn_conv_w[l], rep(gdn_a_log[l]), rep(gdn_dt_bias[l]),
                          jnp.tile(gdn_norm_g[l], GDN_HPB).reshape(1, W), B, S)

        w2 = cmp_w2[l]
        kcx, vct = _cmp_call(
            kv, cmp_pe[l].reshape(2, 1, CMP_BLOCK * HEAD_DIM), cmp_w1[l].astype(BF16),
            cmp_b1[l].reshape(2, 1, CMP_HIDDEN), jnp.pad(w2[0], ((0, 0), (0, LANES - HEAD_DIM))).astype(BF16),
            w2[1].T.astype(BF16), jnp.pad(cmp_b2[l, 0], (0, LANES - HEAD_DIM)).reshape(1, LANES),
            cmp_b2[l, 1].reshape(HEAD_DIM, 1), jnp.pad(nsa_k_norm_g[l, 0], (0, LANES - HEAD_DIM)).reshape(1, LANES),
            B, S)
        y_nsa_t = _nsa_call(nq, kv, vt.reshape(B, S // NSA_TK, VT_ROWS, NSA_TK), kcx, vct, misct,
                            nsa_q_norm_g[l].reshape(1, HEAD_DIM), nsa_k_norm_g[l], B, S)

        x1, h2, gates, gates_t = _out_proj_call(y_gdn, y_nsa_t, x2, g1, sc2, sh2, norm2_g[l].reshape(1, D),
                                       w_out[l].astype(BF16), rwt, rb, S)
        x2 = _moe_call(h2, gates, gates_t, x1, g2, exp_w_gate[l].astype(BF16), exp_w_up[l].astype(BF16),
                       exp_w_down[l].astype(BF16), S)
    return x2.reshape(B, S, D)
```

```python
import functools

import numpy as np
import jax
import jax.numpy as jnp
from jax import lax
from jax.experimental import pallas as pl
from jax.experimental.pallas import tpu as pltpu

F32 = jnp.float32
BF16 = jnp.bfloat16

HEAD_DIM = 64
GDN_HEADS = 8
GDN_WIDTH = GDN_HEADS * HEAD_DIM
GDN_CONV = 4
GDN_CHUNK = 64
NSA_HEADS = 8
NSA_KV_HEADS = 2
NSA_GROUP = NSA_HEADS // NSA_KV_HEADS
NSA_WIDTH = NSA_HEADS * HEAD_DIM
NSA_KV_WIDTH = NSA_KV_HEADS * HEAD_DIM
CMP_BLOCK = 32
CMP_STRIDE = 16
CMP_HIDDEN = 256
SLC_BLOCK = 64
SLC_TOPN = 8
WINDOW = 512
N_EXPERTS = 16
EXPERTS_PER_GROUP = 4
EXPERT_FF = 512
RMS_EPS = 1e-6
NEG_BIG = -1e30
SEL_BIG = 1e9

LANES = 128
VMEM_LIMIT_BYTES = 56 * 1024 * 1024

C_QKV = 0
C_Z = 3 * GDN_WIDTH
C_NQ = C_Z + GDN_WIDTH
C_KV = C_NQ + NSA_WIDTH
KV_COLS = 4 * NSA_KV_WIDTH
C_MISC = C_KV + KV_COLS
IN_PAD_COLS = C_MISC + LANES
MISC_B0 = GDN_HEADS
MISC_GATE0 = 16
VT_SLAB = HEAD_DIM + 16
VT_ROWS = 2 * NSA_KV_HEADS * VT_SLAB


def _cparams(*sem):
    return pltpu.CompilerParams(dimension_semantics=sem, vmem_limit_bytes=VMEM_LIMIT_BYTES)


def _dot(a, b):
    return jnp.dot(a.astype(BF16), b.astype(BF16), preferred_element_type=F32)


def _dot_nt(a, b):
    return lax.dot_general(a.astype(BF16), b.astype(BF16), (((1,), (1,)), ((), ())),
                           preferred_element_type=F32)


def _dot_tn(a, b):
    return lax.dot_general(a.astype(BF16), b.astype(BF16), (((0,), (0,)), ((), ())),
                           preferred_element_type=F32)


def _split2(x):
    hi = x.astype(BF16)
    lo = (x - hi.astype(F32)).astype(BF16)
    return hi, lo


def _split3(x):
    hi = x.astype(BF16)
    r = x - hi.astype(F32)
    mid = r.astype(BF16)
    lo = (r - mid.astype(F32)).astype(BF16)
    return hi, mid, lo


def _dot_hi(a, b):
    ah, al = _split2(a)
    bh, bl = _split2(b)
    return (jnp.dot(ah, bh, preferred_element_type=F32) + jnp.dot(al, bh, preferred_element_type=F32)
            + jnp.dot(ah, bl, preferred_element_type=F32))


def _dot_nt_hi(a, b):
    ah, al = _split2(a)
    bh, bl = _split2(b)
    dn = (((1,), (1,)), ((), ()))
    return (lax.dot_general(ah, bh, dn, preferred_element_type=F32)
            + lax.dot_general(al, bh, dn, preferred_element_type=F32)
            + lax.dot_general(ah, bl, dn, preferred_element_type=F32))


def _dot_exact_rhs(a, b_bf16):
    hi, mid, lo = _split3(a)
    return (jnp.dot(hi, b_bf16, preferred_element_type=F32) + jnp.dot(mid, b_bf16, preferred_element_type=F32)
            + jnp.dot(lo, b_bf16, preferred_element_type=F32))


def _rms(x, g):
    return x * lax.rsqrt(jnp.mean(x * x, axis=-1, keepdims=True) + RMS_EPS) * g


def _sigmoid(x):
    return 0.5 * (jnp.tanh(0.5 * x) + 1.0)


def _silu(x):
    return x * _sigmoid(x)


def _ada_kernel(c_ref, w_ref, b_ref, o_ref):
    o_ref[...] = _dot_hi(_silu(c_ref[...]), w_ref[...]) + b_ref[...]


def _ada_call(c, ada_w, ada_b):
    L, D, N = ada_w.shape
    B = c.shape[0]
    tn = 1024
    return pl.pallas_call(
        _ada_kernel,
        grid=(L, N // tn),
        in_specs=[pl.BlockSpec((B, D), lambda l, j: (0, 0)),
                  pl.BlockSpec((None, D, tn), lambda l, j: (l, 0, j)),
                  pl.BlockSpec((None, 1, tn), lambda l, j: (l, 0, j))],
        out_specs=pl.BlockSpec((None, B, tn), lambda l, j: (l, 0, j)),
        out_shape=jax.ShapeDtypeStruct((L, B, N), F32),
        compiler_params=_cparams("parallel", "parallel"),
        name="ada_mod",
    )(c, ada_w, ada_b.reshape(L, 1, N))


IN_SEGS = ((C_QKV, 3 * GDN_WIDTH), (C_Z, GDN_WIDTH), (C_NQ, NSA_WIDTH), (C_KV, KV_COLS), (C_MISC, LANES))


def _in_proj_kernel(x_ref, sc_ref, sh_ref, g_ref, w_ref, wmt_ref, wvt_ref, *o_refs):
    h = _rms(x_ref[...], g_ref[...]) * (1.0 + sc_ref[...]) + sh_ref[...]
    hb = h.astype(BF16)
    for (c0, n), o_ref in zip(IN_SEGS, o_refs[:-2]):
        o_ref[...] = jnp.dot(hb, w_ref[:, c0:c0 + n], preferred_element_type=F32)
    nt_dims = (((1,), (1,)), ((), ()))
    o_refs[-2][...] = lax.dot_general(wmt_ref[...], hb, nt_dims, preferred_element_type=F32)
    vt = lax.dot_general(wvt_ref[...], hb, nt_dims, preferred_element_type=F32).astype(BF16)
    ones = jnp.ones((VT_SLAB - HEAD_DIM, LANES), BF16)
    for j in range(o_refs[-1].shape[0]):
        for st in range(2 * NSA_KV_HEADS):
            o_refs[-1][j, st * VT_SLAB:st * VT_SLAB + HEAD_DIM, :] = vt[st * HEAD_DIM:(st + 1) * HEAD_DIM,
                                                                         j * LANES:(j + 1) * LANES]
            o_refs[-1][j, st * VT_SLAB + HEAD_DIM:(st + 1) * VT_SLAB, :] = ones


def _in_proj_call(x2, sc, sh, g, w_pad, wvt, S):
    T, D = x2.shape
    tm = 512
    per_b = S // tm
    row = lambda i: (i, 0)
    bat = lambda i: (i // per_b, 0, 0)
    return pl.pallas_call(
        _in_proj_kernel,
        grid=(T // tm,),
        in_specs=[pl.BlockSpec((tm, D), row),
                  pl.BlockSpec((None, 1, D), bat),
                  pl.BlockSpec((None, 1, D), bat),
                  pl.BlockSpec((1, D), lambda i: (0, 0)),
                  pl.BlockSpec((D, IN_PAD_COLS), lambda i: (0, 0)),
                  pl.BlockSpec((LANES, D), lambda i: (0, 0)),
                  pl.BlockSpec((2 * NSA_KV_WIDTH, D), lambda i: (0, 0))],
        out_specs=([pl.BlockSpec((tm, n), row) for _, n in IN_SEGS]
                   + [pl.BlockSpec((LANES, tm), lambda i: (0, i)),
                      pl.BlockSpec((tm // LANES, VT_ROWS, LANES), lambda i: (i, 0, 0))]),
        out_shape=([jax.ShapeDtypeStruct((T, n), F32) for _, n in IN_SEGS]
                   + [jax.ShapeDtypeStruct((LANES, T), F32),
                      jax.ShapeDtypeStruct((T // LANES, VT_ROWS, LANES), BF16)]),
        compiler_params=_cparams("parallel"),
        name="in_proj",
    )(x2, sc, sh, g, w_pad, w_pad[:, C_MISC:].T, wvt)


def _pad_w_in(w_in):
    D = w_in.shape[0]
    W = GDN_WIDTH
    o = 0
    gq, gk, gv, gz = (w_in[:, o + i * W:o + (i + 1) * W] for i in range(4))
    o += 4 * W
    ga = w_in[:, o:o + GDN_HEADS]
    gb = w_in[:, o + GDN_HEADS:o + 2 * GDN_HEADS]
    o += 2 * GDN_HEADS
    nq = w_in[:, o:o + NSA_WIDTH]
    o += NSA_WIDTH
    kc, vc, ks, vs, kw, vw = (w_in[:, o + i * NSA_KV_WIDTH:o + (i + 1) * NSA_KV_WIDTH] for i in range(6))
    o += 6 * NSA_KV_WIDTH
    gates = w_in[:, o:o + 3 * NSA_HEADS]
    misc = jnp.zeros((D, LANES), w_in.dtype)
    misc = misc.at[:, 0:GDN_HEADS].set(ga).at[:, MISC_B0:MISC_B0 + GDN_HEADS].set(gb)
    misc = misc.at[:, MISC_GATE0:MISC_GATE0 + 3 * NSA_HEADS].set(gates)
    cols = [gq, gk, gv, gz, nq, kc, vc, ks, kw, misc]
    return jnp.concatenate(cols, axis=1).astype(BF16), jnp.concatenate([vs, vw], axis=1).T.astype(BF16)


GDN_HPB = 2
GDN_NB = 32


def _gdn_kernel(q_ref, k_ref, v_ref, z_ref, misc_ref, ar_ref, cwq_ref, cwk_ref, cwv_ref,
                alog_ref, dtb_ref, alogr_ref, dtbr_ref, ng_ref, o_ref,
                qs, ks, kbs, xs, gcs, grs, ps, qq, qks, os_, rs, q2s, sev, *, S):
    C = GDN_CHUNK
    N = S // C
    W = GDN_HPB * HEAD_DIM
    row = lax.broadcasted_iota(jnp.int32, (S, W), 0)
    row8 = lax.broadcasted_iota(jnp.int32, (8, W), 0)

    def conv_silu(x_ref, w):
        x = x_ref[...]
        y = x * w[GDN_CONV - 1:GDN_CONV, :]
        for s in range(1, GDN_CONV):
            head = jnp.where(row8 >= s, pltpu.roll(x[:8], s, axis=0), 0.0)
            xs = jnp.concatenate([head, x_ref[pl.ds(8 - s, S - 8), :]], axis=0)
            y = y + xs * w[GDN_CONV - 1 - s:GDN_CONV - s, :]
        return _silu(y)

    bd = jnp.where(lax.broadcasted_iota(jnp.int32, (W, W), 0) // HEAD_DIM
                   == lax.broadcasted_iota(jnp.int32, (W, W), 1) // HEAD_DIM, 1.0, 0.0).astype(BF16)

    def head_sumsq(x):
        hi, lo = _split2(x * x)
        return jnp.dot(hi, bd, preferred_element_type=F32) + jnp.dot(lo, bd, preferred_element_type=F32)

    def head_l2(x):
        return x * lax.rsqrt(head_sumsq(x) + RMS_EPS)

    q = head_l2(conv_silu(q_ref, cwq_ref[...])) * (HEAD_DIM ** -0.5)
    k = head_l2(conv_silu(k_ref, cwk_ref[...]))
    v = conv_silu(v_ref, cwv_ref[...])

    hd0 = pl.program_id(1) * GDN_HPB
    src = lax.broadcasted_iota(jnp.int32, (LANES, W), 0)
    dst = lax.broadcasted_iota(jnp.int32, (LANES, W), 1) // HEAD_DIM + hd0
    a_exp = _dot_exact_rhs(misc_ref[...], jnp.where(src == dst, 1.0, 0.0).astype(BF16))
    b_exp = _dot_exact_rhs(misc_ref[...], jnp.where(src == dst + MISC_B0, 1.0, 0.0).astype(BF16))
    beta = _sigmoid(b_exp)

    def log_decay(a, alog, dtb):
        xx = a + dtb
        sp = jnp.maximum(xx, 0.0) + jnp.log(1.0 + jnp.exp(-jnp.abs(xx)))
        return -jnp.exp(alog) * sp

    gc = log_decay(a_exp, alog_ref[...], dtb_ref[...])
    pos = row % C
    for s in (1, 2, 4, 8, 16, 32):
        gc = gc + jnp.where(pos >= s, pltpu.roll(gc, s, axis=0), 0.0)
    gr = log_decay(ar_ref[...], alogr_ref[...], dtbr_ref[...])
    lpos = lax.broadcasted_iota(jnp.int32, (N, W), 1) % C
    for s in (1, 2, 4, 8, 16, 32):
        gr = gr + jnp.where(lpos >= s, pltpu.roll(gr, s, axis=1), 0.0)

    kb = k * beta
    vb = v * beta
    kbe = kb * jnp.exp(gc)
    for h in range(GDN_HPB):
        sl = slice(h * HEAD_DIM, (h + 1) * HEAD_DIM)
        qs[h] = q[:, sl]
        ks[h] = k[:, sl]
        kbs[h] = kb[:, sl]
        gcs[h] = gc[:, sl]
        for n in range(N):
            grs[h, n] = jnp.broadcast_to(gr[n:n + 1, sl], (8, HEAD_DIM))
        xs[h] = jnp.concatenate([kbe[:, sl], vb[:, sl]], axis=1)

    NB = GDN_NB
    ci3 = lax.broadcasted_iota(jnp.int32, (1, C, C), 1)
    cj3 = lax.broadcasted_iota(jnp.int32, (1, C, C), 2)
    eye3 = jnp.where(ci3 == cj3, 1.0, 0.0)

    def bmm(a, b):
        return lax.dot_general(a.astype(BF16), b.astype(BF16), (((2,), (1,)), ((0,), (0,))),
                               preferred_element_type=F32)

    def bmm_nt(a, b):
        return lax.dot_general(a.astype(BF16), b.astype(BF16), (((2,), (2,)), ((0,), (0,))),
                               preferred_element_type=F32)

    def intra(nb, carry):
        rows = pl.ds(pl.multiple_of(nb * (NB * C), NB * C), NB * C)
        for h in range(GDN_HPB):
            kk = ks[h, rows, :].reshape(NB, C, HEAD_DIM)
            kbq = jnp.concatenate([kbs[h, rows, :].reshape(NB, C, HEAD_DIM),
                                   qs[h, rows, :].reshape(NB, C, HEAD_DIM)], axis=1)
            a2 = bmm_nt(kbq, kk)
            gcol = gcs[h, rows, :].reshape(NB, C, HEAD_DIM)
            grow = grs[h, pl.ds(nb * NB, NB), 0:1, :]
            decay = jnp.exp(jnp.where(ci3 >= cj3, gcol - grow, NEG_BIG))
            low = jnp.where(ci3 > cj3, a2[:, :C] * decay, 0.0)
            qks[h, rows, :] = (a2[:, C:] * decay).reshape(NB * C, C)
            tinv = eye3 - jnp.where(ci3 // 2 == cj3 // 2, low, 0.0)
            for lv in range(1, 6):
                s = 2 ** lv
                off = jnp.where((ci3 // (2 * s) == cj3 // (2 * s)) & (ci3 // s != cj3 // s), low, 0.0)
                tinv = tinv - bmm(tinv, bmm(off, tinv))
            x = bmm(tinv, xs[h, rows, :].reshape(NB, C, 2 * HEAD_DIM))
            xs[h, rows, :] = x.reshape(NB * C, 2 * HEAD_DIM)
            kd = kk * jnp.exp(gcol[:, C - 1:C, :] - gcol)
            pq = lax.dot_general(kd.astype(BF16), x.astype(BF16), (((1,), (1,)), ((0,), (0,))),
                                 preferred_element_type=F32)
            pn = pq[:, :, :HEAD_DIM]
            qn = pq[:, :, HEAD_DIM:]
            ps[h, rows, :] = pn.reshape(NB * C, HEAD_DIM)
            qq[h, rows, :] = qn.reshape(NB * C, HEAD_DIM)
            cd = jnp.exp(gcol[:, C - 1:C, :]).reshape(NB // 2, 2, 1, HEAD_DIM)
            pn = pn.reshape(NB // 2, 2, HEAD_DIM, HEAD_DIM)
            qn = qn.reshape(NB // 2, 2, HEAD_DIM, HEAD_DIM)
            p0, p1, q0, q1, c0, c1 = pn[:, 0], pn[:, 1], qn[:, 0], qn[:, 1], cd[:, 0], cd[:, 1]
            t = bmm(p1, pq.reshape(NB // 2, 2, HEAD_DIM, 2 * HEAD_DIM)[:, 0])
            prow = pl.ds(pl.multiple_of(nb * (NB // 2 * C), NB // 2 * C), NB // 2 * C)
            rs[h, prow, :] = (c1 * p0 + c0 * p1 - t[:, :, :HEAD_DIM]).reshape(NB // 2 * C, HEAD_DIM)
            q2s[h, prow, :] = (c1 * q0 - t[:, :, HEAD_DIM:] + q1).reshape(NB // 2 * C, HEAD_DIM)
        return carry

    lax.fori_loop(0, N // NB, intra, 0)

    def inter(k, states):
        sl = pl.ds(pl.multiple_of(k * C, C), C)
        new_states = []
        for h in range(GDN_HPB):
            st = states[h]
            sev[h, k] = st
            gsum = gcs[h, pl.ds(2 * k * C + C - 1, 1), :] + gcs[h, pl.ds(2 * k * C + 2 * C - 1, 1), :]
            new_states.append(st * jnp.exp(gsum) - _dot_hi(rs[h, sl, :], st) + q2s[h, sl, :])
        return tuple(new_states)

    lax.fori_loop(0, N // 2, inter, tuple(jnp.zeros((HEAD_DIM, HEAD_DIM), F32) for _ in range(GDN_HPB)))

    def outputs(nb, carry):
        rows = pl.ds(pl.multiple_of(nb * (NB * C), NB * C), NB * C)
        for h in range(GDN_HPB):
            gcol = gcs[h, rows, :].reshape(NB, C, HEAD_DIM)
            s0 = sev[h, pl.ds(nb * (NB // 2), NB // 2)]
            p0 = ps[h, rows, :].reshape(NB // 2, 2, HEAD_DIM, HEAD_DIM)[:, 0]
            q0 = qq[h, rows, :].reshape(NB // 2, 2, HEAD_DIM, HEAD_DIM)[:, 0]
            c0 = jnp.exp(gcol[:, C - 1:C, :]).reshape(NB // 2, 2, 1, HEAD_DIM)[:, 0]
            s1 = c0 * s0 - bmm(p0, s0) + q0
            st = jnp.stack([s0, s1], axis=1).reshape(NB, HEAD_DIM, HEAD_DIM)
            x = xs[h, rows, :].reshape(NB, C, 2 * HEAD_DIM)
            qd = qs[h, rows, :].reshape(NB, C, HEAD_DIM) * jnp.exp(gcol)
            t1 = bmm(jnp.concatenate([x[:, :, :HEAD_DIM], qd], axis=1), st)
            vn = x[:, :, HEAD_DIM:] - t1[:, :C]
            o = t1[:, C:] + bmm(qks[h, rows, :].reshape(NB, C, C), vn)
            os_[h, rows, :] = o.reshape(NB * C, HEAD_DIM)
        return carry

    lax.fori_loop(0, N // NB, outputs, 0)

    o = jnp.concatenate([os_[h] for h in range(GDN_HPB)], axis=1)
    o = o * lax.rsqrt(head_sumsq(o) * (1.0 / HEAD_DIM) + RMS_EPS) * ng_ref[...]
    o_ref[...] = o * _silu(z_ref[...])


def _gdn_call(qkv, z, misc, a_rows, conv_w, alog_exp, dtb_exp, norm_g, B, S):
    T = B * S
    W = GDN_HPB * HEAD_DIM
    P = GDN_HEADS // GDN_HPB
    N = S // GDN_CHUNK
    col = lambda off: (lambda b, p: (b, off + p))
    par = lambda off: (lambda b, p: (0, off + p))
    hs = lambda n: pltpu.VMEM((GDN_HPB, S, n), F32)
    return pl.pallas_call(
        functools.partial(_gdn_kernel, S=S),
        grid=(B, P),
        in_specs=[pl.BlockSpec((S, W), col(0)), pl.BlockSpec((S, W), col(P)), pl.BlockSpec((S, W), col(2 * P)),
                  pl.BlockSpec((S, W), col(0)), pl.BlockSpec((S, LANES), lambda b, p: (b, 0)),
                  pl.BlockSpec((None, None, N, W), lambda b, p: (b, p, 0, 0)),
                  pl.BlockSpec((GDN_CONV, W), par(0)), pl.BlockSpec((GDN_CONV, W), par(P)),
                  pl.BlockSpec((GDN_CONV, W), par(2 * P)),
                  pl.BlockSpec((1, W), par(0)), pl.BlockSpec((1, W), par(0)),
                  pl.BlockSpec((1, W), par(0)), pl.BlockSpec((1, W), par(0)),
                  pl.BlockSpec((1, W), lambda b, p: (0, 0))],
        out_specs=pl.BlockSpec((S, W), col(0)),
        out_shape=jax.ShapeDtypeStruct((T, GDN_WIDTH), F32),
        scratch_shapes=[hs(HEAD_DIM), hs(HEAD_DIM), hs(HEAD_DIM), hs(2 * HEAD_DIM), hs(HEAD_DIM),
                        pltpu.VMEM((GDN_HPB, N, 8, HEAD_DIM), F32),
                        hs(HEAD_DIM), hs(HEAD_DIM), hs(HEAD_DIM), hs(HEAD_DIM),
                        pltpu.VMEM((GDN_HPB, S // 2, HEAD_DIM), F32), pltpu.VMEM((GDN_HPB, S // 2, HEAD_DIM), F32),
                        pltpu.VMEM((GDN_HPB, N // 2, HEAD_DIM, HEAD_DIM), F32)],
        compiler_params=_cparams("parallel", "parallel"),
        name="gdn",
    )(qkv, qkv, qkv, z, misc, a_rows, conv_w, conv_w, conv_w,
      alog_exp, dtb_exp, alog_exp, dtb_exp, norm_g)


def _cmp_kernel(kc_in_ref, vc_in_ref, pe_ref, w1_ref, b1_ref, w2k_ref, w2vt_ref, b2k_ref, b2v_ref, kg_ref, kc_ref, vct_ref):
    half = (CMP_BLOCK // 2) * HEAD_DIM
    nrows = kc_in_ref.shape[0] // CMP_STRIDE
    lane = lax.broadcasted_iota(jnp.int32, (nrows, LANES), 1)
    cend = lax.broadcasted_iota(jnp.int32, (nrows, LANES), 0) * CMP_STRIDE + (CMP_BLOCK - 1)
    ext = jnp.where(lane == HEAD_DIM, cend // LANES, jnp.where(lane == HEAD_DIM + 1, cend % LANES, 0)).astype(F32)

    def token_groups(x_ref):
        cols = [[] for _ in range(NSA_KV_HEADS)]
        for m in range(CMP_STRIDE // 2):
            ev = x_ref[pl.ds(2 * m, nrows, stride=CMP_STRIDE), :]
            od = x_ref[pl.ds(2 * m + 1, nrows, stride=CMP_STRIDE), :]
            cols[0].append(jnp.where(lane < HEAD_DIM, ev, pltpu.roll(od, HEAD_DIM, axis=1)))
            cols[1].append(jnp.where(lane < HEAD_DIM, pltpu.roll(ev, HEAD_DIM, axis=1), od))
        return [jnp.concatenate(c, axis=1) for c in cols]

    def hidden(x, j):
        pe = pe_ref[j]
        p = _dot(x + pe[:, :half], w1_ref[j, :half, :])
        q = _dot(x + pe[:, half:], w1_ref[j, half:, :])
        return _silu(p + pltpu.roll(q, nrows - 1, axis=0) + b1_ref[j])

    for g, x in enumerate(token_groups(kc_in_ref)):
        kc = _dot(hidden(x, 0), w2k_ref[...]) + b2k_ref[...]
        ms = jnp.sum(kc * kc, axis=-1, keepdims=True) * (1.0 / HEAD_DIM)
        kc_ref[g] = jnp.where(lane < HEAD_DIM, kc * lax.rsqrt(ms + RMS_EPS) * kg_ref[...], ext)
    for g, x in enumerate(token_groups(vc_in_ref)):
        vct_ref[g] = _dot_nt(w2vt_ref[...], hidden(x, 1)) + b2v_ref[...]


def _cmp_call(kv, pe, w1, b1, w2k, w2vt, b2k, b2v, kg, B, S):
    G = NSA_KV_HEADS
    NC = S // CMP_STRIDE
    full = lambda a: pl.BlockSpec(a.shape, lambda b: (0,) * a.ndim)
    return pl.pallas_call(
        _cmp_kernel,
        grid=(B,),
        in_specs=[pl.BlockSpec((S, NSA_KV_WIDTH), lambda b: (b, 0)), pl.BlockSpec((S, NSA_KV_WIDTH), lambda b: (b, 1))]
        + [full(a) for a in (pe, w1, b1, w2k, w2vt, b2k, b2v, kg)],
        out_specs=[pl.BlockSpec((None, G, NC, LANES), lambda b: (b, 0, 0, 0)),
                   pl.BlockSpec((None, G, HEAD_DIM, NC), lambda b: (b, 0, 0, 0))],
        out_shape=[jax.ShapeDtypeStruct((B, G, NC, LANES), F32), jax.ShapeDtypeStruct((B, G, HEAD_DIM, NC), F32)],
        compiler_params=_cparams("parallel"),
        name="nsa_compress",
    )(kv, kv, pe, w1, b1, w2k, w2vt, b2k, b2v, kg)


NSA_TQ = 128
NSA_TK = 128
NSA_M = NSA_GROUP * NSA_TQ
MASK_BIG = 2.0 ** 100
SEL_LANE0 = HEAD_DIM + 2


def _nsa_kernel(q_ref, kv_ref, vt_ref, kcx_ref, vct_ref, misct_ref, cbias_ref, tbl_ref, qg_ref, kg_ref,
                ovt_ref, ish_ref, o_ref, ks_s, kw_s, live, n_live, *, S):
    tq, tk, M = NSA_TQ, NSA_TK, NSA_M
    G, R = NSA_KV_HEADS, NSA_GROUP
    NS = S // SLC_BLOCK
    n_top = min(SLC_TOPN, NS)
    qi = pl.program_id(1)
    t0 = qi * tq
    nt_dims = (((1,), (1,)), ((), ()))

    @pl.when(qi == 0)
    def _():
        rowi = lax.broadcasted_iota(jnp.int32, (S, LANES), 0)
        lane = lax.broadcasted_iota(jnp.int32, (S, LANES), 1)
        ext = jnp.where(lane == HEAD_DIM, rowi // LANES, jnp.where(lane == HEAD_DIM + 1, rowi % LANES, 0)).astype(F32)
        ext_slc = jnp.where(lane - SEL_LANE0 == rowi // SLC_BLOCK, -MASK_BIG, ext)

        def prep(c0, g, gain, extra):
            kraw = kv_ref[:, c0:c0 + LANES]
            if g == 1:
                kraw = pltpu.roll(kraw, HEAD_DIM, axis=1)
            ms = jnp.sum(jnp.where(lane < HEAD_DIM, kraw * kraw, 0.0), axis=-1, keepdims=True) * (1.0 / HEAD_DIM)
            return jnp.where(lane < HEAD_DIM, kraw * lax.rsqrt(ms + RMS_EPS) * gain, extra).astype(BF16)

        for g in range(G):
            ks_s[g] = prep(2 * NSA_KV_WIDTH, g, kg_ref[1:2, :], ext_slc)
            kw_s[g] = prep(3 * NSA_KV_WIDTH, g, kg_ref[2:3, :], ext)

    lane_q = lax.broadcasted_iota(jnp.int32, (tq, LANES), 1)
    gate_sig = _sigmoid(misct_ref[MISC_GATE0:MISC_GATE0 + 3 * NSA_HEADS, :])
    kt_d = t0 // tk

    def score_tiles(streams, kt, biased):
        rows = pl.ds(pl.multiple_of(kt * tk, tk), tk)
        out = []
        for qpp, k_s, g, _, idx_fn in streams:
            s = lax.dot_general(k_s[g, rows, :], qpp, nt_dims, preferred_element_type=F32)
            out.append(s + tbl_ref[idx_fn(kt)] if biased else s)
        return tuple(out)

    def softmax_pv(streams, kt, scores, states):
        soft = []
        for s, (m, _) in zip(scores, states):
            m_new = jnp.maximum(m, jnp.max(s, axis=0, keepdims=True))
            soft.append((m_new, jnp.exp(m - m_new), jnp.exp(s - m_new).astype(BF16)))
        pvs = [jnp.dot(vt_ref[kt, st[3]:st[3] + VT_SLAB, :], sf[2], preferred_element_type=F32)
               for st, sf in zip(streams, soft)]
        return tuple((m_new, alpha * acc + pv) for (m_new, alpha, _), (_, acc), pv in zip(soft, states, pvs))

    flash_init = (jnp.full((1, M), NEG_BIG, F32), jnp.zeros((VT_SLAB, M), F32))

    qpps, o_cmps, unsel = [], [], []
    for g in range(G):
        qrows = []
        for r in range(R):
            hh = g * R + r
            qpair = q_ref[:, (hh // 2) * LANES:(hh // 2 + 1) * LANES]
            if hh % 2 == 1:
                qpair = pltpu.roll(qpair, HEAD_DIM, axis=1)
            ms = jnp.sum(jnp.where(lane_q < HEAD_DIM, qpair * qpair, 0.0), axis=-1, keepdims=True) * (1.0 / HEAD_DIM)
            qn = qpair * lax.rsqrt(ms + RMS_EPS) * qg_ref[...] * (HEAD_DIM ** -0.5)
            slope = 2.0 ** (-8.0 * (hh + 1) / NSA_HEADS)
            ex = jnp.where(lane_q == HEAD_DIM, slope * LANES, jnp.where(lane_q == HEAD_DIM + 1, slope, 0.0))
            qrows.append(jnp.where(lane_q < HEAD_DIM, qn, ex))
        qbase = jnp.concatenate(qrows, axis=0)

        cb = cbias_ref[...]
        s = (lax.dot_general(kcx_ref[g].astype(BF16), qbase.astype(BF16), nt_dims, preferred_element_type=F32)
             + jnp.concatenate([cb] * R, axis=1))
        mx = jnp.maximum(jnp.max(s, axis=0, keepdims=True), NEG_BIG)
        e = jnp.exp(s - mx)
        l = jnp.sum(e, axis=0, keepdims=True)
        p_cmp = e / jnp.where(l > 0.0, l, 1.0)
        o_cmps.append(_dot(vct_ref[g], p_cmp))

        psum = p_cmp[:, 0:tq]
        for r in range(1, R):
            psum = psum + p_cmp[:, r * tq:(r + 1) * tq]
        hi, mid, lo = _split3(psum)
        ovt = ovt_ref[...]
        imp = (jnp.dot(ovt, hi, preferred_element_type=F32) + jnp.dot(ovt, mid, preferred_element_type=F32)
               + jnp.dot(ovt, lo, preferred_element_type=F32))
        blk = lax.broadcasted_iota(jnp.int32, (NS, tq), 0)
        cur = (t0 + lax.broadcasted_iota(jnp.int32, (NS, tq), 1)) // SLC_BLOCK
        valid = blk <= cur
        forced = (blk == 0) | (blk == cur) | (blk == cur - 1)
        score = jnp.where(forced, SEL_BIG, jnp.where(valid, imp, -SEL_BIG))
        rank = jnp.zeros((NS, tq), F32)
        for j in range(NS):
            cj = score[j:j + 1, :]
            beats = (cj > score) | ((cj == score) & (blk > j))
            rank = rank + jnp.where(beats, 1.0, 0.0)
        nsel = jnp.where((rank < n_top) & valid, 0.0, 1.0)
        unsel.append(jnp.min(nsel, axis=1, keepdims=True))
        nsel = nsel.astype(BF16)
        nsel_q = lax.dot_general(nsel, ish_ref[...], (((0,), (0,)), ((), ())),
                                 preferred_element_type=F32)
        in_sel = (lane_q >= SEL_LANE0) & (lane_q < SEL_LANE0 + NS)
        qpps.append(jnp.concatenate([jnp.where(in_sel, nsel_q, qrows[r]) for r in range(R)], axis=0).astype(BF16))

    slc_idx = lambda kt: jnp.where(kt == kt_d, 1, 0)
    win_lo = kt_d - WINDOW // tk
    win_idx = lambda kt: jnp.where(kt == kt_d, 1, jnp.where(kt == win_lo, 2, 0))
    slc_streams = [(qpps[g], ks_s, g, g * VT_SLAB, slc_idx) for g in range(G)]
    win_streams = [(qpps[g], kw_s, g, (G + g) * VT_SLAB, win_idx) for g in range(G)]
    kt_w = jnp.maximum(win_lo, 0)

    blocks_per_tile = tk // SLC_BLOCK
    unsel_any = unsel[0]
    for g in range(1, G):
        unsel_any = jnp.minimum(unsel_any, unsel[g])
    blk_tile = lax.broadcasted_iota(jnp.int32, (NS, 1), 0) // blocks_per_tile
    bit = jnp.zeros((1, 1), F32)
    for kt in range(S // tk - WINDOW // tk):
        in_tile = jnp.min(jnp.where(blk_tile == kt, unsel_any, 1.0), axis=0, keepdims=True)
        bit = bit + jnp.where(in_tile < 0.5, float(2 ** kt), 0.0)
    live_bits = jnp.max(bit).astype(jnp.int32)
    n_live[0] = 0
    live[0] = 0
    for kt in range(S // tk - WINDOW // tk):
        @pl.when((((live_bits >> kt) & 1) == 1) & (kt < kt_w))
        def _():
            live[n_live[0]] = kt
            n_live[0] = n_live[0] + 1

    n_slc = n_live[0]

    def slc_only(i, carry):
        scores, states = carry
        nxt = score_tiles(slc_streams, live[jnp.minimum(i + 1, n_slc - 1)], False)
        return nxt, softmax_pv(slc_streams, live[i], scores, states)

    first = jnp.where(n_slc > 0, live[0], 0)
    _, st_slc = lax.fori_loop(0, n_slc, slc_only, (score_tiles(slc_streams, first, False), (flash_init,) * G))
    both = slc_streams + win_streams
    st_all = lax.fori_loop(kt_w, kt_d + 1, lambda kt, st: softmax_pv(both, kt, score_tiles(both, kt, True), st),
                           st_slc + (flash_init,) * G)
    o_all = []
    for _, acc in st_all:
        den = acc[HEAD_DIM:HEAD_DIM + 1, :]
        o_all.append(acc[:HEAD_DIM] / jnp.where(den > 0.0, den, 1.0))
    o_slcs, o_wins = o_all[:G], o_all[G:]

    for g in range(G):
        for r in range(R):
            hh = g * R + r
            cs = slice(r * tq, (r + 1) * tq)
            o_ref[hh * HEAD_DIM:(hh + 1) * HEAD_DIM, :] = (
                gate_sig[hh:hh + 1, :] * o_cmps[g][:, cs]
                + gate_sig[NSA_HEADS + hh:NSA_HEADS + hh + 1, :] * o_slcs[g][:, cs]
                + gate_sig[2 * NSA_HEADS + hh:2 * NSA_HEADS + hh + 1, :] * o_wins[g][:, cs])


def _nsa_consts(S):
    tq, tk, M = NSA_TQ, NSA_TK, NSA_M
    NC = S // CMP_STRIDE
    NS = S // SLC_BLOCK
    cs = np.arange(NC)[:, None] * CMP_STRIDE
    ss = np.arange(NS)[None, :] * SLC_BLOCK
    ov = np.clip(np.minimum(cs + CMP_BLOCK, ss + SLC_BLOCK) - np.maximum(cs, ss), 0, None) / CMP_BLOCK
    ov[NC - 1:] = 0.0
    ish = np.zeros((NS, LANES), np.float32)
    ish[np.arange(NS), SEL_LANE0 + np.arange(NS)] = 1.0
    t = np.arange(S).reshape(S // tq, 1, tq)
    n = np.arange(NC).reshape(1, NC, 1)
    cbias = np.where((t >= n * CMP_STRIDE + CMP_BLOCK - 1) & (n < NC - 1), 0.0, -MASK_BIG).astype(np.float32)
    i = np.arange(tk)[:, None]
    j = np.arange(M)[None, :] % tq
    tbl = np.stack([np.zeros((tk, M)), np.where(i <= j, 0.0, -MASK_BIG), np.where(i > j, 0.0, -MASK_BIG)])
    return (jnp.asarray(ov.T, BF16), jnp.asarray(ish, BF16), jnp.asarray(cbias), jnp.asarray(tbl, F32))


def _nsa_call(nq, kv, vt, kcx, vct, misct, q_norm_g, k_norm_g, B, S):
    T = B * S
    tq, tk, M = NSA_TQ, NSA_TK, NSA_M
    G = NSA_KV_HEADS
    nqt = S // tq
    NC = S // CMP_STRIDE
    NS = S // SLC_BLOCK
    ovt, ish, cbias, tbl = _nsa_consts(S)
    tile2 = lambda a: jnp.concatenate([a, a], axis=-1)
    row = lambda b, i: (b * nqt + i, 0)
    cst = lambda b, i: (0, 0)
    return pl.pallas_call(
        functools.partial(_nsa_kernel, S=S),
        grid=(B, nqt),
        in_specs=[pl.BlockSpec((tq, NSA_WIDTH), row),
                  pl.BlockSpec((S, KV_COLS), lambda b, i: (b, 0)),
                  pl.BlockSpec((None, S // tk, VT_ROWS, tk), lambda b, i: (b, 0, 0, 0)),
                  pl.BlockSpec((None, G, NC, LANES), lambda b, i: (b, 0, 0, 0)),
                  pl.BlockSpec((None, G, HEAD_DIM, NC), lambda b, i: (b, 0, 0, 0)),
                  pl.BlockSpec((LANES, tq), lambda b, i: (0, b * nqt + i)),
                  pl.BlockSpec((None, NC, tq), lambda b, i: (i, 0, 0)),
                  pl.BlockSpec((3, tk, M), lambda b, i: (0, 0, 0)),
                  pl.BlockSpec((1, LANES), cst),
                  pl.BlockSpec((3, LANES), cst),
                  pl.BlockSpec((NS, NC), cst),
                  pl.BlockSpec((NS, LANES), cst)],
        out_specs=pl.BlockSpec((NSA_WIDTH, tq), lambda b, i: (0, b * nqt + i)),
        out_shape=jax.ShapeDtypeStruct((NSA_WIDTH, T), F32),
        scratch_shapes=[pltpu.VMEM((G, S, LANES), BF16), pltpu.VMEM((G, S, LANES), BF16),
                        pltpu.SMEM((S // tk,), jnp.int32), pltpu.SMEM((1,), jnp.int32)],
        compiler_params=_cparams("parallel", "arbitrary"),
        name="nsa_attn",
    )(nq, kv, vt, kcx, vct, misct, cbias, tbl, tile2(q_norm_g), tile2(k_norm_g), ovt, ish)


def _out_proj_kernel(yg_ref, yn_ref, x_ref, g1_ref, sc_ref, sh_ref, ng_ref, wo_ref, rwt_ref, rb_ref,
                     x1_ref, h2_ref, gate_ref, gate_t_ref):
    mix = _dot(yg_ref[...], wo_ref[:GDN_WIDTH, :]) + _dot_tn(yn_ref[...], wo_ref[GDN_WIDTH:, :])
    x1 = x_ref[...] + g1_ref[...] * mix
    x1_ref[...] = x1
    h2 = _rms(x1, ng_ref[...]) * (1.0 + sc_ref[...]) + sh_ref[...]
    h2_ref[...] = h2.astype(BF16)

    tm = h2.shape[0]
    logits = _dot_nt_hi(rwt_ref[...], h2)
    score = _sigmoid(logits)
    biased = score + rb_ref[...]
    b = [biased[e:e + 1, :] for e in range(N_EXPERTS)]
    n_groups = N_EXPERTS // EXPERTS_PER_GROUP
    gscore = []
    for gi in range(n_groups):
        vals = b[gi * EXPERTS_PER_GROUP:(gi + 1) * EXPERTS_PER_GROUP]
        best = None
        for i in range(EXPERTS_PER_GROUP):
            for j in range(i + 1, EXPERTS_PER_GROUP):
                pair = vals[i] + vals[j]
                best = pair if best is None else jnp.maximum(best, pair)
        gscore.append(best)
    gbest = jnp.zeros((1, tm), jnp.int32)
    top = gscore[0]
    for gi in range(1, n_groups):
        better = gscore[gi] > top
        gbest = jnp.where(better, gi, gbest)
        top = jnp.where(better, gscore[gi], top)
    erow = lax.broadcasted_iota(jnp.int32, (N_EXPERTS, tm), 0)
    gates = jnp.zeros((N_EXPERTS, tm), F32)
    for e in range(N_EXPERTS):
        gi = e // EXPERTS_PER_GROUP
        rank = jnp.zeros((1, tm), F32)
        for e2 in range(gi * EXPERTS_PER_GROUP, (gi + 1) * EXPERTS_PER_GROUP):
            if e2 == e:
                continue
            beats = (b[e2] > b[e]) | ((b[e2] == b[e]) & (e2 < e))
            rank = rank + jnp.where(beats, 1.0, 0.0)
        chosen = (gbest == gi) & (rank < 2.0)
        ge = jnp.where(chosen, score[e:e + 1, :], 0.0)
        gates = gates + jnp.where(erow == e, ge, 0.0)
    gates = gates / jnp.sum(gates, axis=0, keepdims=True)
    pad_rows = jnp.zeros((8 - EXPERTS_PER_GROUP, tm), F32)
    for gi in range(n_groups):
        gate_t_ref[gi] = jnp.concatenate(
            [gates[gi * EXPERTS_PER_GROUP:(gi + 1) * EXPERTS_PER_GROUP, :], pad_rows], axis=0)
    ident = (lax.broadcasted_iota(jnp.int32, (N_EXPERTS, LANES), 0)
             == lax.broadcasted_iota(jnp.int32, (N_EXPERTS, LANES), 1))
    ident = jnp.where(ident, 1.0, 0.0).astype(BF16)
    hi, mid, lo = _split3(gates)
    dn = (((0,), (0,)), ((), ()))
    gate_ref[...] = (lax.dot_general(hi, ident, dn, preferred_element_type=F32)
                     + lax.dot_general(mid, ident, dn, preferred_element_type=F32)
                     + lax.dot_general(lo, ident, dn, preferred_element_type=F32))


def _out_proj_call(yg, yn, x2, g1, sc2, sh2, ng, wo, rwt, rb, S):
    T, D = x2.shape
    tm = 512
    per_b = S // tm
    row = lambda i: (i, 0)
    bat = lambda i: (i // per_b, 0, 0)
    cst = lambda i: (0, 0)
    return pl.pallas_call(
        _out_proj_kernel,
        grid=(T // tm,),
        in_specs=[pl.BlockSpec((tm, GDN_WIDTH), row), pl.BlockSpec((NSA_WIDTH, tm), lambda i: (0, i)),
                  pl.BlockSpec((tm, D), row),
                  pl.BlockSpec((None, 1, D), bat), pl.BlockSpec((None, 1, D), bat), pl.BlockSpec((None, 1, D), bat),
                  pl.BlockSpec((1, D), cst), pl.BlockSpec((D, D), cst),
                  pl.BlockSpec((N_EXPERTS, D), cst), pl.BlockSpec((N_EXPERTS, 1), cst)],
        out_specs=[pl.BlockSpec((tm, D), row), pl.BlockSpec((tm, D), row), pl.BlockSpec((tm, LANES), row),
                   pl.BlockSpec((N_EXPERTS // EXPERTS_PER_GROUP, 8, tm), lambda i: (0, 0, i))],
        out_shape=[jax.ShapeDtypeStruct((T, D), F32), jax.ShapeDtypeStruct((T, D), BF16),
                   jax.ShapeDtypeStruct((T, LANES), F32),
                   jax.ShapeDtypeStruct((N_EXPERTS // EXPERTS_PER_GROUP, 8, T), F32)],
        compiler_params=_cparams("parallel"),
        name="out_proj_router",
    )(yg, yn, x2, g1, sc2, sh2, ng, wo, rwt, rb)


MOE_TM = 1024
MOE_RB = 288


def _moe_kernel(h_ref, gates_ref, gt8_ref, tri_ref, trit_ref, x1_ref, g2_ref, wg_ref, wu_ref, wd_ref, o_ref):
    g = pl.program_id(1)
    tm, rb = MOE_TM, MOE_RB
    nlb = tm // LANES

    @pl.when(g == 0)
    def _():
        o_ref[...] = jnp.zeros_like(o_ref)

    gt8 = gt8_ref[...]
    in_row = jnp.sum(gt8, axis=0, keepdims=True) > 0.0
    first = lax.broadcasted_iota(jnp.int32, (8, LANES), 0) == 0
    off = jnp.zeros((1, 1), F32)
    keys = []
    for b in range(nlb):
        blk = in_row[:, b * LANES:(b + 1) * LANES]
        one = jnp.where(blk & first, 1.0, 0.0).astype(BF16)
        within = jnp.dot(one, tri_ref[...], preferred_element_type=F32)[0:1, :]
        keys.append(jnp.where(blk, within + off, -1.0))
        off = off + jnp.sum(jnp.where(blk, 1.0, 0.0), axis=1, keepdims=True)
    key_row = jnp.concatenate(keys, axis=1)
    n_g = jnp.sum(off).astype(jnp.int32)
    lane = lax.broadcasted_iota(jnp.int32, (tm, LANES), 1)
    mine = (lane >= g * EXPERTS_PER_GROUP) & (lane < (g + 1) * EXPERTS_PER_GROUP)
    in_col = jnp.sum(jnp.where(mine, gates_ref[...], 0.0), axis=1, keepdims=True) > 0.0
    off = jnp.zeros((1, 1), F32)
    keys = []
    for b in range(nlb):
        blk = in_col[b * LANES:(b + 1) * LANES, :]
        one = jnp.broadcast_to(jnp.where(blk, 1.0, 0.0), (LANES, LANES)).astype(BF16)
        within = jnp.dot(trit_ref[...], one, preferred_element_type=F32)[:, 0:1]
        keys.append(jnp.where(blk, within + off, -1.0))
        off = off + jnp.sum(jnp.where(blk, 1.0, 0.0), axis=0, keepdims=True)
    key_col = jnp.concatenate(keys, axis=0)

    src = lax.broadcasted_iota(jnp.int32, (8, LANES), 0)
    dst = lax.broadcasted_iota(jnp.int32, (8, LANES), 1)
    tn_dims = (((0,), (0,)), ((), ()))
    gcols = jnp.zeros((tm, LANES), F32)
    for p, piece in enumerate(_split3(gt8)):
        place = jnp.where((dst == src + EXPERTS_PER_GROUP * p) & (src < EXPERTS_PER_GROUP), 1.0, 0.0).astype(BF16)
        gcols = gcols + lax.dot_general(piece, place, tn_dims, preferred_element_type=F32)
    gcols = gcols.astype(BF16)

    def block(j, carry):
        r0 = (j * rb).astype(F32)
        sel = jnp.where(key_row - r0 == lax.broadcasted_iota(jnp.int32, (rb, tm), 0).astype(F32), 1.0, 0.0)
        sel = sel.astype(BF16)
        xb = jnp.dot(sel, h_ref[...], preferred_element_type=F32).astype(BF16)
        gp = jnp.dot(sel, gcols, preferred_element_type=F32)
        y = jnp.zeros((rb, o_ref.shape[1]), F32)
        for k in range(EXPERTS_PER_GROUP):
            gate = (gp[:, k:k + 1] + gp[:, EXPERTS_PER_GROUP + k:EXPERTS_PER_GROUP + k + 1]
                    + gp[:, 2 * EXPERTS_PER_GROUP + k:2 * EXPERTS_PER_GROUP + k + 1])
            act = (_silu(jnp.dot(xb, wg_ref[k], preferred_element_type=F32))
                   * jnp.dot(xb, wu_ref[k], preferred_element_type=F32))
            y = y + _dot(act * gate, wd_ref[k])
        sel_t = jnp.where(key_col - r0 == lax.broadcasted_iota(jnp.int32, (tm, rb), 1).astype(F32), 1.0, 0.0)
        o_ref[...] += jnp.dot(sel_t.astype(BF16), y.astype(BF16), preferred_element_type=F32)
        return carry

    lax.fori_loop(0, (n_g + rb - 1) // rb, block, 0)

    @pl.when(g == N_EXPERTS // EXPERTS_PER_GROUP - 1)
    def _():
        o_ref[...] = x1_ref[...] + g2_ref[...] * o_ref[...]


def _moe_call(h2, gates, gates_t, x1, g2, wg, wu, wd, S):
    T, D = x1.shape
    tm = MOE_TM
    per_b = S // tm
    n_groups = N_EXPERTS // EXPERTS_PER_GROUP
    i = np.arange(LANES)
    tri = jnp.asarray(i[:, None] < i[None, :], BF16)
    row = lambda i, g: (i, 0)
    cst = lambda i, g: (0, 0)
    wsel = lambda i, g: (g, 0, 0)
    return pl.pallas_call(
        _moe_kernel,
        grid=(T // tm, n_groups),
        in_specs=[pl.BlockSpec((tm, D), row), pl.BlockSpec((tm, LANES), row),
                  pl.BlockSpec((None, 8, tm), lambda i, g: (g, 0, i)),
                  pl.BlockSpec((LANES, LANES), cst), pl.BlockSpec((LANES, LANES), cst),
                  pl.BlockSpec((tm, D), row),
                  pl.BlockSpec((None, 1, D), lambda i, g: (i // per_b, 0, 0)),
                  pl.BlockSpec((EXPERTS_PER_GROUP, D, EXPERT_FF), wsel),
                  pl.BlockSpec((EXPERTS_PER_GROUP, D, EXPERT_FF), wsel),
                  pl.BlockSpec((EXPERTS_PER_GROUP, EXPERT_FF, D), wsel)],
        out_specs=pl.BlockSpec((tm, D), row),
        out_shape=jax.ShapeDtypeStruct((T, D), F32),
        compiler_params=_cparams("parallel", "arbitrary"),
        name="moe",
    )(h2, gates, gates_t, tri, tri.T, x1, g2, wg, wu, wd)


def kernel(x, c, ada_w, ada_b, norm1_g, norm2_g, w_in, gdn_conv_w, gdn_a_log, gdn_dt_bias, gdn_norm_g,
           nsa_q_norm_g, nsa_k_norm_g, cmp_pe, cmp_w1, cmp_b1, cmp_w2, cmp_b2, w_out, router_w, router_bias,
           exp_w_gate, exp_w_up, exp_w_down):
    B, S, D = x.shape
    L = ada_w.shape[0]
    T = B * S
    G = NSA_KV_HEADS
    NC = S // CMP_STRIDE
    N = S // GDN_CHUNK
    P = GDN_HEADS // GDN_HPB
    W = GDN_HPB * HEAD_DIM

    mod = _ada_call(c, ada_w, ada_b)
    rwt = router_w.T
    rb = router_bias.reshape(N_EXPERTS, 1)
    x2 = x.reshape(T, D)
    for l in range(L):
        m6 = mod[l].reshape(B, 6, 1, D)
        sh1, sc1, g1, sh2, sc2, g2 = (m6[:, i] for i in range(6))
        qkv, z, nq, kv, misc, misct, vt = _in_proj_call(
            x2, sc1, sh1, norm1_g[l].reshape(1, D), *_pad_w_in(w_in[l]), S)

        a_rows = misct[:GDN_HEADS].reshape(P, GDN_HPB, B, N, GDN_CHUNK).transpose(2, 0, 3, 1, 4).reshape(B, P, N, W)
        rep = lambda t: jnp.repeat(t, HEAD_DIM).reshape(1, GDN_WIDTH)
        y_gdn = _gdn_call(qkv, z, misc, a_rows, gdn_conv_w[l], rep(gdn_a_log[l]), rep(gdn_dt_bias[l]),
                          jnp.tile(gdn_norm_g[l], GDN_HPB).reshape(1, W), B, S)

        w2 = cmp_w2[l]
        kcx, vct = _cmp_call(
            kv, cmp_pe[l].reshape(2, 1, CMP_BLOCK * HEAD_DIM), cmp_w1[l].astype(BF16),
            cmp_b1[l].reshape(2, 1, CMP_HIDDEN), jnp.pad(w2[0], ((0, 0), (0, LANES - HEAD_DIM))).astype(BF16),
            w2[1].T.astype(BF16), jnp.pad(cmp_b2[l, 0], (0, LANES - HEAD_DIM)).reshape(1, LANES),
            cmp_b2[l, 1].reshape(HEAD_DIM, 1), jnp.pad(nsa_k_norm_g[l, 0], (0, LANES - HEAD_DIM)).reshape(1, LANES),
            B, S)
        y_nsa_t = _nsa_call(nq, kv, vt.reshape(B, S // NSA_TK, VT_ROWS, NSA_TK), kcx, vct, misct,
                            nsa_q_norm_g[l].reshape(1, HEAD_DIM), nsa_k_norm_g[l], B, S)

        x1, h2, gates, gates_t = _out_proj_call(y_gdn, y_nsa_t, x2, g1, sc2, sh2, norm2_g[l].reshape(1, D),
                                       w_out[l].astype(BF16), rwt, rb, S)
        x2 = _moe_call(h2, gates, gates_t, x1, g2, exp_w_gate[l].astype(BF16), exp_w_up[l].astype(BF16),
                       exp_w_down[l].astype(BF16), S)
    return x2.reshape(B, S, D)
```

```python
import functools

import numpy as np
import jax
import jax.numpy as jnp
from jax import lax
from jax.experimental import pallas as pl
from jax.experimental.pallas import tpu as pltpu

F32 = jnp.float32
BF16 = jnp.bfloat16

HEAD_DIM = 64
GDN_HEADS = 8
GDN_WIDTH = GDN_HEADS * HEAD_DIM
GDN_CONV = 4
GDN_CHUNK = 64
NSA_HEADS = 8
NSA_KV_HEADS = 2
NSA_GROUP = NSA_HEADS // NSA_KV_HEADS
NSA_WIDTH = NSA_HEADS * HEAD_DIM
NSA_KV_WIDTH = NSA_KV_HEADS * HEAD_DIM
CMP_BLOCK = 32
CMP_STRIDE = 16
CMP_HIDDEN = 256
SLC_BLOCK = 64
SLC_TOPN = 8
WINDOW = 512
N_EXPERTS = 16
EXPERTS_PER_GROUP = 4
EXPERT_FF = 512
RMS_EPS = 1e-6
NEG_BIG = -1e30
SEL_BIG = 1e9

LANES = 128
VMEM_LIMIT_BYTES = 56 * 1024 * 1024

C_QKV = 0
C_Z = 3 * GDN_WIDTH
C_NQ = C_Z + GDN_WIDTH
C_KV = C_NQ + NSA_WIDTH
KV_COLS = 4 * NSA_KV_WIDTH
C_MISC = C_KV + KV_COLS
IN_PAD_COLS = C_MISC + LANES
MISC_B0 = GDN_HEADS
MISC_GATE0 = 16
VT_SLAB = HEAD_DIM + 16
VT_ROWS = 2 * NSA_KV_HEADS * VT_SLAB


def _cparams(*sem):
    return pltpu.CompilerParams(dimension_semantics=sem, vmem_limit_bytes=VMEM_LIMIT_BYTES)


def _dot(a, b):
    return jnp.dot(a.astype(BF16), b.astype(BF16), preferred_element_type=F32)


def _dot_nt(a, b):
    return lax.dot_general(a.astype(BF16), b.astype(BF16), (((1,), (1,)), ((), ())),
                           preferred_element_type=F32)


def _dot_tn(a, b):
    return lax.dot_general(a.astype(BF16), b.astype(BF16), (((0,), (0,)), ((), ())),
                           preferred_element_type=F32)


def _split2(x):
    hi = x.astype(BF16)
    lo = (x - hi.astype(F32)).astype(BF16)
    return hi, lo


def _split3(x):
    hi = x.astype(BF16)
    r = x - hi.astype(F32)
    mid = r.astype(BF16)
    lo = (r - mid.astype(F32)).astype(BF16)
    return hi, mid, lo


def _dot_hi(a, b):
    ah, al = _split2(a)
    bh, bl = _split2(b)
    return (jnp.dot(ah, bh, preferred_element_type=F32) + jnp.dot(al, bh, preferred_element_type=F32)
            + jnp.dot(ah, bl, preferred_element_type=F32))


def _dot_nt_hi(a, b):
    ah, al = _split2(a)
    bh, bl = _split2(b)
    dn = (((1,), (1,)), ((), ()))
    return (lax.dot_general(ah, bh, dn, preferred_element_type=F32)
            + lax.dot_general(al, bh, dn, preferred_element_type=F32)
            + lax.dot_general(ah, bl, dn, preferred_element_type=F32))


def _dot_exact_rhs(a, b_bf16):
    hi, mid, lo = _split3(a)
    return (jnp.dot(hi, b_bf16, preferred_element_type=F32) + jnp.dot(mid, b_bf16, preferred_element_type=F32)
            + jnp.dot(lo, b_bf16, preferred_element_type=F32))


def _rms(x, g):
    return x * lax.rsqrt(jnp.mean(x * x, axis=-1, keepdims=True) + RMS_EPS) * g


def _sigmoid(x):
    return 0.5 * (jnp.tanh(0.5 * x) + 1.0)


def _silu(x):
    return x * _sigmoid(x)


def _ada_kernel(c_ref, w_ref, b_ref, o_ref):
    o_ref[...] = _dot_hi(_silu(c_ref[...]), w_ref[...]) + b_ref[...]


def _ada_call(c, ada_w, ada_b):
    L, D, N = ada_w.shape
    B = c.shape[0]
    tn = 1024
    return pl.pallas_call(
        _ada_kernel,
        grid=(L, N // tn),
        in_specs=[pl.BlockSpec((B, D), lambda l, j: (0, 0)),
                  pl.BlockSpec((None, D, tn), lambda l, j: (l, 0, j)),
                  pl.BlockSpec((None, 1, tn), lambda l, j: (l, 0, j))],
        out_specs=pl.BlockSpec((None, B, tn), lambda l, j: (l, 0, j)),
        out_shape=jax.ShapeDtypeStruct((L, B, N), F32),
        compiler_params=_cparams("parallel", "parallel"),
        name="ada_mod",
    )(c, ada_w, ada_b.reshape(L, 1, N))


IN_SEGS = ((C_QKV, 3 * GDN_WIDTH), (C_Z, GDN_WIDTH), (C_NQ, NSA_WIDTH), (C_KV, KV_COLS), (C_MISC, LANES))


def _in_proj_kernel(x_ref, sc_ref, sh_ref, g_ref, w_ref, wmt_ref, wvt_ref, *o_refs):
    h = _rms(x_ref[...], g_ref[...]) * (1.0 + sc_ref[...]) + sh_ref[...]
    hb = h.astype(BF16)
    for (c0, n), o_ref in zip(IN_SEGS, o_refs[:-2]):
        o_ref[...] = jnp.dot(hb, w_ref[:, c0:c0 + n], preferred_element_type=F32)
    nt_dims = (((1,), (1,)), ((), ()))
    o_refs[-2][...] = lax.dot_general(wmt_ref[...], hb, nt_dims, preferred_element_type=F32)
    vt = lax.dot_general(wvt_ref[...], hb, nt_dims, preferred_element_type=F32).astype(BF16)
    ones = jnp.ones((VT_SLAB - HEAD_DIM, LANES), BF16)
    for j in range(o_refs[-1].shape[0]):
        for st in range(2 * NSA_KV_HEADS):
            o_refs[-1][j, st * VT_SLAB:st * VT_SLAB + HEAD_DIM, :] = vt[st * HEAD_DIM:(st + 1) * HEAD_DIM,
                                                                         j * LANES:(j + 1) * LANES]
            o_refs[-1][j, st * VT_SLAB + HEAD_DIM:(st + 1) * VT_SLAB, :] = ones


def _in_proj_call(x2, sc, sh, g, w_pad, wvt, S):
    T, D = x2.shape
    tm = 512
    per_b = S // tm
    row = lambda i: (i, 0)
    bat = lambda i: (i // per_b, 0, 0)
    return pl.pallas_call(
        _in_proj_kernel,
        grid=(T // tm,),
        in_specs=[pl.BlockSpec((tm, D), row),
                  pl.BlockSpec((None, 1, D), bat),
                  pl.BlockSpec((None, 1, D), bat),
                  pl.BlockSpec((1, D), lambda i: (0, 0)),
                  pl.BlockSpec((D, IN_PAD_COLS), lambda i: (0, 0)),
                  pl.BlockSpec((LANES, D), lambda i: (0, 0)),
                  pl.BlockSpec((2 * NSA_KV_WIDTH, D), lambda i: (0, 0))],
        out_specs=([pl.BlockSpec((tm, n), row) for _, n in IN_SEGS]
                   + [pl.BlockSpec((LANES, tm), lambda i: (0, i)),
                      pl.BlockSpec((tm // LANES, VT_ROWS, LANES), lambda i: (i, 0, 0))]),
        out_shape=([jax.ShapeDtypeStruct((T, n), F32) for _, n in IN_SEGS]
                   + [jax.ShapeDtypeStruct((LANES, T), F32),
                      jax.ShapeDtypeStruct((T // LANES, VT_ROWS, LANES), BF16)]),
        compiler_params=_cparams("parallel"),
        name="in_proj",
    )(x2, sc, sh, g, w_pad, w_pad[:, C_MISC:].T, wvt)


def _pad_w_in(w_in):
    D = w_in.shape[0]
    W = GDN_WIDTH
    o = 0
    gq, gk, gv, gz = (w_in[:, o + i * W:o + (i + 1) * W] for i in range(4))
    o += 4 * W
    ga = w_in[:, o:o + GDN_HEADS]
    gb = w_in[:, o + GDN_HEADS:o + 2 * GDN_HEADS]
    o += 2 * GDN_HEADS
    nq = w_in[:, o:o + NSA_WIDTH]
    o += NSA_WIDTH
    kc, vc, ks, vs, kw, vw = (w_in[:, o + i * NSA_KV_WIDTH:o + (i + 1) * NSA_KV_WIDTH] for i in range(6))
    o += 6 * NSA_KV_WIDTH
    gates = w_in[:, o:o + 3 * NSA_HEADS]
    misc = jnp.zeros((D, LANES), w_in.dtype)
    misc = misc.at[:, 0:GDN_HEADS].set(ga).at[:, MISC_B0:MISC_B0 + GDN_HEADS].set(gb)
    misc = misc.at[:, MISC_GATE0:MISC_GATE0 + 3 * NSA_HEADS].set(gates)
    cols = [gq, gk, gv, gz, nq, kc, vc, ks, kw, misc]
    return jnp.concatenate(cols, axis=1).astype(BF16), jnp.concatenate([vs, vw], axis=1).T.astype(BF16)


GDN_HPB = 2
GDN_NB = 32


def _gdn_kernel(q_ref, k_ref, v_ref, z_ref, misc_ref, ar_ref, cwq_ref, cwk_ref, cwv_ref,
                alog_ref, dtb_ref, alogr_ref, dtbr_ref, ng_ref, o_ref,
                qs, ks, kbs, xs, gcs, grs, ps, qq, qks, os_, rs, q2s, sev, *, S):
    C = GDN_CHUNK
    N = S // C
    W = GDN_HPB * HEAD_DIM
    row = lax.broadcasted_iota(jnp.int32, (S, W), 0)
    row8 = lax.broadcasted_iota(jnp.int32, (8, W), 0)

    def conv_silu(x_ref, w):
        x = x_ref[...]
        y = x * w[GDN_CONV - 1:GDN_CONV, :]
        for s in range(1, GDN_CONV):
            head = jnp.where(row8 >= s, pltpu.roll(x[:8], s, axis=0), 0.0)
            xs = jnp.concatenate([head, x_ref[pl.ds(8 - s, S - 8), :]], axis=0)
            y = y + xs * w[GDN_CONV - 1 - s:GDN_CONV - s, :]
        return _silu(y)

    bd = jnp.where(lax.broadcasted_iota(jnp.int32, (W, W), 0) // HEAD_DIM
                   == lax.broadcasted_iota(jnp.int32, (W, W), 1) // HEAD_DIM, 1.0, 0.0).astype(BF16)

    def head_sumsq(x):
        hi, lo = _split2(x * x)
        return jnp.dot(hi, bd, preferred_element_type=F32) + jnp.dot(lo, bd, preferred_element_type=F32)

    def head_l2(x):
        return x * lax.rsqrt(head_sumsq(x) + RMS_EPS)

    q = head_l2(conv_silu(q_ref, cwq_ref[...])) * (HEAD_DIM ** -0.5)
    k = head_l2(conv_silu(k_ref, cwk_ref[...]))
    v = conv_silu(v_ref, cwv_ref[...])

    hd0 = pl.program_id(1) * GDN_HPB
    src = lax.broadcasted_iota(jnp.int32, (LANES, W), 0)
    dst = lax.broadcasted_iota(jnp.int32, (LANES, W), 1) // HEAD_DIM + hd0
    a_exp = _dot_exact_rhs(misc_ref[...], jnp.where(src == dst, 1.0, 0.0).astype(BF16))
    b_exp = _dot_exact_rhs(misc_ref[...], jnp.where(src == dst + MISC_B0, 1.0, 0.0).astype(BF16))
    beta = _sigmoid(b_exp)

    def log_decay(a, alog, dtb):
        xx = a + dtb
        sp = jnp.maximum(xx, 0.0) + jnp.log(1.0 + jnp.exp(-jnp.abs(xx)))
        return -jnp.exp(alog) * sp

    gc = log_decay(a_exp, alog_ref[...], dtb_ref[...])
    pos = row % C
    for s in (1, 2, 4, 8, 16, 32):
        gc = gc + jnp.where(pos >= s, pltpu.roll(gc, s, axis=0), 0.0)
    gr = log_decay(ar_ref[...], alogr_ref[...], dtbr_ref[...])
    lpos = lax.broadcasted_iota(jnp.int32, (N, W), 1) % C
    for s in (1, 2, 4, 8, 16, 32):
        gr = gr + jnp.where(lpos >= s, pltpu.roll(gr, s, axis=1), 0.0)

    kb = k * beta
    vb = v * beta
    kbe = kb * jnp.exp(gc)
    for h in range(GDN_HPB):
        sl = slice(h * HEAD_DIM, (h + 1) * HEAD_DIM)
        qs[h] = q[:, sl]
        ks[h] = k[:, sl]
        kbs[h] = kb[:, sl]
        gcs[h] = gc[:, sl]
        for n in range(N):
            grs[h, n] = jnp.broadcast_to(gr[n:n + 1, sl], (8, HEAD_DIM))
        xs[h] = jnp.concatenate([kbe[:, sl], vb[:, sl]], axis=1)

    NB = GDN_NB
    ci3 = lax.broadcasted_iota(jnp.int32, (1, C, C), 1)
    cj3 = lax.broadcasted_iota(jnp.int32, (1, C, C), 2)
    eye3 = jnp.where(ci3 == cj3, 1.0, 0.0)

    def bmm(a, b):
        return lax.dot_general(a.astype(BF16), b.astype(BF16), (((2,), (1,)), ((0,), (0,))),
                               preferred_element_type=F32)

    def bmm_nt(a, b):
        return lax.dot_general(a.astype(BF16), b.astype(BF16), (((2,), (2,)), ((0,), (0,))),
                               preferred_element_type=F32)

    def intra(nb, carry):
        rows = pl.ds(pl.multiple_of(nb * (NB * C), NB * C), NB * C)
        for h in range(GDN_HPB):
            kk = ks[h, rows, :].reshape(NB, C, HEAD_DIM)
            kbq = jnp.concatenate([kbs[h, rows, :].reshape(NB, C, HEAD_DIM),
                                   qs[h, rows, :].reshape(NB, C, HEAD_DIM)], axis=1)
            a2 = bmm_nt(kbq, kk)
            gcol = gcs[h, rows, :].reshape(NB, C, HEAD_DIM)
            grow = grs[h, pl.ds(nb * NB, NB), 0:1, :]
            decay = jnp.exp(jnp.where(ci3 >= cj3, gcol - grow, NEG_BIG))
            low = jnp.where(ci3 > cj3, a2[:, :C] * decay, 0.0)
            qks[h, rows, :] = (a2[:, C:] * decay).reshape(NB * C, C)
            tinv = eye3 - jnp.where(ci3 // 2 == cj3 // 2, low, 0.0)
            for lv in range(1, 6):
                s = 2 ** lv
                off = jnp.where((ci3 // (2 * s) == cj3 // (2 * s)) & (ci3 // s != cj3 // s), low, 0.0)
                tinv = tinv - bmm(tinv, bmm(off, tinv))
            x = bmm(tinv, xs[h, rows, :].reshape(NB, C, 2 * HEAD_DIM))
            xs[h, rows, :] = x.reshape(NB * C, 2 * HEAD_DIM)
            kd = kk * jnp.exp(gcol[:, C - 1:C, :] - gcol)
            pq = lax.dot_general(kd.astype(BF16), x.astype(BF16), (((1,), (1,)), ((0,), (0,))),
                                 preferred_element_type=F32)
            pn = pq[:, :, :HEAD_DIM]
            qn = pq[:, :, HEAD_DIM:]
            ps[h, rows, :] = pn.reshape(NB * C, HEAD_DIM)
            qq[h, rows, :] = qn.reshape(NB * C, HEAD_DIM)
            cd = jnp.exp(gcol[:, C - 1:C, :]).reshape(NB // 2, 2, 1, HEAD_DIM)
            pn = pn.reshape(NB // 2, 2, HEAD_DIM, HEAD_DIM)
            qn = qn.reshape(NB // 2, 2, HEAD_DIM, HEAD_DIM)
            p0, p1, q0, q1, c0, c1 = pn[:, 0], pn[:, 1], qn[:, 0], qn[:, 1], cd[:, 0], cd[:, 1]
            t = bmm(p1, pq.reshape(NB // 2, 2, HEAD_DIM, 2 * HEAD_DIM)[:, 0])
            prow = pl.ds(pl.multiple_of(nb * (NB // 2 * C), NB // 2 * C), NB // 2 * C)
            rs[h, prow, :] = (c1 * p0 + c0 * p1 - t[:, :, :HEAD_DIM]).reshape(NB // 2 * C, HEAD_DIM)
            q2s[h, prow, :] = (c1 * q0 - t[:, :, HEAD_DIM:] + q1).reshape(NB // 2 * C, HEAD_DIM)
        return carry

    lax.fori_loop(0, N // NB, intra, 0)

    def inter(k, states):
        sl = pl.ds(pl.multiple_of(k * C, C), C)
        new_states = []
        for h in range(GDN_HPB):
            st = states[h]
            sev[h, k] = st
            gsum = gcs[h, pl.ds(2 * k * C + C - 1, 1), :] + gcs[h, pl.ds(2 * k * C + 2 * C - 1, 1), :]
            new_states.append(st * jnp.exp(gsum) - _dot_hi(rs[h, sl, :], st) + q2s[h, sl, :])
        return tuple(new_states)

    lax.fori_loop(0, N // 2, inter, tuple(jnp.zeros((HEAD_DIM, HEAD_DIM), F32) for _ in range(GDN_HPB)))

    def outputs(nb, carry):
        rows = pl.ds(pl.multiple_of(nb * (NB * C), NB * C), NB * C)
        for h in range(GDN_HPB):
            gcol = gcs[h, rows, :].reshape(NB, C, HEAD_DIM)
            s0 = sev[h, pl.ds(nb * (NB // 2), NB // 2)]
            p0 = ps[h, rows, :].reshape(NB // 2, 2, HEAD_DIM, HEAD_DIM)[:, 0]
            q0 = qq[h, rows, :].reshape(NB // 2, 2, HEAD_DIM, HEAD_DIM)[:, 0]
            c0 = jnp.exp(gcol[:, C - 1:C, :]).reshape(NB // 2, 2, 1, HEAD_DIM)[:, 0]
            s1 = c0 * s0 - bmm(p0, s0) + q0
            st = jnp.stack([s0, s1], axis=1).reshape(NB, HEAD_DIM, HEAD_DIM)
            x = xs[h, rows, :].reshape(NB, C, 2 * HEAD_DIM)
            qd = qs[h, rows, :].reshape(NB, C, HEAD_DIM) * jnp.exp(gcol)
            t1 = bmm(jnp.concatenate([x[:, :, :HEAD_DIM], qd], axis=1), st)
            vn = x[:, :, HEAD_DIM:] - t1[:, :C]
            o = t1[:, C:] + bmm(qks[h, rows, :].reshape(NB, C, C), vn)
            os_[h, rows, :] = o.reshape(NB * C, HEAD_DIM)
        return carry

    lax.fori_loop(0, N // NB, outputs, 0)

    o = jnp.concatenate([os_[h] for h in range(GDN_HPB)], axis=1)
    o = o * lax.rsqrt(head_sumsq(o) * (1.0 / HEAD_DIM) + RMS_EPS) * ng_ref[...]
    o_ref[...] = o * _silu(z_ref[...])


def _gdn_call(qkv, z, misc, a_rows, conv_w, alog_exp, dtb_exp, norm_g, B, S):
    T = B * S
    W = GDN_HPB * HEAD_DIM
    P = GDN_HEADS // GDN_HPB
    N = S // GDN_CHUNK
    col = lambda off: (lambda b, p: (b, off + p))
    par = lambda off: (lambda b, p: (0, off + p))
    hs = lambda n: pltpu.VMEM((GDN_HPB, S, n), F32)
    return pl.pallas_call(
        functools.partial(_gdn_kernel, S=S),
        grid=(B, P),
        in_specs=[pl.BlockSpec((S, W), col(0)), pl.BlockSpec((S, W), col(P)), pl.BlockSpec((S, W), col(2 * P)),
                  pl.BlockSpec((S, W), col(0)), pl.BlockSpec((S, LANES), lambda b, p: (b, 0)),
                  pl.BlockSpec((None, None, N, W), lambda b, p: (b, p, 0, 0)),
                  pl.BlockSpec((GDN_CONV, W), par(0)), pl.BlockSpec((GDN_CONV, W), par(P)),
                  pl.BlockSpec((GDN_CONV, W), par(2 * P)),
                  pl.BlockSpec((1, W), par(0)), pl.BlockSpec((1, W), par(0)),
                  pl.BlockSpec((1, W), par(0)), pl.BlockSpec((1, W), par(0)),
                  pl.BlockSpec((1, W), lambda b, p: (0, 0))],
        out_specs=pl.BlockSpec((S, W), col(0)),
        out_shape=jax.ShapeDtypeStruct((T, GDN_WIDTH), F32),
        scratch_shapes=[hs(HEAD_DIM), hs(HEAD_DIM), hs(HEAD_DIM), hs(2 * HEAD_DIM), hs(HEAD_DIM),
                        pltpu.VMEM((GDN_HPB, N, 8, HEAD_DIM), F32),
                        hs(HEAD_DIM), hs(HEAD_DIM), hs(HEAD_DIM), hs(HEAD_DIM),
                        pltpu.VMEM((GDN_HPB, S // 2, HEAD_DIM), F32), pltpu.VMEM((GDN_HPB, S // 2, HEAD_DIM), F32),
                        pltpu.VMEM((GDN_HPB, N // 2, HEAD_DIM, HEAD_DIM), F32)],
        compiler_params=_cparams("parallel", "parallel"),
        name="gdn",
    )(qkv, qkv, qkv, z, misc, a_rows, conv_w, conv_w, conv_w,
      alog_exp, dtb_exp, alog_exp, dtb_exp, norm_g)


def _cmp_kernel(kc_in_ref, vc_in_ref, pe_ref, w1_ref, b1_ref, w2k_ref, w2vt_ref, b2k_ref, b2v_ref, kg_ref, kc_ref, vct_ref):
    half = (CMP_BLOCK // 2) * HEAD_DIM
    nrows = kc_in_ref.shape[0] // CMP_STRIDE
    lane = lax.broadcasted_iota(jnp.int32, (nrows, LANES), 1)
    cend = lax.broadcasted_iota(jnp.int32, (nrows, LANES), 0) * CMP_STRIDE + (CMP_BLOCK - 1)
    ext = jnp.where(lane == HEAD_DIM, cend // LANES, jnp.where(lane == HEAD_DIM + 1, cend % LANES, 0)).astype(F32)

    def token_groups(x_ref):
        cols = [[] for _ in range(NSA_KV_HEADS)]
        for m in range(CMP_STRIDE // 2):
            ev = x_ref[pl.ds(2 * m, nrows, stride=CMP_STRIDE), :]
            od = x_ref[pl.ds(2 * m + 1, nrows, stride=CMP_STRIDE), :]
            cols[0].append(jnp.where(lane < HEAD_DIM, ev, pltpu.roll(od, HEAD_DIM, axis=1)))
            cols[1].append(jnp.where(lane < HEAD_DIM, pltpu.roll(ev, HEAD_DIM, axis=1), od))
        return [jnp.concatenate(c, axis=1) for c in cols]

    def hidden(x, j):
        pe = pe_ref[j]
        p = _dot(x + pe[:, :half], w1_ref[j, :half, :])
        q = _dot(x + pe[:, half:], w1_ref[j, half:, :])
        return _silu(p + pltpu.roll(q, nrows - 1, axis=0) + b1_ref[j])

    for g, x in enumerate(token_groups(kc_in_ref)):
        kc = _dot(hidden(x, 0), w2k_ref[...]) + b2k_ref[...]
        ms = jnp.sum(kc * kc, axis=-1, keepdims=True) * (1.0 / HEAD_DIM)
        kc_ref[g] = jnp.where(lane < HEAD_DIM, kc * lax.rsqrt(ms + RMS_EPS) * kg_ref[...], ext)
    for g, x in enumerate(token_groups(vc_in_ref)):
        vct_ref[g] = _dot_nt(w2vt_ref[...], hidden(x, 1)) + b2v_ref[...]


def _cmp_call(kv, pe, w1, b1, w2k, w2vt, b2k, b2v, kg, B, S):
    G = NSA_KV_HEADS
    NC = S // CMP_STRIDE
    full = lambda a: pl.BlockSpec(a.shape, lambda b: (0,) * a.ndim)
    return pl.pallas_call(
        _cmp_kernel,
        grid=(B,),
        in_specs=[pl.BlockSpec((S, NSA_KV_WIDTH), lambda b: (b, 0)), pl.BlockSpec((S, NSA_KV_WIDTH), lambda b: (b, 1))]
        + [full(a) for a in (pe, w1, b1, w2k, w2vt, b2k, b2v, kg)],
        out_specs=[pl.BlockSpec((None, G, NC, LANES), lambda b: (b, 0, 0, 0)),
                   pl.BlockSpec((None, G, HEAD_DIM, NC), lambda b: (b, 0, 0, 0))],
        out_shape=[jax.ShapeDtypeStruct((B, G, NC, LANES), F32), jax.ShapeDtypeStruct((B, G, HEAD_DIM, NC), F32)],
        compiler_params=_cparams("parallel"),
        name="nsa_compress",
    )(kv, kv, pe, w1, b1, w2k, w2vt, b2k, b2v, kg)


NSA_TQ = 128
NSA_TK = 128
NSA_M = NSA_GROUP * NSA_TQ
MASK_BIG = 2.0 ** 100
SEL_LANE0 = HEAD_DIM + 2


def _nsa_kernel(q_ref, kv_ref, vt_ref, kcx_ref, vct_ref, misct_ref, cbias_ref, tbl_ref, qg_ref, kg_ref,
                ovt_ref, ish_ref, o_ref, ks_s, kw_s, live, n_live, *, S):
    tq, tk, M = NSA_TQ, NSA_TK, NSA_M
    G, R = NSA_KV_HEADS, NSA_GROUP
    NS = S // SLC_BLOCK
    n_top = min(SLC_TOPN, NS)
    qi = pl.program_id(1)
    t0 = qi * tq
    nt_dims = (((1,), (1,)), ((), ()))

    @pl.when(qi == 0)
    def _():
        rowi = lax.broadcasted_iota(jnp.int32, (S, LANES), 0)
        lane = lax.broadcasted_iota(jnp.int32, (S, LANES), 1)
        ext = jnp.where(lane == HEAD_DIM, rowi // LANES, jnp.where(lane == HEAD_DIM + 1, rowi % LANES, 0)).astype(F32)
        ext_slc = jnp.where(lane - SEL_LANE0 == rowi // SLC_BLOCK, -MASK_BIG, ext)

        def prep(c0, g, gain, extra):
            kraw = kv_ref[:, c0:c0 + LANES]
            if g == 1:
                kraw = pltpu.roll(kraw, HEAD_DIM, axis=1)
            ms = jnp.sum(jnp.where(lane < HEAD_DIM, kraw * kraw, 0.0), axis=-1, keepdims=True) * (1.0 / HEAD_DIM)
            return jnp.where(lane < HEAD_DIM, kraw * lax.rsqrt(ms + RMS_EPS) * gain, extra).astype(BF16)

        for g in range(G):
            ks_s[g] = prep(2 * NSA_KV_WIDTH, g, kg_ref[1:2, :], ext_slc)
            kw_s[g] = prep(3 * NSA_KV_WIDTH, g, kg_ref[2:3, :], ext)

    lane_q = lax.broadcasted_iota(jnp.int32, (tq, LANES), 1)
    gate_sig = _sigmoid(misct_ref[MISC_GATE0:MISC_GATE0 + 3 * NSA_HEADS, :])
    kt_d = t0 // tk

    def score_tiles(streams, kt, biased):
        rows = pl.ds(pl.multiple_of(kt * tk, tk), tk)
        out = []
        for qpp, k_s, g, _, idx_fn in streams:
            s = lax.dot_general(k_s[g, rows, :], qpp, nt_dims, preferred_element_type=F32)
            out.append(s + tbl_ref[idx_fn(kt)] if biased else s)
        return tuple(out)

    def softmax_pv(streams, kt, scores, states):
        soft = []
        for s, (m, _) in zip(scores, states):
            m_new = jnp.maximum(m, jnp.max(s, axis=0, keepdims=True))
            soft.append((m_new, jnp.exp(m - m_new), jnp.exp(s - m_new).astype(BF16)))
        pvs = [jnp.dot(vt_ref[kt, st[3]:st[3] + VT_SLAB, :], sf[2], preferred_element_type=F32)
               for st, sf in zip(streams, soft)]
        return tuple((m_new, alpha * acc + pv) for (m_new, alpha, _), (_, acc), pv in zip(soft, states, pvs))

    flash_init = (jnp.full((1, M), NEG_BIG, F32), jnp.zeros((VT_SLAB, M), F32))

    qpps, o_cmps, unsel = [], [], []
    for g in range(G):
        qrows = []
        for r in range(R):
            hh = g * R + r
            qpair = q_ref[:, (hh // 2) * LANES:(hh // 2 + 1) * LANES]
            if hh % 2 == 1:
                qpair = pltpu.roll(qpair, HEAD_DIM, axis=1)
            ms = jnp.sum(jnp.where(lane_q < HEAD_DIM, qpair * qpair, 0.0), axis=-1, keepdims=True) * (1.0 / HEAD_DIM)
            qn = qpair * lax.rsqrt(ms + RMS_EPS) * qg_ref[...] * (HEAD_DIM ** -0.5)
            slope = 2.0 ** (-8.0 * (hh + 1) / NSA_HEADS)
            ex = jnp.where(lane_q == HEAD_DIM, slope * LANES, jnp.where(lane_q == HEAD_DIM + 1, slope, 0.0))
            qrows.append(jnp.where(lane_q < HEAD_DIM, qn, ex))
        qbase = jnp.concatenate(qrows, axis=0)

        cb = cbias_ref[...]
        s = (lax.dot_general(kcx_ref[g].astype(BF16), qbase.astype(BF16), nt_dims, preferred_element_type=F32)
             + jnp.concatenate([cb] * R, axis=1))
        mx = jnp.maximum(jnp.max(s, axis=0, keepdims=True), NEG_BIG)
        e = jnp.exp(s - mx)
        l = jnp.sum(e, axis=0, keepdims=True)
        p_cmp = e / jnp.where(l > 0.0, l, 1.0)
        o_cmps.append(_dot(vct_ref[g], p_cmp))

        psum = p_cmp[:, 0:tq]
        for r in range(1, R):
            psum = psum + p_cmp[:, r * tq:(r + 1) * tq]
        hi, mid, lo = _split3(psum)
        ovt = ovt_ref[...]
        imp = (jnp.dot(ovt, hi, preferred_element_type=F32) + jnp.dot(ovt, mid, preferred_element_type=F32)
               + jnp.dot(ovt, lo, preferred_element_type=F32))
        blk = lax.broadcasted_iota(jnp.int32, (NS, tq), 0)
        cur = (t0 + lax.broadcasted_iota(jnp.int32, (NS, tq), 1)) // SLC_BLOCK
        valid = blk <= cur
        forced = (blk == 0) | (blk == cur) | (blk == cur - 1)
        score = jnp.where(forced, SEL_BIG, jnp.where(valid, imp, -SEL_BIG))
        rank = jnp.zeros((NS, tq), F32)
        for j in range(NS):
            cj = score[j:j + 1, :]
            beats = (cj > score) | ((cj == score) & (blk > j))
            rank = rank + jnp.where(beats, 1.0, 0.0)
        nsel = jnp.where((rank < n_top) & valid, 0.0, 1.0)
        unsel.append(jnp.min(nsel, axis=1, keepdims=True))
        nsel = nsel.astype(BF16)
        nsel_q = lax.dot_general(nsel, ish_ref[...], (((0,), (0,)), ((), ())),
                                 preferred_element_type=F32)
        in_sel = (lane_q >= SEL_LANE0) & (lane_q < SEL_LANE0 + NS)
        qpps.append(jnp.concatenate([jnp.where(in_sel, nsel_q, qrows[r]) for r in range(R)], axis=0).astype(BF16))

    slc_idx = lambda kt: jnp.where(kt == kt_d, 1, 0)
    win_lo = kt_d - WINDOW // tk
    win_idx = lambda kt: jnp.where(kt == kt_d, 1, jnp.where(kt == win_lo, 2, 0))
    slc_streams = [(qpps[g], ks_s, g, g * VT_SLAB, slc_idx) for g in range(G)]
    win_streams = [(qpps[g], kw_s, g, (G + g) * VT_SLAB, win_idx) for g in range(G)]
    kt_w = jnp.maximum(win_lo, 0)

    blocks_per_tile = tk // SLC_BLOCK
    unsel_any = unsel[0]
    for g in range(1, G):
        unsel_any = jnp.minimum(unsel_any, unsel[g])
    blk_tile = lax.broadcasted_iota(jnp.int32, (NS, 1), 0) // blocks_per_tile
    bit = jnp.zeros((1, 1), F32)
    for kt in range(S // tk - WINDOW // tk):
        in_tile = jnp.min(jnp.where(blk_tile == kt, unsel_any, 1.0), axis=0, keepdims=True)
        bit = bit + jnp.where(in_tile < 0.5, float(2 ** kt), 0.0)
    live_bits = jnp.max(bit).astype(jnp.int32)
    n_live[0] = 0
    live[0] = 0
    for kt in range(S // tk - WINDOW // tk):
        @pl.when((((live_bits >> kt) & 1) == 1) & (kt < kt_w))
        def _():
            live[n_live[0]] = kt
            n_live[0] = n_live[0] + 1

    n_slc = n_live[0]

    def slc_only(i, carry):
        scores, states = carry
        nxt = score_tiles(slc_streams, live[jnp.minimum(i + 1, n_slc - 1)], False)
        return nxt, softmax_pv(slc_streams, live[i], scores, states)

    first = jnp.where(n_slc > 0, live[0], 0)
    _, st_slc = lax.fori_loop(0, n_slc, slc_only, (score_tiles(slc_streams, first, False), (flash_init,) * G))
    both = slc_streams + win_streams
    st_all = lax.fori_loop(kt_w, kt_d + 1, lambda kt, st: softmax_pv(both, kt, score_tiles(both, kt, True), st),
                           st_slc + (flash_init,) * G)
    o_all = []
    for _, acc in st_all:
        den = acc[HEAD_DIM:HEAD_DIM + 1, :]
        o_all.append(acc[:HEAD_DIM] / jnp.where(den > 0.0, den, 1.0))
    o_slcs, o_wins = o_all[:G], o_all[G:]

    for g in range(G):
        for r in range(R):
            hh = g * R + r
            cs = slice(r * tq, (r + 1) * tq)
            o_ref[hh * HEAD_DIM:(hh + 1) * HEAD_DIM, :] = (
                gate_sig[hh:hh + 1, :] * o_cmps[g][:, cs]
                + gate_sig[NSA_HEADS + hh:NSA_HEADS + hh + 1, :] * o_slcs[g][:, cs]
                + gate_sig[2 * NSA_HEADS + hh:2 * NSA_HEADS + hh + 1, :] * o_wins[g][:, cs])


def _nsa_consts(S):
    tq, tk, M = NSA_TQ, NSA_TK, NSA_M
    NC = S // CMP_STRIDE
    NS = S // SLC_BLOCK
    cs = np.arange(NC)[:, None] * CMP_STRIDE
    ss = np.arange(NS)[None, :] * SLC_BLOCK
    ov = np.clip(np.minimum(cs + CMP_BLOCK, ss + SLC_BLOCK) - np.maximum(cs, ss), 0, None) / CMP_BLOCK
    ov[NC - 1:] = 0.0
    ish = np.zeros((NS, LANES), np.float32)
    ish[np.arange(NS), SEL_LANE0 + np.arange(NS)] = 1.0
    t = np.arange(S).reshape(S // tq, 1, tq)
    n = np.arange(NC).reshape(1, NC, 1)
    cbias = np.where((t >= n * CMP_STRIDE + CMP_BLOCK - 1) & (n < NC - 1), 0.0, -MASK_BIG).astype(np.float32)
    i = np.arange(tk)[:, None]
    j = np.arange(M)[None, :] % tq
    tbl = np.stack([np.zeros((tk, M)), np.where(i <= j, 0.0, -MASK_BIG), np.where(i > j, 0.0, -MASK_BIG)])
    return (jnp.asarray(ov.T, BF16), jnp.asarray(ish, BF16), jnp.asarray(cbias), jnp.asarray(tbl, F32))


def _nsa_call(nq, kv, vt, kcx, vct, misct, q_norm_g, k_norm_g, B, S):
    T = B * S
    tq, tk, M = NSA_TQ, NSA_TK, NSA_M
    G = NSA_KV_HEADS
    nqt = S // tq
    NC = S // CMP_STRIDE
    NS = S // SLC_BLOCK
    ovt, ish, cbias, tbl = _nsa_consts(S)
    tile2 = lambda a: jnp.concatenate([a, a], axis=-1)
    row = lambda b, i: (b * nqt + i, 0)
    cst = lambda b, i: (0, 0)
    return pl.pallas_call(
        functools.partial(_nsa_kernel, S=S),
        grid=(B, nqt),
        in_specs=[pl.BlockSpec((tq, NSA_WIDTH), row),
                  pl.BlockSpec((S, KV_COLS), lambda b, i: (b, 0)),
                  pl.BlockSpec((None, S // tk, VT_ROWS, tk), lambda b, i: (b, 0, 0, 0)),
                  pl.BlockSpec((None, G, NC, LANES), lambda b, i: (b, 0, 0, 0)),
                  pl.BlockSpec((None, G, HEAD_DIM, NC), lambda b, i: (b, 0, 0, 0)),
                  pl.BlockSpec((LANES, tq), lambda b, i: (0, b * nqt + i)),
                  pl.BlockSpec((None, NC, tq), lambda b, i: (i, 0, 0)),
                  pl.BlockSpec((3, tk, M), lambda b, i: (0, 0, 0)),
                  pl.BlockSpec((1, LANES), cst),
                  pl.BlockSpec((3, LANES), cst),
                  pl.BlockSpec((NS, NC), cst),
                  pl.BlockSpec((NS, LANES), cst)],
        out_specs=pl.BlockSpec((NSA_WIDTH, tq), lambda b, i: (0, b * nqt + i)),
        out_shape=jax.ShapeDtypeStruct((NSA_WIDTH, T), F32),
        scratch_shapes=[pltpu.VMEM((G, S, LANES), BF16), pltpu.VMEM((G, S, LANES), BF16),
                        pltpu.SMEM((S // tk,), jnp.int32), pltpu.SMEM((1,), jnp.int32)],
        compiler_params=_cparams("parallel", "arbitrary"),
        name="nsa_attn",
    )(nq, kv, vt, kcx, vct, misct, cbias, tbl, tile2(q_norm_g), tile2(k_norm_g), ovt, ish)


def _out_proj_kernel(yg_ref, yn_ref, x_ref, g1_ref, sc_ref, sh_ref, ng_ref, wo_ref, rwt_ref, rb_ref,
                     x1_ref, h2_ref, gate_ref, gate_t_ref):
    mix = _dot(yg_ref[...], wo_ref[:GDN_WIDTH, :]) + _dot_tn(yn_ref[...], wo_ref[GDN_WIDTH:, :])
    x1 = x_ref[...] + g1_ref[...] * mix
    x1_ref[...] = x1
    h2 = _rms(x1, ng_ref[...]) * (1.0 + sc_ref[...]) + sh_ref[...]
    h2_ref[...] = h2.astype(BF16)

    tm = h2.shape[0]
    logits = _dot_nt_hi(rwt_ref[...], h2)
    score = _sigmoid(logits)
    biased = score + rb_ref[...]
    b = [biased[e:e + 1, :] for e in range(N_EXPERTS)]
    n_groups = N_EXPERTS // EXPERTS_PER_GROUP
    gscore = []
    for gi in range(n_groups):
        vals = b[gi * EXPERTS_PER_GROUP:(gi + 1) * EXPERTS_PER_GROUP]
        best = None
        for i in range(EXPERTS_PER_GROUP):
            for j in range(i + 1, EXPERTS_PER_GROUP):
                pair = vals[i] + vals[j]
                best = pair if best is None else jnp.maximum(best, pair)
        gscore.append(best)
    gbest = jnp.zeros((1, tm), jnp.int32)
    top = gscore[0]
    for gi in range(1, n_groups):
        better = gscore[gi] > top
        gbest = jnp.where(better, gi, gbest)
        top = jnp.where(better, gscore[gi], top)
    erow = lax.broadcasted_iota(jnp.int32, (N_EXPERTS, tm), 0)
    gates = jnp.zeros((N_EXPERTS, tm), F32)
    for e in range(N_EXPERTS):
        gi = e // EXPERTS_PER_GROUP
        rank = jnp.zeros((1, tm), F32)
        for e2 in range(gi * EXPERTS_PER_GROUP, (gi + 1) * EXPERTS_PER_GROUP):
            if e2 == e:
                continue
            beats = (b[e2] > b[e]) | ((b[e2] == b[e]) & (e2 < e))
            rank = rank + jnp.where(beats, 1.0, 0.0)
        chosen = (gbest == gi) & (rank < 2.0)
        ge = jnp.where(chosen, score[e:e + 1, :], 0.0)
        gates = gates + jnp.where(erow == e, ge, 0.0)
    gates = gates / jnp.sum(gates, axis=0, keepdims=True)
    pad_rows = jnp.zeros((8 - EXPERTS_PER_GROUP, tm), F32)
    for gi in range(n_groups):
        gate_t_ref[gi] = jnp.concatenate(
            [gates[gi * EXPERTS_PER_GROUP:(gi + 1) * EXPERTS_PER_GROUP, :], pad_rows], axis=0)
    ident = (lax.broadcasted_iota(jnp.int32, (N_EXPERTS, LANES), 0)
             == lax.broadcasted_iota(jnp.int32, (N_EXPERTS, LANES), 1))
    ident = jnp.where(ident, 1.0, 0.0).astype(BF16)
    hi, mid, lo = _split3(gates)
    dn = (((0,), (0,)), ((), ()))
    gate_ref[...] = (lax.dot_general(hi, ident, dn, preferred_element_type=F32)
                     + lax.dot_general(mid, ident, dn, preferred_element_type=F32)
                     + lax.dot_general(lo, ident, dn, preferred_element_type=F32))


def _out_proj_call(yg, yn, x2, g1, sc2, sh2, ng, wo, rwt, rb, S):
    T, D = x2.shape
    tm = 1024
    per_b = S // tm
    row = lambda i: (i, 0)
    bat = lambda i: (i // per_b, 0, 0)
    cst = lambda i: (0, 0)
    return pl.pallas_call(
        _out_proj_kernel,
        grid=(T // tm,),
        in_specs=[pl.BlockSpec((tm, GDN_WIDTH), row), pl.BlockSpec((NSA_WIDTH, tm), lambda i: (0, i)),
                  pl.BlockSpec((tm, D), row),
                  pl.BlockSpec((None, 1, D), bat), pl.BlockSpec((None, 1, D), bat), pl.BlockSpec((None, 1, D), bat),
                  pl.BlockSpec((1, D), cst), pl.BlockSpec((D, D), cst),
                  pl.BlockSpec((N_EXPERTS, D), cst), pl.BlockSpec((N_EXPERTS, 1), cst)],
        out_specs=[pl.BlockSpec((tm, D), row), pl.BlockSpec((tm, D), row), pl.BlockSpec((tm, LANES), row),
                   pl.BlockSpec((N_EXPERTS // EXPERTS_PER_GROUP, 8, tm), lambda i: (0, 0, i))],
        out_shape=[jax.ShapeDtypeStruct((T, D), F32), jax.ShapeDtypeStruct((T, D), BF16),
                   jax.ShapeDtypeStruct((T, LANES), F32),
                   jax.ShapeDtypeStruct((N_EXPERTS // EXPERTS_PER_GROUP, 8, T), F32)],
        compiler_params=_cparams("parallel"),
        name="out_proj_router",
    )(yg, yn, x2, g1, sc2, sh2, ng, wo, rwt, rb)


MOE_TM = 1024
MOE_RB = 288


def _moe_kernel(h_ref, gates_ref, gt8_ref, tri_ref, trit_ref, x1_ref, g2_ref, wg_ref, wu_ref, wd_ref, o_ref):
    g = pl.program_id(1)
    tm, rb = MOE_TM, MOE_RB
    nlb = tm // LANES

    @pl.when(g == 0)
    def _():
        o_ref[...] = jnp.zeros_like(o_ref)

    gt8 = gt8_ref[...]
    in_row = jnp.sum(gt8, axis=0, keepdims=True) > 0.0
    first = lax.broadcasted_iota(jnp.int32, (8, LANES), 0) == 0
    off = jnp.zeros((1, 1), F32)
    keys = []
    for b in range(nlb):
        blk = in_row[:, b * LANES:(b + 1) * LANES]
        one = jnp.where(blk & first, 1.0, 0.0).astype(BF16)
        within = jnp.dot(one, tri_ref[...], preferred_element_type=F32)[0:1, :]
        keys.append(jnp.where(blk, within + off, -1.0))
        off = off + jnp.sum(jnp.where(blk, 1.0, 0.0), axis=1, keepdims=True)
    key_row = jnp.concatenate(keys, axis=1)
    n_g = jnp.sum(off).astype(jnp.int32)
    lane = lax.broadcasted_iota(jnp.int32, (tm, LANES), 1)
    mine = (lane >= g * EXPERTS_PER_GROUP) & (lane < (g + 1) * EXPERTS_PER_GROUP)
    in_col = jnp.sum(jnp.where(mine, gates_ref[...], 0.0), axis=1, keepdims=True) > 0.0
    off = jnp.zeros((1, 1), F32)
    keys = []
    for b in range(nlb):
        blk = in_col[b * LANES:(b + 1) * LANES, :]
        one = jnp.broadcast_to(jnp.where(blk, 1.0, 0.0), (LANES, LANES)).astype(BF16)
        within = jnp.dot(trit_ref[...], one, preferred_element_type=F32)[:, 0:1]
        keys.append(jnp.where(blk, within + off, -1.0))
        off = off + jnp.sum(jnp.where(blk, 1.0, 0.0), axis=0, keepdims=True)
    key_col = jnp.concatenate(keys, axis=0)

    src = lax.broadcasted_iota(jnp.int32, (8, LANES), 0)
    dst = lax.broadcasted_iota(jnp.int32, (8, LANES), 1)
    tn_dims = (((0,), (0,)), ((), ()))
    gcols = jnp.zeros((tm, LANES), F32)
    for p, piece in enumerate(_split3(gt8)):
        place = jnp.where((dst == src + EXPERTS_PER_GROUP * p) & (src < EXPERTS_PER_GROUP), 1.0, 0.0).astype(BF16)
        gcols = gcols + lax.dot_general(piece, place, tn_dims, preferred_element_type=F32)
    gcols = gcols.astype(BF16)

    def block(j, carry):
        r0 = (j * rb).astype(F32)
        sel = jnp.where(key_row - r0 == lax.broadcasted_iota(jnp.int32, (rb, tm), 0).astype(F32), 1.0, 0.0)
        sel = sel.astype(BF16)
        xb = jnp.dot(sel, h_ref[...], preferred_element_type=F32).astype(BF16)
        gp = jnp.dot(sel, gcols, preferred_element_type=F32)
        y = jnp.zeros((rb, o_ref.shape[1]), F32)
        for k in range(EXPERTS_PER_GROUP):
            gate = (gp[:, k:k + 1] + gp[:, EXPERTS_PER_GROUP + k:EXPERTS_PER_GROUP + k + 1]
                    + gp[:, 2 * EXPERTS_PER_GROUP + k:2 * EXPERTS_PER_GROUP + k + 1])
            act = (_silu(jnp.dot(xb, wg_ref[k], preferred_element_type=F32))
                   * jnp.dot(xb, wu_ref[k], preferred_element_type=F32))
            y = y + _dot(act * gate, wd_ref[k])
        sel_t = jnp.where(key_col - r0 == lax.broadcasted_iota(jnp.int32, (tm, rb), 1).astype(F32), 1.0, 0.0)
        o_ref[...] += jnp.dot(sel_t.astype(BF16), y.astype(BF16), preferred_element_type=F32)
        return carry

    lax.fori_loop(0, (n_g + rb - 1) // rb, block, 0)

    @pl.when(g == N_EXPERTS // EXPERTS_PER_GROUP - 1)
    def _():
        o_ref[...] = x1_ref[...] + g2_ref[...] * o_ref[...]


def _moe_call(h2, gates, gates_t, x1, g2, wg, wu, wd, S):
    T, D = x1.shape
    tm = MOE_TM
    per_b = S // tm
    n_groups = N_EXPERTS // EXPERTS_PER_GROUP
    i = np.arange(LANES)
    tri = jnp.asarray(i[:, None] < i[None, :], BF16)
    row = lambda i, g: (i, 0)
    cst = lambda i, g: (0, 0)
    wsel = lambda i, g: (g, 0, 0)
    return pl.pallas_call(
        _moe_kernel,
        grid=(T // tm, n_groups),
        in_specs=[pl.BlockSpec((tm, D), row), pl.BlockSpec((tm, LANES), row),
                  pl.BlockSpec((None, 8, tm), lambda i, g: (g, 0, i)),
                  pl.BlockSpec((LANES, LANES), cst), pl.BlockSpec((LANES, LANES), cst),
                  pl.BlockSpec((tm, D), row),
                  pl.BlockSpec((None, 1, D), lambda i, g: (i // per_b, 0, 0)),
                  pl.BlockSpec((EXPERTS_PER_GROUP, D, EXPERT_FF), wsel),
                  pl.BlockSpec((EXPERTS_PER_GROUP, D, EXPERT_FF), wsel),
                  pl.BlockSpec((EXPERTS_PER_GROUP, EXPERT_FF, D), wsel)],
        out_specs=pl.BlockSpec((tm, D), row),
        out_shape=jax.ShapeDtypeStruct((T, D), F32),
        compiler_params=_cparams("parallel", "arbitrary"),
        name="moe",
    )(h2, gates, gates_t, tri, tri.T, x1, g2, wg, wu, wd)


def kernel(x, c, ada_w, ada_b, norm1_g, norm2_g, w_in, gdn_conv_w, gdn_a_log, gdn_dt_bias, gdn_norm_g,
           nsa_q_norm_g, nsa_k_norm_g, cmp_pe, cmp_w1, cmp_b1, cmp_w2, cmp_b2, w_out, router_w, router_bias,
           exp_w_gate, exp_w_up, exp_w_down):
    B, S, D = x.shape
    L = ada_w.shape[0]
    T = B * S
    G = NSA_KV_HEADS
    NC = S // CMP_STRIDE
    N = S // GDN_CHUNK
    P = GDN_HEADS // GDN_HPB
    W = GDN_HPB * HEAD_DIM

    mod = _ada_call(c, ada_w, ada_b)
    rwt = router_w.T
    rb = router_bias.reshape(N_EXPERTS, 1)
    x2 = x.reshape(T, D)
    for l in range(L):
        m6 = mod[l].reshape(B, 6, 1, D)
        sh1, sc1, g1, sh2, sc2, g2 = (m6[:, i] for i in range(6))
        qkv, z, nq, kv, misc, misct, vt = _in_proj_call(
            x2, sc1, sh1, norm1_g[l].reshape(1, D), *_pad_w_in(w_in[l]), S)

        a_rows = misct[:GDN_HEADS].reshape(P, GDN_HPB, B, N, GDN_CHUNK).transpose(2, 0, 3, 1, 4).reshape(B, P, N, W)
        rep = lambda t: jnp.repeat(t, HEAD_DIM).reshape(1, GDN_WIDTH)
        y_gdn = _gdn_call(qkv, z, misc, a_rows, gdn_conv_w[l], rep(gdn_a_log[l]), rep(gdn_dt_bias[l]),
                          jnp.tile(gdn_norm_g[l], GDN_HPB).reshape(1, W), B, S)

        w2 = cmp_w2[l]
        kcx, vct = _cmp_call(
            kv, cmp_pe[l].reshape(2, 1, CMP_BLOCK * HEAD_DIM), cmp_w1[l].astype(BF16),
            cmp_b1[l].reshape(2, 1, CMP_HIDDEN), jnp.pad(w2[0], ((0, 0), (0, LANES - HEAD_DIM))).astype(BF16),
            w2[1].T.astype(BF16), jnp.pad(cmp_b2[l, 0], (0, LANES - HEAD_DIM)).reshape(1, LANES),
            cmp_b2[l, 1].reshape(HEAD_DIM, 1), jnp.pad(nsa_k_norm_g[l, 0], (0, LANES - HEAD_DIM)).reshape(1, LANES),
            B, S)
        y_nsa_t = _nsa_call(nq, kv, vt.reshape(B, S // NSA_TK, VT_ROWS, NSA_TK), kcx, vct, misct,
                            nsa_q_norm_g[l].reshape(1, HEAD_DIM), nsa_k_norm_g[l], B, S)

        x1, h2, gates, gates_t = _out_proj_call(y_gdn, y_nsa_t, x2, g1, sc2, sh2, norm2_g[l].reshape(1, D),
                                       w_out[l].astype(BF16), rwt, rb, S)
        x2 = _moe_call(h2, gates, gates_t, x1, g2, exp_w_gate[l].astype(BF16), exp_w_up[l].astype(BF16),
                       exp_w_down[l].astype(BF16), S)
    return x2.reshape(B, S, D)
```

```python
import functools

import numpy as np
import jax
import jax.numpy as jnp
from jax import lax
from jax.experimental import pallas as pl
from jax.experimental.pallas import tpu as pltpu

F32 = jnp.float32
BF16 = jnp.bfloat16

HEAD_DIM = 64
GDN_HEADS = 8
GDN_WIDTH = GDN_HEADS * HEAD_DIM
GDN_CONV = 4
GDN_CHUNK = 64
NSA_HEADS = 8
NSA_KV_HEADS = 2
NSA_GROUP = NSA_HEADS // NSA_KV_HEADS
NSA_WIDTH = NSA_HEADS * HEAD_DIM
NSA_KV_WIDTH = NSA_KV_HEADS * HEAD_DIM
CMP_BLOCK = 32
CMP_STRIDE = 16
CMP_HIDDEN = 256
SLC_BLOCK = 64
SLC_TOPN = 8
WINDOW = 512
N_EXPERTS = 16
EXPERTS_PER_GROUP = 4
EXPERT_FF = 512
RMS_EPS = 1e-6
NEG_BIG = -1e30
SEL_BIG = 1e9

LANES = 128
VMEM_LIMIT_BYTES = 56 * 1024 * 1024

C_QKV = 0
C_Z = 3 * GDN_WIDTH
C_NQ = C_Z + GDN_WIDTH
C_KV = C_NQ + NSA_WIDTH
KV_COLS = 4 * NSA_KV_WIDTH
C_MISC = C_KV + KV_COLS
IN_PAD_COLS = C_MISC + LANES
MISC_B0 = GDN_HEADS
MISC_GATE0 = 16
VT_SLAB = HEAD_DIM + 16
VT_ROWS = 2 * NSA_KV_HEADS * VT_SLAB


def _cparams(*sem):
    return pltpu.CompilerParams(dimension_semantics=sem, vmem_limit_bytes=VMEM_LIMIT_BYTES)


def _dot(a, b):
    return jnp.dot(a.astype(BF16), b.astype(BF16), preferred_element_type=F32)


def _dot_nt(a, b):
    return lax.dot_general(a.astype(BF16), b.astype(BF16), (((1,), (1,)), ((), ())),
                           preferred_element_type=F32)


def _dot_tn(a, b):
    return lax.dot_general(a.astype(BF16), b.astype(BF16), (((0,), (0,)), ((), ())),
                           preferred_element_type=F32)


def _split2(x):
    hi = x.astype(BF16)
    lo = (x - hi.astype(F32)).astype(BF16)
    return hi, lo


def _split3(x):
    hi = x.astype(BF16)
    r = x - hi.astype(F32)
    mid = r.astype(BF16)
    lo = (r - mid.astype(F32)).astype(BF16)
    return hi, mid, lo


def _dot_hi(a, b):
    ah, al = _split2(a)
    bh, bl = _split2(b)
    return (jnp.dot(ah, bh, preferred_element_type=F32) + jnp.dot(al, bh, preferred_element_type=F32)
            + jnp.dot(ah, bl, preferred_element_type=F32))


def _dot_nt_hi(a, b):
    ah, al = _split2(a)
    bh, bl = _split2(b)
    dn = (((1,), (1,)), ((), ()))
    return (lax.dot_general(ah, bh, dn, preferred_element_type=F32)
            + lax.dot_general(al, bh, dn, preferred_element_type=F32)
            + lax.dot_general(ah, bl, dn, preferred_element_type=F32))


def _dot_exact_rhs(a, b_bf16):
    hi, mid, lo = _split3(a)
    return (jnp.dot(hi, b_bf16, preferred_element_type=F32) + jnp.dot(mid, b_bf16, preferred_element_type=F32)
            + jnp.dot(lo, b_bf16, preferred_element_type=F32))


def _rms(x, g):
    return x * lax.rsqrt(jnp.mean(x * x, axis=-1, keepdims=True) + RMS_EPS) * g


def _sigmoid(x):
    return 0.5 * (jnp.tanh(0.5 * x) + 1.0)


def _silu(x):
    return x * _sigmoid(x)


def _ada_kernel(c_ref, w_ref, b_ref, o_ref):
    o_ref[...] = _dot_hi(_silu(c_ref[...]), w_ref[...]) + b_ref[...]


def _ada_call(c, ada_w, ada_b):
    L, D, N = ada_w.shape
    B = c.shape[0]
    tn = 1024
    return pl.pallas_call(
        _ada_kernel,
        grid=(L, N // tn),
        in_specs=[pl.BlockSpec((B, D), lambda l, j: (0, 0)),
                  pl.BlockSpec((None, D, tn), lambda l, j: (l, 0, j)),
                  pl.BlockSpec((None, 1, tn), lambda l, j: (l, 0, j))],
        out_specs=pl.BlockSpec((None, B, tn), lambda l, j: (l, 0, j)),
        out_shape=jax.ShapeDtypeStruct((L, B, N), F32),
        compiler_params=_cparams("parallel", "parallel"),
        name="ada_mod",
    )(c, ada_w, ada_b.reshape(L, 1, N))


IN_SEGS = ((C_QKV, 3 * GDN_WIDTH), (C_Z, GDN_WIDTH), (C_NQ, NSA_WIDTH), (C_KV, KV_COLS), (C_MISC, LANES))


def _in_proj_kernel(x_ref, sc_ref, sh_ref, g_ref, w_ref, wmt_ref, wvt_ref, *o_refs):
    h = _rms(x_ref[...], g_ref[...]) * (1.0 + sc_ref[...]) + sh_ref[...]
    hb = h.astype(BF16)
    for (c0, n), o_ref in zip(IN_SEGS, o_refs[:-2]):
        o_ref[...] = jnp.dot(hb, w_ref[:, c0:c0 + n], preferred_element_type=F32)
    nt_dims = (((1,), (1,)), ((), ()))
    o_refs[-2][...] = lax.dot_general(wmt_ref[...], hb, nt_dims, preferred_element_type=F32)
    vt = lax.dot_general(wvt_ref[...], hb, nt_dims, preferred_element_type=F32).astype(BF16)
    ones = jnp.ones((VT_SLAB - HEAD_DIM, LANES), BF16)
    for j in range(o_refs[-1].shape[0]):
        for st in range(2 * NSA_KV_HEADS):
            o_refs[-1][j, st * VT_SLAB:st * VT_SLAB + HEAD_DIM, :] = vt[st * HEAD_DIM:(st + 1) * HEAD_DIM,
                                                                         j * LANES:(j + 1) * LANES]
            o_refs[-1][j, st * VT_SLAB + HEAD_DIM:(st + 1) * VT_SLAB, :] = ones


def _in_proj_call(x2, sc, sh, g, w_pad, wvt, S):
    T, D = x2.shape
    tm = 512
    per_b = S // tm
    row = lambda i: (i, 0)
    bat = lambda i: (i // per_b, 0, 0)
    return pl.pallas_call(
        _in_proj_kernel,
        grid=(T // tm,),
        in_specs=[pl.BlockSpec((tm, D), row),
                  pl.BlockSpec((None, 1, D), bat),
                  pl.BlockSpec((None, 1, D), bat),
                  pl.BlockSpec((1, D), lambda i: (0, 0)),
                  pl.BlockSpec((D, IN_PAD_COLS), lambda i: (0, 0)),
                  pl.BlockSpec((LANES, D), lambda i: (0, 0)),
                  pl.BlockSpec((2 * NSA_KV_WIDTH, D), lambda i: (0, 0))],
        out_specs=([pl.BlockSpec((tm, n), row) for _, n in IN_SEGS]
                   + [pl.BlockSpec((LANES, tm), lambda i: (0, i)),
                      pl.BlockSpec((tm // LANES, VT_ROWS, LANES), lambda i: (i, 0, 0))]),
        out_shape=([jax.ShapeDtypeStruct((T, n), F32) for _, n in IN_SEGS]
                   + [jax.ShapeDtypeStruct((LANES, T), F32),
                      jax.ShapeDtypeStruct((T // LANES, VT_ROWS, LANES), BF16)]),
        compiler_params=_cparams("parallel"),
        name="in_proj",
    )(x2, sc, sh, g, w_pad, w_pad[:, C_MISC:].T, wvt)


def _pad_w_in(w_in):
    D = w_in.shape[0]
    W = GDN_WIDTH
    o = 0
    gq, gk, gv, gz = (w_in[:, o + i * W:o + (i + 1) * W] for i in range(4))
    o += 4 * W
    ga = w_in[:, o:o + GDN_HEADS]
    gb = w_in[:, o + GDN_HEADS:o + 2 * GDN_HEADS]
    o += 2 * GDN_HEADS
    nq = w_in[:, o:o + NSA_WIDTH]
    o += NSA_WIDTH
    kc, vc, ks, vs, kw, vw = (w_in[:, o + i * NSA_KV_WIDTH:o + (i + 1) * NSA_KV_WIDTH] for i in range(6))
    o += 6 * NSA_KV_WIDTH
    gates = w_in[:, o:o + 3 * NSA_HEADS]
    misc = jnp.zeros((D, LANES), w_in.dtype)
    misc = misc.at[:, 0:GDN_HEADS].set(ga).at[:, MISC_B0:MISC_B0 + GDN_HEADS].set(gb)
    misc = misc.at[:, MISC_GATE0:MISC_GATE0 + 3 * NSA_HEADS].set(gates)
    cols = [gq, gk, gv, gz, nq, kc, vc, ks, kw, misc]
    return jnp.concatenate(cols, axis=1).astype(BF16), jnp.concatenate([vs, vw], axis=1).T.astype(BF16)


GDN_HPB = 2
GDN_NB = 32


def _gdn_kernel(q_ref, k_ref, v_ref, z_ref, misc_ref, ar_ref, cwq_ref, cwk_ref, cwv_ref,
                alog_ref, dtb_ref, alogr_ref, dtbr_ref, ng_ref, o_ref,
                qs, ks, kbs, xs, gcs, grs, ps, qq, qks, os_, rs, q2s, sev, *, S):
    C = GDN_CHUNK
    N = S // C
    W = GDN_HPB * HEAD_DIM
    row = lax.broadcasted_iota(jnp.int32, (S, W), 0)
    row8 = lax.broadcasted_iota(jnp.int32, (8, W), 0)

    def conv_silu(x_ref, w):
        x = x_ref[...]
        y = x * w[GDN_CONV - 1:GDN_CONV, :]
        for s in range(1, GDN_CONV):
            head = jnp.where(row8 >= s, pltpu.roll(x[:8], s, axis=0), 0.0)
            xs = jnp.concatenate([head, x_ref[pl.ds(8 - s, S - 8), :]], axis=0)
            y = y + xs * w[GDN_CONV - 1 - s:GDN_CONV - s, :]
        return _silu(y)

    bd = jnp.where(lax.broadcasted_iota(jnp.int32, (W, W), 0) // HEAD_DIM
                   == lax.broadcasted_iota(jnp.int32, (W, W), 1) // HEAD_DIM, 1.0, 0.0).astype(BF16)

    def head_sumsq(x):
        hi, lo = _split2(x * x)
        return jnp.dot(hi, bd, preferred_element_type=F32) + jnp.dot(lo, bd, preferred_element_type=F32)

    def head_l2(x):
        return x * lax.rsqrt(head_sumsq(x) + RMS_EPS)

    q = head_l2(conv_silu(q_ref, cwq_ref[...])) * (HEAD_DIM ** -0.5)
    k = head_l2(conv_silu(k_ref, cwk_ref[...]))
    v = conv_silu(v_ref, cwv_ref[...])

    hd0 = pl.program_id(1) * GDN_HPB
    src = lax.broadcasted_iota(jnp.int32, (LANES, W), 0)
    dst = lax.broadcasted_iota(jnp.int32, (LANES, W), 1) // HEAD_DIM + hd0
    a_exp = _dot_exact_rhs(misc_ref[...], jnp.where(src == dst, 1.0, 0.0).astype(BF16))
    b_exp = _dot_exact_rhs(misc_ref[...], jnp.where(src == dst + MISC_B0, 1.0, 0.0).astype(BF16))
    beta = _sigmoid(b_exp)

    def log_decay(a, alog, dtb):
        xx = a + dtb
        sp = jnp.maximum(xx, 0.0) + jnp.log(1.0 + jnp.exp(-jnp.abs(xx)))
        return -jnp.exp(alog) * sp

    gc = log_decay(a_exp, alog_ref[...], dtb_ref[...])
    pos = row % C
    for s in (1, 2, 4, 8, 16, 32):
        gc = gc + jnp.where(pos >= s, pltpu.roll(gc, s, axis=0), 0.0)
    gr = log_decay(ar_ref[...], alogr_ref[...], dtbr_ref[...])
    lpos = lax.broadcasted_iota(jnp.int32, (N, W), 1) % C
    for s in (1, 2, 4, 8, 16, 32):
        gr = gr + jnp.where(lpos >= s, pltpu.roll(gr, s, axis=1), 0.0)

    kb = k * beta
    vb = v * beta
    kbe = kb * jnp.exp(gc)
    for h in range(GDN_HPB):
        sl = slice(h * HEAD_DIM, (h + 1) * HEAD_DIM)
        qs[h] = q[:, sl]
        ks[h] = k[:, sl]
        kbs[h] = kb[:, sl]
        gcs[h] = gc[:, sl]
        for n in range(N):
            grs[h, n] = jnp.broadcast_to(gr[n:n + 1, sl], (8, HEAD_DIM))
        xs[h] = jnp.concatenate([kbe[:, sl], vb[:, sl]], axis=1)

    NB = GDN_NB
    ci3 = lax.broadcasted_iota(jnp.int32, (1, C, C), 1)
    cj3 = lax.broadcasted_iota(jnp.int32, (1, C, C), 2)
    eye3 = jnp.where(ci3 == cj3, 1.0, 0.0)

    def bmm(a, b):
        return lax.dot_general(a.astype(BF16), b.astype(BF16), (((2,), (1,)), ((0,), (0,))),
                               preferred_element_type=F32)

    def bmm_nt(a, b):
        return lax.dot_general(a.astype(BF16), b.astype(BF16), (((2,), (2,)), ((0,), (0,))),
                               preferred_element_type=F32)

    def intra(nb, carry):
        rows = pl.ds(pl.multiple_of(nb * (NB * C), NB * C), NB * C)
        for h in range(GDN_HPB):
            kk = ks[h, rows, :].reshape(NB, C, HEAD_DIM)
            kbq = jnp.concatenate([kbs[h, rows, :].reshape(NB, C, HEAD_DIM),
                                   qs[h, rows, :].reshape(NB, C, HEAD_DIM)], axis=1)
            a2 = bmm_nt(kbq, kk)
            gcol = gcs[h, rows, :].reshape(NB, C, HEAD_DIM)
            grow = grs[h, pl.ds(nb * NB, NB), 0:1, :]
            decay = jnp.exp(jnp.where(ci3 >= cj3, gcol - grow, NEG_BIG))
            low = jnp.where(ci3 > cj3, a2[:, :C] * decay, 0.0)
            qks[h, rows, :] = (a2[:, C:] * decay).reshape(NB * C, C)
            tinv = eye3 - jnp.where(ci3 // 2 == cj3 // 2, low, 0.0)
            for lv in range(1, 6):
                s = 2 ** lv
                off = jnp.where((ci3 // (2 * s) == cj3 // (2 * s)) & (ci3 // s != cj3 // s), low, 0.0)
                tinv = tinv - bmm(tinv, bmm(off, tinv))
            x = bmm(tinv, xs[h, rows, :].reshape(NB, C, 2 * HEAD_DIM))
            xs[h, rows, :] = x.reshape(NB * C, 2 * HEAD_DIM)
            kd = kk * jnp.exp(gcol[:, C - 1:C, :] - gcol)
            pq = lax.dot_general(kd.astype(BF16), x.astype(BF16), (((1,), (1,)), ((0,), (0,))),
                                 preferred_element_type=F32)
            pn = pq[:, :, :HEAD_DIM]
            qn = pq[:, :, HEAD_DIM:]
            ps[h, rows, :] = pn.reshape(NB * C, HEAD_DIM)
            qq[h, rows, :] = qn.reshape(NB * C, HEAD_DIM)
            cd = jnp.exp(gcol[:, C - 1:C, :]).reshape(NB // 2, 2, 1, HEAD_DIM)
            pn = pn.reshape(NB // 2, 2, HEAD_DIM, HEAD_DIM)
            qn = qn.reshape(NB // 2, 2, HEAD_DIM, HEAD_DIM)
            p0, p1, q0, q1, c0, c1 = pn[:, 0], pn[:, 1], qn[:, 0], qn[:, 1], cd[:, 0], cd[:, 1]
            t = bmm(p1, pq.reshape(NB // 2, 2, HEAD_DIM, 2 * HEAD_DIM)[:, 0])
            prow = pl.ds(pl.multiple_of(nb * (NB // 2 * C), NB // 2 * C), NB // 2 * C)
            rs[h, prow, :] = (c1 * p0 + c0 * p1 - t[:, :, :HEAD_DIM]).reshape(NB // 2 * C, HEAD_DIM)
            q2s[h, prow, :] = (c1 * q0 - t[:, :, HEAD_DIM:] + q1).reshape(NB // 2 * C, HEAD_DIM)
        return carry

    lax.fori_loop(0, N // NB, intra, 0)

    def inter(k, states):
        sl = pl.ds(pl.multiple_of(k * C, C), C)
        new_states = []
        for h in range(GDN_HPB):
            st = states[h]
            sev[h, k] = st
            gsum = gcs[h, pl.ds(2 * k * C + C - 1, 1), :] + gcs[h, pl.ds(2 * k * C + 2 * C - 1, 1), :]
            new_states.append(st * jnp.exp(gsum) - _dot_hi(rs[h, sl, :], st) + q2s[h, sl, :])
        return tuple(new_states)

    lax.fori_loop(0, N // 2, inter, tuple(jnp.zeros((HEAD_DIM, HEAD_DIM), F32) for _ in range(GDN_HPB)))

    def outputs(nb, carry):
        rows = pl.ds(pl.multiple_of(nb * (NB * C), NB * C), NB * C)
        for h in range(GDN_HPB):
            gcol = gcs[h, rows, :].reshape(NB, C, HEAD_DIM)
            s0 = sev[h, pl.ds(nb * (NB // 2), NB // 2)]
            p0 = ps[h, rows, :].reshape(NB // 2, 2, HEAD_DIM, HEAD_DIM)[:, 0]
            q0 = qq[h, rows, :].reshape(NB // 2, 2, HEAD_DIM, HEAD_DIM)[:, 0]
            c0 = jnp.exp(gcol[:, C - 1:C, :]).reshape(NB // 2, 2, 1, HEAD_DIM)[:, 0]
            s1 = c0 * s0 - bmm(p0, s0) + q0
            st = jnp.stack([s0, s1], axis=1).reshape(NB, HEAD_DIM, HEAD_DIM)
            x = xs[h, rows, :].reshape(NB, C, 2 * HEAD_DIM)
            qd = qs[h, rows, :].reshape(NB, C, HEAD_DIM) * jnp.exp(gcol)
            t1 = bmm(jnp.concatenate([x[:, :, :HEAD_DIM], qd], axis=1), st)
            vn = x[:, :, HEAD_DIM:] - t1[:, :C]
            o = t1[:, C:] + bmm(qks[h, rows, :].reshape(NB, C, C), vn)
            os_[h, rows, :] = o.reshape(NB * C, HEAD_DIM)
        return carry

    lax.fori_loop(0, N // NB, outputs, 0)

    o = jnp.concatenate([os_[h] for h in range(GDN_HPB)], axis=1)
    o = o * lax.rsqrt(head_sumsq(o) * (1.0 / HEAD_DIM) + RMS_EPS) * ng_ref[...]
    o_ref[...] = o * _silu(z_ref[...])


def _gdn_call(qkv, z, misc, a_rows, conv_w, alog_exp, dtb_exp, norm_g, B, S):
    T = B * S
    W = GDN_HPB * HEAD_DIM
    P = GDN_HEADS // GDN_HPB
    N = S // GDN_CHUNK
    col = lambda off: (lambda b, p: (b, off + p))
    par = lambda off: (lambda b, p: (0, off + p))
    hs = lambda n: pltpu.VMEM((GDN_HPB, S, n), F32)
    return pl.pallas_call(
        functools.partial(_gdn_kernel, S=S),
        grid=(B, P),
        in_specs=[pl.BlockSpec((S, W), col(0)), pl.BlockSpec((S, W), col(P)), pl.BlockSpec((S, W), col(2 * P)),
                  pl.BlockSpec((S, W), col(0)), pl.BlockSpec((S, LANES), lambda b, p: (b, 0)),
                  pl.BlockSpec((None, None, N, W), lambda b, p: (b, p, 0, 0)),
                  pl.BlockSpec((GDN_CONV, W), par(0)), pl.BlockSpec((GDN_CONV, W), par(P)),
                  pl.BlockSpec((GDN_CONV, W), par(2 * P)),
                  pl.BlockSpec((1, W), par(0)), pl.BlockSpec((1, W), par(0)),
                  pl.BlockSpec((1, W), par(0)), pl.BlockSpec((1, W), par(0)),
                  pl.BlockSpec((1, W), lambda b, p: (0, 0))],
        out_specs=pl.BlockSpec((S, W), col(0)),
        out_shape=jax.ShapeDtypeStruct((T, GDN_WIDTH), F32),
        scratch_shapes=[hs(HEAD_DIM), hs(HEAD_DIM), hs(HEAD_DIM), hs(2 * HEAD_DIM), hs(HEAD_DIM),
                        pltpu.VMEM((GDN_HPB, N, 8, HEAD_DIM), F32),
                        hs(HEAD_DIM), hs(HEAD_DIM), hs(HEAD_DIM), hs(HEAD_DIM),
                        pltpu.VMEM((GDN_HPB, S // 2, HEAD_DIM), F32), pltpu.VMEM((GDN_HPB, S // 2, HEAD_DIM), F32),
                        pltpu.VMEM((GDN_HPB, N // 2, HEAD_DIM, HEAD_DIM), F32)],
        compiler_params=_cparams("parallel", "parallel"),
        name="gdn",
    )(qkv, qkv, qkv, z, misc, a_rows, conv_w, conv_w, conv_w,
      alog_exp, dtb_exp, alog_exp, dtb_exp, norm_g)


def _cmp_kernel(kc_in_ref, vc_in_ref, pe_ref, w1_ref, b1_ref, w2k_ref, w2vt_ref, b2k_ref, b2v_ref, kg_ref, kc_ref, vct_ref):
    half = (CMP_BLOCK // 2) * HEAD_DIM
    nrows = kc_in_ref.shape[0] // CMP_STRIDE
    lane = lax.broadcasted_iota(jnp.int32, (nrows, LANES), 1)
    cend = lax.broadcasted_iota(jnp.int32, (nrows, LANES), 0) * CMP_STRIDE + (CMP_BLOCK - 1)
    ext = jnp.where(lane == HEAD_DIM, cend // LANES, jnp.where(lane == HEAD_DIM + 1, cend % LANES, 0)).astype(F32)

    def token_groups(x_ref):
        cols = [[] for _ in range(NSA_KV_HEADS)]
        for m in range(CMP_STRIDE // 2):
            ev = x_ref[pl.ds(2 * m, nrows, stride=CMP_STRIDE), :]
            od = x_ref[pl.ds(2 * m + 1, nrows, stride=CMP_STRIDE), :]
            cols[0].append(jnp.where(lane < HEAD_DIM, ev, pltpu.roll(od, HEAD_DIM, axis=1)))
            cols[1].append(jnp.where(lane < HEAD_DIM, pltpu.roll(ev, HEAD_DIM, axis=1), od))
        return [jnp.concatenate(c, axis=1) for c in cols]

    def hidden(x, j):
        pe = pe_ref[j]
        p = _dot(x + pe[:, :half], w1_ref[j, :half, :])
        q = _dot(x + pe[:, half:], w1_ref[j, half:, :])
        return _silu(p + pltpu.roll(q, nrows - 1, axis=0) + b1_ref[j])

    for g, x in enumerate(token_groups(kc_in_ref)):
        kc = _dot(hidden(x, 0), w2k_ref[...]) + b2k_ref[...]
        ms = jnp.sum(kc * kc, axis=-1, keepdims=True) * (1.0 / HEAD_DIM)
        kc_ref[g] = jnp.where(lane < HEAD_DIM, kc * lax.rsqrt(ms + RMS_EPS) * kg_ref[...], ext)
    for g, x in enumerate(token_groups(vc_in_ref)):
        vct_ref[g] = _dot_nt(w2vt_ref[...], hidden(x, 1)) + b2v_ref[...]


def _cmp_call(kv, pe, w1, b1, w2k, w2vt, b2k, b2v, kg, B, S):
    G = NSA_KV_HEADS
    NC = S // CMP_STRIDE
    full = lambda a: pl.BlockSpec(a.shape, lambda b: (0,) * a.ndim)
    return pl.pallas_call(
        _cmp_kernel,
        grid=(B,),
        in_specs=[pl.BlockSpec((S, NSA_KV_WIDTH), lambda b: (b, 0)), pl.BlockSpec((S, NSA_KV_WIDTH), lambda b: (b, 1))]
        + [full(a) for a in (pe, w1, b1, w2k, w2vt, b2k, b2v, kg)],
        out_specs=[pl.BlockSpec((None, G, NC, LANES), lambda b: (b, 0, 0, 0)),
                   pl.BlockSpec((None, G, HEAD_DIM, NC), lambda b: (b, 0, 0, 0))],
        out_shape=[jax.ShapeDtypeStruct((B, G, NC, LANES), F32), jax.ShapeDtypeStruct((B, G, HEAD_DIM, NC), F32)],
        compiler_params=_cparams("parallel"),
        name="nsa_compress",
    )(kv, kv, pe, w1, b1, w2k, w2vt, b2k, b2v, kg)


NSA_TQ = 128
NSA_TK = 128
NSA_M = NSA_GROUP * NSA_TQ
MASK_BIG = 2.0 ** 100
SEL_LANE0 = HEAD_DIM + 2


def _nsa_kernel(q_ref, kv_ref, vt_ref, kcx_ref, vct_ref, misct_ref, cbias_ref, tbl_ref, qg_ref, kg_ref,
                ovt_ref, ish_ref, o_ref, ks_s, kw_s, live, n_live, *, S):
    tq, tk, M = NSA_TQ, NSA_TK, NSA_M
    G, R = NSA_KV_HEADS, NSA_GROUP
    NS = S // SLC_BLOCK
    n_top = min(SLC_TOPN, NS)
    qi = pl.program_id(1)
    t0 = qi * tq
    nt_dims = (((1,), (1,)), ((), ()))

    @pl.when(qi == 0)
    def _():
        rowi = lax.broadcasted_iota(jnp.int32, (S, LANES), 0)
        lane = lax.broadcasted_iota(jnp.int32, (S, LANES), 1)
        ext = jnp.where(lane == HEAD_DIM, rowi // LANES, jnp.where(lane == HEAD_DIM + 1, rowi % LANES, 0)).astype(F32)
        ext_slc = jnp.where(lane - SEL_LANE0 == rowi // SLC_BLOCK, -MASK_BIG, ext)

        def prep(c0, g, gain, extra):
            kraw = kv_ref[:, c0:c0 + LANES]
            if g == 1:
                kraw = pltpu.roll(kraw, HEAD_DIM, axis=1)
            ms = jnp.sum(jnp.where(lane < HEAD_DIM, kraw * kraw, 0.0), axis=-1, keepdims=True) * (1.0 / HEAD_DIM)
            return jnp.where(lane < HEAD_DIM, kraw * lax.rsqrt(ms + RMS_EPS) * gain, extra).astype(BF16)

        for g in range(G):
            ks_s[g] = prep(2 * NSA_KV_WIDTH, g, kg_ref[1:2, :], ext_slc)
            kw_s[g] = prep(3 * NSA_KV_WIDTH, g, kg_ref[2:3, :], ext)

    lane_q = lax.broadcasted_iota(jnp.int32, (tq, LANES), 1)
    gate_sig = _sigmoid(misct_ref[MISC_GATE0:MISC_GATE0 + 3 * NSA_HEADS, :])
    kt_d = t0 // tk

    def score_tiles(streams, kt, biased):
        rows = pl.ds(pl.multiple_of(kt * tk, tk), tk)
        out = []
        for qpp, k_s, g, _, idx_fn in streams:
            s = lax.dot_general(k_s[g, rows, :], qpp, nt_dims, preferred_element_type=F32)
            out.append(s + tbl_ref[idx_fn(kt)] if biased else s)
        return tuple(out)

    def softmax_pv(streams, kt, scores, states):
        soft = []
        for s, (m, _) in zip(scores, states):
            m_new = jnp.maximum(m, jnp.max(s, axis=0, keepdims=True))
            soft.append((m_new, jnp.exp(m - m_new), jnp.exp(s - m_new).astype(BF16)))
        pvs = [jnp.dot(vt_ref[kt, st[3]:st[3] + VT_SLAB, :], sf[2], preferred_element_type=F32)
               for st, sf in zip(streams, soft)]
        return tuple((m_new, alpha * acc + pv) for (m_new, alpha, _), (_, acc), pv in zip(soft, states, pvs))

    flash_init = (jnp.full((1, M), NEG_BIG, F32), jnp.zeros((VT_SLAB, M), F32))

    qpps, o_cmps, unsel = [], [], []
    for g in range(G):
        qrows = []
        for r in range(R):
            hh = g * R + r
            qpair = q_ref[:, (hh // 2) * LANES:(hh // 2 + 1) * LANES]
            if hh % 2 == 1:
                qpair = pltpu.roll(qpair, HEAD_DIM, axis=1)
            ms = jnp.sum(jnp.where(lane_q < HEAD_DIM, qpair * qpair, 0.0), axis=-1, keepdims=True) * (1.0 / HEAD_DIM)
            qn = qpair * lax.rsqrt(ms + RMS_EPS) * qg_ref[...] * (HEAD_DIM ** -0.5)
            slope = 2.0 ** (-8.0 * (hh + 1) / NSA_HEADS)
            ex = jnp.where(lane_q == HEAD_DIM, slope * LANES, jnp.where(lane_q == HEAD_DIM + 1, slope, 0.0))
            qrows.append(jnp.where(lane_q < HEAD_DIM, qn, ex))
        qbase = jnp.concatenate(qrows, axis=0)

        cb = cbias_ref[...]
        s = (lax.dot_general(kcx_ref[g].astype(BF16), qbase.astype(BF16), nt_dims, preferred_element_type=F32)
             + jnp.concatenate([cb] * R, axis=1))
        mx = jnp.maximum(jnp.max(s, axis=0, keepdims=True), NEG_BIG)
        e = jnp.exp(s - mx)
        l = jnp.sum(e, axis=0, keepdims=True)
        p_cmp = e / jnp.where(l > 0.0, l, 1.0)
        o_cmps.append(_dot(vct_ref[g], p_cmp))

        psum = p_cmp[:, 0:tq]
        for r in range(1, R):
            psum = psum + p_cmp[:, r * tq:(r + 1) * tq]
        hi, mid, lo = _split3(psum)
        ovt = ovt_ref[...]
        imp = (jnp.dot(ovt, hi, preferred_element_type=F32) + jnp.dot(ovt, mid, preferred_element_type=F32)
               + jnp.dot(ovt, lo, preferred_element_type=F32))
        blk = lax.broadcasted_iota(jnp.int32, (NS, tq), 0)
        cur = (t0 + lax.broadcasted_iota(jnp.int32, (NS, tq), 1)) // SLC_BLOCK
        valid = blk <= cur
        forced = (blk == 0) | (blk == cur) | (blk == cur - 1)
        score = jnp.where(forced, SEL_BIG, jnp.where(valid, imp, -SEL_BIG))
        rank = jnp.zeros((NS, tq), F32)
        for j in range(NS):
            cj = score[j:j + 1, :]
            beats = (cj > score) | ((cj == score) & (blk > j))
            rank = rank + jnp.where(beats, 1.0, 0.0)
        nsel = jnp.where((rank < n_top) & valid, 0.0, 1.0)
        unsel.append(jnp.min(nsel, axis=1, keepdims=True))
        nsel = nsel.astype(BF16)
        nsel_q = lax.dot_general(nsel, ish_ref[...], (((0,), (0,)), ((), ())),
                                 preferred_element_type=F32)
        in_sel = (lane_q >= SEL_LANE0) & (lane_q < SEL_LANE0 + NS)
        qpps.append(jnp.concatenate([jnp.where(in_sel, nsel_q, qrows[r]) for r in range(R)], axis=0).astype(BF16))

    slc_idx = lambda kt: jnp.where(kt == kt_d, 1, 0)
    win_lo = kt_d - WINDOW // tk
    win_idx = lambda kt: jnp.where(kt == kt_d, 1, jnp.where(kt == win_lo, 2, 0))
    slc_streams = [(qpps[g], ks_s, g, g * VT_SLAB, slc_idx) for g in range(G)]
    win_streams = [(qpps[g], kw_s, g, (G + g) * VT_SLAB, win_idx) for g in range(G)]
    kt_w = jnp.maximum(win_lo, 0)

    blocks_per_tile = tk // SLC_BLOCK
    unsel_any = unsel[0]
    for g in range(1, G):
        unsel_any = jnp.minimum(unsel_any, unsel[g])
    blk_tile = lax.broadcasted_iota(jnp.int32, (NS, 1), 0) // blocks_per_tile
    bit = jnp.zeros((1, 1), F32)
    for kt in range(S // tk - WINDOW // tk):
        in_tile = jnp.min(jnp.where(blk_tile == kt, unsel_any, 1.0), axis=0, keepdims=True)
        bit = bit + jnp.where(in_tile < 0.5, float(2 ** kt), 0.0)
    live_bits = jnp.max(bit).astype(jnp.int32)
    n_live[0] = 0
    live[0] = 0
    for kt in range(S // tk - WINDOW // tk):
        @pl.when((((live_bits >> kt) & 1) == 1) & (kt < kt_w))
        def _():
            live[n_live[0]] = kt
            n_live[0] = n_live[0] + 1

    n_slc = n_live[0]

    def slc_only(i, carry):
        scores, states = carry
        nxt = score_tiles(slc_streams, live[jnp.minimum(i + 1, n_slc - 1)], False)
        return nxt, softmax_pv(slc_streams, live[i], scores, states)

    first = jnp.where(n_slc > 0, live[0], 0)
    _, st_slc = lax.fori_loop(0, n_slc, slc_only, (score_tiles(slc_streams, first, False), (flash_init,) * G))
    both = slc_streams + win_streams
    st_all = lax.fori_loop(kt_w, kt_d + 1, lambda kt, st: softmax_pv(both, kt, score_tiles(both, kt, True), st),
                           st_slc + (flash_init,) * G)
    o_all = []
    for _, acc in st_all:
        den = acc[HEAD_DIM:HEAD_DIM + 1, :]
        o_all.append(acc[:HEAD_DIM] / jnp.where(den > 0.0, den, 1.0))
    o_slcs, o_wins = o_all[:G], o_all[G:]

    for g in range(G):
        for r in range(R):
            hh = g * R + r
            cs = slice(r * tq, (r + 1) * tq)
            o_ref[hh * HEAD_DIM:(hh + 1) * HEAD_DIM, :] = (
                gate_sig[hh:hh + 1, :] * o_cmps[g][:, cs]
                + gate_sig[NSA_HEADS + hh:NSA_HEADS + hh + 1, :] * o_slcs[g][:, cs]
                + gate_sig[2 * NSA_HEADS + hh:2 * NSA_HEADS + hh + 1, :] * o_wins[g][:, cs])


def _nsa_consts(S):
    tq, tk, M = NSA_TQ, NSA_TK, NSA_M
    NC = S // CMP_STRIDE
    NS = S // SLC_BLOCK
    cs = np.arange(NC)[:, None] * CMP_STRIDE
    ss = np.arange(NS)[None, :] * SLC_BLOCK
    ov = np.clip(np.minimum(cs + CMP_BLOCK, ss + SLC_BLOCK) - np.maximum(cs, ss), 0, None) / CMP_BLOCK
    ov[NC - 1:] = 0.0
    ish = np.zeros((NS, LANES), np.float32)
    ish[np.arange(NS), SEL_LANE0 + np.arange(NS)] = 1.0
    t = np.arange(S).reshape(S // tq, 1, tq)
    n = np.arange(NC).reshape(1, NC, 1)
    cbias = np.where((t >= n * CMP_STRIDE + CMP_BLOCK - 1) & (n < NC - 1), 0.0, -MASK_BIG).astype(np.float32)
    i = np.arange(tk)[:, None]
    j = np.arange(M)[None, :] % tq
    tbl = np.stack([np.zeros((tk, M)), np.where(i <= j, 0.0, -MASK_BIG), np.where(i > j, 0.0, -MASK_BIG)])
    return (jnp.asarray(ov.T, BF16), jnp.asarray(ish, BF16), jnp.asarray(cbias), jnp.asarray(tbl, F32))


def _nsa_call(nq, kv, vt, kcx, vct, misct, q_norm_g, k_norm_g, B, S):
    T = B * S
    tq, tk, M = NSA_TQ, NSA_TK, NSA_M
    G = NSA_KV_HEADS
    nqt = S // tq
    NC = S // CMP_STRIDE
    NS = S // SLC_BLOCK
    ovt, ish, cbias, tbl = _nsa_consts(S)
    tile2 = lambda a: jnp.concatenate([a, a], axis=-1)
    row = lambda b, i: (b * nqt + i, 0)
    cst = lambda b, i: (0, 0)
    return pl.pallas_call(
        functools.partial(_nsa_kernel, S=S),
        grid=(B, nqt),
        in_specs=[pl.BlockSpec((tq, NSA_WIDTH), row),
                  pl.BlockSpec((S, KV_COLS), lambda b, i: (b, 0)),
                  pl.BlockSpec((None, S // tk, VT_ROWS, tk), lambda b, i: (b, 0, 0, 0)),
                  pl.BlockSpec((None, G, NC, LANES), lambda b, i: (b, 0, 0, 0)),
                  pl.BlockSpec((None, G, HEAD_DIM, NC), lambda b, i: (b, 0, 0, 0)),
                  pl.BlockSpec((LANES, tq), lambda b, i: (0, b * nqt + i)),
                  pl.BlockSpec((None, NC, tq), lambda b, i: (i, 0, 0)),
                  pl.BlockSpec((3, tk, M), lambda b, i: (0, 0, 0)),
                  pl.BlockSpec((1, LANES), cst),
                  pl.BlockSpec((3, LANES), cst),
                  pl.BlockSpec((NS, NC), cst),
                  pl.BlockSpec((NS, LANES), cst)],
        out_specs=pl.BlockSpec((NSA_WIDTH, tq), lambda b, i: (0, b * nqt + i)),
        out_shape=jax.ShapeDtypeStruct((NSA_WIDTH, T), F32),
        scratch_shapes=[pltpu.VMEM((G, S, LANES), BF16), pltpu.VMEM((G, S, LANES), BF16),
                        pltpu.SMEM((S // tk,), jnp.int32), pltpu.SMEM((1,), jnp.int32)],
        compiler_params=_cparams("parallel", "arbitrary"),
        name="nsa_attn",
    )(nq, kv, vt, kcx, vct, misct, cbias, tbl, tile2(q_norm_g), tile2(k_norm_g), ovt, ish)


def _out_proj_kernel(yg_ref, yn_ref, x_ref, g1_ref, sc_ref, sh_ref, ng_ref, wo_ref, rwt_ref, rb_ref,
                     x1_ref, h2_ref, gate_ref, gate_t_ref):
    mix = _dot(yg_ref[...], wo_ref[:GDN_WIDTH, :]) + _dot_tn(yn_ref[...], wo_ref[GDN_WIDTH:, :])
    x1 = x_ref[...] + g1_ref[...] * mix
    x1_ref[...] = x1
    h2 = _rms(x1, ng_ref[...]) * (1.0 + sc_ref[...]) + sh_ref[...]
    h2_ref[...] = h2.astype(BF16)

    tm = h2.shape[0]
    logits = _dot_nt_hi(rwt_ref[...], h2)
    score = _sigmoid(logits)
    biased = score + rb_ref[...]
    b = [biased[e:e + 1, :] for e in range(N_EXPERTS)]
    n_groups = N_EXPERTS // EXPERTS_PER_GROUP
    gscore = []
    for gi in range(n_groups):
        vals = b[gi * EXPERTS_PER_GROUP:(gi + 1) * EXPERTS_PER_GROUP]
        best = None
        for i in range(EXPERTS_PER_GROUP):
            for j in range(i + 1, EXPERTS_PER_GROUP):
                pair = vals[i] + vals[j]
                best = pair if best is None else jnp.maximum(best, pair)
        gscore.append(best)
    gbest = jnp.zeros((1, tm), jnp.int32)
    top = gscore[0]
    for gi in range(1, n_groups):
        better = gscore[gi] > top
        gbest = jnp.where(better, gi, gbest)
        top = jnp.where(better, gscore[gi], top)
    erow = lax.broadcasted_iota(jnp.int32, (N_EXPERTS, tm), 0)
    gates = jnp.zeros((N_EXPERTS, tm), F32)
    for e in range(N_EXPERTS):
        gi = e // EXPERTS_PER_GROUP
        rank = jnp.zeros((1, tm), F32)
        for e2 in range(gi * EXPERTS_PER_GROUP, (gi + 1) * EXPERTS_PER_GROUP):
            if e2 == e:
                continue
            beats = (b[e2] > b[e]) | ((b[e2] == b[e]) & (e2 < e))
            rank = rank + jnp.where(beats, 1.0, 0.0)
        chosen = (gbest == gi) & (rank < 2.0)
        ge = jnp.where(chosen, score[e:e + 1, :], 0.0)
        gates = gates + jnp.where(erow == e, ge, 0.0)
    gates = gates / jnp.sum(gates, axis=0, keepdims=True)
    pad_rows = jnp.zeros((8 - EXPERTS_PER_GROUP, tm), F32)
    for gi in range(n_groups):
        gate_t_ref[gi] = jnp.concatenate(
            [gates[gi * EXPERTS_PER_GROUP:(gi + 1) * EXPERTS_PER_GROUP, :], pad_rows], axis=0)
    ident = (lax.broadcasted_iota(jnp.int32, (N_EXPERTS, LANES), 0)
             == lax.broadcasted_iota(jnp.int32, (N_EXPERTS, LANES), 1))
    ident = jnp.where(ident, 1.0, 0.0).astype(BF16)
    hi, mid, lo = _split3(gates)
    dn = (((0,), (0,)), ((), ()))
    gate_ref[...] = (lax.dot_general(hi, ident, dn, preferred_element_type=F32)
                     + lax.dot_general(mid, ident, dn, preferred_element_type=F32)
                     + lax.dot_general(lo, ident, dn, preferred_element_type=F32))


def _out_proj_call(yg, yn, x2, g1, sc2, sh2, ng, wo, rwt, rb, S):
    T, D = x2.shape
    tm = 1024
    per_b = S // tm
    row = lambda i: (i, 0)
    bat = lambda i: (i // per_b, 0, 0)
    cst = lambda i: (0, 0)
    return pl.pallas_call(
        _out_proj_kernel,
        grid=(T // tm,),
        in_specs=[pl.BlockSpec((tm, GDN_WIDTH), row), pl.BlockSpec((NSA_WIDTH, tm), lambda i: (0, i)),
                  pl.BlockSpec((tm, D), row),
                  pl.BlockSpec((None, 1, D), bat), pl.BlockSpec((None, 1, D), bat), pl.BlockSpec((None, 1, D), bat),
                  pl.BlockSpec((1, D), cst), pl.BlockSpec((D, D), cst),
                  pl.BlockSpec((N_EXPERTS, D), cst), pl.BlockSpec((N_EXPERTS, 1), cst)],
        out_specs=[pl.BlockSpec((tm, D), row), pl.BlockSpec((tm, D), row), pl.BlockSpec((tm, LANES), row),
                   pl.BlockSpec((N_EXPERTS // EXPERTS_PER_GROUP, 8, tm), lambda i: (0, 0, i))],
        out_shape=[jax.ShapeDtypeStruct((T, D), F32), jax.ShapeDtypeStruct((T, D), BF16),
                   jax.ShapeDtypeStruct((T, LANES), F32),
                   jax.ShapeDtypeStruct((N_EXPERTS // EXPERTS_PER_GROUP, 8, T), F32)],
        compiler_params=_cparams("parallel"),
        name="out_proj_router",
    )(yg, yn, x2, g1, sc2, sh2, ng, wo, rwt, rb)


MOE_TM = 1024
MOE_RB = 288
MOE_PASS = 256


def _moe_kernel(h_ref, gates_ref, gt8_ref, tri_ref, trit_ref, x1_ref, g2_ref, wg_ref, wu_ref, wd_ref, o_ref):
    g = pl.program_id(1)
    tm, rb = MOE_TM, MOE_RB
    nlb = tm // LANES

    @pl.when(g == 0)
    def _():
        o_ref[...] = jnp.zeros_like(o_ref)

    gt8 = gt8_ref[...]
    in_row = jnp.sum(gt8, axis=0, keepdims=True) > 0.0
    first = lax.broadcasted_iota(jnp.int32, (8, LANES), 0) == 0
    off = jnp.zeros((1, 1), F32)
    keys = []
    for b in range(nlb):
        blk = in_row[:, b * LANES:(b + 1) * LANES]
        one = jnp.where(blk & first, 1.0, 0.0).astype(BF16)
        within = jnp.dot(one, tri_ref[...], preferred_element_type=F32)[0:1, :]
        keys.append(jnp.where(blk, within + off, -1.0))
        off = off + jnp.sum(jnp.where(blk, 1.0, 0.0), axis=1, keepdims=True)
    key_row = jnp.concatenate(keys, axis=1)
    n_g = jnp.sum(off).astype(jnp.int32)
    lane = lax.broadcasted_iota(jnp.int32, (tm, LANES), 1)
    mine = (lane >= g * EXPERTS_PER_GROUP) & (lane < (g + 1) * EXPERTS_PER_GROUP)
    in_col = jnp.sum(jnp.where(mine, gates_ref[...], 0.0), axis=1, keepdims=True) > 0.0
    off = jnp.zeros((1, 1), F32)
    keys = []
    for b in range(nlb):
        blk = in_col[b * LANES:(b + 1) * LANES, :]
        one = jnp.broadcast_to(jnp.where(blk, 1.0, 0.0), (LANES, LANES)).astype(BF16)
        within = jnp.dot(trit_ref[...], one, preferred_element_type=F32)[:, 0:1]
        keys.append(jnp.where(blk, within + off, -1.0))
        off = off + jnp.sum(jnp.where(blk, 1.0, 0.0), axis=0, keepdims=True)
    key_col = jnp.concatenate(keys, axis=0)

    src = lax.broadcasted_iota(jnp.int32, (8, LANES), 0)
    dst = lax.broadcasted_iota(jnp.int32, (8, LANES), 1)
    tn_dims = (((0,), (0,)), ((), ()))
    gcols = jnp.zeros((tm, LANES), F32)
    for p, piece in enumerate(_split3(gt8)):
        place = jnp.where((dst == src + EXPERTS_PER_GROUP * p) & (src < EXPERTS_PER_GROUP), 1.0, 0.0).astype(BF16)
        gcols = gcols + lax.dot_general(piece, place, tn_dims, preferred_element_type=F32)
    gcols = gcols.astype(BF16)

    def block(j, carry):
        r0 = (j * rb).astype(F32)
        sel = jnp.where(key_row - r0 == lax.broadcasted_iota(jnp.int32, (rb, tm), 0).astype(F32), 1.0, 0.0)
        sel = sel.astype(BF16)
        xb = jnp.dot(sel, h_ref[...], preferred_element_type=F32).astype(BF16)
        gp = jnp.dot(sel, gcols, preferred_element_type=F32)
        y = jnp.zeros((rb, o_ref.shape[1]), F32)
        for k in range(EXPERTS_PER_GROUP):
            gate = (gp[:, k:k + 1] + gp[:, EXPERTS_PER_GROUP + k:EXPERTS_PER_GROUP + k + 1]
                    + gp[:, 2 * EXPERTS_PER_GROUP + k:2 * EXPERTS_PER_GROUP + k + 1])
            act = (_silu(jnp.dot(xb, wg_ref[k], preferred_element_type=F32))
                   * jnp.dot(xb, wu_ref[k], preferred_element_type=F32))
            y = y + _dot(act * gate, wd_ref[k])
        yb = y.astype(BF16)
        rel = key_col - r0

        def scatter(lo, hi):
            sel_t = jnp.where(rel == (lo + lax.broadcasted_iota(jnp.int32, (tm, hi - lo), 1)).astype(F32), 1.0, 0.0)
            o_ref[...] += jnp.dot(sel_t.astype(BF16), yb[lo:hi], preferred_element_type=F32)

        scatter(0, MOE_PASS)

        @pl.when(n_g > j * rb + MOE_PASS)
        def _():
            scatter(MOE_PASS, rb)

        return carry

    lax.fori_loop(0, (n_g + rb - 1) // rb, block, 0)

    @pl.when(g == N_EXPERTS // EXPERTS_PER_GROUP - 1)
    def _():
        o_ref[...] = x1_ref[...] + g2_ref[...] * o_ref[...]


def _moe_call(h2, gates, gates_t, x1, g2, wg, wu, wd, S):
    T, D = x1.shape
    tm = MOE_TM
    per_b = S // tm
    n_groups = N_EXPERTS // EXPERTS_PER_GROUP
    i = np.arange(LANES)
    tri = jnp.asarray(i[:, None] < i[None, :], BF16)
    row = lambda i, g: (i, 0)
    cst = lambda i, g: (0, 0)
    wsel = lambda i, g: (g, 0, 0)
    return pl.pallas_call(
        _moe_kernel,
        grid=(T // tm, n_groups),
        in_specs=[pl.BlockSpec((tm, D), row), pl.BlockSpec((tm, LANES), row),
                  pl.BlockSpec((None, 8, tm), lambda i, g: (g, 0, i)),
                  pl.BlockSpec((LANES, LANES), cst), pl.BlockSpec((LANES, LANES), cst),
                  pl.BlockSpec((tm, D), row),
                  pl.BlockSpec((None, 1, D), lambda i, g: (i // per_b, 0, 0)),
                  pl.BlockSpec((EXPERTS_PER_GROUP, D, EXPERT_FF), wsel),
                  pl.BlockSpec((EXPERTS_PER_GROUP, D, EXPERT_FF), wsel),
                  pl.BlockSpec((EXPERTS_PER_GROUP, EXPERT_FF, D), wsel)],
        out_specs=pl.BlockSpec((tm, D), row),
        out_shape=jax.ShapeDtypeStruct((T, D), F32),
        compiler_params=_cparams("parallel", "arbitrary"),
        name="moe",
    )(h2, gates, gates_t, tri, tri.T, x1, g2, wg, wu, wd)


def kernel(x, c, ada_w, ada_b, norm1_g, norm2_g, w_in, gdn_conv_w, gdn_a_log, gdn_dt_bias, gdn_norm_g,
           nsa_q_norm_g, nsa_k_norm_g, cmp_pe, cmp_w1, cmp_b1, cmp_w2, cmp_b2, w_out, router_w, router_bias,
           exp_w_gate, exp_w_up, exp_w_down):
    B, S, D = x.shape
    L = ada_w.shape[0]
    T = B * S
    G = NSA_KV_HEADS
    NC = S // CMP_STRIDE
    N = S // GDN_CHUNK
    P = GDN_HEADS // GDN_HPB
    W = GDN_HPB * HEAD_DIM

    mod = _ada_call(c, ada_w, ada_b)
    rwt = router_w.T
    rb = router_bias.reshape(N_EXPERTS, 1)
    x2 = x.reshape(T, D)
    for l in range(L):
        m6 = mod[l].reshape(B, 6, 1, D)
        sh1, sc1, g1, sh2, sc2, g2 = (m6[:, i] for i in range(6))
        qkv, z, nq, kv, misc, misct, vt = _in_proj_call(
            x2, sc1, sh1, norm1_g[l].reshape(1, D), *_pad_w_in(w_in[l]), S)

        a_rows = misct[:GDN_HEADS].reshape(P, GDN_HPB, B, N, GDN_CHUNK).transpose(2, 0, 3, 1, 4).reshape(B, P, N, W)
        rep = lambda t: jnp.repeat(t, HEAD_DIM).reshape(1, GDN_WIDTH)
        y_gdn = _gdn_call(qkv, z, misc, a_rows, gdn_conv_w[l], rep(gdn_a_log[l]), rep(gdn_dt_bias[l]),
                          jnp.tile(gdn_norm_g[l], GDN_HPB).reshape(1, W), B, S)

        w2 = cmp_w2[l]
        kcx, vct = _cmp_call(
            kv, cmp_pe[l].reshape(2, 1, CMP_BLOCK * HEAD_DIM), cmp_w1[l].astype(BF16),
            cmp_b1[l].reshape(2, 1, CMP_HIDDEN), jnp.pad(w2[0], ((0, 0), (0, LANES - HEAD_DIM))).astype(BF16),
            w2[1].T.astype(BF16), jnp.pad(cmp_b2[l, 0], (0, LANES - HEAD_DIM)).reshape(1, LANES),
            cmp_b2[l, 1].reshape(HEAD_DIM, 1), jnp.pad(nsa_k_norm_g[l, 0], (0, LANES - HEAD_DIM)).reshape(1, LANES),
            B, S)
        y_nsa_t = _nsa_call(nq, kv, vt.reshape(B, S // NSA_TK, VT_ROWS, NSA_TK), kcx, vct, misct,
                            nsa_q_norm_g[l].reshape(1, HEAD_DIM), nsa_k_norm_g[l], B, S)

        x1, h2, gates, gates_t = _out_proj_call(y_gdn, y_nsa_t, x2, g1, sc2, sh2, norm2_g[l].reshape(1, D),
                                       w_out[l].astype(BF16), rwt, rb, S)
        x2 = _moe_call(h2, gates, gates_t, x1, g2, exp_w_gate[l].astype(BF16), exp_w_up[l].astype(BF16),
                       exp_w_down[l].astype(BF16), S)
    return x2.reshape(B, S, D)
```

```python
import functools

import numpy as np
import jax
import jax.numpy as jnp
from jax import lax
from jax.experimental import pallas as pl
from jax.experimental.pallas import tpu as pltpu

F32 = jnp.float32
BF16 = jnp.bfloat16

HEAD_DIM = 64
GDN_HEADS = 8
GDN_WIDTH = GDN_HEADS * HEAD_DIM
GDN_CONV = 4
GDN_CHUNK = 64
NSA_HEADS = 8
NSA_KV_HEADS = 2
NSA_GROUP = NSA_HEADS // NSA_KV_HEADS
NSA_WIDTH = NSA_HEADS * HEAD_DIM
NSA_KV_WIDTH = NSA_KV_HEADS * HEAD_DIM
CMP_BLOCK = 32
CMP_STRIDE = 16
CMP_HIDDEN = 256
SLC_BLOCK = 64
SLC_TOPN = 8
WINDOW = 512
N_EXPERTS = 16
EXPERTS_PER_GROUP = 4
EXPERT_FF = 512
RMS_EPS = 1e-6
NEG_BIG = -1e30
SEL_BIG = 1e9

LANES = 128
VMEM_LIMIT_BYTES = 56 * 1024 * 1024

C_QKV = 0
C_Z = 3 * GDN_WIDTH
C_NQ = C_Z + GDN_WIDTH
C_KV = C_NQ + NSA_WIDTH
KV_COLS = 4 * NSA_KV_WIDTH
C_MISC = C_KV + KV_COLS
IN_PAD_COLS = C_MISC + LANES
MISC_B0 = GDN_HEADS
MISC_GATE0 = 16
VT_SLAB = HEAD_DIM + 16
VT_ROWS = 2 * NSA_KV_HEADS * VT_SLAB


def _cparams(*sem):
    return pltpu.CompilerParams(dimension_semantics=sem, vmem_limit_bytes=VMEM_LIMIT_BYTES)


def _dot(a, b):
    return jnp.dot(a.astype(BF16), b.astype(BF16), preferred_element_type=F32)


def _dot_nt(a, b):
    return lax.dot_general(a.astype(BF16), b.astype(BF16), (((1,), (1,)), ((), ())),
                           preferred_element_type=F32)


def _dot_tn(a, b):
    return lax.dot_general(a.astype(BF16), b.astype(BF16), (((0,), (0,)), ((), ())),
                           preferred_element_type=F32)


def _split2(x):
    hi = x.astype(BF16)
    lo = (x - hi.astype(F32)).astype(BF16)
    return hi, lo


def _split3(x):
    hi = x.astype(BF16)
    r = x - hi.astype(F32)
    mid = r.astype(BF16)
    lo = (r - mid.astype(F32)).astype(BF16)
    return hi, mid, lo


def _dot_hi(a, b):
    ah, al = _split2(a)
    bh, bl = _split2(b)
    return (jnp.dot(ah, bh, preferred_element_type=F32) + jnp.dot(al, bh, preferred_element_type=F32)
            + jnp.dot(ah, bl, preferred_element_type=F32))


def _dot_nt_hi(a, b):
    ah, al = _split2(a)
    bh, bl = _split2(b)
    dn = (((1,), (1,)), ((), ()))
    return (lax.dot_general(ah, bh, dn, preferred_element_type=F32)
            + lax.dot_general(al, bh, dn, preferred_element_type=F32)
            + lax.dot_general(ah, bl, dn, preferred_element_type=F32))


def _dot_exact_rhs(a, b_bf16):
    hi, mid, lo = _split3(a)
    return (jnp.dot(hi, b_bf16, preferred_element_type=F32) + jnp.dot(mid, b_bf16, preferred_element_type=F32)
            + jnp.dot(lo, b_bf16, preferred_element_type=F32))


def _rms(x, g):
    return x * lax.rsqrt(jnp.mean(x * x, axis=-1, keepdims=True) + RMS_EPS) * g


def _sigmoid(x):
    return 0.5 * (jnp.tanh(0.5 * x) + 1.0)


def _silu(x):
    return x * _sigmoid(x)


def _ada_kernel(c_ref, w_ref, b_ref, o_ref):
    o_ref[...] = _dot_hi(_silu(c_ref[...]), w_ref[...]) + b_ref[...]


def _ada_call(c, ada_w, ada_b):
    L, D, N = ada_w.shape
    B = c.shape[0]
    tn = 1024
    return pl.pallas_call(
        _ada_kernel,
        grid=(L, N // tn),
        in_specs=[pl.BlockSpec((B, D), lambda l, j: (0, 0)),
                  pl.BlockSpec((None, D, tn), lambda l, j: (l, 0, j)),
                  pl.BlockSpec((None, 1, tn), lambda l, j: (l, 0, j))],
        out_specs=pl.BlockSpec((None, B, tn), lambda l, j: (l, 0, j)),
        out_shape=jax.ShapeDtypeStruct((L, B, N), F32),
        compiler_params=_cparams("parallel", "parallel"),
        name="ada_mod",
    )(c, ada_w, ada_b.reshape(L, 1, N))


IN_SEGS = ((C_QKV, 3 * GDN_WIDTH), (C_Z, GDN_WIDTH), (C_NQ, NSA_WIDTH), (C_KV, KV_COLS), (C_MISC, LANES))


def _in_proj_kernel(x_ref, sc_ref, sh_ref, g_ref, w_ref, wmt_ref, wvt_ref, *o_refs):
    h = _rms(x_ref[...], g_ref[...]) * (1.0 + sc_ref[...]) + sh_ref[...]
    hb = h.astype(BF16)
    for (c0, n), o_ref in zip(IN_SEGS, o_refs[:-2]):
        o_ref[...] = jnp.dot(hb, w_ref[:, c0:c0 + n], preferred_element_type=F32)
    nt_dims = (((1,), (1,)), ((), ()))
    o_refs[-2][...] = lax.dot_general(wmt_ref[...], hb, nt_dims, preferred_element_type=F32)
    vt = lax.dot_general(wvt_ref[...], hb, nt_dims, preferred_element_type=F32).astype(BF16)
    ones = jnp.ones((VT_SLAB - HEAD_DIM, LANES), BF16)
    for j in range(o_refs[-1].shape[0]):
        for st in range(2 * NSA_KV_HEADS):
            o_refs[-1][j, st * VT_SLAB:st * VT_SLAB + HEAD_DIM, :] = vt[st * HEAD_DIM:(st + 1) * HEAD_DIM,
                                                                         j * LANES:(j + 1) * LANES]
            o_refs[-1][j, st * VT_SLAB + HEAD_DIM:(st + 1) * VT_SLAB, :] = ones


def _in_proj_call(x2, sc, sh, g, w_pad, wvt, S):
    T, D = x2.shape
    tm = 512
    per_b = S // tm
    row = lambda i: (i, 0)
    bat = lambda i: (i // per_b, 0, 0)
    return pl.pallas_call(
        _in_proj_kernel,
        grid=(T // tm,),
        in_specs=[pl.BlockSpec((tm, D), row),
                  pl.BlockSpec((None, 1, D), bat),
                  pl.BlockSpec((None, 1, D), bat),
                  pl.BlockSpec((1, D), lambda i: (0, 0)),
                  pl.BlockSpec((D, IN_PAD_COLS), lambda i: (0, 0)),
                  pl.BlockSpec((LANES, D), lambda i: (0, 0)),
                  pl.BlockSpec((2 * NSA_KV_WIDTH, D), lambda i: (0, 0))],
        out_specs=([pl.BlockSpec((tm, n), row) for _, n in IN_SEGS]
                   + [pl.BlockSpec((LANES, tm), lambda i: (0, i)),
                      pl.BlockSpec((tm // LANES, VT_ROWS, LANES), lambda i: (i, 0, 0))]),
        out_shape=([jax.ShapeDtypeStruct((T, n), F32) for _, n in IN_SEGS]
                   + [jax.ShapeDtypeStruct((LANES, T), F32),
                      jax.ShapeDtypeStruct((T // LANES, VT_ROWS, LANES), BF16)]),
        compiler_params=_cparams("parallel"),
        name="in_proj",
    )(x2, sc, sh, g, w_pad, w_pad[:, C_MISC:].T, wvt)


def _pad_w_in(w_in):
    D = w_in.shape[0]
    W = GDN_WIDTH
    o = 0
    gq, gk, gv, gz = (w_in[:, o + i * W:o + (i + 1) * W] for i in range(4))
    o += 4 * W
    ga = w_in[:, o:o + GDN_HEADS]
    gb = w_in[:, o + GDN_HEADS:o + 2 * GDN_HEADS]
    o += 2 * GDN_HEADS
    nq = w_in[:, o:o + NSA_WIDTH]
    o += NSA_WIDTH
    kc, vc, ks, vs, kw, vw = (w_in[:, o + i * NSA_KV_WIDTH:o + (i + 1) * NSA_KV_WIDTH] for i in range(6))
    o += 6 * NSA_KV_WIDTH
    gates = w_in[:, o:o + 3 * NSA_HEADS]
    misc = jnp.zeros((D, LANES), w_in.dtype)
    misc = misc.at[:, 0:GDN_HEADS].set(ga).at[:, MISC_B0:MISC_B0 + GDN_HEADS].set(gb)
    misc = misc.at[:, MISC_GATE0:MISC_GATE0 + 3 * NSA_HEADS].set(gates)
    cols = [gq, gk, gv, gz, nq, kc, vc, ks, kw, misc]
    return jnp.concatenate(cols, axis=1).astype(BF16), jnp.concatenate([vs, vw], axis=1).T.astype(BF16)


GDN_HPB = 2
GDN_NB = 32


def _gdn_kernel(q_ref, k_ref, v_ref, z_ref, misc_ref, ar_ref, cwq_ref, cwk_ref, cwv_ref,
                alog_ref, dtb_ref, alogr_ref, dtbr_ref, ng_ref, o_ref,
                qs, ks, kbs, xs, gcs, grs, ps, qq, qks, os_, rs, q2s, sev, *, S):
    C = GDN_CHUNK
    N = S // C
    W = GDN_HPB * HEAD_DIM
    row = lax.broadcasted_iota(jnp.int32, (S, W), 0)
    row8 = lax.broadcasted_iota(jnp.int32, (8, W), 0)

    def conv_silu(x_ref, w):
        x = x_ref[...]
        y = x * w[GDN_CONV - 1:GDN_CONV, :]
        for s in range(1, GDN_CONV):
            head = jnp.where(row8 >= s, pltpu.roll(x[:8], s, axis=0), 0.0)
            xs = jnp.concatenate([head, x_ref[pl.ds(8 - s, S - 8), :]], axis=0)
            y = y + xs * w[GDN_CONV - 1 - s:GDN_CONV - s, :]
        return _silu(y)

    bd = jnp.where(lax.broadcasted_iota(jnp.int32, (W, W), 0) // HEAD_DIM
                   == lax.broadcasted_iota(jnp.int32, (W, W), 1) // HEAD_DIM, 1.0, 0.0).astype(BF16)

    def head_sumsq(x):
        hi, lo = _split2(x * x)
        return jnp.dot(hi, bd, preferred_element_type=F32) + jnp.dot(lo, bd, preferred_element_type=F32)

    def head_l2(x):
        return x * lax.rsqrt(head_sumsq(x) + RMS_EPS)

    q = head_l2(conv_silu(q_ref, cwq_ref[...])) * (HEAD_DIM ** -0.5)
    k = head_l2(conv_silu(k_ref, cwk_ref[...]))
    v = conv_silu(v_ref, cwv_ref[...])

    hd0 = pl.program_id(1) * GDN_HPB
    src = lax.broadcasted_iota(jnp.int32, (LANES, W), 0)
    dst = lax.broadcasted_iota(jnp.int32, (LANES, W), 1) // HEAD_DIM + hd0
    a_exp = _dot_exact_rhs(misc_ref[...], jnp.where(src == dst, 1.0, 0.0).astype(BF16))
    b_exp = _dot_exact_rhs(misc_ref[...], jnp.where(src == dst + MISC_B0, 1.0, 0.0).astype(BF16))
    beta = _sigmoid(b_exp)

    def log_decay(a, alog, dtb):
        xx = a + dtb
        sp = jnp.maximum(xx, 0.0) + jnp.log(1.0 + jnp.exp(-jnp.abs(xx)))
        return -jnp.exp(alog) * sp

    gc = log_decay(a_exp, alog_ref[...], dtb_ref[...])
    pos = row % C
    for s in (1, 2, 4, 8, 16, 32):
        gc = gc + jnp.where(pos >= s, pltpu.roll(gc, s, axis=0), 0.0)
    gr = log_decay(ar_ref[...], alogr_ref[...], dtbr_ref[...])
    lpos = lax.broadcasted_iota(jnp.int32, (N, W), 1) % C
    for s in (1, 2, 4, 8, 16, 32):
        gr = gr + jnp.where(lpos >= s, pltpu.roll(gr, s, axis=1), 0.0)

    kb = k * beta
    vb = v * beta
    kbe = kb * jnp.exp(gc)
    for h in range(GDN_HPB):
        sl = slice(h * HEAD_DIM, (h + 1) * HEAD_DIM)
        qs[h] = q[:, sl]
        ks[h] = k[:, sl]
        kbs[h] = kb[:, sl]
        gcs[h] = gc[:, sl]
        for n in range(N):
            grs[h, n] = jnp.broadcast_to(gr[n:n + 1, sl], (8, HEAD_DIM))
        xs[h] = jnp.concatenate([kbe[:, sl], vb[:, sl]], axis=1)

    NB = GDN_NB
    ci3 = lax.broadcasted_iota(jnp.int32, (1, C, C), 1)
    cj3 = lax.broadcasted_iota(jnp.int32, (1, C, C), 2)
    eye3 = jnp.where(ci3 == cj3, 1.0, 0.0)

    def bmm(a, b):
        return lax.dot_general(a.astype(BF16), b.astype(BF16), (((2,), (1,)), ((0,), (0,))),
                               preferred_element_type=F32)

    def bmm_nt(a, b):
        return lax.dot_general(a.astype(BF16), b.astype(BF16), (((2,), (2,)), ((0,), (0,))),
                               preferred_element_type=F32)

    def intra(nb, carry):
        rows = pl.ds(pl.multiple_of(nb * (NB * C), NB * C), NB * C)
        for h in range(GDN_HPB):
            kk = ks[h, rows, :].reshape(NB, C, HEAD_DIM)
            kbq = jnp.concatenate([kbs[h, rows, :].reshape(NB, C, HEAD_DIM),
                                   qs[h, rows, :].reshape(NB, C, HEAD_DIM)], axis=1)
            a2 = bmm_nt(kbq, kk)
            gcol = gcs[h, rows, :].reshape(NB, C, HEAD_DIM)
            grow = grs[h, pl.ds(nb * NB, NB), 0:1, :]
            decay = jnp.exp(jnp.where(ci3 >= cj3, gcol - grow, NEG_BIG))
            low = jnp.where(ci3 > cj3, a2[:, :C] * decay, 0.0)
            qks[h, rows, :] = (a2[:, C:] * decay).reshape(NB * C, C)
            tinv = eye3 - jnp.where(ci3 // 2 == cj3 // 2, low, 0.0)
            for lv in range(1, 6):
                s = 2 ** lv
                off = jnp.where((ci3 // (2 * s) == cj3 // (2 * s)) & (ci3 // s != cj3 // s), low, 0.0)
                tinv = tinv - bmm(tinv, bmm(off, tinv))
            x = bmm(tinv, xs[h, rows, :].reshape(NB, C, 2 * HEAD_DIM))
            xs[h, rows, :] = x.reshape(NB * C, 2 * HEAD_DIM)
            kd = kk * jnp.exp(gcol[:, C - 1:C, :] - gcol)
            pq = lax.dot_general(kd.astype(BF16), x.astype(BF16), (((1,), (1,)), ((0,), (0,))),
                                 preferred_element_type=F32)
            pn = pq[:, :, :HEAD_DIM]
            qn = pq[:, :, HEAD_DIM:]
            ps[h, rows, :] = pn.reshape(NB * C, HEAD_DIM)
            qq[h, rows, :] = qn.reshape(NB * C, HEAD_DIM)
            cd = jnp.exp(gcol[:, C - 1:C, :]).reshape(NB // 2, 2, 1, HEAD_DIM)
            pn = pn.reshape(NB // 2, 2, HEAD_DIM, HEAD_DIM)
            qn = qn.reshape(NB // 2, 2, HEAD_DIM, HEAD_DIM)
            p0, p1, q0, q1, c0, c1 = pn[:, 0], pn[:, 1], qn[:, 0], qn[:, 1], cd[:, 0], cd[:, 1]
            t = bmm(p1, pq.reshape(NB // 2, 2, HEAD_DIM, 2 * HEAD_DIM)[:, 0])
            prow = pl.ds(pl.multiple_of(nb * (NB // 2 * C), NB // 2 * C), NB // 2 * C)
            rs[h, prow, :] = (c1 * p0 + c0 * p1 - t[:, :, :HEAD_DIM]).reshape(NB // 2 * C, HEAD_DIM)
            q2s[h, prow, :] = (c1 * q0 - t[:, :, HEAD_DIM:] + q1).reshape(NB // 2 * C, HEAD_DIM)
        return carry

    lax.fori_loop(0, N // NB, intra, 0)

    def inter(k, states):
        sl = pl.ds(pl.multiple_of(k * C, C), C)
        new_states = []
        for h in range(GDN_HPB):
            st = states[h]
            sev[h, k] = st
            gsum = gcs[h, pl.ds(2 * k * C + C - 1, 1), :] + gcs[h, pl.ds(2 * k * C + 2 * C - 1, 1), :]
            new_states.append(st * jnp.exp(gsum) - _dot_hi(rs[h, sl, :], st) + q2s[h, sl, :])
        return tuple(new_states)

    lax.fori_loop(0, N // 2, inter, tuple(jnp.zeros((HEAD_DIM, HEAD_DIM), F32) for _ in range(GDN_HPB)))

    def outputs(nb, carry):
        rows = pl.ds(pl.multiple_of(nb * (NB * C), NB * C), NB * C)
        for h in range(GDN_HPB):
            gcol = gcs[h, rows, :].reshape(NB, C, HEAD_DIM)
            s0 = sev[h, pl.ds(nb * (NB // 2), NB // 2)]
            p0 = ps[h, rows, :].reshape(NB // 2, 2, HEAD_DIM, HEAD_DIM)[:, 0]
            q0 = qq[h, rows, :].reshape(NB // 2, 2, HEAD_DIM, HEAD_DIM)[:, 0]
            c0 = jnp.exp(gcol[:, C - 1:C, :]).reshape(NB // 2, 2, 1, HEAD_DIM)[:, 0]
            s1 = c0 * s0 - bmm(p0, s0) + q0
            st = jnp.stack([s0, s1], axis=1).reshape(NB, HEAD_DIM, HEAD_DIM)
            x = xs[h, rows, :].reshape(NB, C, 2 * HEAD_DIM)
            qd = qs[h, rows, :].reshape(NB, C, HEAD_DIM) * jnp.exp(gcol)
            t1 = bmm(jnp.concatenate([x[:, :, :HEAD_DIM], qd], axis=1), st)
            vn = x[:, :, HEAD_DIM:] - t1[:, :C]
            o = t1[:, C:] + bmm(qks[h, rows, :].reshape(NB, C, C), vn)
            os_[h, rows, :] = o.reshape(NB * C, HEAD_DIM)
        return carry

    lax.fori_loop(0, N // NB, outputs, 0)

    o = jnp.concatenate([os_[h] for h in range(GDN_HPB)], axis=1)
    o = o * lax.rsqrt(head_sumsq(o) * (1.0 / HEAD_DIM) + RMS_EPS) * ng_ref[...]
    o_ref[...] = o * _silu(z_ref[...])


def _gdn_call(qkv, z, misc, a_rows, conv_w, alog_exp, dtb_exp, norm_g, B, S):
    T = B * S
    W = GDN_HPB * HEAD_DIM
    P = GDN_HEADS // GDN_HPB
    N = S // GDN_CHUNK
    col = lambda off: (lambda b, p: (b, off + p))
    par = lambda off: (lambda b, p: (0, off + p))
    hs = lambda n: pltpu.VMEM((GDN_HPB, S, n), F32)
    return pl.pallas_call(
        functools.partial(_gdn_kernel, S=S),
        grid=(B, P),
        in_specs=[pl.BlockSpec((S, W), col(0)), pl.BlockSpec((S, W), col(P)), pl.BlockSpec((S, W), col(2 * P)),
                  pl.BlockSpec((S, W), col(0)), pl.BlockSpec((S, LANES), lambda b, p: (b, 0)),
                  pl.BlockSpec((None, None, N, W), lambda b, p: (b, p, 0, 0)),
                  pl.BlockSpec((GDN_CONV, W), par(0)), pl.BlockSpec((GDN_CONV, W), par(P)),
                  pl.BlockSpec((GDN_CONV, W), par(2 * P)),
                  pl.BlockSpec((1, W), par(0)), pl.BlockSpec((1, W), par(0)),
                  pl.BlockSpec((1, W), par(0)), pl.BlockSpec((1, W), par(0)),
                  pl.BlockSpec((1, W), lambda b, p: (0, 0))],
        out_specs=pl.BlockSpec((S, W), col(0)),
        out_shape=jax.ShapeDtypeStruct((T, GDN_WIDTH), F32),
        scratch_shapes=[hs(HEAD_DIM), hs(HEAD_DIM), hs(HEAD_DIM), hs(2 * HEAD_DIM), hs(HEAD_DIM),
                        pltpu.VMEM((GDN_HPB, N, 8, HEAD_DIM), F32),
                        hs(HEAD_DIM), hs(HEAD_DIM), hs(HEAD_DIM), hs(HEAD_DIM),
                        pltpu.VMEM((GDN_HPB, S // 2, HEAD_DIM), F32), pltpu.VMEM((GDN_HPB, S // 2, HEAD_DIM), F32),
                        pltpu.VMEM((GDN_HPB, N // 2, HEAD_DIM, HEAD_DIM), F32)],
        compiler_params=_cparams("parallel", "parallel"),
        name="gdn",
    )(qkv, qkv, qkv, z, misc, a_rows, conv_w, conv_w, conv_w,
      alog_exp, dtb_exp, alog_exp, dtb_exp, norm_g)


def _cmp_kernel(kc_in_ref, vc_in_ref, pe_ref, w1_ref, b1_ref, w2k_ref, w2vt_ref, b2k_ref, b2v_ref, kg_ref, kc_ref, vct_ref):
    half = (CMP_BLOCK // 2) * HEAD_DIM
    nrows = kc_in_ref.shape[0] // CMP_STRIDE
    lane = lax.broadcasted_iota(jnp.int32, (nrows, LANES), 1)
    cend = lax.broadcasted_iota(jnp.int32, (nrows, LANES), 0) * CMP_STRIDE + (CMP_BLOCK - 1)
    ext = jnp.where(lane == HEAD_DIM, cend // LANES, jnp.where(lane == HEAD_DIM + 1, cend % LANES, 0)).astype(F32)

    def token_groups(x_ref):
        cols = [[] for _ in range(NSA_KV_HEADS)]
        for m in range(CMP_STRIDE // 2):
            ev = x_ref[pl.ds(2 * m, nrows, stride=CMP_STRIDE), :]
            od = x_ref[pl.ds(2 * m + 1, nrows, stride=CMP_STRIDE), :]
            cols[0].append(jnp.where(lane < HEAD_DIM, ev, pltpu.roll(od, HEAD_DIM, axis=1)))
            cols[1].append(jnp.where(lane < HEAD_DIM, pltpu.roll(ev, HEAD_DIM, axis=1), od))
        return [jnp.concatenate(c, axis=1) for c in cols]

    def hidden(x, j):
        pe = pe_ref[j]
        p = _dot(x + pe[:, :half], w1_ref[j, :half, :])
        q = _dot(x + pe[:, half:], w1_ref[j, half:, :])
        return _silu(p + pltpu.roll(q, nrows - 1, axis=0) + b1_ref[j])

    for g, x in enumerate(token_groups(kc_in_ref)):
        kc = _dot(hidden(x, 0), w2k_ref[...]) + b2k_ref[...]
        ms = jnp.sum(kc * kc, axis=-1, keepdims=True) * (1.0 / HEAD_DIM)
        kc_ref[g] = jnp.where(lane < HEAD_DIM, kc * lax.rsqrt(ms + RMS_EPS) * kg_ref[...], ext)
    for g, x in enumerate(token_groups(vc_in_ref)):
        vct_ref[g] = _dot_nt(w2vt_ref[...], hidden(x, 1)) + b2v_ref[...]


def _cmp_call(kv, pe, w1, b1, w2k, w2vt, b2k, b2v, kg, B, S):
    G = NSA_KV_HEADS
    NC = S // CMP_STRIDE
    full = lambda a: pl.BlockSpec(a.shape, lambda b: (0,) * a.ndim)
    return pl.pallas_call(
        _cmp_kernel,
        grid=(B,),
        in_specs=[pl.BlockSpec((S, NSA_KV_WIDTH), lambda b: (b, 0)), pl.BlockSpec((S, NSA_KV_WIDTH), lambda b: (b, 1))]
        + [full(a) for a in (pe, w1, b1, w2k, w2vt, b2k, b2v, kg)],
        out_specs=[pl.BlockSpec((None, G, NC, LANES), lambda b: (b, 0, 0, 0)),
                   pl.BlockSpec((None, G, HEAD_DIM, NC), lambda b: (b, 0, 0, 0))],
        out_shape=[jax.ShapeDtypeStruct((B, G, NC, LANES), F32), jax.ShapeDtypeStruct((B, G, HEAD_DIM, NC), F32)],
        compiler_params=_cparams("parallel"),
        name="nsa_compress",
    )(kv, kv, pe, w1, b1, w2k, w2vt, b2k, b2v, kg)


NSA_TQ = 128
NSA_TK = 128
NSA_M = NSA_GROUP * NSA_TQ
MASK_BIG = 2.0 ** 100
SEL_LANE0 = HEAD_DIM + 2


def _nsa_kernel(q_ref, kv_ref, vt_ref, kcx_ref, vct_ref, misct_ref, cbias_ref, tbl_ref, qg_ref, kg_ref,
                ovt_ref, ish_ref, o_ref, ks_s, kw_s, live, n_live, *, S):
    tq, tk, M = NSA_TQ, NSA_TK, NSA_M
    G, R = NSA_KV_HEADS, NSA_GROUP
    NS = S // SLC_BLOCK
    n_top = min(SLC_TOPN, NS)
    qi = pl.program_id(1)
    t0 = qi * tq
    nt_dims = (((1,), (1,)), ((), ()))

    @pl.when(qi == 0)
    def _():
        rowi = lax.broadcasted_iota(jnp.int32, (S, LANES), 0)
        lane = lax.broadcasted_iota(jnp.int32, (S, LANES), 1)
        ext = jnp.where(lane == HEAD_DIM, rowi // LANES, jnp.where(lane == HEAD_DIM + 1, rowi % LANES, 0)).astype(F32)
        ext_slc = jnp.where(lane - SEL_LANE0 == rowi // SLC_BLOCK, -MASK_BIG, ext)

        def prep(c0, g, gain, extra):
            kraw = kv_ref[:, c0:c0 + LANES]
            if g == 1:
                kraw = pltpu.roll(kraw, HEAD_DIM, axis=1)
            ms = jnp.sum(jnp.where(lane < HEAD_DIM, kraw * kraw, 0.0), axis=-1, keepdims=True) * (1.0 / HEAD_DIM)
            return jnp.where(lane < HEAD_DIM, kraw * lax.rsqrt(ms + RMS_EPS) * gain, extra).astype(BF16)

        for g in range(G):
            ks_s[g] = prep(2 * NSA_KV_WIDTH, g, kg_ref[1:2, :], ext_slc)
            kw_s[g] = prep(3 * NSA_KV_WIDTH, g, kg_ref[2:3, :], ext)

    lane_q = lax.broadcasted_iota(jnp.int32, (tq, LANES), 1)
    gate_sig = _sigmoid(misct_ref[MISC_GATE0:MISC_GATE0 + 3 * NSA_HEADS, :])
    kt_d = t0 // tk

    def score_tiles(streams, kt, biased):
        rows = pl.ds(pl.multiple_of(kt * tk, tk), tk)
        out = []
        for qpp, k_s, g, _, idx_fn in streams:
            s = lax.dot_general(k_s[g, rows, :], qpp, nt_dims, preferred_element_type=F32)
            out.append(s + tbl_ref[idx_fn(kt)] if biased else s)
        return tuple(out)

    def softmax_pv(streams, kt, scores, states):
        soft = []
        for s, (m, _) in zip(scores, states):
            m_new = jnp.maximum(m, jnp.max(s, axis=0, keepdims=True))
            soft.append((m_new, jnp.exp(m - m_new), jnp.exp(s - m_new).astype(BF16)))
        pvs = [jnp.dot(vt_ref[kt, st[3]:st[3] + VT_SLAB, :], sf[2], preferred_element_type=F32)
               for st, sf in zip(streams, soft)]
        return tuple((m_new, alpha * acc + pv) for (m_new, alpha, _), (_, acc), pv in zip(soft, states, pvs))

    flash_init = (jnp.full((1, M), NEG_BIG, F32), jnp.zeros((VT_SLAB, M), F32))

    qpps, o_cmps, unsel = [], [], []
    for g in range(G):
        qrows = []
        for r in range(R):
            hh = g * R + r
            qpair = q_ref[:, (hh // 2) * LANES:(hh // 2 + 1) * LANES]
            if hh % 2 == 1:
                qpair = pltpu.roll(qpair, HEAD_DIM, axis=1)
            ms = jnp.sum(jnp.where(lane_q < HEAD_DIM, qpair * qpair, 0.0), axis=-1, keepdims=True) * (1.0 / HEAD_DIM)
            qn = qpair * lax.rsqrt(ms + RMS_EPS) * qg_ref[...] * (HEAD_DIM ** -0.5)
            slope = 2.0 ** (-8.0 * (hh + 1) / NSA_HEADS)
            ex = jnp.where(lane_q == HEAD_DIM, slope * LANES, jnp.where(lane_q == HEAD_DIM + 1, slope, 0.0))
            qrows.append(jnp.where(lane_q < HEAD_DIM, qn, ex))
        qbase = jnp.concatenate(qrows, axis=0)

        cb = cbias_ref[...]
        s = (lax.dot_general(kcx_ref[g].astype(BF16), qbase.astype(BF16), nt_dims, preferred_element_type=F32)
             + jnp.concatenate([cb] * R, axis=1))
        mx = jnp.maximum(jnp.max(s, axis=0, keepdims=True), NEG_BIG)
        e = jnp.exp(s - mx)
        l = jnp.sum(e, axis=0, keepdims=True)
        p_cmp = e / jnp.where(l > 0.0, l, 1.0)
        o_cmps.append(_dot(vct_ref[g], p_cmp))

        psum = p_cmp[:, 0:tq]
        for r in range(1, R):
            psum = psum + p_cmp[:, r * tq:(r + 1) * tq]
        hi, mid, lo = _split3(psum)
        ovt = ovt_ref[...]
        imp = (jnp.dot(ovt, hi, preferred_element_type=F32) + jnp.dot(ovt, mid, preferred_element_type=F32)
               + jnp.dot(ovt, lo, preferred_element_type=F32))
        blk = lax.broadcasted_iota(jnp.int32, (NS, tq), 0)
        cur = (t0 + lax.broadcasted_iota(jnp.int32, (NS, tq), 1)) // SLC_BLOCK
        valid = blk <= cur
        forced = (blk == 0) | (blk == cur) | (blk == cur - 1)
        score = jnp.where(forced, SEL_BIG, jnp.where(valid, imp, -SEL_BIG))
        rank = jnp.zeros((NS, tq), F32)
        for j in range(NS):
            cj = score[j:j + 1, :]
            beats = (cj > score) | ((cj == score) & (blk > j))
            rank = rank + jnp.where(beats, 1.0, 0.0)
        nsel = jnp.where((rank < n_top) & valid, 0.0, 1.0)
        unsel.append(jnp.min(nsel, axis=1, keepdims=True))
        nsel = nsel.astype(BF16)
        nsel_q = lax.dot_general(nsel, ish_ref[...], (((0,), (0,)), ((), ())),
                                 preferred_element_type=F32)
        in_sel = (lane_q >= SEL_LANE0) & (lane_q < SEL_LANE0 + NS)
        qpps.append(jnp.concatenate([jnp.where(in_sel, nsel_q, qrows[r]) for r in range(R)], axis=0).astype(BF16))

    slc_idx = lambda kt: jnp.where(kt == kt_d, 1, 0)
    win_lo = kt_d - WINDOW // tk
    win_idx = lambda kt: jnp.where(kt == kt_d, 1, jnp.where(kt == win_lo, 2, 0))
    slc_streams = [(qpps[g], ks_s, g, g * VT_SLAB, slc_idx) for g in range(G)]
    win_streams = [(qpps[g], kw_s, g, (G + g) * VT_SLAB, win_idx) for g in range(G)]
    kt_w = jnp.maximum(win_lo, 0)

    blocks_per_tile = tk // SLC_BLOCK
    unsel_any = unsel[0]
    for g in range(1, G):
        unsel_any = jnp.minimum(unsel_any, unsel[g])
    blk_tile = lax.broadcasted_iota(jnp.int32, (NS, 1), 0) // blocks_per_tile
    bit = jnp.zeros((1, 1), F32)
    for kt in range(S // tk - WINDOW // tk):
        in_tile = jnp.min(jnp.where(blk_tile == kt, unsel_any, 1.0), axis=0, keepdims=True)
        bit = bit + jnp.where(in_tile < 0.5, float(2 ** kt), 0.0)
    live_bits = jnp.max(bit).astype(jnp.int32)
    n_live[0] = 0
    live[0] = 0
    for kt in range(S // tk - WINDOW // tk):
        @pl.when((((live_bits >> kt) & 1) == 1) & (kt < kt_w))
        def _():
            live[n_live[0]] = kt
            n_live[0] = n_live[0] + 1

    n_slc = n_live[0]

    def slc_only(i, carry):
        scores, states = carry
        nxt = score_tiles(slc_streams, live[jnp.minimum(i + 1, n_slc - 1)], False)
        return nxt, softmax_pv(slc_streams, live[i], scores, states)

    first = jnp.where(n_slc > 0, live[0], 0)
    _, st_slc = lax.fori_loop(0, n_slc, slc_only, (score_tiles(slc_streams, first, False), (flash_init,) * G))
    both = slc_streams + win_streams
    st_all = lax.fori_loop(kt_w, kt_d + 1, lambda kt, st: softmax_pv(both, kt, score_tiles(both, kt, True), st),
                           st_slc + (flash_init,) * G)
    o_all = []
    for _, acc in st_all:
        den = acc[HEAD_DIM:HEAD_DIM + 1, :]
        o_all.append(acc[:HEAD_DIM] / jnp.where(den > 0.0, den, 1.0))
    o_slcs, o_wins = o_all[:G], o_all[G:]

    for g in range(G):
        for r in range(R):
            hh = g * R + r
            cs = slice(r * tq, (r + 1) * tq)
            o_ref[hh * HEAD_DIM:(hh + 1) * HEAD_DIM, :] = (
                gate_sig[hh:hh + 1, :] * o_cmps[g][:, cs]
                + gate_sig[NSA_HEADS + hh:NSA_HEADS + hh + 1, :] * o_slcs[g][:, cs]
                + gate_sig[2 * NSA_HEADS + hh:2 * NSA_HEADS + hh + 1, :] * o_wins[g][:, cs])


def _nsa_consts(S):
    tq, tk, M = NSA_TQ, NSA_TK, NSA_M
    NC = S // CMP_STRIDE
    NS = S // SLC_BLOCK
    cs = np.arange(NC)[:, None] * CMP_STRIDE
    ss = np.arange(NS)[None, :] * SLC_BLOCK
    ov = np.clip(np.minimum(cs + CMP_BLOCK, ss + SLC_BLOCK) - np.maximum(cs, ss), 0, None) / CMP_BLOCK
    ov[NC - 1:] = 0.0
    ish = np.zeros((NS, LANES), np.float32)
    ish[np.arange(NS), SEL_LANE0 + np.arange(NS)] = 1.0
    t = np.arange(S).reshape(S // tq, 1, tq)
    n = np.arange(NC).reshape(1, NC, 1)
    cbias = np.where((t >= n * CMP_STRIDE + CMP_BLOCK - 1) & (n < NC - 1), 0.0, -MASK_BIG).astype(np.float32)
    i = np.arange(tk)[:, None]
    j = np.arange(M)[None, :] % tq
    tbl = np.stack([np.zeros((tk, M)), np.where(i <= j, 0.0, -MASK_BIG), np.where(i > j, 0.0, -MASK_BIG)])
    return (jnp.asarray(ov.T, BF16), jnp.asarray(ish, BF16), jnp.asarray(cbias), jnp.asarray(tbl, F32))


def _nsa_call(nq, kv, vt, kcx, vct, misct, q_norm_g, k_norm_g, B, S):
    T = B * S
    tq, tk, M = NSA_TQ, NSA_TK, NSA_M
    G = NSA_KV_HEADS
    nqt = S // tq
    NC = S // CMP_STRIDE
    NS = S // SLC_BLOCK
    ovt, ish, cbias, tbl = _nsa_consts(S)
    tile2 = lambda a: jnp.concatenate([a, a], axis=-1)
    row = lambda b, i: (b * nqt + i, 0)
    cst = lambda b, i: (0, 0)
    return pl.pallas_call(
        functools.partial(_nsa_kernel, S=S),
        grid=(B, nqt),
        in_specs=[pl.BlockSpec((tq, NSA_WIDTH), row),
                  pl.BlockSpec((S, KV_COLS), lambda b, i: (b, 0)),
                  pl.BlockSpec((None, S // tk, VT_ROWS, tk), lambda b, i: (b, 0, 0, 0)),
                  pl.BlockSpec((None, G, NC, LANES), lambda b, i: (b, 0, 0, 0)),
                  pl.BlockSpec((None, G, HEAD_DIM, NC), lambda b, i: (b, 0, 0, 0)),
                  pl.BlockSpec((LANES, tq), lambda b, i: (0, b * nqt + i)),
                  pl.BlockSpec((None, NC, tq), lambda b, i: (i, 0, 0)),
                  pl.BlockSpec((3, tk, M), lambda b, i: (0, 0, 0)),
                  pl.BlockSpec((1, LANES), cst),
                  pl.BlockSpec((3, LANES), cst),
                  pl.BlockSpec((NS, NC), cst),
                  pl.BlockSpec((NS, LANES), cst)],
        out_specs=pl.BlockSpec((NSA_WIDTH, tq), lambda b, i: (0, b * nqt + i)),
        out_shape=jax.ShapeDtypeStruct((NSA_WIDTH, T), F32),
        scratch_shapes=[pltpu.VMEM((G, S, LANES), BF16), pltpu.VMEM((G, S, LANES), BF16),
                        pltpu.SMEM((S // tk,), jnp.int32), pltpu.SMEM((1,), jnp.int32)],
        compiler_params=_cparams("parallel", "arbitrary"),
        name="nsa_attn",
    )(nq, kv, vt, kcx, vct, misct, cbias, tbl, tile2(q_norm_g), tile2(k_norm_g), ovt, ish)


def _out_proj_kernel(yg_ref, yn_ref, x_ref, g1_ref, sc_ref, sh_ref, ng_ref, wo_ref, rwt_ref, rb_ref,
                     x1_ref, h2_ref, gate_ref, gate_t_ref):
    mix = _dot(yg_ref[...], wo_ref[:GDN_WIDTH, :]) + _dot_tn(yn_ref[...], wo_ref[GDN_WIDTH:, :])
    x1 = x_ref[...] + g1_ref[...] * mix
    x1_ref[...] = x1
    h2 = _rms(x1, ng_ref[...]) * (1.0 + sc_ref[...]) + sh_ref[...]
    h2_ref[...] = h2.astype(BF16)

    tm = h2.shape[0]
    logits = _dot_nt_hi(rwt_ref[...], h2)
    score = _sigmoid(logits)
    biased = score + rb_ref[...]
    b = [biased[e:e + 1, :] for e in range(N_EXPERTS)]
    n_groups = N_EXPERTS // EXPERTS_PER_GROUP
    gscore = []
    for gi in range(n_groups):
        vals = b[gi * EXPERTS_PER_GROUP:(gi + 1) * EXPERTS_PER_GROUP]
        best = None
        for i in range(EXPERTS_PER_GROUP):
            for j in range(i + 1, EXPERTS_PER_GROUP):
                pair = vals[i] + vals[j]
                best = pair if best is None else jnp.maximum(best, pair)
        gscore.append(best)
    gbest = jnp.zeros((1, tm), jnp.int32)
    top = gscore[0]
    for gi in range(1, n_groups):
        better = gscore[gi] > top
        gbest = jnp.where(better, gi, gbest)
        top = jnp.where(better, gscore[gi], top)
    erow = lax.broadcasted_iota(jnp.int32, (N_EXPERTS, tm), 0)
    gates = jnp.zeros((N_EXPERTS, tm), F32)
    for e in range(N_EXPERTS):
        gi = e // EXPERTS_PER_GROUP
        rank = jnp.zeros((1, tm), F32)
        for e2 in range(gi * EXPERTS_PER_GROUP, (gi + 1) * EXPERTS_PER_GROUP):
            if e2 == e:
                continue
            beats = (b[e2] > b[e]) | ((b[e2] == b[e]) & (e2 < e))
            rank = rank + jnp.where(beats, 1.0, 0.0)
        chosen = (gbest == gi) & (rank < 2.0)
        ge = jnp.where(chosen, score[e:e + 1, :], 0.0)
        gates = gates + jnp.where(erow == e, ge, 0.0)
    gates = gates / jnp.sum(gates, axis=0, keepdims=True)
    pad_rows = jnp.zeros((8 - EXPERTS_PER_GROUP, tm), F32)
    for gi in range(n_groups):
        gate_t_ref[gi] = jnp.concatenate(
            [gates[gi * EXPERTS_PER_GROUP:(gi + 1) * EXPERTS_PER_GROUP, :], pad_rows], axis=0)
    ident = (lax.broadcasted_iota(jnp.int32, (N_EXPERTS, LANES), 0)
             == lax.broadcasted_iota(jnp.int32, (N_EXPERTS, LANES), 1))
    ident = jnp.where(ident, 1.0, 0.0).astype(BF16)
    hi, mid, lo = _split3(gates)
    dn = (((0,), (0,)), ((), ()))
    gate_ref[...] = (lax.dot_general(hi, ident, dn, preferred_element_type=F32)
                     + lax.dot_general(mid, ident, dn, preferred_element_type=F32)
                     + lax.dot_general(lo, ident, dn, preferred_element_type=F32))


def _out_proj_call(yg, yn, x2, g1, sc2, sh2, ng, wo, rwt, rb, S):
    T, D = x2.shape
    tm = 1024
    per_b = S // tm
    row = lambda i: (i, 0)
    bat = lambda i: (i // per_b, 0, 0)
    cst = lambda i: (0, 0)
    return pl.pallas_call(
        _out_proj_kernel,
        grid=(T // tm,),
        in_specs=[pl.BlockSpec((tm, GDN_WIDTH), row), pl.BlockSpec((NSA_WIDTH, tm), lambda i: (0, i)),
                  pl.BlockSpec((tm, D), row),
                  pl.BlockSpec((None, 1, D), bat), pl.BlockSpec((None, 1, D), bat), pl.BlockSpec((None, 1, D), bat),
                  pl.BlockSpec((1, D), cst), pl.BlockSpec((D, D), cst),
                  pl.BlockSpec((N_EXPERTS, D), cst), pl.BlockSpec((N_EXPERTS, 1), cst)],
        out_specs=[pl.BlockSpec((tm, D), row), pl.BlockSpec((tm, D), row), pl.BlockSpec((tm, LANES), row),
                   pl.BlockSpec((N_EXPERTS // EXPERTS_PER_GROUP, 8, tm), lambda i: (0, 0, i))],
        out_shape=[jax.ShapeDtypeStruct((T, D), F32), jax.ShapeDtypeStruct((T, D), BF16),
                   jax.ShapeDtypeStruct((T, LANES), F32),
                   jax.ShapeDtypeStruct((N_EXPERTS // EXPERTS_PER_GROUP, 8, T), F32)],
        compiler_params=_cparams("parallel"),
        name="out_proj_router",
    )(yg, yn, x2, g1, sc2, sh2, ng, wo, rwt, rb)


MOE_TM = 1024
MOE_RB = 288
MOE_PASS = 256


def _moe_kernel(h_ref, gates_ref, gt8_ref, tri_ref, trit_ref, x1_ref, g2_ref, wg_ref, wu_ref, wd_ref, o_ref):
    g = pl.program_id(1)
    tm, rb = MOE_TM, MOE_RB
    nlb = tm // LANES

    @pl.when(g == 0)
    def _():
        o_ref[...] = jnp.zeros_like(o_ref)

    gt8 = gt8_ref[...]
    in_row = jnp.sum(gt8, axis=0, keepdims=True) > 0.0
    first = lax.broadcasted_iota(jnp.int32, (8, LANES), 0) == 0
    off = jnp.zeros((1, 1), F32)
    keys = []
    for b in range(nlb):
        blk = in_row[:, b * LANES:(b + 1) * LANES]
        one = jnp.where(blk & first, 1.0, 0.0).astype(BF16)
        within = jnp.dot(one, tri_ref[...], preferred_element_type=F32)[0:1, :]
        keys.append(jnp.where(blk, within + off, -1.0))
        off = off + jnp.sum(jnp.where(blk, 1.0, 0.0), axis=1, keepdims=True)
    key_row = jnp.concatenate(keys, axis=1)
    n_g = jnp.sum(off).astype(jnp.int32)
    lane = lax.broadcasted_iota(jnp.int32, (tm, LANES), 1)
    mine = (lane >= g * EXPERTS_PER_GROUP) & (lane < (g + 1) * EXPERTS_PER_GROUP)
    in_col = jnp.sum(jnp.where(mine, gates_ref[...], 0.0), axis=1, keepdims=True) > 0.0
    off = jnp.zeros((1, 1), F32)
    keys = []
    for b in range(nlb):
        blk = in_col[b * LANES:(b + 1) * LANES, :]
        one = jnp.broadcast_to(jnp.where(blk, 1.0, 0.0), (LANES, LANES)).astype(BF16)
        within = jnp.dot(trit_ref[...], one, preferred_element_type=F32)[:, 0:1]
        keys.append(jnp.where(blk, within + off, -1.0))
        off = off + jnp.sum(jnp.where(blk, 1.0, 0.0), axis=0, keepdims=True)
    key_col = jnp.concatenate(keys, axis=0)

    src = lax.broadcasted_iota(jnp.int32, (8, LANES), 0)
    dst = lax.broadcasted_iota(jnp.int32, (8, LANES), 1)
    tn_dims = (((0,), (0,)), ((), ()))
    gcols = jnp.zeros((tm, LANES), F32)
    for p, piece in enumerate(_split3(gt8)):
        place = jnp.where((dst == src + EXPERTS_PER_GROUP * p) & (src < EXPERTS_PER_GROUP), 1.0, 0.0).astype(BF16)
        gcols = gcols + lax.dot_general(piece, place, tn_dims, preferred_element_type=F32)
    gcols = gcols.astype(BF16)

    def rows_block(j, nrows):
        r0 = (j * rb).astype(F32)
        sel = jnp.where(key_row - r0 == lax.broadcasted_iota(jnp.int32, (nrows, tm), 0).astype(F32), 1.0, 0.0)
        sel = sel.astype(BF16)
        xb = jnp.dot(sel, h_ref[...], preferred_element_type=F32).astype(BF16)
        gp = jnp.dot(sel, gcols, preferred_element_type=F32)
        y = jnp.zeros((nrows, o_ref.shape[1]), F32)
        for k in range(EXPERTS_PER_GROUP):
            gate = (gp[:, k:k + 1] + gp[:, EXPERTS_PER_GROUP + k:EXPERTS_PER_GROUP + k + 1]
                    + gp[:, 2 * EXPERTS_PER_GROUP + k:2 * EXPERTS_PER_GROUP + k + 1])
            act = (_silu(jnp.dot(xb, wg_ref[k], preferred_element_type=F32))
                   * jnp.dot(xb, wu_ref[k], preferred_element_type=F32))
            y = y + _dot(act * gate, wd_ref[k])
        yb = y.astype(BF16)
        rel = key_col - r0
        for lo in range(0, nrows, MOE_PASS):
            hi = min(lo + MOE_PASS, nrows)
            sel_t = jnp.where(rel == (lo + lax.broadcasted_iota(jnp.int32, (tm, hi - lo), 1)).astype(F32), 1.0, 0.0)
            o_ref[...] += jnp.dot(sel_t.astype(BF16), yb[lo:hi], preferred_element_type=F32)

    def block(j, carry):
        @pl.when(n_g <= j * rb + MOE_PASS)
        def _():
            rows_block(j, MOE_PASS)

        @pl.when(n_g > j * rb + MOE_PASS)
        def _():
            rows_block(j, rb)

        return carry

    lax.fori_loop(0, (n_g + rb - 1) // rb, block, 0)

    @pl.when(g == N_EXPERTS // EXPERTS_PER_GROUP - 1)
    def _():
        o_ref[...] = x1_ref[...] + g2_ref[...] * o_ref[...]


def _moe_call(h2, gates, gates_t, x1, g2, wg, wu, wd, S):
    T, D = x1.shape
    tm = MOE_TM
    per_b = S // tm
    n_groups = N_EXPERTS // EXPERTS_PER_GROUP
    i = np.arange(LANES)
    tri = jnp.asarray(i[:, None] < i[None, :], BF16)
    row = lambda i, g: (i, 0)
    cst = lambda i, g: (0, 0)
    wsel = lambda i, g: (g, 0, 0)
    return pl.pallas_call(
        _moe_kernel,
        grid=(T // tm, n_groups),
        in_specs=[pl.BlockSpec((tm, D), row), pl.BlockSpec((tm, LANES), row),
                  pl.BlockSpec((None, 8, tm), lambda i, g: (g, 0, i)),
                  pl.BlockSpec((LANES, LANES), cst), pl.BlockSpec((LANES, LANES), cst),
                  pl.BlockSpec((tm, D), row),
                  pl.BlockSpec((None, 1, D), lambda i, g: (i // per_b, 0, 0)),
                  pl.BlockSpec((EXPERTS_PER_GROUP, D, EXPERT_FF), wsel),
                  pl.BlockSpec((EXPERTS_PER_GROUP, D, EXPERT_FF), wsel),
                  pl.BlockSpec((EXPERTS_PER_GROUP, EXPERT_FF, D), wsel)],
        out_specs=pl.BlockSpec((tm, D), row),
        out_shape=jax.ShapeDtypeStruct((T, D), F32),
        compiler_params=_cparams("parallel", "arbitrary"),
        name="moe",
    )(h2, gates, gates_t, tri, tri.T, x1, g2, wg, wu, wd)


def kernel(x, c, ada_w, ada_b, norm1_g, norm2_g, w_in, gdn_conv_w, gdn_a_log, gdn_dt_bias, gdn_norm_g,
           nsa_q_norm_g, nsa_k_norm_g, cmp_pe, cmp_w1, cmp_b1, cmp_w2, cmp_b2, w_out, router_w, router_bias,
           exp_w_gate, exp_w_up, exp_w_down):
    B, S, D = x.shape
    L = ada_w.shape[0]
    T = B * S
    G = NSA_KV_HEADS
    NC = S // CMP_STRIDE
    N = S // GDN_CHUNK
    P = GDN_HEADS // GDN_HPB
    W = GDN_HPB * HEAD_DIM

    mod = _ada_call(c, ada_w, ada_b)
    rwt = router_w.T
    rb = router_bias.reshape(N_EXPERTS, 1)
    x2 = x.reshape(T, D)
    for l in range(L):
        m6 = mod[l].reshape(B, 6, 1, D)
        sh1, sc1, g1, sh2, sc2, g2 = (m6[:, i] for i in range(6))
        qkv, z, nq, kv, misc, misct, vt = _in_proj_call(
            x2, sc1, sh1, norm1_g[l].reshape(1, D), *_pad_w_in(w_in[l]), S)

        a_rows = misct[:GDN_HEADS].reshape(P, GDN_HPB, B, N, GDN_CHUNK).transpose(2, 0, 3, 1, 4).reshape(B, P, N, W)
        rep = lambda t: jnp.repeat(t, HEAD_DIM).reshape(1, GDN_WIDTH)
        y_gdn = _gdn_call(qkv, z, misc, a_rows, gdn_conv_w[l], rep(gdn_a_log[l]), rep(gdn_dt_bias[l]),
                          jnp.tile(gdn_norm_g[l], GDN_HPB).reshape(1, W), B, S)

        w2 = cmp_w2[l]
        kcx, vct = _cmp_call(
            kv, cmp_pe[l].reshape(2, 1, CMP_BLOCK * HEAD_DIM), cmp_w1[l].astype(BF16),
            cmp_b1[l].reshape(2, 1, CMP_HIDDEN), jnp.pad(w2[0], ((0, 0), (0, LANES - HEAD_DIM))).astype(BF16),
            w2[1].T.astype(BF16), jnp.pad(cmp_b2[l, 0], (0, LANES - HEAD_DIM)).reshape(1, LANES),
            cmp_b2[l, 1].reshape(HEAD_DIM, 1), jnp.pad(nsa_k_norm_g[l, 0], (0, LANES - HEAD_DIM)).reshape(1, LANES),
            B, S)
        y_nsa_t = _nsa_call(nq, kv, vt.reshape(B, S // NSA_TK, VT_ROWS, NSA_TK), kcx, vct, misct,
                            nsa_q_norm_g[l].reshape(1, HEAD_DIM), nsa_k_norm_g[l], B, S)

        x1, h2, gates, gates_t = _out_proj_call(y_gdn, y_nsa_t, x2, g1, sc2, sh2, norm2_g[l].reshape(1, D),
                                       w_out[l].astype(BF16), rwt, rb, S)
        x2 = _moe_call(h2, gates, gates_t, x1, g2, exp_w_gate[l].astype(BF16), exp_w_up[l].astype(BF16),
                       exp_w_down[l].astype(BF16), S)
    return x2.reshape(B, S, D)
```
